```python
import jax, jax.numpy as jnp
from jax import lax
import numpy as np

D_MODEL = 2048
BATCH = 2
SEQ = 8192
DEPTH = 2

HEAD_DIM = 128
BLOCK_Q = 128
ROPE_THETA = 10000.0
LN_EPS = 1e-5
RMS_EPS = 1e-6

MLA_HEADS = 10
MLA_Q_RANK = 448
MLA_KV_RANK = 128
MLA_NOPE_DIM = 128
MLA_ROPE_DIM = 64
MLA_V_DIM = 128
MLA_QK_DIM = MLA_NOPE_DIM + MLA_ROPE_DIM

DIL_PATTERNS = ((128, 1), (512, 4), (2048, 16))
DIL_GROUPS = len(DIL_PATTERNS)
DIL_HEADS = 6

OFF_CQ = 0
OFF_CKV = OFF_CQ + MLA_Q_RANK
OFF_KROPE = OFF_CKV + MLA_KV_RANK
OFF_DQ = OFF_KROPE + MLA_ROPE_DIM
OFF_DK = OFF_DQ + DIL_GROUPS * DIL_HEADS * HEAD_DIM
OFF_DV = OFF_DK + DIL_HEADS * HEAD_DIM
W_IN_COLS = OFF_DV + DIL_HEADS * HEAD_DIM
MIX_WIDTH = MLA_HEADS * MLA_V_DIM + DIL_HEADS * HEAD_DIM

FOX_HEADS = 16
FOX_WIDTH = FOX_HEADS * HEAD_DIM

D_FF = 5632
N_EXPERTS = 8
TOP_K = 2

ALPHA = (2.0 * DEPTH) ** 0.25
BETA = (8.0 * DEPTH) ** -0.25
N_EVEN_LAYERS = (DEPTH + 1) // 2
N_ODD_LAYERS = DEPTH // 2

kernel_name = "hybrid_mla_dilated_fox_moe_deepnorm"

F32 = jnp.float32


def layer_norm(x, g, b):
    xf = x.astype(F32)
    mu = jnp.mean(xf, axis=-1, keepdims=True)
    var = jnp.mean(jnp.square(xf - mu), axis=-1, keepdims=True)
    return ((xf - mu) * lax.rsqrt(var + LN_EPS) * g.astype(F32) + b.astype(F32)).astype(x.dtype)


def rms_norm(x, g):
    xf = x.astype(F32)
    y = xf * lax.rsqrt(jnp.mean(jnp.square(xf), axis=-1, keepdims=True) + RMS_EPS)
    return (y * g.astype(F32)).astype(x.dtype)


def rope_tables(seq, dim):
    inv_freq = 1.0 / (ROPE_THETA ** (jnp.arange(0, dim, 2, dtype=F32) / dim))
    ang = jnp.arange(seq, dtype=F32)[:, None] * inv_freq[None, :]
    return jnp.cos(ang), jnp.sin(ang)


def apply_rope(x, cos, sin):
    half = x.shape[-1] // 2
    xf = x.astype(F32)
    x1, x2 = xf[..., :half], xf[..., half:]
    c = cos[None, :, None, :]
    s = sin[None, :, None, :]
    return jnp.concatenate([x1 * c - x2 * s, x2 * c + x1 * s], axis=-1).astype(x.dtype)


def _to_blocks(a):
    b, s = a.shape[0], a.shape[1]
    a = a.reshape((b, s // BLOCK_Q, BLOCK_Q) + a.shape[2:])
    return jnp.moveaxis(a, 1, 0)


def _from_blocks(a):
    a = jnp.moveaxis(a, 0, 1)
    return a.reshape((a.shape[0], a.shape[1] * a.shape[2]) + a.shape[3:])


def causal_block_attention(q, k, v, scale, log_f_cum=None):
    seq = q.shape[1]
    kpos = jnp.arange(seq)
    qb = _to_blocks(q)
    blk_idx = jnp.arange(qb.shape[0])
    if log_f_cum is None:
        xs = (blk_idx, qb)
    else:
        c_keys = jnp.transpose(log_f_cum, (0, 2, 1))[:, :, None, :]
        xs = (blk_idx, qb, _to_blocks(log_f_cum))

    def one_block(args):
        i, q_blk = args[0], args[1]
        qpos = i * BLOCK_Q + jnp.arange(BLOCK_Q)
        s = jnp.einsum('bqhd,bkhd->bhqk', q_blk, k, preferred_element_type=F32) * scale
        if log_f_cum is not None:
            c_q = jnp.transpose(args[2], (0, 2, 1))[..., None]
            s = s + (c_q - c_keys)
        s = jnp.where((kpos[None, :] <= qpos[:, None])[None, None], s, -jnp.inf)
        p = jax.nn.softmax(s, axis=-1)
        return jnp.einsum('bhqk,bkhd->bqhd', p.astype(v.dtype), v)

    return _from_blocks(lax.map(one_block, xs))


def dilated_group_attention(q, k, v, window, dilation, scale):
    offs = jnp.arange(window // dilation + 1) * dilation
    qb = _to_blocks(q)

    def one_block(args):
        i, q_blk = args
        qpos = i * BLOCK_Q + jnp.arange(BLOCK_Q)
        kidx = qpos[:, None] - offs[None, :]
        valid = kidx >= 0
        kidx = jnp.maximum(kidx, 0)
        k_g = jnp.take(k, kidx, axis=1)
        v_g = jnp.take(v, kidx, axis=1)
        s = jnp.einsum('bqhd,bqjhd->bhqj', q_blk, k_g, preferred_element_type=F32) * scale
        s = jnp.where(valid[None, None], s, -jnp.inf)
        lse = jax.nn.logsumexp(s, axis=-1, keepdims=True)
        p = jnp.exp(s - lse)
        o = jnp.einsum('bhqj,bqjhd->bqhd', p.astype(v.dtype), v_g)
        return o, jnp.transpose(lse[..., 0], (0, 2, 1))

    o, lse = lax.map(one_block, (jnp.arange(qb.shape[0]), qb))
    return _from_blocks(o), _from_blocks(lse)


def mla_dilated_mixer(x, w_in, q_norm, w_q_b, kv_norm, w_kv_b, w_out, cos64, sin64, cos128, sin128):
    b, s, _ = x.shape
    h = x @ w_in
    c_q = rms_norm(h[..., OFF_CQ:OFF_CKV], q_norm)
    q = (c_q @ w_q_b).reshape(b, s, MLA_HEADS, MLA_QK_DIM)
    q = jnp.concatenate([q[..., :MLA_NOPE_DIM],
                         apply_rope(q[..., MLA_NOPE_DIM:], cos64, sin64)], axis=-1)
    c_kv = rms_norm(h[..., OFF_CKV:OFF_KROPE], kv_norm)
    kv = (c_kv @ w_kv_b).reshape(b, s, MLA_HEADS, MLA_NOPE_DIM + MLA_V_DIM)
    k_rope = apply_rope(h[..., OFF_KROPE:OFF_DQ].reshape(b, s, 1, MLA_ROPE_DIM), cos64, sin64)
    k = jnp.concatenate([kv[..., :MLA_NOPE_DIM],
                         jnp.broadcast_to(k_rope, (b, s, MLA_HEADS, MLA_ROPE_DIM))], axis=-1)
    v = kv[..., MLA_NOPE_DIM:]
    o_mla = causal_block_attention(q, k, v, MLA_QK_DIM ** -0.5)
    dq = apply_rope(h[..., OFF_DQ:OFF_DK].reshape(b, s, DIL_GROUPS * DIL_HEADS, HEAD_DIM), cos128, sin128)
    dq = dq.reshape(b, s, DIL_GROUPS, DIL_HEADS, HEAD_DIM)
    dk = apply_rope(h[..., OFF_DK:OFF_DV].reshape(b, s, DIL_HEADS, HEAD_DIM), cos128, sin128)
    dv = h[..., OFF_DV:W_IN_COLS].reshape(b, s, DIL_HEADS, HEAD_DIM)
    outs, lses = [], []
    for g, (window, dilation) in enumerate(DIL_PATTERNS):
        o_g, lse_g = dilated_group_attention(dq[:, :, g], dk, dv, window, dilation, HEAD_DIM ** -0.5)
        outs.append(o_g)
        lses.append(lse_g)
    wts = jax.nn.softmax(jnp.stack(lses, axis=0), axis=0)
    o_dil = jnp.sum(wts[..., None] * jnp.stack(outs, axis=0).astype(F32), axis=0).astype(x.dtype)
    mixed = jnp.concatenate([o_mla.reshape(b, s, MLA_HEADS * MLA_V_DIM),
                             o_dil.reshape(b, s, DIL_HEADS * HEAD_DIM)], axis=-1)
    return mixed @ w_out


def forgetting_attention_mixer(x, w_qkv, w_f, b_f, w_out):
    b, s, _ = x.shape
    qkv = (x @ w_qkv).reshape(b, s, 3, FOX_HEADS, HEAD_DIM)
    log_f = jax.nn.log_sigmoid((x @ w_f + b_f).astype(F32))
    c = jnp.cumsum(log_f, axis=1)
    o = causal_block_attention(qkv[:, :, 0], qkv[:, :, 1], qkv[:, :, 2], HEAD_DIM ** -0.5, c)
    return o.reshape(b, s, FOX_WIDTH) @ w_out


def swiglu(x, w_gate, w_up, w_down):
    return (jax.nn.silu(x @ w_gate) * (x @ w_up)) @ w_down


def top2_moe(x, router_w, router_b, w_gate, w_up, w_down):
    logits = (x @ router_w + router_b).astype(F32)
    top_logits, top_idx = lax.top_k(logits, TOP_K)
    top_w = jax.nn.softmax(top_logits, axis=-1)
    gate = jnp.sum(jax.nn.one_hot(top_idx, N_EXPERTS, dtype=F32) * top_w[..., None], axis=-2).astype(x.dtype)
    out = jnp.zeros_like(x)
    for e in range(N_EXPERTS):
        out = out + gate[..., e:e + 1] * swiglu(x, w_gate[e], w_up[e], w_down[e])
    return out


def setup_inputs(seed: int = 0) -> dict:
    key = jax.random.key(seed)
    ks = jax.random.split(key, 32)

    def w(k, shape, fan_in, scale=1.0):
        return jax.random.normal(k, shape, F32) * (scale * fan_in ** -0.5)

    def gain(k, shape):
        return 1.0 + 0.02 * jax.random.normal(k, shape, F32)

    def bias(k, shape, scale=0.02):
        return scale * jax.random.normal(k, shape, F32)

    ne, no = N_EVEN_LAYERS, N_ODD_LAYERS
    in_col_scale = jnp.ones((W_IN_COLS,), F32).at[OFF_DV:].set(BETA)
    kv_b_col_scale = jnp.tile(jnp.concatenate([jnp.ones((MLA_NOPE_DIM,), F32),
                                               jnp.full((MLA_V_DIM,), BETA, F32)]), MLA_HEADS)
    qkv_col_scale = jnp.ones((3 * FOX_WIDTH,), F32).at[2 * FOX_WIDTH:].set(BETA)
    return {
        "x": jax.random.normal(ks[0], (BATCH, SEQ, D_MODEL), F32),
        "ev_w_in": w(ks[1], (ne, D_MODEL, W_IN_COLS), D_MODEL) * in_col_scale,
        "ev_q_norm": gain(ks[2], (ne, MLA_Q_RANK)),
        "ev_w_q_b": w(ks[3], (ne, MLA_Q_RANK, MLA_HEADS * MLA_QK_DIM), MLA_Q_RANK),
        "ev_kv_norm": gain(ks[4], (ne, MLA_KV_RANK)),
        "ev_w_kv_b": w(ks[5], (ne, MLA_KV_RANK, MLA_HEADS * (MLA_NOPE_DIM + MLA_V_DIM)), MLA_KV_RANK) * kv_b_col_scale,
        "ev_w_out": w(ks[6], (ne, MIX_WIDTH, D_MODEL), MIX_WIDTH, BETA),
        "ev_ln1_g": gain(ks[7], (ne, D_MODEL)),
        "ev_ln1_b": bias(ks[8], (ne, D_MODEL)),
        "ev_ffn_w_gate": w(ks[9], (ne, D_MODEL, D_FF), D_MODEL),
        "ev_ffn_w_up": w(ks[10], (ne, D_MODEL, D_FF), D_MODEL, BETA),
        "ev_ffn_w_down": w(ks[11], (ne, D_FF, D_MODEL), D_FF, BETA),
        "ev_ln2_g": gain(ks[12], (ne, D_MODEL)),
        "ev_ln2_b": bias(ks[13], (ne, D_MODEL)),
        "od_w_qkv": w(ks[14], (no, D_MODEL, 3 * FOX_WIDTH), D_MODEL) * qkv_col_scale,
        "od_w_f": w(ks[15], (no, D_MODEL, FOX_HEADS), D_MODEL),
        "od_b_f": bias(ks[16], (no, FOX_HEADS), 0.1),
        "od_w_out": w(ks[17], (no, FOX_WIDTH, D_MODEL), FOX_WIDTH, BETA),
        "od_ln1_g": gain(ks[18], (no, D_MODEL)),
        "od_ln1_b": bias(ks[19], (no, D_MODEL)),
        "od_router_w": w(ks[20], (no, D_MODEL, N_EXPERTS), D_MODEL),
        "od_router_b": bias(ks[21], (no, N_EXPERTS), 0.01),
        "od_exp_w_gate": w(ks[22], (no, N_EXPERTS, D_MODEL, D_FF), D_MODEL),
        "od_exp_w_up": w(ks[23], (no, N_EXPERTS, D_MODEL, D_FF), D_MODEL, BETA),
        "od_exp_w_down": w(ks[24], (no, N_EXPERTS, D_FF, D_MODEL), D_FF, BETA),
        "od_ln2_g": gain(ks[25], (no, D_MODEL)),
        "od_ln2_b": bias(ks[26], (no, D_MODEL)),
    }


def reference(x, ev_w_in, ev_q_norm, ev_w_q_b, ev_kv_norm, ev_w_kv_b, ev_w_out, ev_ln1_g, ev_ln1_b,
              ev_ffn_w_gate, ev_ffn_w_up, ev_ffn_w_down, ev_ln2_g, ev_ln2_b,
              od_w_qkv, od_w_f, od_b_f, od_w_out, od_ln1_g, od_ln1_b,
              od_router_w, od_router_b, od_exp_w_gate, od_exp_w_up, od_exp_w_down, od_ln2_g, od_ln2_b):
    seq = x.shape[1]
    cos64, sin64 = rope_tables(seq, MLA_ROPE_DIM)
    cos128, sin128 = rope_tables(seq, HEAD_DIM)
    for layer in range(DEPTH):
        i = layer // 2
        if layer % 2 == 0:
            mix = mla_dilated_mixer(x, ev_w_in[i], ev_q_norm[i], ev_w_q_b[i], ev_kv_norm[i], ev_w_kv_b[i],
                                    ev_w_out[i], cos64, sin64, cos128, sin128)
            x = layer_norm(ALPHA * x + mix, ev_ln1_g[i], ev_ln1_b[i])
            ffn = swiglu(x, ev_ffn_w_gate[i], ev_ffn_w_up[i], ev_ffn_w_down[i])
            x = layer_norm(ALPHA * x + ffn, ev_ln2_g[i], ev_ln2_b[i])
        else:
            mix = forgetting_attention_mixer(x, od_w_qkv[i], od_w_f[i], od_b_f[i], od_w_out[i])
            x = layer_norm(ALPHA * x + mix, od_ln1_g[i], od_ln1_b[i])
            ffn = top2_moe(x, od_router_w[i], od_router_b[i], od_exp_w_gate[i], od_exp_w_up[i], od_exp_w_down[i])
            x = layer_norm(ALPHA * x + ffn, od_ln2_g[i], od_ln2_b[i])
    return x
```

```python
import functools

import jax
import jax.numpy as jnp
from jax import lax
from jax.experimental import pallas as pl
from jax.experimental.pallas import tpu as pltpu

F32 = jnp.float32
BF16 = jnp.bfloat16

D_MODEL = 2048
HEAD_DIM = 128
LANES = 128
ROPE_THETA = 10000.0
LN_EPS = 1e-5
RMS_EPS = 1e-6

MLA_HEADS = 10
MLA_Q_RANK = 448
MLA_Q_RANK_PAD = 512
MLA_KV_RANK = 128
MLA_NOPE_DIM = 128
MLA_ROPE_DIM = 64
MLA_V_DIM = 128
MLA_QK_DIM = MLA_NOPE_DIM + MLA_ROPE_DIM
MLA_QK_PAD = 256

DIL_PATTERNS = ((128, 1), (512, 4), (2048, 16))
DIL_GROUPS = 3
DIL_HEADS = 6
DIL_WIDTH = DIL_HEADS * HEAD_DIM
DIL_SPAN = 128

OFF_CQ = 0
OFF_CKV = OFF_CQ + MLA_Q_RANK
OFF_KROPE = OFF_CKV + MLA_KV_RANK
OFF_DQ = OFF_KROPE + MLA_ROPE_DIM
OFF_DK = OFF_DQ + DIL_GROUPS * DIL_WIDTH
OFF_DV = OFF_DK + DIL_WIDTH
W_IN_COLS = OFF_DV + DIL_WIDTH

FOX_HEADS = 16
FOX_WIDTH = FOX_HEADS * HEAD_DIM

D_FF = 5632
N_EXPERTS = 8
DEPTH = 2
ALPHA = (2.0 * DEPTH) ** 0.25

NEG = -1e30
VMEM_LIMIT = 56 * 1024 * 1024


def _params(*sem, vmem=VMEM_LIMIT):
    return pltpu.CompilerParams(dimension_semantics=sem, vmem_limit_bytes=vmem)


def _dot(a, b):
    return jnp.dot(a, b, preferred_element_type=F32)


def _dot_nt(a, b):
    return lax.dot_general(a, b, (((1,), (1,)), ((), ())), preferred_element_type=F32)


def _layer_norm(y, g, b):
    mu = jnp.mean(y, axis=-1, keepdims=True)
    d = y - mu
    var = jnp.mean(d * d, axis=-1, keepdims=True)
    return d * lax.rsqrt(var + LN_EPS) * g + b


def _rope128(x, cos_f, sin_f):
    return x * cos_f + pltpu.roll(x, 64, 1) * sin_f


def _rope64(x, cos_t, sin_a, sin_b):
    return x * cos_t + pltpu.roll(x, 96, 1) * sin_a + pltpu.roll(x, 32, 1) * sin_b


def _mla_prep_kernel(x_ref, wa_ref, kvg_ref, qg_ref, wq_ref, wkv_ref, cos_ref, sa_ref, sb_ref,
                     q_ref, k_ref, v_ref):
    h = _dot(x_ref[...], wa_ref[...])
    ckv = h[:, 0:128]
    kr = h[:, 128:256]
    cq = h[:, 256:768]
    ckv_n = (ckv * lax.rsqrt(jnp.mean(ckv * ckv, axis=-1, keepdims=True) + RMS_EPS) * kvg_ref[...]).astype(BF16)
    cq_ms = jnp.sum(cq * cq, axis=-1, keepdims=True) * (1.0 / MLA_Q_RANK)
    cq_n = (cq * lax.rsqrt(cq_ms + RMS_EPS) * qg_ref[...]).astype(BF16)
    cos_t, sin_a, sin_b = cos_ref[...], sa_ref[...], sb_ref[...]
    kr_r = _rope64(kr, cos_t, sin_a, sin_b).astype(BF16)
    scale = MLA_QK_DIM ** -0.5
    q = _dot(cq_n, wq_ref[...])
    kv = _dot(ckv_n, wkv_ref[...])
    for hd in range(MLA_HEADS):
        o = hd * MLA_QK_PAD
        q_ref[:, o:o + 128] = (q[:, o:o + 128] * scale).astype(BF16)
        q_ref[:, o + 128:o + 256] = (_rope64(q[:, o + 128:o + 256], cos_t, sin_a, sin_b) * scale).astype(BF16)
        k_ref[:, o:o + 128] = kv[:, hd * 128:(hd + 1) * 128].astype(BF16)
        k_ref[:, o + 128:o + 256] = kr_r
    v_ref[...] = kv[:, MLA_HEADS * 128:].astype(BF16)


def _mla_prep(xb, wa, kvg, qg, wq, wkv, cos_t, sin_a, sin_b, seq, tm):
    m = xb.shape[0]
    nrow = seq // tm
    full = lambda shape: pl.BlockSpec(shape, lambda i: (0, 0))
    tab = pl.BlockSpec((tm, LANES), lambda i: (i % nrow, 0))
    wide = MLA_HEADS * MLA_QK_PAD
    return pl.pallas_call(
        _mla_prep_kernel,
        grid=(m // tm,),
        in_specs=[pl.BlockSpec((tm, D_MODEL), lambda i: (i, 0)), full(wa.shape), full(kvg.shape), full(qg.shape),
                  full(wq.shape), full(wkv.shape), tab, tab, tab],
        out_specs=[pl.BlockSpec((tm, wide), lambda i: (i, 0)), pl.BlockSpec((tm, wide), lambda i: (i, 0)),
                   pl.BlockSpec((tm, MLA_HEADS * MLA_V_DIM), lambda i: (i, 0))],
        out_shape=[jax.ShapeDtypeStruct((m, wide), BF16), jax.ShapeDtypeStruct((m, wide), BF16),
                   jax.ShapeDtypeStruct((m, MLA_HEADS * MLA_V_DIM), BF16)],
        compiler_params=_params("parallel"),
        name="mla_prep",
    )(xb, wa, kvg, qg, wq, wkv, cos_t, sin_a, sin_b)


def _dqkv_kernel(x_ref, w_ref, cos_ref, sin_ref, o_ref):
    j = pl.program_id(0)
    h = _dot(x_ref[...], w_ref[...])

    @pl.when(j < 4)
    def _():
        cos_f, sin_f = cos_ref[...], sin_ref[...]
        sc = jnp.where(j < 3, HEAD_DIM ** -0.5, 1.0).astype(F32)
        for hd in range(DIL_HEADS):
            sl = slice(hd * 128, (hd + 1) * 128)
            o_ref[:, sl] = (_rope128(h[:, sl], cos_f, sin_f) * sc).astype(BF16)

    @pl.when(j == 4)
    def _():
        o_ref[...] = h.astype(BF16)


def _dqkv(xb, wd, cos_f, sin_f, seq, tm):
    m = xb.shape[0]
    nrow = seq // tm
    n_col = wd.shape[1] // DIL_WIDTH
    tab = pl.BlockSpec((tm, LANES), lambda j, i: (i % nrow, 0))
    return pl.pallas_call(
        _dqkv_kernel,
        grid=(n_col, m // tm),
        in_specs=[pl.BlockSpec((tm, D_MODEL), lambda j, i: (i, 0)),
                  pl.BlockSpec((D_MODEL, DIL_WIDTH), lambda j, i: (0, j)), tab, tab],
        out_specs=pl.BlockSpec((tm, DIL_WIDTH), lambda j, i: (i, j)),
        out_shape=jax.ShapeDtypeStruct((m, wd.shape[1]), BF16),
        compiler_params=_params("parallel", "parallel"),
        name="dil_qkv",
    )(xb, wd, cos_f, sin_f)


def _flash_kernel(*refs, tq, tk, use_c):
    if use_c:
        q_ref, k_ref, v_ref, c_ref, o_ref = refs
    else:
        q_ref, k_ref, v_ref, o_ref = refs
    i = pl.program_id(2)
    q = q_ref[...]
    dv = v_ref.shape[-1]
    if use_c:
        c_base = c_ref[:, pl.ds(pl.multiple_of(i * tq, LANES), LANES)][:, 0:1]

    def step(j, carry, masked):
        m, l, acc = carry
        start = pl.multiple_of(j * tk, tk)
        s = _dot_nt(q, k_ref[pl.ds(start, tk), :])
        if use_c:
            s = s + (c_base - c_ref[:, pl.ds(start, tk)])
        if masked:
            row = lax.broadcasted_iota(jnp.int32, (tq, tk), 0) + i * tq
            col = lax.broadcasted_iota(jnp.int32, (tq, tk), 1) + j * tk
            s = jnp.where(col <= row, s, NEG)
        m_new = jnp.maximum(m, jnp.max(s, axis=-1, keepdims=True))
        a = jnp.exp(m - m_new)
        p = jnp.exp(s - m_new)
        l = a * l + jnp.sum(p, axis=-1, keepdims=True)
        acc = a * acc + _dot(p.astype(BF16), v_ref[pl.ds(start, tk), :])
        return m_new, l, acc

    r = tq // tk
    init = (jnp.full((tq, 1), NEG, F32), jnp.zeros((tq, 1), F32), jnp.zeros((tq, dv), F32))
    carry = lax.fori_loop(0, i * r, functools.partial(step, masked=False), init)
    for d in range(r):
        carry = step(i * r + d, carry, True)
    _, l, acc = carry
    o_ref[...] = (acc / l).astype(o_ref.dtype)


def _flash(q, k, v, c, *, heads, dk, dv, q_off, k_off, v_off, tq, tk):
    b, s, _ = q.shape
    use_c = c is not None
    in_specs = [pl.BlockSpec((None, tq, dk), lambda bi, h, i: (bi, i, q_off + h)),
                pl.BlockSpec((None, s, dk), lambda bi, h, i: (bi, 0, k_off + h)),
                pl.BlockSpec((None, s, dv), lambda bi, h, i: (bi, 0, v_off + h))]
    args = [q, k, v]
    if use_c:
        in_specs.append(pl.BlockSpec((None, None, 1, s), lambda bi, h, i: (bi, h, 0, 0)))
        args.append(c)
    return pl.pallas_call(
        functools.partial(_flash_kernel, tq=tq, tk=tk, use_c=use_c),
        grid=(b, heads, s // tq),
        in_specs=in_specs,
        out_specs=pl.BlockSpec((None, tq, dv), lambda bi, h, i: (bi, i, h)),
        out_shape=jax.ShapeDtypeStruct((b, s, heads * dv), BF16),
        compiler_params=_params("parallel", "parallel", "arbitrary"),
        name="flash_fox" if use_c else "flash_mla",
    )(*args)


def _dilated_kernel(q_ref, kc_ref, kp_ref, vc_ref, vp_ref, o_ref, lse_ref, *, tu):
    i = pl.program_id(2)
    nk = DIL_SPAN + tu
    a = lax.broadcasted_iota(jnp.int32, (tu, nk), 0)
    c = lax.broadcasted_iota(jnp.int32, (tu, nk), 1)
    back = a + DIL_SPAN - c
    first = jnp.where(i > 0, 0, DIL_SPAN)
    bias = jnp.where(back >= 0, jnp.where(back <= DIL_SPAN, jnp.where(c >= first, 0.0, NEG), NEG), NEG)
    lane = lax.broadcasted_iota(jnp.int32, (tu, LANES), 1)
    lse_all = jnp.zeros((tu, LANES), F32)
    for hd in range(DIL_HEADS):
        sl = slice(hd * 128, (hd + 1) * 128)
        k = jnp.concatenate([kp_ref[tu - DIL_SPAN:, sl], kc_ref[:, sl]], axis=0)
        v = jnp.concatenate([vp_ref[tu - DIL_SPAN:, sl], vc_ref[:, sl]], axis=0)
        s = _dot_nt(q_ref[:, sl], k) + bias
        m = jnp.max(s, axis=-1, keepdims=True)
        p = jnp.exp(s - m)
        l = jnp.sum(p, axis=-1, keepdims=True)
        o_ref[:, sl] = _dot(p.astype(BF16), v) / l
        lse_all = jnp.where(lane == hd, m + jnp.log(l), lse_all)
    lse_ref[...] = lse_all


def _dilated_group(dqkv, g, dil, batch, seq, tu):
    su = seq // dil
    nblk = dqkv.shape[1] // DIL_WIDTH
    view = dqkv.reshape(batch, su, dil * dqkv.shape[1])
    cur = lambda blk: pl.BlockSpec((None, tu, DIL_WIDTH), lambda b, r, i: (b, i, r * nblk + blk))
    prev = lambda blk: pl.BlockSpec((None, tu, DIL_WIDTH), lambda b, r, i: (b, jnp.maximum(i - 1, 0), r * nblk + blk))
    o, lse = pl.pallas_call(
        functools.partial(_dilated_kernel, tu=tu),
        grid=(batch, dil, su // tu),
        in_specs=[cur(g), cur(3), prev(3), cur(4), prev(4)],
        out_specs=[pl.BlockSpec((None, tu, DIL_WIDTH), lambda b, r, i: (b, i, r)),
                   pl.BlockSpec((None, tu, LANES), lambda b, r, i: (b, i, r))],
        out_shape=[jax.ShapeDtypeStruct((batch, su, dil * DIL_WIDTH), F32),
                   jax.ShapeDtypeStruct((batch, su, dil * LANES), F32)],
        compiler_params=_params("parallel", "parallel", "arbitrary"),
        name=f"dilated_{dil}",
    )(view, view, view, view, view)
    return o.reshape(batch * seq, DIL_WIDTH), lse.reshape(batch * seq, LANES)


def _dil_merge_kernel(o0, o1, o2, l0, l1, l2, out_ref):
    ls = [l0[...], l1[...], l2[...]]
    mx = jnp.maximum(jnp.maximum(ls[0], ls[1]), ls[2])
    es = [jnp.exp(x - mx) for x in ls]
    den = es[0] + es[1] + es[2]
    ws = [e / den for e in es]
    os_ = [o0, o1, o2]
    for hd in range(DIL_HEADS):
        sl = slice(hd * 128, (hd + 1) * 128)
        acc = ws[0][:, hd:hd + 1] * os_[0][:, sl]
        for g in (1, 2):
            acc = acc + ws[g][:, hd:hd + 1] * os_[g][:, sl]
        out_ref[:, sl] = acc.astype(BF16)


def _dil_merge(outs, lses, tm):
    m = outs[0].shape[0]
    ob = pl.BlockSpec((tm, DIL_WIDTH), lambda i: (i, 0))
    lb = pl.BlockSpec((tm, LANES), lambda i: (i, 0))
    return pl.pallas_call(
        _dil_merge_kernel,
        grid=(m // tm,),
        in_specs=[ob, ob, ob, lb, lb, lb],
        out_specs=ob,
        out_shape=jax.ShapeDtypeStruct((m, DIL_WIDTH), BF16),
        compiler_params=_params("parallel"),
        name="dil_merge",
    )(*outs, *lses)


def _proj_ln_kernel(*refs, n_lhs):
    lhs = refs[:n_lhs]
    ws = refs[n_lhs:2 * n_lhs]
    x_ref, g_ref, b_ref, o_ref, ob_ref = refs[2 * n_lhs:]
    y = _dot(lhs[0][...], ws[0][...])
    for a, w in zip(lhs[1:], ws[1:]):
        y = y + _dot(a[...], w[...])
    out = _layer_norm(ALPHA * x_ref[...] + y, g_ref[...], b_ref[...])
    o_ref[...] = out
    ob_ref[...] = out.astype(BF16)


def _proj_ln(lhs, ws, x, g, b, tm):
    m = x.shape[0]
    row = lambda width: pl.BlockSpec((tm, width), lambda i: (i, 0))
    full = lambda shape: pl.BlockSpec(shape, lambda i: (0, 0))
    return pl.pallas_call(
        functools.partial(_proj_ln_kernel, n_lhs=len(lhs)),
        grid=(m // tm,),
        in_specs=[row(a.shape[1]) for a in lhs] + [full(w.shape) for w in ws]
        + [row(D_MODEL), full(g.shape), full(b.shape)],
        out_specs=[row(D_MODEL), row(D_MODEL)],
        out_shape=[jax.ShapeDtypeStruct((m, D_MODEL), F32), jax.ShapeDtypeStruct((m, D_MODEL), BF16)],
        compiler_params=_params("parallel"),
        name="proj_ln",
    )(*lhs, *ws, x, g, b)


def _silu_mul(g, u):
    return g * (1.0 / (1.0 + jnp.exp(-g))) * u


def _ffn_up_kernel(x_ref, wg_ref, wu_ref, o_ref):
    x = x_ref[...]
    o_ref[...] = _silu_mul(_dot(x, wg_ref[...]), _dot(x, wu_ref[...])).astype(BF16)


def _ffn_up(xb, wg, wu, tm, tn):
    m = xb.shape[0]
    return pl.pallas_call(
        _ffn_up_kernel,
        grid=(D_FF // tn, m // tm),
        in_specs=[pl.BlockSpec((tm, D_MODEL), lambda j, i: (i, 0)),
                  pl.BlockSpec((D_MODEL, tn), lambda j, i: (0, j)),
                  pl.BlockSpec((D_MODEL, tn), lambda j, i: (0, j))],
        out_specs=pl.BlockSpec((tm, tn), lambda j, i: (i, j)),
        out_shape=jax.ShapeDtypeStruct((m, D_FF), BF16),
        compiler_params=_params("parallel", "parallel"),
        name="ffn_up",
    )(xb, wg, wu)


def _ffn_down_ln_kernel(h_ref, w_ref, x_ref, g_ref, b_ref, o_ref, ob_ref, acc_ref):
    k = pl.program_id(1)

    @pl.when(k == 0)
    def _():
        acc_ref[...] = jnp.zeros_like(acc_ref)

    acc_ref[...] += _dot(h_ref[...], w_ref[...])

    @pl.when(k == pl.num_programs(1) - 1)
    def _():
        out = _layer_norm(ALPHA * x_ref[...] + acc_ref[...], g_ref[...], b_ref[...])
        o_ref[...] = out
        ob_ref[...] = out.astype(BF16)


def _ffn_down_ln(h, wd, x, g, b, tm, tk):
    m = x.shape[0]
    row = pl.BlockSpec((tm, D_MODEL), lambda i, k: (i, 0))
    vec = pl.BlockSpec((1, D_MODEL), lambda i, k: (0, 0))
    return pl.pallas_call(
        _ffn_down_ln_kernel,
        grid=(m // tm, D_FF // tk),
        in_specs=[pl.BlockSpec((tm, tk), lambda i, k: (i, k)), pl.BlockSpec((tk, D_MODEL), lambda i, k: (k, 0)),
                  row, vec, vec],
        out_specs=[row, row],
        out_shape=[jax.ShapeDtypeStruct((m, D_MODEL), F32), jax.ShapeDtypeStruct((m, D_MODEL), BF16)],
        scratch_shapes=[pltpu.VMEM((tm, D_MODEL), F32)],
        compiler_params=_params("parallel", "arbitrary"),
        name="ffn_down_ln",
    )(h, wd, x, g, b)


def _qkv_kernel(x_ref, w_ref, o_ref, *, n_q_tiles):
    sc = jnp.where(pl.program_id(0) < n_q_tiles, HEAD_DIM ** -0.5, 1.0).astype(F32)
    o_ref[...] = (_dot(x_ref[...], w_ref[...]) * sc).astype(BF16)


def _qkv_proj(xb, w, tm, tn):
    m = xb.shape[0]
    n = w.shape[1]
    return pl.pallas_call(
        functools.partial(_qkv_kernel, n_q_tiles=FOX_WIDTH // tn),
        grid=(n // tn, m // tm),
        in_specs=[pl.BlockSpec((tm, D_MODEL), lambda j, i: (i, 0)), pl.BlockSpec((D_MODEL, tn), lambda j, i: (0, j))],
        out_specs=pl.BlockSpec((tm, tn), lambda j, i: (i, j)),
        out_shape=jax.ShapeDtypeStruct((m, n), BF16),
        compiler_params=_params("parallel", "parallel"),
        name="fox_qkv",
    )(xb, w)


def _fgate_kernel(x_ref, w_ref, b_ref, c_ref, carry_ref):
    @pl.when(pl.program_id(1) == 0)
    def _():
        carry_ref[...] = jnp.zeros_like(carry_ref)

    z = _dot(x_ref[...], w_ref[...]) + b_ref[...]
    log_f = jnp.minimum(z, 0.0) - jnp.log(1.0 + jnp.exp(-jnp.abs(z)))
    tm = z.shape[0]
    tri = (lax.broadcasted_iota(jnp.int32, (tm, tm), 1) <= lax.broadcasted_iota(jnp.int32, (tm, tm), 0)).astype(F32)
    c = jnp.dot(tri, log_f, preferred_element_type=F32, precision=lax.Precision.HIGHEST) + carry_ref[...]
    c_ref[...] = c
    carry_ref[...] = c[tm - 1:tm, :]


def _fgate(xb3, wf, bf, tm):
    b, s, _ = xb3.shape
    return pl.pallas_call(
        _fgate_kernel,
        grid=(b, s // tm),
        in_specs=[pl.BlockSpec((None, tm, D_MODEL), lambda bi, i: (bi, i, 0)),
                  pl.BlockSpec(wf.shape, lambda bi, i: (0, 0)), pl.BlockSpec(bf.shape, lambda bi, i: (0, 0))],
        out_specs=pl.BlockSpec((None, tm, LANES), lambda bi, i: (bi, i, 0)),
        out_shape=jax.ShapeDtypeStruct((b, s, LANES), F32),
        scratch_shapes=[pltpu.VMEM((1, LANES), F32)],
        compiler_params=_params("parallel", "arbitrary"),
        name="fox_gate",
    )(xb3, wf, bf)


def _router_kernel(x_ref, w_ref, b_ref, route_ref, cnt_ref, carry_ref):
    @pl.when(pl.program_id(0) == 0)
    def _():
        carry_ref[...] = jnp.zeros_like(carry_ref)

    logits = jnp.dot(x_ref[...], w_ref[...], preferred_element_type=F32, precision=lax.Precision.HIGHEST) + b_ref[...]
    tm = logits.shape[0]
    lane = lax.broadcasted_iota(jnp.int32, (tm, LANES), 1)
    l1 = jnp.max(logits, axis=-1, keepdims=True)
    i1 = jnp.min(jnp.where(logits == l1, lane, LANES), axis=-1, keepdims=True)
    rest = jnp.where(lane == i1, NEG, logits)
    l2 = jnp.max(rest, axis=-1, keepdims=True)
    i2 = jnp.min(jnp.where(rest == l2, lane, LANES), axis=-1, keepdims=True)
    e = jnp.exp(l2 - l1)
    w1 = 1.0 / (1.0 + e)
    w2 = e / (1.0 + e)
    hot1 = (lane == i1).astype(F32)
    hot2 = (lane == i2).astype(F32)
    cnt = hot1 + hot2
    strict = (lax.broadcasted_iota(jnp.int32, (tm, tm), 1) < lax.broadcasted_iota(jnp.int32, (tm, tm), 0)).astype(BF16)
    before = _dot(strict, cnt.astype(BF16)) + carry_ref[...]
    r1 = jnp.sum(before * hot1, axis=-1, keepdims=True)
    r2 = jnp.sum(before * hot2, axis=-1, keepdims=True)
    vals = (i1.astype(F32), i2.astype(F32), w1, w2, r1, r2)
    route = jnp.zeros((tm, LANES), F32)
    for idx, val in enumerate(vals):
        route = jnp.where(lane == idx, val, route)
    route_ref[...] = route
    total = carry_ref[...] + jnp.sum(cnt, axis=0, keepdims=True)
    carry_ref[...] = total
    cnt_ref[...] = jnp.broadcast_to(total, cnt_ref.shape)


def _router(x, rw, rb, tm):
    m = x.shape[0]
    return pl.pallas_call(
        _router_kernel,
        grid=(m // tm,),
        in_specs=[pl.BlockSpec((tm, D_MODEL), lambda i: (i, 0)), pl.BlockSpec(rw.shape, lambda i: (0, 0)),
                  pl.BlockSpec(rb.shape, lambda i: (0, 0))],
        out_specs=[pl.BlockSpec((tm, LANES), lambda i: (i, 0)), pl.BlockSpec((8, LANES), lambda i: (0, 0))],
        out_shape=[jax.ShapeDtypeStruct((m, LANES), F32), jax.ShapeDtypeStruct((8, LANES), F32)],
        scratch_shapes=[pltpu.VMEM((1, LANES), F32)],
        compiler_params=_params("arbitrary"),
        name="moe_router",
    )(x, rw, rb)


def _dispatch_kernel(pos_ref, x_ref, zeros_ref, xs_ref, sem):
    del zeros_ref
    tm = x_ref.shape[0]

    def copy(r, k):
        return pltpu.make_async_copy(x_ref.at[pl.ds(r, 1), :], xs_ref.at[pl.ds(pos_ref[0, 0, 2 * r + k], 1), :], sem)

    def start(r, _):
        copy(r, 0).start()
        copy(r, 1).start()
        return 0

    def wait(r, _):
        copy(r, 0).wait()
        copy(r, 1).wait()
        return 0

    lax.fori_loop(0, tm, start, 0)
    lax.fori_loop(0, tm, wait, 0)


def _dispatch(x, pos, n_rows, tm):
    m = x.shape[0]
    pos3 = pos.reshape(m // tm, 1, 2 * tm)
    return pl.pallas_call(
        _dispatch_kernel,
        grid=(m // tm,),
        in_specs=[pl.BlockSpec((1, 1, 2 * tm), lambda i: (i, 0, 0), memory_space=pltpu.SMEM),
                  pl.BlockSpec((tm, D_MODEL), lambda i: (i, 0)),
                  pl.BlockSpec(memory_space=pl.ANY)],
        out_specs=pl.BlockSpec(memory_space=pl.ANY),
        out_shape=jax.ShapeDtypeStruct((n_rows, D_MODEL), x.dtype),
        scratch_shapes=[pltpu.SemaphoreType.DMA(())],
        input_output_aliases={2: 0},
        compiler_params=_params("arbitrary"),
        name="moe_dispatch",
    )(pos3, x, jnp.zeros((n_rows, D_MODEL), x.dtype))


def _moe_up_kernel(te_ref, nu_ref, x_ref, wg_ref, wu_ref, o_ref):
    del te_ref

    live = pl.program_id(1) < nu_ref[0]

    @pl.when(live)
    def _():
        x = x_ref[...].astype(BF16)
        o_ref[...] = _silu_mul(_dot(x, wg_ref[...]), _dot(x, wu_ref[...])).astype(BF16)

    @pl.when(jnp.logical_not(live))
    def _():
        o_ref[...] = jnp.zeros_like(o_ref)


def _moe_up(xs, wg, wu, tile_expert, n_used, tm, tn):
    p = xs.shape[0]
    live = lambda t, nu: jnp.minimum(t, nu[0] - 1)
    return pl.pallas_call(
        _moe_up_kernel,
        grid_spec=pltpu.PrefetchScalarGridSpec(
            num_scalar_prefetch=2,
            grid=(D_FF // tn, p // tm),
            in_specs=[pl.BlockSpec((tm, D_MODEL), lambda j, t, te, nu: (live(t, nu), 0)),
                      pl.BlockSpec((None, D_MODEL, tn), lambda j, t, te, nu: (te[live(t, nu)], 0, j)),
                      pl.BlockSpec((None, D_MODEL, tn), lambda j, t, te, nu: (te[live(t, nu)], 0, j))],
            out_specs=pl.BlockSpec((tm, tn), lambda j, t, te, nu: (t, j)),
        ),
        out_shape=jax.ShapeDtypeStruct((p, D_FF), BF16),
        compiler_params=_params("arbitrary", "arbitrary"),
        name="moe_up",
    )(tile_expert, n_used, xs, wg, wu)


def _moe_down_kernel(te_ref, nu_ref, h_ref, w_ref, o_ref, acc_ref):
    del te_ref
    k = pl.program_id(1)

    live = pl.program_id(0) < nu_ref[0]

    @pl.when(k == 0)
    def _():
        acc_ref[...] = jnp.zeros_like(acc_ref)

    @pl.when(live)
    def _():
        acc_ref[...] += _dot(h_ref[...], w_ref[...])

    @pl.when(k == pl.num_programs(1) - 1)
    def _():
        o_ref[...] = acc_ref[...]


def _moe_down(hs, wd, tile_expert, n_used, tm, tk):
    p = hs.shape[0]
    live = lambda t, nu: jnp.minimum(t, nu[0] - 1)
    return pl.pallas_call(
        _moe_down_kernel,
        grid_spec=pltpu.PrefetchScalarGridSpec(
            num_scalar_prefetch=2,
            grid=(p // tm, D_FF // tk),
            in_specs=[pl.BlockSpec((tm, tk), lambda t, k, te, nu: (live(t, nu), k)),
                      pl.BlockSpec((None, tk, D_MODEL), lambda t, k, te, nu: (te[live(t, nu)], k, 0))],
            out_specs=pl.BlockSpec((tm, D_MODEL), lambda t, k, te, nu: (t, 0)),
            scratch_shapes=[pltpu.VMEM((tm, D_MODEL), F32)],
        ),
        out_shape=jax.ShapeDtypeStruct((p, D_MODEL), F32),
        compiler_params=_params("arbitrary", "arbitrary"),
        name="moe_down",
    )(tile_expert, n_used, hs, wd)


def _combine_ln_kernel(pos_ref, ys_ref, x_ref, route_ref, g_ref, b_ref, o_ref, buf0, buf1, sem):
    tm = x_ref.shape[0]

    def copies(r):
        c0 = pltpu.make_async_copy(ys_ref.at[pl.ds(pos_ref[0, 0, 2 * r], 1), :], buf0.at[pl.ds(r, 1), :], sem)
        c1 = pltpu.make_async_copy(ys_ref.at[pl.ds(pos_ref[0, 0, 2 * r + 1], 1), :], buf1.at[pl.ds(r, 1), :], sem)
        return c0, c1

    def start(r, _):
        c0, c1 = copies(r)
        c0.start()
        c1.start()
        return 0

    def wait(r, _):
        c0, c1 = copies(r)
        c0.wait()
        c1.wait()
        return 0

    lax.fori_loop(0, tm, start, 0)
    lax.fori_loop(0, tm, wait, 0)
    route = route_ref[...]
    y = route[:, 2:3] * buf0[...] + route[:, 3:4] * buf1[...]
    o_ref[...] = _layer_norm(ALPHA * x_ref[...] + y, g_ref[...], b_ref[...])


def _combine_ln(ys, pos, x, route, g, b, tm):
    m = x.shape[0]
    pos3 = pos.reshape(m // tm, 1, 2 * tm)
    vec = pl.BlockSpec((1, D_MODEL), lambda i: (0, 0))
    return pl.pallas_call(
        _combine_ln_kernel,
        grid=(m // tm,),
        in_specs=[pl.BlockSpec((1, 1, 2 * tm), lambda i: (i, 0, 0), memory_space=pltpu.SMEM),
                  pl.BlockSpec(memory_space=pl.ANY),
                  pl.BlockSpec((tm, D_MODEL), lambda i: (i, 0)),
                  pl.BlockSpec((tm, LANES), lambda i: (i, 0)), vec, vec],
        out_specs=pl.BlockSpec((tm, D_MODEL), lambda i: (i, 0)),
        out_shape=jax.ShapeDtypeStruct((m, D_MODEL), F32),
        scratch_shapes=[pltpu.VMEM((tm, D_MODEL), F32), pltpu.VMEM((tm, D_MODEL), F32), pltpu.SemaphoreType.DMA(())],
        compiler_params=_params("arbitrary"),
        name="moe_combine_ln",
    )(pos3, ys, x, route, g, b)


def _rope_tables(seq):
    def angles(dim):
        inv_freq = 1.0 / (ROPE_THETA ** (jnp.arange(0, dim, 2, dtype=F32) / dim))
        ang = jnp.arange(seq, dtype=F32)[:, None] * inv_freq[None, :]
        return jnp.cos(ang), jnp.sin(ang)

    c128, s128 = angles(HEAD_DIM)
    cos_f = jnp.concatenate([c128, c128], axis=-1)
    sin_f = jnp.concatenate([-s128, s128], axis=-1)
    c64, s64 = angles(MLA_ROPE_DIM)
    z32 = jnp.zeros_like(c64)
    cos_t = jnp.concatenate([c64, c64, z32, z32], axis=-1)
    sin_a = jnp.concatenate([-s64, z32, z32, z32], axis=-1)
    sin_b = jnp.concatenate([z32, s64, z32, z32], axis=-1)
    return cos_f, sin_f, cos_t, sin_a, sin_b


def _pad_cols(a, width):
    return jnp.pad(a, ((0, 0), (0, width - a.shape[1])))


def _row(v):
    return v.reshape(1, -1).astype(F32)


def _even_layer(x, xb, batch, seq, tables, w_in, q_norm, w_q_b, kv_norm, w_kv_b, w_out, ln1_g, ln1_b,
                w_gate, w_up, w_down, ln2_g, ln2_b):
    cos_f, sin_f, cos_t, sin_a, sin_b = tables
    tm = min(512, seq)
    wa = jnp.concatenate([w_in[:, OFF_CKV:OFF_KROPE], _pad_cols(w_in[:, OFF_KROPE:OFF_DQ], LANES),
                          _pad_cols(w_in[:, OFF_CQ:OFF_CKV], MLA_Q_RANK_PAD)], axis=1).astype(BF16)
    qg = _pad_cols(_row(q_norm), MLA_Q_RANK_PAD)
    wq = jnp.pad(w_q_b.reshape(MLA_Q_RANK, MLA_HEADS, MLA_QK_DIM),
                 ((0, MLA_Q_RANK_PAD - MLA_Q_RANK), (0, 0), (0, MLA_QK_PAD - MLA_QK_DIM)))
    wq = wq.reshape(MLA_Q_RANK_PAD, MLA_HEADS * MLA_QK_PAD).astype(BF16)
    wkv3 = w_kv_b.reshape(MLA_KV_RANK, MLA_HEADS, MLA_NOPE_DIM + MLA_V_DIM)
    wkv = jnp.concatenate([wkv3[:, :, :MLA_NOPE_DIM].reshape(MLA_KV_RANK, -1),
                           wkv3[:, :, MLA_NOPE_DIM:].reshape(MLA_KV_RANK, -1)], axis=1).astype(BF16)
    q_mla, k_mla, v_mla = _mla_prep(xb, wa, _row(kv_norm), qg, wq, wkv, cos_t, sin_a, sin_b, seq, tm)
    tq = min(512, seq)
    o_mla = _flash(q_mla.reshape(batch, seq, -1), k_mla.reshape(batch, seq, -1), v_mla.reshape(batch, seq, -1), None,
                   heads=MLA_HEADS, dk=MLA_QK_PAD, dv=MLA_V_DIM, q_off=0, k_off=0, v_off=0, tq=tq, tk=tq)
    o_mla = o_mla.reshape(batch * seq, -1)

    dqkv = _dqkv(xb, w_in[:, OFF_DQ:].astype(BF16), cos_f, sin_f, seq, tm)
    outs, lses = [], []
    for g, (window, dil) in enumerate(DIL_PATTERNS):
        assert window == DIL_SPAN * dil
        o_g, lse_g = _dilated_group(dqkv, g, dil, batch, seq, min(256, seq // dil))
        outs.append(o_g)
        lses.append(lse_g)
    o_dil = _dil_merge(outs, lses, tm)

    n_mla = MLA_HEADS * MLA_V_DIM
    wo = w_out.astype(BF16)
    x1, x1b = _proj_ln([o_mla, o_dil], [wo[:n_mla], wo[n_mla:]], x, _row(ln1_g), _row(ln1_b), tm)
    hmid = _ffn_up(x1b, w_gate.astype(BF16), w_up.astype(BF16), tm, 1408)
    return _ffn_down_ln(hmid, w_down.astype(BF16), x1, _row(ln2_g), _row(ln2_b), tm, 1408)


def _odd_layer(x, xb, batch, seq, w_qkv, w_f, b_f, w_out, ln1_g, ln1_b, router_w, router_b,
               exp_w_gate, exp_w_up, exp_w_down, ln2_g, ln2_b):
    m = batch * seq
    tm = min(512, seq)
    qkv = _qkv_proj(xb, w_qkv.astype(BF16), tm, 1024)
    c = _fgate(xb.reshape(batch, seq, D_MODEL), _pad_cols(w_f, LANES).astype(BF16), _pad_cols(_row(b_f), LANES),
               min(256, seq))
    c_t = jnp.transpose(c[:, :, :FOX_HEADS], (0, 2, 1)).reshape(batch, FOX_HEADS, 1, seq)
    qkv3 = qkv.reshape(batch, seq, -1)
    tq = min(512, seq)
    o = _flash(qkv3, qkv3, qkv3, c_t, heads=FOX_HEADS, dk=HEAD_DIM, dv=HEAD_DIM,
               q_off=0, k_off=FOX_HEADS, v_off=2 * FOX_HEADS, tq=tq, tk=tq)
    x1, x1b = _proj_ln([o.reshape(m, -1)], [w_out.astype(BF16)], x, _row(ln1_g), _row(ln1_b), tm)

    rb = jnp.full((1, LANES), NEG, F32).at[0, :N_EXPERTS].set(router_b.astype(F32))
    route, counts = _router(x1, _pad_cols(router_w, LANES), rb, min(256, seq))
    tile = 512
    n_tiles = (2 * m) // tile + N_EXPERTS
    cnt = counts[0, :N_EXPERTS].astype(jnp.int32)
    tiles_per = (cnt + tile - 1) // tile
    tile_end = jnp.cumsum(tiles_per)
    offset = (tile_end - tiles_per) * tile
    idx = route[:, 0:2].astype(jnp.int32)
    pos = (offset[idx] + route[:, 4:6].astype(jnp.int32)).reshape(-1)
    n_used = tile_end[-1:]
    tile_expert = jnp.minimum(jnp.searchsorted(tile_end, jnp.arange(n_tiles, dtype=jnp.int32), side="right"),
                              N_EXPERTS - 1).astype(jnp.int32)
    td = min(256, seq)
    xs = _dispatch(x1, pos, n_tiles * tile, td)
    hs = _moe_up(xs, exp_w_gate.astype(BF16), exp_w_up.astype(BF16), tile_expert, n_used, tile, 1408)
    ys = _moe_down(hs, exp_w_down.astype(BF16), tile_expert, n_used, tile, 1408)
    return _combine_ln(ys, pos, x1, route, _row(ln2_g), _row(ln2_b), td)


def kernel(x, ev_w_in, ev_q_norm, ev_w_q_b, ev_kv_norm, ev_w_kv_b, ev_w_out, ev_ln1_g, ev_ln1_b, ev_ffn_w_gate, ev_ffn_w_up, ev_ffn_w_down, ev_ln2_g, ev_ln2_b, od_w_qkv, od_w_f, od_b_f, od_w_out, od_ln1_g, od_ln1_b, od_router_w, od_router_b, od_exp_w_gate, od_exp_w_up, od_exp_w_down, od_ln2_g, od_ln2_b):
    batch, seq, _ = x.shape
    tables = _rope_tables(seq)
    h = x.reshape(batch * seq, D_MODEL)
    hb = h.astype(BF16)
    for layer in range(DEPTH):
        i = layer // 2
        if layer % 2 == 0:
            h, hb = _even_layer(h, hb, batch, seq, tables, ev_w_in[i], ev_q_norm[i], ev_w_q_b[i], ev_kv_norm[i],
                                ev_w_kv_b[i], ev_w_out[i], ev_ln1_g[i], ev_ln1_b[i], ev_ffn_w_gate[i],
                                ev_ffn_w_up[i], ev_ffn_w_down[i], ev_ln2_g[i], ev_ln2_b[i])
        else:
            h = _odd_layer(h, hb, batch, seq, od_w_qkv[i], od_w_f[i], od_b_f[i], od_w_out[i], od_ln1_g[i],
                           od_ln1_b[i], od_router_w[i], od_router_b[i], od_exp_w_gate[i], od_exp_w_up[i],
                           od_exp_w_down[i], od_ln2_g[i], od_ln2_b[i])
            hb = h.astype(BF16)
    return h.reshape(batch, seq, D_MODEL)
```

```python
import functools

import jax
import jax.numpy as jnp
from jax import lax
from jax.experimental import pallas as pl
from jax.experimental.pallas import tpu as pltpu

F32 = jnp.float32
BF16 = jnp.bfloat16

D_MODEL = 2048
HEAD_DIM = 128
LANES = 128
ROPE_THETA = 10000.0
LN_EPS = 1e-5
RMS_EPS = 1e-6

MLA_HEADS = 10
MLA_Q_RANK = 448
MLA_Q_RANK_PAD = 512
MLA_KV_RANK = 128
MLA_NOPE_DIM = 128
MLA_ROPE_DIM = 64
MLA_V_DIM = 128
MLA_QK_DIM = MLA_NOPE_DIM + MLA_ROPE_DIM
MLA_QK_PAD = 256

DIL_PATTERNS = ((128, 1), (512, 4), (2048, 16))
DIL_GROUPS = 3
DIL_HEADS = 6
DIL_WIDTH = DIL_HEADS * HEAD_DIM
DIL_SPAN = 128

OFF_CQ = 0
OFF_CKV = OFF_CQ + MLA_Q_RANK
OFF_KROPE = OFF_CKV + MLA_KV_RANK
OFF_DQ = OFF_KROPE + MLA_ROPE_DIM
OFF_DK = OFF_DQ + DIL_GROUPS * DIL_WIDTH
OFF_DV = OFF_DK + DIL_WIDTH
W_IN_COLS = OFF_DV + DIL_WIDTH

FOX_HEADS = 16
FOX_WIDTH = FOX_HEADS * HEAD_DIM

D_FF = 5632
N_EXPERTS = 8
DEPTH = 2
ALPHA = (2.0 * DEPTH) ** 0.25

NEG = -1e30
LOG2E = 1.4426950408889634
FLASH_TQ = 512
FLASH_TK = 512
VMEM_LIMIT = 56 * 1024 * 1024


def _params(*sem, vmem=VMEM_LIMIT):
    return pltpu.CompilerParams(dimension_semantics=sem, vmem_limit_bytes=vmem)


def _dot(a, b):
    return jnp.dot(a, b, preferred_element_type=F32)


def _dot_nt(a, b):
    return lax.dot_general(a, b, (((1,), (1,)), ((), ())), preferred_element_type=F32)


def _layer_norm(y, g, b):
    mu = jnp.mean(y, axis=-1, keepdims=True)
    d = y - mu
    var = jnp.mean(d * d, axis=-1, keepdims=True)
    return d * lax.rsqrt(var + LN_EPS) * g + b


def _rope128(x, cos_f, sin_f):
    return x * cos_f + pltpu.roll(x, 64, 1) * sin_f


def _rope64(x, cos_t, sin_a, sin_b):
    return x * cos_t + pltpu.roll(x, 96, 1) * sin_a + pltpu.roll(x, 32, 1) * sin_b


def _mla_prep_kernel(x_ref, wa_ref, kvg_ref, qg_ref, wqt_ref, wk_ref, wvt_ref, cos_ref, sa_ref, sb_ref,
                     ct_ref, st_ref, qt_ref, k_ref, vt_ref):
    h = _dot(x_ref[...], wa_ref[...])
    ckv = h[:, 0:128]
    kr = h[:, 128:256]
    cq = h[:, 256:768]
    ckv_n = (ckv * lax.rsqrt(jnp.mean(ckv * ckv, axis=-1, keepdims=True) + RMS_EPS) * kvg_ref[...]).astype(BF16)
    cq_ms = jnp.sum(cq * cq, axis=-1, keepdims=True) * (1.0 / MLA_Q_RANK)
    cq_n = (cq * lax.rsqrt(cq_ms + RMS_EPS) * qg_ref[...]).astype(BF16)
    kr_r = _rope64(kr, cos_ref[...], sa_ref[...], sb_ref[...]).astype(BF16)
    scale = MLA_QK_DIM ** -0.5 * LOG2E
    qt = _dot_nt(wqt_ref[...], cq_n)
    c, s = ct_ref[...], st_ref[...]
    k_nope = _dot(ckv_n, wk_ref[...])
    for hd in range(MLA_HEADS):
        o = hd * MLA_QK_PAD
        qt_ref[o:o + 128, :] = (qt[o:o + 128] * scale).astype(BF16)
        x1, x2 = qt[o + 128:o + 160], qt[o + 160:o + 192]
        qt_ref[o + 128:o + 160, :] = ((x1 * c - x2 * s) * scale).astype(BF16)
        qt_ref[o + 160:o + 192, :] = ((x2 * c + x1 * s) * scale).astype(BF16)
        qt_ref[o + 192:o + 256, :] = jnp.zeros((64, qt.shape[1]), BF16)
        k_ref[:, o:o + 128] = k_nope[:, hd * 128:(hd + 1) * 128].astype(BF16)
        k_ref[:, o + 128:o + 256] = kr_r
    vt_ref[...] = _dot_nt(wvt_ref[...], ckv_n).astype(BF16)


def _mla_prep(xb3, wa, kvg, qg, wqt, wk, wvt, cos_t, sin_a, sin_b, cos_tt, sin_tt, tm):
    b, s, _ = xb3.shape
    full = lambda shape: pl.BlockSpec(shape, lambda bi, i: (0, 0))
    tab = pl.BlockSpec((tm, LANES), lambda bi, i: (i, 0))
    tab_t = pl.BlockSpec((MLA_ROPE_DIM // 2, tm), lambda bi, i: (0, i))
    wide = MLA_HEADS * MLA_QK_PAD
    vw = MLA_HEADS * MLA_V_DIM
    return pl.pallas_call(
        _mla_prep_kernel,
        grid=(b, s // tm),
        in_specs=[pl.BlockSpec((None, tm, D_MODEL), lambda bi, i: (bi, i, 0)), full(wa.shape), full(kvg.shape),
                  full(qg.shape), full(wqt.shape), full(wk.shape), full(wvt.shape), tab, tab, tab, tab_t, tab_t],
        out_specs=[pl.BlockSpec((None, wide, tm), lambda bi, i: (bi, 0, i)),
                   pl.BlockSpec((None, tm, wide), lambda bi, i: (bi, i, 0)),
                   pl.BlockSpec((None, vw, tm), lambda bi, i: (bi, 0, i))],
        out_shape=[jax.ShapeDtypeStruct((b, wide, s), BF16), jax.ShapeDtypeStruct((b, s, wide), BF16),
                   jax.ShapeDtypeStruct((b, vw, s), BF16)],
        compiler_params=_params("parallel", "parallel"),
        name="mla_prep",
    )(xb3, wa, kvg, qg, wqt, wk, wvt, cos_t, sin_a, sin_b, cos_tt, sin_tt)


def _dqkv_kernel(x_ref, w_ref, cos_ref, sin_ref, o_ref):
    j = pl.program_id(0)
    h = _dot(x_ref[...], w_ref[...])

    @pl.when(j < 4)
    def _():
        cos_f, sin_f = cos_ref[...], sin_ref[...]
        sc = jnp.where(j < 3, HEAD_DIM ** -0.5, 1.0).astype(F32)
        for hd in range(DIL_HEADS):
            sl = slice(hd * 128, (hd + 1) * 128)
            o_ref[:, sl] = (_rope128(h[:, sl], cos_f, sin_f) * sc).astype(BF16)

    @pl.when(j == 4)
    def _():
        o_ref[...] = h.astype(BF16)


def _dqkv(xb, wd, cos_f, sin_f, seq, tm):
    m = xb.shape[0]
    nrow = seq // tm
    n_col = wd.shape[1] // DIL_WIDTH
    tab = pl.BlockSpec((tm, LANES), lambda j, i: (i % nrow, 0))
    return pl.pallas_call(
        _dqkv_kernel,
        grid=(n_col, m // tm),
        in_specs=[pl.BlockSpec((tm, D_MODEL), lambda j, i: (i, 0)),
                  pl.BlockSpec((D_MODEL, DIL_WIDTH), lambda j, i: (0, j)), tab, tab],
        out_specs=pl.BlockSpec((tm, DIL_WIDTH), lambda j, i: (i, j)),
        out_shape=jax.ShapeDtypeStruct((m, wd.shape[1]), BF16),
        compiler_params=_params("parallel", "parallel"),
        name="dil_qkv",
    )(xb, wd, cos_f, sin_f)


def _flash_kernel(*refs, tq, tk, use_c):
    assert tq == tk
    if use_c:
        qt_ref, k_ref, vt_ref, c_ref, o_ref, acc_ref, s0_ref, s1_ref, kaug_ref = refs
    else:
        qt_ref, k_ref, vt_ref, o_ref, acc_ref, s0_ref, s1_ref = refs
    i = pl.program_id(2)
    seq = k_ref.shape[0]
    if use_c:
        @pl.when(i == 0)
        def _():
            row = lax.broadcasted_iota(jnp.int32, (LANES, tk), 0)

            def build(j, _):
                start = pl.multiple_of(j * tk, tk)
                neg = -LOG2E * c_ref[:, pl.ds(start, tk)]
                hi = neg.astype(BF16).astype(F32)
                mid = (neg - hi).astype(BF16).astype(F32)
                lo = neg - hi - mid
                blk = jnp.where(row == 0, hi, jnp.where(row == 1, mid, jnp.where(row == 2, lo, 0.0)))
                kaug_ref[pl.ds(start, tk), :] = blk.T.astype(BF16)
                return 0

            lax.fori_loop(0, seq // tk, build, 0)

    q = qt_ref[...]
    if use_c:
        ones = (lax.broadcasted_iota(jnp.int32, (LANES, tq), 0) < 3).astype(BF16)
        q = jnp.concatenate([q, ones], axis=0)
    acc_ref[...] = jnp.zeros_like(acc_ref)

    def scores(j, s_ref):
        start = pl.multiple_of(j * tk, tk)
        kt = k_ref[pl.ds(start, tk), :]
        if use_c:
            kt = jnp.concatenate([kt, kaug_ref[pl.ds(start, tk), :]], axis=1)
        s_ref[...] = _dot(kt, q)

    def update(j, s_ref, stats, masked):
        m, l = stats
        s = s_ref[...]
        if masked:
            on_diag = lax.broadcasted_iota(jnp.int32, (tk, tq), 0) <= lax.broadcasted_iota(jnp.int32, (tk, tq), 1)
            s = jnp.where(on_diag, s, NEG)
        m_new = jnp.maximum(m, jnp.max(s, axis=0, keepdims=True))
        a = jnp.exp2(m - m_new)
        p = jnp.exp2(s - m_new)
        l = a * l + jnp.sum(p, axis=0, keepdims=True)
        start = pl.multiple_of(j * tk, tk)
        acc_ref[...] = a * acc_ref[...] + _dot(vt_ref[:, pl.ds(start, tk)], p.astype(BF16))
        return m_new, l

    scores(0, s0_ref)

    def pair(jj, stats):
        scores(2 * jj + 1, s1_ref)
        stats = update(2 * jj, s0_ref, stats, False)
        scores(2 * jj + 2, s0_ref)
        return update(2 * jj + 1, s1_ref, stats, False)

    init = (jnp.full((1, tq), NEG, F32), jnp.zeros((1, tq), F32))
    stats = lax.fori_loop(0, i // 2, pair, init)

    def odd_tail(stats):
        scores(i, s1_ref)
        stats = update(i - 1, s0_ref, stats, False)
        return update(i, s1_ref, stats, True)

    def even_tail(stats):
        return update(i, s0_ref, stats, True)

    _, l = lax.cond(i % 2 == 1, odd_tail, even_tail, stats)
    o_ref[...] = (acc_ref[...] / l).T.astype(o_ref.dtype)


def _flash(qt, k, vt, c, *, heads, dq, dk, dv, tq, tk):
    b, s, _ = k.shape
    use_c = c is not None
    in_specs = [pl.BlockSpec((None, dq, tq), lambda bi, h, i: (bi, h, i)),
                pl.BlockSpec((None, s, dk), lambda bi, h, i: (bi, 0, h)),
                pl.BlockSpec((None, dv, s), lambda bi, h, i: (bi, h, 0))]
    args = [qt, k, vt]
    scratch = [pltpu.VMEM((dv, tq), F32), pltpu.VMEM((tk, tq), F32), pltpu.VMEM((tk, tq), F32)]
    if use_c:
        in_specs.append(pl.BlockSpec((None, None, 1, s), lambda bi, h, i: (bi, h, 0, 0)))
        args.append(c)
        scratch.append(pltpu.VMEM((s, LANES), BF16))
    return pl.pallas_call(
        functools.partial(_flash_kernel, tq=tq, tk=tk, use_c=use_c),
        grid=(b, heads, s // tq),
        in_specs=in_specs,
        out_specs=pl.BlockSpec((None, tq, dv), lambda bi, h, i: (bi, i, h)),
        out_shape=jax.ShapeDtypeStruct((b, s, heads * dv), BF16),
        scratch_shapes=scratch,
        compiler_params=_params("parallel", "parallel", "arbitrary"),
        name="flash_fox" if use_c else "flash_mla",
    )(*args)


def _dilated_kernel(q_ref, kc_ref, kp_ref, vc_ref, vp_ref, o_ref, lse_ref, *, tu):
    i = pl.program_id(2)
    nk = DIL_SPAN + tu
    a = lax.broadcasted_iota(jnp.int32, (tu, nk), 0)
    c = lax.broadcasted_iota(jnp.int32, (tu, nk), 1)
    back = a + DIL_SPAN - c
    first = jnp.where(i > 0, 0, DIL_SPAN)
    bias = jnp.where(back >= 0, jnp.where(back <= DIL_SPAN, jnp.where(c >= first, 0.0, NEG), NEG), NEG)
    lane = lax.broadcasted_iota(jnp.int32, (tu, LANES), 1)
    lse_all = jnp.zeros((tu, LANES), F32)
    for hd in range(DIL_HEADS):
        sl = slice(hd * 128, (hd + 1) * 128)
        k = jnp.concatenate([kp_ref[tu - DIL_SPAN:, sl], kc_ref[:, sl]], axis=0)
        v = jnp.concatenate([vp_ref[tu - DIL_SPAN:, sl], vc_ref[:, sl]], axis=0)
        s = _dot_nt(q_ref[:, sl], k) + bias
        m = jnp.max(s, axis=-1, keepdims=True)
        p = jnp.exp(s - m)
        l = jnp.sum(p, axis=-1, keepdims=True)
        o_ref[:, sl] = _dot(p.astype(BF16), v) / l
        lse_all = jnp.where(lane == hd, m + jnp.log(l), lse_all)
    lse_ref[...] = lse_all


def _dilated_group(dqkv, g, dil, batch, seq, tu):
    su = seq // dil
    nblk = dqkv.shape[1] // DIL_WIDTH
    view = dqkv.reshape(batch, su, dil * dqkv.shape[1])
    cur = lambda blk: pl.BlockSpec((None, tu, DIL_WIDTH), lambda b, r, i: (b, i, r * nblk + blk))
    prev = lambda blk: pl.BlockSpec((None, tu, DIL_WIDTH), lambda b, r, i: (b, jnp.maximum(i - 1, 0), r * nblk + blk))
    o, lse = pl.pallas_call(
        functools.partial(_dilated_kernel, tu=tu),
        grid=(batch, dil, su // tu),
        in_specs=[cur(g), cur(3), prev(3), cur(4), prev(4)],
        out_specs=[pl.BlockSpec((None, tu, DIL_WIDTH), lambda b, r, i: (b, i, r)),
                   pl.BlockSpec((None, tu, LANES), lambda b, r, i: (b, i, r))],
        out_shape=[jax.ShapeDtypeStruct((batch, su, dil * DIL_WIDTH), F32),
                   jax.ShapeDtypeStruct((batch, su, dil * LANES), F32)],
        compiler_params=_params("parallel", "parallel", "arbitrary"),
        name=f"dilated_{dil}",
    )(view, view, view, view, view)
    return o.reshape(batch * seq, DIL_WIDTH), lse.reshape(batch * seq, LANES)


def _dil_merge_kernel(o0, o1, o2, l0, l1, l2, out_ref):
    ls = [l0[...], l1[...], l2[...]]
    mx = jnp.maximum(jnp.maximum(ls[0], ls[1]), ls[2])
    es = [jnp.exp(x - mx) for x in ls]
    den = es[0] + es[1] + es[2]
    ws = [e / den for e in es]
    os_ = [o0, o1, o2]
    for hd in range(DIL_HEADS):
        sl = slice(hd * 128, (hd + 1) * 128)
        acc = ws[0][:, hd:hd + 1] * os_[0][:, sl]
        for g in (1, 2):
            acc = acc + ws[g][:, hd:hd + 1] * os_[g][:, sl]
        out_ref[:, sl] = acc.astype(BF16)


def _dil_merge(outs, lses, tm):
    m = outs[0].shape[0]
    ob = pl.BlockSpec((tm, DIL_WIDTH), lambda i: (i, 0))
    lb = pl.BlockSpec((tm, LANES), lambda i: (i, 0))
    return pl.pallas_call(
        _dil_merge_kernel,
        grid=(m // tm,),
        in_specs=[ob, ob, ob, lb, lb, lb],
        out_specs=ob,
        out_shape=jax.ShapeDtypeStruct((m, DIL_WIDTH), BF16),
        compiler_params=_params("parallel"),
        name="dil_merge",
    )(*outs, *lses)


def _proj_ln_kernel(*refs, n_lhs):
    lhs = refs[:n_lhs]
    ws = refs[n_lhs:2 * n_lhs]
    x_ref, g_ref, b_ref, o_ref, ob_ref = refs[2 * n_lhs:]
    y = _dot(lhs[0][...], ws[0][...])
    for a, w in zip(lhs[1:], ws[1:]):
        y = y + _dot(a[...], w[...])
    out = _layer_norm(ALPHA * x_ref[...] + y, g_ref[...], b_ref[...])
    o_ref[...] = out
    ob_ref[...] = out.astype(BF16)


def _proj_ln(lhs, ws, x, g, b, tm):
    m = x.shape[0]
    row = lambda width: pl.BlockSpec((tm, width), lambda i: (i, 0))
    full = lambda shape: pl.BlockSpec(shape, lambda i: (0, 0))
    return pl.pallas_call(
        functools.partial(_proj_ln_kernel, n_lhs=len(lhs)),
        grid=(m // tm,),
        in_specs=[row(a.shape[1]) for a in lhs] + [full(w.shape) for w in ws]
        + [row(D_MODEL), full(g.shape), full(b.shape)],
        out_specs=[row(D_MODEL), row(D_MODEL)],
        out_shape=[jax.ShapeDtypeStruct((m, D_MODEL), F32), jax.ShapeDtypeStruct((m, D_MODEL), BF16)],
        compiler_params=_params("parallel"),
        name="proj_ln",
    )(*lhs, *ws, x, g, b)


def _silu_mul(g, u):
    return g * (1.0 / (1.0 + jnp.exp(-g))) * u


def _ffn_up_kernel(x_ref, wg_ref, wu_ref, o_ref, wgb_ref, wub_ref):
    @pl.when(pl.program_id(1) == 0)
    def _():
        wgb_ref[...] = wg_ref[...].astype(BF16)
        wub_ref[...] = wu_ref[...].astype(BF16)

    x = x_ref[...]
    o_ref[...] = _silu_mul(_dot(x, wgb_ref[...]), _dot(x, wub_ref[...])).astype(BF16)


def _ffn_up(xb, wg, wu, tm, tn):
    m = xb.shape[0]
    return pl.pallas_call(
        _ffn_up_kernel,
        grid=(D_FF // tn, m // tm),
        in_specs=[pl.BlockSpec((tm, D_MODEL), lambda j, i: (i, 0)),
                  pl.BlockSpec((D_MODEL, tn), lambda j, i: (0, j)),
                  pl.BlockSpec((D_MODEL, tn), lambda j, i: (0, j))],
        out_specs=pl.BlockSpec((tm, tn), lambda j, i: (i, j)),
        out_shape=jax.ShapeDtypeStruct((m, D_FF), BF16),
        scratch_shapes=[pltpu.VMEM((D_MODEL, tn), BF16), pltpu.VMEM((D_MODEL, tn), BF16)],
        compiler_params=_params("arbitrary", "arbitrary"),
        name="ffn_up",
    )(xb, wg, wu)


def _ffn_down_ln_kernel(h_ref, w_ref, x_ref, g_ref, b_ref, o_ref, ob_ref, acc_ref):
    k = pl.program_id(1)

    @pl.when(k == 0)
    def _():
        acc_ref[...] = jnp.zeros_like(acc_ref)

    acc_ref[...] += _dot(h_ref[...], w_ref[...])

    @pl.when(k == pl.num_programs(1) - 1)
    def _():
        out = _layer_norm(ALPHA * x_ref[...] + acc_ref[...], g_ref[...], b_ref[...])
        o_ref[...] = out
        ob_ref[...] = out.astype(BF16)


def _ffn_down_ln(h, wd, x, g, b, tm, tk):
    m = x.shape[0]
    row = pl.BlockSpec((tm, D_MODEL), lambda i, k: (i, 0))
    vec = pl.BlockSpec((1, D_MODEL), lambda i, k: (0, 0))
    return pl.pallas_call(
        _ffn_down_ln_kernel,
        grid=(m // tm, D_FF // tk),
        in_specs=[pl.BlockSpec((tm, tk), lambda i, k: (i, k)), pl.BlockSpec((tk, D_MODEL), lambda i, k: (k, 0)),
                  row, vec, vec],
        out_specs=[row, row],
        out_shape=[jax.ShapeDtypeStruct((m, D_MODEL), F32), jax.ShapeDtypeStruct((m, D_MODEL), BF16)],
        scratch_shapes=[pltpu.VMEM((tm, D_MODEL), F32)],
        compiler_params=_params("parallel", "arbitrary"),
        name="ffn_down_ln",
    )(h, wd, x, g, b)


def _proj_kernel(x_ref, w_ref, o_ref):
    o_ref[...] = _dot(x_ref[...], w_ref[...]).astype(BF16)


def _proj(xb, w, tm, tn):
    m = xb.shape[0]
    n = w.shape[1]
    return pl.pallas_call(
        _proj_kernel,
        grid=(n // tn, m // tm),
        in_specs=[pl.BlockSpec((tm, D_MODEL), lambda j, i: (i, 0)), pl.BlockSpec((D_MODEL, tn), lambda j, i: (0, j))],
        out_specs=pl.BlockSpec((tm, tn), lambda j, i: (i, j)),
        out_shape=jax.ShapeDtypeStruct((m, n), BF16),
        compiler_params=_params("parallel", "parallel"),
        name="proj",
    )(xb, w)


def _proj_t_kernel(w_ref, x_ref, o_ref, *, scale):
    o_ref[...] = (_dot_nt(w_ref[...], x_ref[...]) * scale).astype(BF16)


def _proj_t(xb3, wt, scale, tm, tn):
    b, s, _ = xb3.shape
    n = wt.shape[0]
    return pl.pallas_call(
        functools.partial(_proj_t_kernel, scale=scale),
        grid=(n // tn, b, s // tm),
        in_specs=[pl.BlockSpec((tn, D_MODEL), lambda j, bi, i: (j, 0)),
                  pl.BlockSpec((None, tm, D_MODEL), lambda j, bi, i: (bi, i, 0))],
        out_specs=pl.BlockSpec((None, tn, tm), lambda j, bi, i: (bi, j, i)),
        out_shape=jax.ShapeDtypeStruct((b, n, s), BF16),
        compiler_params=_params("parallel", "parallel", "parallel"),
        name="proj_t",
    )(wt, xb3)


def _fgate_kernel(x_ref, w_ref, b_ref, c_ref, carry_ref):
    @pl.when(pl.program_id(1) == 0)
    def _():
        carry_ref[...] = jnp.zeros_like(carry_ref)

    z = _dot(x_ref[...], w_ref[...]) + b_ref[...]
    log_f = jnp.minimum(z, 0.0) - jnp.log(1.0 + jnp.exp(-jnp.abs(z)))
    tm = z.shape[0]
    tri = (lax.broadcasted_iota(jnp.int32, (tm, tm), 1) <= lax.broadcasted_iota(jnp.int32, (tm, tm), 0)).astype(F32)
    c = jnp.dot(tri, log_f, preferred_element_type=F32, precision=lax.Precision.HIGHEST) + carry_ref[...]
    c_ref[...] = c
    carry_ref[...] = c[tm - 1:tm, :]


def _fgate(xb3, wf, bf, tm):
    b, s, _ = xb3.shape
    return pl.pallas_call(
        _fgate_kernel,
        grid=(b, s // tm),
        in_specs=[pl.BlockSpec((None, tm, D_MODEL), lambda bi, i: (bi, i, 0)),
                  pl.BlockSpec(wf.shape, lambda bi, i: (0, 0)), pl.BlockSpec(bf.shape, lambda bi, i: (0, 0))],
        out_specs=pl.BlockSpec((None, tm, LANES), lambda bi, i: (bi, i, 0)),
        out_shape=jax.ShapeDtypeStruct((b, s, LANES), F32),
        scratch_shapes=[pltpu.VMEM((1, LANES), F32)],
        compiler_params=_params("parallel", "arbitrary"),
        name="fox_gate",
    )(xb3, wf, bf)


def _router_kernel(x_ref, w_ref, b_ref, route_ref, cnt_ref, carry_ref):
    @pl.when(pl.program_id(0) == 0)
    def _():
        carry_ref[...] = jnp.zeros_like(carry_ref)

    logits = jnp.dot(x_ref[...], w_ref[...], preferred_element_type=F32, precision=lax.Precision.HIGHEST) + b_ref[...]
    tm = logits.shape[0]
    lane = lax.broadcasted_iota(jnp.int32, (tm, LANES), 1)
    l1 = jnp.max(logits, axis=-1, keepdims=True)
    i1 = jnp.min(jnp.where(logits == l1, lane, LANES), axis=-1, keepdims=True)
    rest = jnp.where(lane == i1, NEG, logits)
    l2 = jnp.max(rest, axis=-1, keepdims=True)
    i2 = jnp.min(jnp.where(rest == l2, lane, LANES), axis=-1, keepdims=True)
    e = jnp.exp(l2 - l1)
    w1 = 1.0 / (1.0 + e)
    w2 = e / (1.0 + e)
    hot1 = (lane == i1).astype(F32)
    hot2 = (lane == i2).astype(F32)
    cnt = hot1 + hot2
    strict = (lax.broadcasted_iota(jnp.int32, (tm, tm), 1) < lax.broadcasted_iota(jnp.int32, (tm, tm), 0)).astype(BF16)
    before = _dot(strict, cnt.astype(BF16)) + carry_ref[...]
    r1 = jnp.sum(before * hot1, axis=-1, keepdims=True)
    r2 = jnp.sum(before * hot2, axis=-1, keepdims=True)
    vals = (i1.astype(F32), i2.astype(F32), w1, w2, r1, r2)
    route = jnp.zeros((tm, LANES), F32)
    for idx, val in enumerate(vals):
        route = jnp.where(lane == idx, val, route)
    route_ref[...] = route
    total = carry_ref[...] + jnp.sum(cnt, axis=0, keepdims=True)
    carry_ref[...] = total
    cnt_ref[...] = jnp.broadcast_to(total, cnt_ref.shape)


def _router(x, rw, rb, tm):
    m = x.shape[0]
    return pl.pallas_call(
        _router_kernel,
        grid=(m // tm,),
        in_specs=[pl.BlockSpec((tm, D_MODEL), lambda i: (i, 0)), pl.BlockSpec(rw.shape, lambda i: (0, 0)),
                  pl.BlockSpec(rb.shape, lambda i: (0, 0))],
        out_specs=[pl.BlockSpec((tm, LANES), lambda i: (i, 0)), pl.BlockSpec((8, LANES), lambda i: (0, 0))],
        out_shape=[jax.ShapeDtypeStruct((m, LANES), F32), jax.ShapeDtypeStruct((8, LANES), F32)],
        scratch_shapes=[pltpu.VMEM((1, LANES), F32)],
        compiler_params=_params("arbitrary"),
        name="moe_router",
    )(x, rw, rb)


def _dispatch_kernel(pos_ref, x_ref, zeros_ref, xs_ref, sem):
    del zeros_ref
    tm = x_ref.shape[0]

    def copy(r, k):
        return pltpu.make_async_copy(x_ref.at[pl.ds(r, 1), :], xs_ref.at[pl.ds(pos_ref[0, 0, 2 * r + k], 1), :], sem)

    def start(r, _):
        copy(r, 0).start()
        copy(r, 1).start()
        return 0

    def wait(r, _):
        copy(r, 0).wait()
        copy(r, 1).wait()
        return 0

    lax.fori_loop(0, tm, start, 0)
    lax.fori_loop(0, tm, wait, 0)


def _dispatch(x, pos, n_rows, tm):
    m = x.shape[0]
    pos3 = pos.reshape(m // tm, 1, 2 * tm)
    return pl.pallas_call(
        _dispatch_kernel,
        grid=(m // tm,),
        in_specs=[pl.BlockSpec((1, 1, 2 * tm), lambda i: (i, 0, 0), memory_space=pltpu.SMEM),
                  pl.BlockSpec((tm, D_MODEL), lambda i: (i, 0)),
                  pl.BlockSpec(memory_space=pl.ANY)],
        out_specs=pl.BlockSpec(memory_space=pl.ANY),
        out_shape=jax.ShapeDtypeStruct((n_rows, D_MODEL), x.dtype),
        scratch_shapes=[pltpu.SemaphoreType.DMA(())],
        input_output_aliases={2: 0},
        compiler_params=_params("arbitrary"),
        name="moe_dispatch",
    )(pos3, x, jnp.zeros((n_rows, D_MODEL), x.dtype))


def _moe_up_kernel(te_ref, nu_ref, x_ref, wg_ref, wu_ref, o_ref, wgb_ref, wub_ref):
    t = pl.program_id(1)
    live = t < nu_ref[0]
    fresh = jnp.logical_or(t == 0, te_ref[t] != te_ref[jnp.maximum(t - 1, 0)])

    @pl.when(jnp.logical_and(live, fresh))
    def _():
        wgb_ref[...] = wg_ref[...].astype(BF16)
        wub_ref[...] = wu_ref[...].astype(BF16)

    @pl.when(live)
    def _():
        x = x_ref[...].astype(BF16)
        o_ref[...] = _silu_mul(_dot(x, wgb_ref[...]), _dot(x, wub_ref[...])).astype(BF16)

    @pl.when(jnp.logical_not(live))
    def _():
        o_ref[...] = jnp.zeros_like(o_ref)


def _moe_up(xs, wg, wu, tile_expert, n_used, tm, tn):
    p = xs.shape[0]
    live = lambda t, nu: jnp.minimum(t, nu[0] - 1)
    return pl.pallas_call(
        _moe_up_kernel,
        grid_spec=pltpu.PrefetchScalarGridSpec(
            num_scalar_prefetch=2,
            grid=(D_FF // tn, p // tm),
            in_specs=[pl.BlockSpec((tm, D_MODEL), lambda j, t, te, nu: (live(t, nu), 0)),
                      pl.BlockSpec((None, D_MODEL, tn), lambda j, t, te, nu: (te[live(t, nu)], 0, j)),
                      pl.BlockSpec((None, D_MODEL, tn), lambda j, t, te, nu: (te[live(t, nu)], 0, j))],
            out_specs=pl.BlockSpec((tm, tn), lambda j, t, te, nu: (t, j)),
            scratch_shapes=[pltpu.VMEM((D_MODEL, tn), BF16), pltpu.VMEM((D_MODEL, tn), BF16)],
        ),
        out_shape=jax.ShapeDtypeStruct((p, D_FF), BF16),
        compiler_params=_params("arbitrary", "arbitrary"),
        name="moe_up",
    )(tile_expert, n_used, xs, wg, wu)


def _moe_down_kernel(te_ref, nu_ref, h_ref, w_ref, o_ref, acc_ref):
    del te_ref
    k = pl.program_id(1)

    live = pl.program_id(0) < nu_ref[0]

    @pl.when(k == 0)
    def _():
        acc_ref[...] = jnp.zeros_like(acc_ref)

    @pl.when(live)
    def _():
        acc_ref[...] += _dot(h_ref[...], w_ref[...])

    @pl.when(k == pl.num_programs(1) - 1)
    def _():
        o_ref[...] = acc_ref[...]


def _moe_down(hs, wd, tile_expert, n_used, tm, tk):
    p = hs.shape[0]
    live = lambda t, nu: jnp.minimum(t, nu[0] - 1)
    return pl.pallas_call(
        _moe_down_kernel,
        grid_spec=pltpu.PrefetchScalarGridSpec(
            num_scalar_prefetch=2,
            grid=(p // tm, D_FF // tk),
            in_specs=[pl.BlockSpec((tm, tk), lambda t, k, te, nu: (live(t, nu), k)),
                      pl.BlockSpec((None, tk, D_MODEL), lambda t, k, te, nu: (te[live(t, nu)], k, 0))],
            out_specs=pl.BlockSpec((tm, D_MODEL), lambda t, k, te, nu: (t, 0)),
            scratch_shapes=[pltpu.VMEM((tm, D_MODEL), F32)],
        ),
        out_shape=jax.ShapeDtypeStruct((p, D_MODEL), F32),
        compiler_params=_params("arbitrary", "arbitrary"),
        name="moe_down",
    )(tile_expert, n_used, hs, wd)


def _combine_ln_kernel(pos_ref, ys_ref, x_ref, route_ref, g_ref, b_ref, o_ref, buf0, buf1, sem):
    tm = x_ref.shape[0]

    def copies(r):
        c0 = pltpu.make_async_copy(ys_ref.at[pl.ds(pos_ref[0, 0, 2 * r], 1), :], buf0.at[pl.ds(r, 1), :], sem)
        c1 = pltpu.make_async_copy(ys_ref.at[pl.ds(pos_ref[0, 0, 2 * r + 1], 1), :], buf1.at[pl.ds(r, 1), :], sem)
        return c0, c1

    def start(r, _):
        c0, c1 = copies(r)
        c0.start()
        c1.start()
        return 0

    def wait(r, _):
        c0, c1 = copies(r)
        c0.wait()
        c1.wait()
        return 0

    lax.fori_loop(0, tm, start, 0)
    lax.fori_loop(0, tm, wait, 0)
    route = route_ref[...]
    y = route[:, 2:3] * buf0[...] + route[:, 3:4] * buf1[...]
    o_ref[...] = _layer_norm(ALPHA * x_ref[...] + y, g_ref[...], b_ref[...])


def _combine_ln(ys, pos, x, route, g, b, tm):
    m = x.shape[0]
    pos3 = pos.reshape(m // tm, 1, 2 * tm)
    vec = pl.BlockSpec((1, D_MODEL), lambda i: (0, 0))
    return pl.pallas_call(
        _combine_ln_kernel,
        grid=(m // tm,),
        in_specs=[pl.BlockSpec((1, 1, 2 * tm), lambda i: (i, 0, 0), memory_space=pltpu.SMEM),
                  pl.BlockSpec(memory_space=pl.ANY),
                  pl.BlockSpec((tm, D_MODEL), lambda i: (i, 0)),
                  pl.BlockSpec((tm, LANES), lambda i: (i, 0)), vec, vec],
        out_specs=pl.BlockSpec((tm, D_MODEL), lambda i: (i, 0)),
        out_shape=jax.ShapeDtypeStruct((m, D_MODEL), F32),
        scratch_shapes=[pltpu.VMEM((tm, D_MODEL), F32), pltpu.VMEM((tm, D_MODEL), F32), pltpu.SemaphoreType.DMA(())],
        compiler_params=_params("arbitrary"),
        name="moe_combine_ln",
    )(pos3, ys, x, route, g, b)


def _rope_tables(seq):
    def angles(dim):
        inv_freq = 1.0 / (ROPE_THETA ** (jnp.arange(0, dim, 2, dtype=F32) / dim))
        ang = jnp.arange(seq, dtype=F32)[:, None] * inv_freq[None, :]
        return jnp.cos(ang), jnp.sin(ang)

    c128, s128 = angles(HEAD_DIM)
    cos_f = jnp.concatenate([c128, c128], axis=-1)
    sin_f = jnp.concatenate([-s128, s128], axis=-1)
    c64, s64 = angles(MLA_ROPE_DIM)
    z32 = jnp.zeros_like(c64)
    cos_t = jnp.concatenate([c64, c64, z32, z32], axis=-1)
    sin_a = jnp.concatenate([-s64, z32, z32, z32], axis=-1)
    sin_b = jnp.concatenate([z32, s64, z32, z32], axis=-1)
    return cos_f, sin_f, cos_t, sin_a, sin_b, c64.T, s64.T


def _pad_cols(a, width):
    return jnp.pad(a, ((0, 0), (0, width - a.shape[1])))


def _row(v):
    return v.reshape(1, -1).astype(F32)


def _even_layer(x, xb, batch, seq, tables, w_in, q_norm, w_q_b, kv_norm, w_kv_b, w_out, ln1_g, ln1_b,
                w_gate, w_up, w_down, ln2_g, ln2_b):
    cos_f, sin_f, cos_t, sin_a, sin_b, cos_tt, sin_tt = tables
    tm = min(512, seq)
    wa = jnp.concatenate([w_in[:, OFF_CKV:OFF_KROPE], _pad_cols(w_in[:, OFF_KROPE:OFF_DQ], LANES),
                          _pad_cols(w_in[:, OFF_CQ:OFF_CKV], MLA_Q_RANK_PAD)], axis=1).astype(BF16)
    qg = _pad_cols(_row(q_norm), MLA_Q_RANK_PAD)
    wq = jnp.pad(w_q_b.reshape(MLA_Q_RANK, MLA_HEADS, MLA_QK_DIM),
                 ((0, MLA_Q_RANK_PAD - MLA_Q_RANK), (0, 0), (0, MLA_QK_PAD - MLA_QK_DIM)))
    wqt = wq.reshape(MLA_Q_RANK_PAD, MLA_HEADS * MLA_QK_PAD).T.astype(BF16)
    wkv3 = w_kv_b.reshape(MLA_KV_RANK, MLA_HEADS, MLA_NOPE_DIM + MLA_V_DIM)
    wk = wkv3[:, :, :MLA_NOPE_DIM].reshape(MLA_KV_RANK, -1).astype(BF16)
    wvt = wkv3[:, :, MLA_NOPE_DIM:].reshape(MLA_KV_RANK, -1).T.astype(BF16)
    xb3 = xb.reshape(batch, seq, D_MODEL)
    qt_mla, k_mla, vt_mla = _mla_prep(xb3, wa, _row(kv_norm), qg, wqt, wk, wvt, cos_t, sin_a, sin_b,
                                      cos_tt, sin_tt, tm)
    tq = min(FLASH_TQ, seq)
    o_mla = _flash(qt_mla, k_mla, vt_mla, None, heads=MLA_HEADS, dq=MLA_QK_PAD, dk=MLA_QK_PAD, dv=MLA_V_DIM,
                   tq=tq, tk=min(FLASH_TK, tq))
    o_mla = o_mla.reshape(batch * seq, -1)

    dqkv = _dqkv(xb, w_in[:, OFF_DQ:].astype(BF16), cos_f, sin_f, seq, tm)
    outs, lses = [], []
    for g, (window, dil) in enumerate(DIL_PATTERNS):
        assert window == DIL_SPAN * dil
        o_g, lse_g = _dilated_group(dqkv, g, dil, batch, seq, min(256, seq // dil))
        outs.append(o_g)
        lses.append(lse_g)
    o_dil = _dil_merge(outs, lses, tm)

    n_mla = MLA_HEADS * MLA_V_DIM
    wo = w_out.astype(BF16)
    x1, x1b = _proj_ln([o_mla, o_dil], [wo[:n_mla], wo[n_mla:]], x, _row(ln1_g), _row(ln1_b), tm)
    hmid = _ffn_up(x1b, w_gate, w_up, tm, 512)
    return _ffn_down_ln(hmid, w_down.astype(BF16), x1, _row(ln2_g), _row(ln2_b), tm, 1408)


def _odd_layer(x, xb, batch, seq, w_qkv, w_f, b_f, w_out, ln1_g, ln1_b, router_w, router_b,
               exp_w_gate, exp_w_up, exp_w_down, ln2_g, ln2_b):
    m = batch * seq
    tm = min(512, seq)
    xb3 = xb.reshape(batch, seq, D_MODEL)
    wb = w_qkv.astype(BF16)
    qt = _proj_t(xb3, wb[:, :FOX_WIDTH].T, HEAD_DIM ** -0.5 * LOG2E, tm, 1024)
    k = _proj(xb, wb[:, FOX_WIDTH:2 * FOX_WIDTH], tm, 1024).reshape(batch, seq, FOX_WIDTH)
    vt = _proj_t(xb3, wb[:, 2 * FOX_WIDTH:].T, 1.0, tm, 1024)
    c = _fgate(xb3, _pad_cols(w_f, LANES).astype(BF16), _pad_cols(_row(b_f), LANES), min(256, seq))
    c_t = jnp.transpose(c[:, :, :FOX_HEADS], (0, 2, 1)).reshape(batch, FOX_HEADS, 1, seq)
    tq = min(FLASH_TQ, seq)
    o = _flash(qt, k, vt, c_t, heads=FOX_HEADS, dq=HEAD_DIM, dk=HEAD_DIM, dv=HEAD_DIM, tq=tq, tk=min(FLASH_TK, tq))
    x1, x1b = _proj_ln([o.reshape(m, -1)], [w_out.astype(BF16)], x, _row(ln1_g), _row(ln1_b), tm)

    rb = jnp.full((1, LANES), NEG, F32).at[0, :N_EXPERTS].set(router_b.astype(F32))
    route, counts = _router(x1, _pad_cols(router_w, LANES), rb, min(256, seq))
    tile = 512
    n_tiles = (2 * m) // tile + N_EXPERTS
    cnt = counts[0, :N_EXPERTS].astype(jnp.int32)
    tiles_per = (cnt + tile - 1) // tile
    tile_end = jnp.cumsum(tiles_per)
    offset = (tile_end - tiles_per) * tile
    idx = route[:, 0:2].astype(jnp.int32)
    pos = (offset[idx] + route[:, 4:6].astype(jnp.int32)).reshape(-1)
    n_used = tile_end[-1:]
    tile_ids = jnp.arange(n_tiles, dtype=jnp.int32)
    tile_expert = jnp.minimum(jnp.sum((tile_end[None, :] <= tile_ids[:, None]).astype(jnp.int32), axis=1),
                              N_EXPERTS - 1)
    td = min(256, seq)
    xs = _dispatch(x1, pos, n_tiles * tile, td)
    hs = _moe_up(xs, exp_w_gate, exp_w_up, tile_expert, n_used, tile, 512)
    ys = _moe_down(hs, exp_w_down.astype(BF16), tile_expert, n_used, tile, 1408)
    return _combine_ln(ys, pos, x1, route, _row(ln2_g), _row(ln2_b), td)


def kernel(x, ev_w_in, ev_q_norm, ev_w_q_b, ev_kv_norm, ev_w_kv_b, ev_w_out, ev_ln1_g, ev_ln1_b, ev_ffn_w_gate, ev_ffn_w_up, ev_ffn_w_down, ev_ln2_g, ev_ln2_b, od_w_qkv, od_w_f, od_b_f, od_w_out, od_ln1_g, od_ln1_b, od_router_w, od_router_b, od_exp_w_gate, od_exp_w_up, od_exp_w_down, od_ln2_g, od_ln2_b):
    batch, seq, _ = x.shape
    tables = _rope_tables(seq)
    h = x.reshape(batch * seq, D_MODEL)
    hb = h.astype(BF16)
    for layer in range(DEPTH):
        i = layer // 2
        if layer % 2 == 0:
            h, hb = _even_layer(h, hb, batch, seq, tables, ev_w_in[i], ev_q_norm[i], ev_w_q_b[i], ev_kv_norm[i],
                                ev_w_kv_b[i], ev_w_out[i], ev_ln1_g[i], ev_ln1_b[i], ev_ffn_w_gate[i],
                                ev_ffn_w_up[i], ev_ffn_w_down[i], ev_ln2_g[i], ev_ln2_b[i])
        else:
            h = _odd_layer(h, hb, batch, seq, od_w_qkv[i], od_w_f[i], od_b_f[i], od_w_out[i], od_ln1_g[i],
                           od_ln1_b[i], od_router_w[i], od_router_b[i], od_exp_w_gate[i], od_exp_w_up[i],
                           od_exp_w_down[i], od_ln2_g[i], od_ln2_b[i])
            hb = h.astype(BF16)
    return h.reshape(batch, seq, D_MODEL)
```

```python
import functools

import jax
import jax.numpy as jnp
from jax import lax
from jax.experimental import pallas as pl
from jax.experimental.pallas import tpu as pltpu

F32 = jnp.float32
BF16 = jnp.bfloat16

D_MODEL = 2048
HEAD_DIM = 128
LANES = 128
ROPE_THETA = 10000.0
LN_EPS = 1e-5
RMS_EPS = 1e-6

MLA_HEADS = 10
MLA_Q_RANK = 448
MLA_Q_RANK_PAD = 512
MLA_KV_RANK = 128
MLA_NOPE_DIM = 128
MLA_ROPE_DIM = 64
MLA_V_DIM = 128
MLA_QK_DIM = MLA_NOPE_DIM + MLA_ROPE_DIM
MLA_QK_PAD = 256

DIL_PATTERNS = ((128, 1), (512, 4), (2048, 16))
DIL_GROUPS = 3
DIL_HEADS = 6
DIL_WIDTH = DIL_HEADS * HEAD_DIM
DIL_SPAN = 128

OFF_CQ = 0
OFF_CKV = OFF_CQ + MLA_Q_RANK
OFF_KROPE = OFF_CKV + MLA_KV_RANK
OFF_DQ = OFF_KROPE + MLA_ROPE_DIM
OFF_DK = OFF_DQ + DIL_GROUPS * DIL_WIDTH
OFF_DV = OFF_DK + DIL_WIDTH
W_IN_COLS = OFF_DV + DIL_WIDTH

FOX_HEADS = 16
FOX_WIDTH = FOX_HEADS * HEAD_DIM

D_FF = 5632
N_EXPERTS = 8
DEPTH = 2
ALPHA = (2.0 * DEPTH) ** 0.25

NEG = -1e30
LOG2E = 1.4426950408889634
FLASH_TQ = 512
FLASH_TK = 512
DMA_UNROLL = 8
VMEM_LIMIT = 56 * 1024 * 1024


def _params(*sem, vmem=VMEM_LIMIT):
    return pltpu.CompilerParams(dimension_semantics=sem, vmem_limit_bytes=vmem)


def _dot(a, b):
    return jnp.dot(a, b, preferred_element_type=F32)


def _dot_nt(a, b):
    return lax.dot_general(a, b, (((1,), (1,)), ((), ())), preferred_element_type=F32)


def _layer_norm(y, g, b):
    mu = jnp.mean(y, axis=-1, keepdims=True)
    d = y - mu
    var = jnp.mean(d * d, axis=-1, keepdims=True)
    return d * lax.rsqrt(var + LN_EPS) * g + b


def _rope128(x, cos_f, sin_f):
    return x * cos_f + pltpu.roll(x, 64, 1) * sin_f


def _rope64(x, cos_t, sin_a, sin_b):
    return x * cos_t + pltpu.roll(x, 96, 1) * sin_a + pltpu.roll(x, 32, 1) * sin_b


def _mla_prep_kernel(x_ref, wa_ref, kvg_ref, qg_ref, wqt_ref, wk_ref, wvt_ref, cos_ref, sa_ref, sb_ref,
                     ct_ref, st_ref, qt_ref, k_ref, vt_ref):
    h = _dot(x_ref[...], wa_ref[...])
    ckv = h[:, 0:128]
    kr = h[:, 128:256]
    cq = h[:, 256:768]
    ckv_n = (ckv * lax.rsqrt(jnp.mean(ckv * ckv, axis=-1, keepdims=True) + RMS_EPS) * kvg_ref[...]).astype(BF16)
    cq_ms = jnp.sum(cq * cq, axis=-1, keepdims=True) * (1.0 / MLA_Q_RANK)
    cq_n = (cq * lax.rsqrt(cq_ms + RMS_EPS) * qg_ref[...]).astype(BF16)
    kr_r = _rope64(kr, cos_ref[...], sa_ref[...], sb_ref[...]).astype(BF16)
    scale = MLA_QK_DIM ** -0.5 * LOG2E
    qt = _dot_nt(wqt_ref[...], cq_n)
    c, s = ct_ref[...], st_ref[...]
    k_nope = _dot(ckv_n, wk_ref[...])
    for hd in range(MLA_HEADS):
        o = hd * MLA_QK_PAD
        qt_ref[o:o + 128, :] = (qt[o:o + 128] * scale).astype(BF16)
        x1, x2 = qt[o + 128:o + 160], qt[o + 160:o + 192]
        qt_ref[o + 128:o + 160, :] = ((x1 * c - x2 * s) * scale).astype(BF16)
        qt_ref[o + 160:o + 192, :] = ((x2 * c + x1 * s) * scale).astype(BF16)
        qt_ref[o + 192:o + 256, :] = jnp.zeros((64, qt.shape[1]), BF16)
        k_ref[:, o:o + 128] = k_nope[:, hd * 128:(hd + 1) * 128].astype(BF16)
        k_ref[:, o + 128:o + 256] = kr_r
    vt_ref[...] = _dot_nt(wvt_ref[...], ckv_n).astype(BF16)


def _mla_prep(xb3, wa, kvg, qg, wqt, wk, wvt, cos_t, sin_a, sin_b, cos_tt, sin_tt, tm):
    b, s, _ = xb3.shape
    full = lambda shape: pl.BlockSpec(shape, lambda bi, i: (0, 0))
    tab = pl.BlockSpec((tm, LANES), lambda bi, i: (i, 0))
    tab_t = pl.BlockSpec((MLA_ROPE_DIM // 2, tm), lambda bi, i: (0, i))
    wide = MLA_HEADS * MLA_QK_PAD
    vw = MLA_HEADS * MLA_V_DIM
    return pl.pallas_call(
        _mla_prep_kernel,
        grid=(b, s // tm),
        in_specs=[pl.BlockSpec((None, tm, D_MODEL), lambda bi, i: (bi, i, 0)), full(wa.shape), full(kvg.shape),
                  full(qg.shape), full(wqt.shape), full(wk.shape), full(wvt.shape), tab, tab, tab, tab_t, tab_t],
        out_specs=[pl.BlockSpec((None, wide, tm), lambda bi, i: (bi, 0, i)),
                   pl.BlockSpec((None, tm, wide), lambda bi, i: (bi, i, 0)),
                   pl.BlockSpec((None, vw, tm), lambda bi, i: (bi, 0, i))],
        out_shape=[jax.ShapeDtypeStruct((b, wide, s), BF16), jax.ShapeDtypeStruct((b, s, wide), BF16),
                   jax.ShapeDtypeStruct((b, vw, s), BF16)],
        compiler_params=_params("parallel", "parallel"),
        name="mla_prep",
    )(xb3, wa, kvg, qg, wqt, wk, wvt, cos_t, sin_a, sin_b, cos_tt, sin_tt)


def _dqkv_kernel(x_ref, w_ref, cos_ref, sin_ref, o_ref):
    j = pl.program_id(0)
    h = _dot(x_ref[...], w_ref[...])

    @pl.when(j < 4)
    def _():
        cos_f, sin_f = cos_ref[...], sin_ref[...]
        sc = jnp.where(j < 3, HEAD_DIM ** -0.5, 1.0).astype(F32)
        for hd in range(DIL_HEADS):
            sl = slice(hd * 128, (hd + 1) * 128)
            o_ref[:, sl] = (_rope128(h[:, sl], cos_f, sin_f) * sc).astype(BF16)

    @pl.when(j == 4)
    def _():
        o_ref[...] = h.astype(BF16)


def _dqkv(xb, wd, cos_f, sin_f, seq, tm):
    m = xb.shape[0]
    nrow = seq // tm
    n_col = wd.shape[1] // DIL_WIDTH
    tab = pl.BlockSpec((tm, LANES), lambda j, i: (i % nrow, 0))
    return pl.pallas_call(
        _dqkv_kernel,
        grid=(n_col, m // tm),
        in_specs=[pl.BlockSpec((tm, D_MODEL), lambda j, i: (i, 0)),
                  pl.BlockSpec((D_MODEL, DIL_WIDTH), lambda j, i: (0, j)), tab, tab],
        out_specs=pl.BlockSpec((tm, DIL_WIDTH), lambda j, i: (i, j)),
        out_shape=jax.ShapeDtypeStruct((m, wd.shape[1]), BF16),
        compiler_params=_params("parallel", "parallel"),
        name="dil_qkv",
    )(xb, wd, cos_f, sin_f)


def _flash_kernel(*refs, tq, tk, use_c):
    assert tq == tk
    if use_c:
        qt_ref, k_ref, vt_ref, c_ref, o_ref, acc_ref, s0_ref, s1_ref, kaug_ref = refs
    else:
        qt_ref, k_ref, vt_ref, o_ref, acc_ref, s0_ref, s1_ref = refs
    i = pl.program_id(2)
    seq = k_ref.shape[0]
    if use_c:
        @pl.when(i == 0)
        def _():
            row = lax.broadcasted_iota(jnp.int32, (LANES, tk), 0)

            def build(j, _):
                start = pl.multiple_of(j * tk, tk)
                neg = -LOG2E * c_ref[:, pl.ds(start, tk)]
                hi = neg.astype(BF16).astype(F32)
                mid = (neg - hi).astype(BF16).astype(F32)
                lo = neg - hi - mid
                blk = jnp.where(row == 0, hi, jnp.where(row == 1, mid, jnp.where(row == 2, lo, 0.0)))
                kaug_ref[pl.ds(start, tk), :] = blk.T.astype(BF16)
                return 0

            lax.fori_loop(0, seq // tk, build, 0)

    q = qt_ref[...]
    if use_c:
        ones = (lax.broadcasted_iota(jnp.int32, (LANES, tq), 0) < 3).astype(BF16)
        q = jnp.concatenate([q, ones], axis=0)
    acc_ref[...] = jnp.zeros_like(acc_ref)

    def scores(j, s_ref):
        start = pl.multiple_of(j * tk, tk)
        kt = k_ref[pl.ds(start, tk), :]
        if use_c:
            kt = jnp.concatenate([kt, kaug_ref[pl.ds(start, tk), :]], axis=1)
        s_ref[...] = _dot(kt, q)

    def update(j, s_ref, stats, masked):
        m, l = stats
        s = s_ref[...]
        if masked:
            on_diag = lax.broadcasted_iota(jnp.int32, (tk, tq), 0) <= lax.broadcasted_iota(jnp.int32, (tk, tq), 1)
            s = jnp.where(on_diag, s, NEG)
        m_new = jnp.maximum(m, jnp.max(s, axis=0, keepdims=True))
        a = jnp.exp2(m - m_new)
        p = jnp.exp2(s - m_new)
        l = a * l + jnp.sum(p, axis=0, keepdims=True)
        start = pl.multiple_of(j * tk, tk)
        acc_ref[...] = a * acc_ref[...] + _dot(vt_ref[:, pl.ds(start, tk)], p.astype(BF16))
        return m_new, l

    scores(0, s0_ref)

    def pair(jj, stats):
        scores(2 * jj + 1, s1_ref)
        stats = update(2 * jj, s0_ref, stats, False)
        scores(2 * jj + 2, s0_ref)
        return update(2 * jj + 1, s1_ref, stats, False)

    init = (jnp.full((1, tq), NEG, F32), jnp.zeros((1, tq), F32))
    stats = lax.fori_loop(0, i // 2, pair, init)

    def odd_tail(stats):
        scores(i, s1_ref)
        stats = update(i - 1, s0_ref, stats, False)
        return update(i, s1_ref, stats, True)

    def even_tail(stats):
        return update(i, s0_ref, stats, True)

    _, l = lax.cond(i % 2 == 1, odd_tail, even_tail, stats)
    o_ref[...] = (acc_ref[...] / l).T.astype(o_ref.dtype)


def _flash(qt, k, vt, c, *, heads, dq, dk, dv, tq, tk):
    b, s, _ = k.shape
    use_c = c is not None
    in_specs = [pl.BlockSpec((None, dq, tq), lambda bi, h, i: (bi, h, i)),
                pl.BlockSpec((None, s, dk), lambda bi, h, i: (bi, 0, h)),
                pl.BlockSpec((None, dv, s), lambda bi, h, i: (bi, h, 0))]
    args = [qt, k, vt]
    scratch = [pltpu.VMEM((dv, tq), F32), pltpu.VMEM((tk, tq), F32), pltpu.VMEM((tk, tq), F32)]
    if use_c:
        in_specs.append(pl.BlockSpec((None, None, 1, s), lambda bi, h, i: (bi, h, 0, 0)))
        args.append(c)
        scratch.append(pltpu.VMEM((s, LANES), BF16))
    return pl.pallas_call(
        functools.partial(_flash_kernel, tq=tq, tk=tk, use_c=use_c),
        grid=(b, heads, s // tq),
        in_specs=in_specs,
        out_specs=pl.BlockSpec((None, tq, dv), lambda bi, h, i: (bi, i, h)),
        out_shape=jax.ShapeDtypeStruct((b, s, heads * dv), BF16),
        scratch_shapes=scratch,
        compiler_params=_params("parallel", "parallel", "arbitrary"),
        name="flash_fox" if use_c else "flash_mla",
    )(*args)


def _dilated_kernel(q_ref, kc_ref, kp_ref, vc_ref, vp_ref, o_ref, lse_ref, *, tu):
    i = pl.program_id(2)
    nk = DIL_SPAN + tu
    a = lax.broadcasted_iota(jnp.int32, (tu, nk), 0)
    c = lax.broadcasted_iota(jnp.int32, (tu, nk), 1)
    back = a + DIL_SPAN - c
    first = jnp.where(i > 0, 0, DIL_SPAN)
    bias = jnp.where(back >= 0, jnp.where(back <= DIL_SPAN, jnp.where(c >= first, 0.0, NEG), NEG), NEG)
    lane = lax.broadcasted_iota(jnp.int32, (tu, LANES), 1)
    lse_all = jnp.zeros((tu, LANES), F32)
    for hd in range(DIL_HEADS):
        sl = slice(hd * 128, (hd + 1) * 128)
        k = jnp.concatenate([kp_ref[tu - DIL_SPAN:, sl], kc_ref[:, sl]], axis=0)
        v = jnp.concatenate([vp_ref[tu - DIL_SPAN:, sl], vc_ref[:, sl]], axis=0)
        s = _dot_nt(q_ref[:, sl], k) + bias
        m = jnp.max(s, axis=-1, keepdims=True)
        p = jnp.exp(s - m)
        l = jnp.sum(p, axis=-1, keepdims=True)
        o_ref[:, sl] = _dot(p.astype(BF16), v) / l
        lse_all = jnp.where(lane == hd, m + jnp.log(l), lse_all)
    lse_ref[...] = lse_all


def _dilated_group(dqkv, g, dil, batch, seq, tu):
    su = seq // dil
    nblk = dqkv.shape[1] // DIL_WIDTH
    view = dqkv.reshape(batch, su, dil * dqkv.shape[1])
    cur = lambda blk: pl.BlockSpec((None, tu, DIL_WIDTH), lambda b, r, i: (b, i, r * nblk + blk))
    prev = lambda blk: pl.BlockSpec((None, tu, DIL_WIDTH), lambda b, r, i: (b, jnp.maximum(i - 1, 0), r * nblk + blk))
    o, lse = pl.pallas_call(
        functools.partial(_dilated_kernel, tu=tu),
        grid=(batch, dil, su // tu),
        in_specs=[cur(g), cur(3), prev(3), cur(4), prev(4)],
        out_specs=[pl.BlockSpec((None, tu, DIL_WIDTH), lambda b, r, i: (b, i, r)),
                   pl.BlockSpec((None, tu, LANES), lambda b, r, i: (b, i, r))],
        out_shape=[jax.ShapeDtypeStruct((batch, su, dil * DIL_WIDTH), F32),
                   jax.ShapeDtypeStruct((batch, su, dil * LANES), F32)],
        compiler_params=_params("parallel", "parallel", "arbitrary"),
        name=f"dilated_{dil}",
    )(view, view, view, view, view)
    return o.reshape(batch * seq, DIL_WIDTH), lse.reshape(batch * seq, LANES)


def _dil_merge_kernel(o0, o1, o2, l0, l1, l2, out_ref):
    ls = [l0[...], l1[...], l2[...]]
    mx = jnp.maximum(jnp.maximum(ls[0], ls[1]), ls[2])
    es = [jnp.exp(x - mx) for x in ls]
    den = es[0] + es[1] + es[2]
    ws = [e / den for e in es]
    os_ = [o0, o1, o2]
    for hd in range(DIL_HEADS):
        sl = slice(hd * 128, (hd + 1) * 128)
        acc = ws[0][:, hd:hd + 1] * os_[0][:, sl]
        for g in (1, 2):
            acc = acc + ws[g][:, hd:hd + 1] * os_[g][:, sl]
        out_ref[:, sl] = acc.astype(BF16)


def _dil_merge(outs, lses, tm):
    m = outs[0].shape[0]
    ob = pl.BlockSpec((tm, DIL_WIDTH), lambda i: (i, 0))
    lb = pl.BlockSpec((tm, LANES), lambda i: (i, 0))
    return pl.pallas_call(
        _dil_merge_kernel,
        grid=(m // tm,),
        in_specs=[ob, ob, ob, lb, lb, lb],
        out_specs=ob,
        out_shape=jax.ShapeDtypeStruct((m, DIL_WIDTH), BF16),
        compiler_params=_params("parallel"),
        name="dil_merge",
    )(*outs, *lses)


def _pack_halves(x):
    bits = lax.bitcast_convert_type(x.astype(BF16).astype(F32), jnp.uint32)
    n = x.shape[1] // 2
    return (bits[:, :n] >> 16) | (bits[:, n:] & jnp.uint32(0xFFFF0000))


def _unpack_halves(w):
    lo = lax.bitcast_convert_type(w << 16, F32)
    hi = lax.bitcast_convert_type(w & jnp.uint32(0xFFFF0000), F32)
    return jnp.concatenate([lo, hi], axis=1).astype(BF16)


def _proj_ln_kernel(*refs, n_lhs, packed):
    lhs = refs[:n_lhs]
    ws = refs[n_lhs:2 * n_lhs]
    x_ref, g_ref, b_ref, o_ref, ob_ref = refs[2 * n_lhs:]
    y = _dot(lhs[0][...], ws[0][...])
    for a, w in zip(lhs[1:], ws[1:]):
        y = y + _dot(a[...], w[...])
    out = _layer_norm(ALPHA * x_ref[...] + y, g_ref[...], b_ref[...])
    o_ref[...] = out
    ob_ref[...] = _pack_halves(out) if packed else out.astype(BF16)


def _proj_ln(lhs, ws, x, g, b, tm, packed=False):
    m = x.shape[0]
    row = lambda width: pl.BlockSpec((tm, width), lambda i: (i, 0))
    full = lambda shape: pl.BlockSpec(shape, lambda i: (0, 0))
    second = ((m, D_MODEL // 2), jnp.uint32) if packed else ((m, D_MODEL), BF16)
    return pl.pallas_call(
        functools.partial(_proj_ln_kernel, n_lhs=len(lhs), packed=packed),
        grid=(m // tm,),
        in_specs=[row(a.shape[1]) for a in lhs] + [full(w.shape) for w in ws]
        + [row(D_MODEL), full(g.shape), full(b.shape)],
        out_specs=[row(D_MODEL), row(second[0][1])],
        out_shape=[jax.ShapeDtypeStruct((m, D_MODEL), F32), jax.ShapeDtypeStruct(*second)],
        compiler_params=_params("parallel"),
        name="proj_ln",
    )(*lhs, *ws, x, g, b)


def _silu_mul(g, u):
    return g * (1.0 / (1.0 + jnp.exp(-g))) * u


def _ffn_up_kernel(x_ref, wg_ref, wu_ref, o_ref, wgb_ref, wub_ref):
    @pl.when(pl.program_id(1) == 0)
    def _():
        wgb_ref[...] = wg_ref[...].astype(BF16)
        wub_ref[...] = wu_ref[...].astype(BF16)

    x = x_ref[...]
    o_ref[...] = _silu_mul(_dot(x, wgb_ref[...]), _dot(x, wub_ref[...])).astype(BF16)


def _ffn_up(xb, wg, wu, tm, tn):
    m = xb.shape[0]
    return pl.pallas_call(
        _ffn_up_kernel,
        grid=(D_FF // tn, m // tm),
        in_specs=[pl.BlockSpec((tm, D_MODEL), lambda j, i: (i, 0)),
                  pl.BlockSpec((D_MODEL, tn), lambda j, i: (0, j)),
                  pl.BlockSpec((D_MODEL, tn), lambda j, i: (0, j))],
        out_specs=pl.BlockSpec((tm, tn), lambda j, i: (i, j)),
        out_shape=jax.ShapeDtypeStruct((m, D_FF), BF16),
        scratch_shapes=[pltpu.VMEM((D_MODEL, tn), BF16), pltpu.VMEM((D_MODEL, tn), BF16)],
        compiler_params=_params("arbitrary", "arbitrary"),
        name="ffn_up",
    )(xb, wg, wu)


def _ffn_down_ln_kernel(h_ref, w_ref, x_ref, g_ref, b_ref, o_ref, ob_ref, acc_ref):
    k = pl.program_id(1)

    @pl.when(k == 0)
    def _():
        acc_ref[...] = jnp.zeros_like(acc_ref)

    acc_ref[...] += _dot(h_ref[...], w_ref[...])

    @pl.when(k == pl.num_programs(1) - 1)
    def _():
        out = _layer_norm(ALPHA * x_ref[...] + acc_ref[...], g_ref[...], b_ref[...])
        o_ref[...] = out
        ob_ref[...] = out.astype(BF16)


def _ffn_down_ln(h, wd, x, g, b, tm, tk):
    m = x.shape[0]
    row = pl.BlockSpec((tm, D_MODEL), lambda i, k: (i, 0))
    vec = pl.BlockSpec((1, D_MODEL), lambda i, k: (0, 0))
    return pl.pallas_call(
        _ffn_down_ln_kernel,
        grid=(m // tm, D_FF // tk),
        in_specs=[pl.BlockSpec((tm, tk), lambda i, k: (i, k)), pl.BlockSpec((tk, D_MODEL), lambda i, k: (k, 0)),
                  row, vec, vec],
        out_specs=[row, row],
        out_shape=[jax.ShapeDtypeStruct((m, D_MODEL), F32), jax.ShapeDtypeStruct((m, D_MODEL), BF16)],
        scratch_shapes=[pltpu.VMEM((tm, D_MODEL), F32)],
        compiler_params=_params("parallel", "arbitrary"),
        name="ffn_down_ln",
    )(h, wd, x, g, b)


def _proj_kernel(x_ref, w_ref, o_ref):
    o_ref[...] = _dot(x_ref[...], w_ref[...]).astype(BF16)


def _proj(xb, w, tm, tn):
    m = xb.shape[0]
    n = w.shape[1]
    return pl.pallas_call(
        _proj_kernel,
        grid=(n // tn, m // tm),
        in_specs=[pl.BlockSpec((tm, D_MODEL), lambda j, i: (i, 0)), pl.BlockSpec((D_MODEL, tn), lambda j, i: (0, j))],
        out_specs=pl.BlockSpec((tm, tn), lambda j, i: (i, j)),
        out_shape=jax.ShapeDtypeStruct((m, n), BF16),
        compiler_params=_params("parallel", "parallel"),
        name="proj",
    )(xb, w)


def _proj_t_kernel(w_ref, x_ref, o_ref, *, scale):
    o_ref[...] = (_dot_nt(w_ref[...], x_ref[...]) * scale).astype(BF16)


def _proj_t(xb3, wt, scale, tm, tn):
    b, s, _ = xb3.shape
    n = wt.shape[0]
    return pl.pallas_call(
        functools.partial(_proj_t_kernel, scale=scale),
        grid=(n // tn, b, s // tm),
        in_specs=[pl.BlockSpec((tn, D_MODEL), lambda j, bi, i: (j, 0)),
                  pl.BlockSpec((None, tm, D_MODEL), lambda j, bi, i: (bi, i, 0))],
        out_specs=pl.BlockSpec((None, tn, tm), lambda j, bi, i: (bi, j, i)),
        out_shape=jax.ShapeDtypeStruct((b, n, s), BF16),
        compiler_params=_params("parallel", "parallel", "parallel"),
        name="proj_t",
    )(wt, xb3)


def _fgate_kernel(x_ref, w_ref, b_ref, c_ref, carry_ref):
    @pl.when(pl.program_id(1) == 0)
    def _():
        carry_ref[...] = jnp.zeros_like(carry_ref)

    z = _dot(x_ref[...], w_ref[...]) + b_ref[...]
    log_f = jnp.minimum(z, 0.0) - jnp.log(1.0 + jnp.exp(-jnp.abs(z)))
    tm = z.shape[0]
    tri = (lax.broadcasted_iota(jnp.int32, (tm, tm), 1) <= lax.broadcasted_iota(jnp.int32, (tm, tm), 0)).astype(F32)
    c = jnp.dot(tri, log_f, preferred_element_type=F32, precision=lax.Precision.HIGHEST) + carry_ref[...]
    c_ref[...] = c
    carry_ref[...] = c[tm - 1:tm, :]


def _fgate(xb3, wf, bf, tm):
    b, s, _ = xb3.shape
    return pl.pallas_call(
        _fgate_kernel,
        grid=(b, s // tm),
        in_specs=[pl.BlockSpec((None, tm, D_MODEL), lambda bi, i: (bi, i, 0)),
                  pl.BlockSpec(wf.shape, lambda bi, i: (0, 0)), pl.BlockSpec(bf.shape, lambda bi, i: (0, 0))],
        out_specs=pl.BlockSpec((None, tm, LANES), lambda bi, i: (bi, i, 0)),
        out_shape=jax.ShapeDtypeStruct((b, s, LANES), F32),
        scratch_shapes=[pltpu.VMEM((1, LANES), F32)],
        compiler_params=_params("parallel", "arbitrary"),
        name="fox_gate",
    )(xb3, wf, bf)


def _router_kernel(x_ref, w_ref, b_ref, route_ref, cnt_ref, carry_ref):
    @pl.when(pl.program_id(0) == 0)
    def _():
        carry_ref[...] = jnp.zeros_like(carry_ref)

    logits = jnp.dot(x_ref[...], w_ref[...], preferred_element_type=F32, precision=lax.Precision.HIGHEST) + b_ref[...]
    tm = logits.shape[0]
    lane = lax.broadcasted_iota(jnp.int32, (tm, LANES), 1)
    l1 = jnp.max(logits, axis=-1, keepdims=True)
    i1 = jnp.min(jnp.where(logits == l1, lane, LANES), axis=-1, keepdims=True)
    rest = jnp.where(lane == i1, NEG, logits)
    l2 = jnp.max(rest, axis=-1, keepdims=True)
    i2 = jnp.min(jnp.where(rest == l2, lane, LANES), axis=-1, keepdims=True)
    e = jnp.exp(l2 - l1)
    w1 = 1.0 / (1.0 + e)
    w2 = e / (1.0 + e)
    hot1 = (lane == i1).astype(F32)
    hot2 = (lane == i2).astype(F32)
    cnt = hot1 + hot2
    strict = (lax.broadcasted_iota(jnp.int32, (tm, tm), 1) < lax.broadcasted_iota(jnp.int32, (tm, tm), 0)).astype(BF16)
    before = _dot(strict, cnt.astype(BF16)) + carry_ref[...]
    r1 = jnp.sum(before * hot1, axis=-1, keepdims=True)
    r2 = jnp.sum(before * hot2, axis=-1, keepdims=True)
    vals = (i1.astype(F32), i2.astype(F32), w1, w2, r1, r2)
    route = jnp.zeros((tm, LANES), F32)
    for idx, val in enumerate(vals):
        route = jnp.where(lane == idx, val, route)
    route_ref[...] = route
    total = carry_ref[...] + jnp.sum(cnt, axis=0, keepdims=True)
    carry_ref[...] = total
    cnt_ref[...] = jnp.broadcast_to(total, cnt_ref.shape)


def _router(x, rw, rb, tm):
    m = x.shape[0]
    return pl.pallas_call(
        _router_kernel,
        grid=(m // tm,),
        in_specs=[pl.BlockSpec((tm, D_MODEL), lambda i: (i, 0)), pl.BlockSpec(rw.shape, lambda i: (0, 0)),
                  pl.BlockSpec(rb.shape, lambda i: (0, 0))],
        out_specs=[pl.BlockSpec((tm, LANES), lambda i: (i, 0)), pl.BlockSpec((8, LANES), lambda i: (0, 0))],
        out_shape=[jax.ShapeDtypeStruct((m, LANES), F32), jax.ShapeDtypeStruct((8, LANES), F32)],
        scratch_shapes=[pltpu.VMEM((1, LANES), F32)],
        compiler_params=_params("arbitrary"),
        name="moe_router",
    )(x, rw, rb)


def _dispatch_kernel(pos_ref, x_ref, zeros_ref, xs_ref, sem):
    del zeros_ref
    tm = x_ref.shape[0]

    def copy(r, k):
        return pltpu.make_async_copy(x_ref.at[pl.ds(r, 1), :], xs_ref.at[pl.ds(pos_ref[0, 0, 2 * r + k], 1), :], sem)

    def start(r, _):
        copy(r, 0).start()
        copy(r, 1).start()
        return 0

    def wait(r, _):
        copy(r, 0).wait()
        copy(r, 1).wait()
        return 0

    lax.fori_loop(0, tm, start, 0, unroll=DMA_UNROLL)
    lax.fori_loop(0, tm, wait, 0, unroll=DMA_UNROLL)


def _dispatch(x, pos, n_rows, tm):
    m, width = x.shape
    pos3 = pos.reshape(m // tm, 1, 2 * tm)
    return pl.pallas_call(
        _dispatch_kernel,
        grid=(m // tm,),
        in_specs=[pl.BlockSpec((1, 1, 2 * tm), lambda i: (i, 0, 0), memory_space=pltpu.SMEM),
                  pl.BlockSpec((tm, width), lambda i: (i, 0)),
                  pl.BlockSpec(memory_space=pl.ANY)],
        out_specs=pl.BlockSpec(memory_space=pl.ANY),
        out_shape=jax.ShapeDtypeStruct((n_rows, width), x.dtype),
        scratch_shapes=[pltpu.SemaphoreType.DMA(())],
        input_output_aliases={2: 0},
        compiler_params=_params("arbitrary"),
        name="moe_dispatch",
    )(pos3, x, jnp.zeros((n_rows, width), x.dtype))


def _moe_up_kernel(te_ref, nu_ref, x_ref, wg_ref, wu_ref, o_ref, wgb_ref, wub_ref):
    t = pl.program_id(1)
    live = t < nu_ref[0]
    fresh = jnp.logical_or(t == 0, te_ref[t] != te_ref[jnp.maximum(t - 1, 0)])

    @pl.when(jnp.logical_and(live, fresh))
    def _():
        wgb_ref[...] = wg_ref[...].astype(BF16)
        wub_ref[...] = wu_ref[...].astype(BF16)

    @pl.when(live)
    def _():
        x = _unpack_halves(x_ref[...])
        o_ref[...] = _silu_mul(_dot(x, wgb_ref[...]), _dot(x, wub_ref[...])).astype(BF16)

    @pl.when(jnp.logical_not(live))
    def _():
        o_ref[...] = jnp.zeros_like(o_ref)


def _moe_up(xs, wg, wu, tile_expert, n_used, tm, tn):
    p = xs.shape[0]
    live = lambda t, nu: jnp.minimum(t, nu[0] - 1)
    return pl.pallas_call(
        _moe_up_kernel,
        grid_spec=pltpu.PrefetchScalarGridSpec(
            num_scalar_prefetch=2,
            grid=(D_FF // tn, p // tm),
            in_specs=[pl.BlockSpec((tm, D_MODEL // 2), lambda j, t, te, nu: (live(t, nu), 0)),
                      pl.BlockSpec((None, D_MODEL, tn), lambda j, t, te, nu: (te[live(t, nu)], 0, j)),
                      pl.BlockSpec((None, D_MODEL, tn), lambda j, t, te, nu: (te[live(t, nu)], 0, j))],
            out_specs=pl.BlockSpec((tm, tn), lambda j, t, te, nu: (t, j)),
            scratch_shapes=[pltpu.VMEM((D_MODEL, tn), BF16), pltpu.VMEM((D_MODEL, tn), BF16)],
        ),
        out_shape=jax.ShapeDtypeStruct((p, D_FF), BF16),
        compiler_params=_params("arbitrary", "arbitrary"),
        name="moe_up",
    )(tile_expert, n_used, xs, wg, wu)


def _moe_down_kernel(te_ref, nu_ref, h_ref, w_ref, o_ref, wb_ref):
    t = pl.program_id(1)
    live = t < nu_ref[0]
    fresh = jnp.logical_or(t == 0, te_ref[t] != te_ref[jnp.maximum(t - 1, 0)])

    @pl.when(jnp.logical_and(live, fresh))
    def _():
        wb_ref[...] = w_ref[...].astype(BF16)

    @pl.when(live)
    def _():
        o_ref[...] = _dot(h_ref[...], wb_ref[...])

    @pl.when(jnp.logical_not(live))
    def _():
        o_ref[...] = jnp.zeros_like(o_ref)


def _moe_down(hs, wd, tile_expert, n_used, tm, tn):
    p = hs.shape[0]
    live = lambda t, nu: jnp.minimum(t, nu[0] - 1)
    return pl.pallas_call(
        _moe_down_kernel,
        grid_spec=pltpu.PrefetchScalarGridSpec(
            num_scalar_prefetch=2,
            grid=(D_MODEL // tn, p // tm),
            in_specs=[pl.BlockSpec((tm, D_FF), lambda j, t, te, nu: (live(t, nu), 0)),
                      pl.BlockSpec((None, D_FF, tn), lambda j, t, te, nu: (te[live(t, nu)], 0, j))],
            out_specs=pl.BlockSpec((tm, tn), lambda j, t, te, nu: (t, j)),
            scratch_shapes=[pltpu.VMEM((D_FF, tn), BF16)],
        ),
        out_shape=jax.ShapeDtypeStruct((p, D_MODEL), F32),
        compiler_params=_params("arbitrary", "arbitrary"),
        name="moe_down",
    )(tile_expert, n_used, hs, wd)


def _combine_ln_kernel(pos_ref, ys_ref, x_ref, route_ref, g_ref, b_ref, o_ref, buf0, buf1, sem):
    tm = x_ref.shape[0]

    def copies(r):
        c0 = pltpu.make_async_copy(ys_ref.at[pl.ds(pos_ref[0, 0, 2 * r], 1), :], buf0.at[pl.ds(r, 1), :], sem)
        c1 = pltpu.make_async_copy(ys_ref.at[pl.ds(pos_ref[0, 0, 2 * r + 1], 1), :], buf1.at[pl.ds(r, 1), :], sem)
        return c0, c1

    def start(r, _):
        c0, c1 = copies(r)
        c0.start()
        c1.start()
        return 0

    def wait(r, _):
        c0, c1 = copies(r)
        c0.wait()
        c1.wait()
        return 0

    lax.fori_loop(0, tm, start, 0, unroll=DMA_UNROLL)
    lax.fori_loop(0, tm, wait, 0, unroll=DMA_UNROLL)
    route = route_ref[...]
    y = route[:, 2:3] * buf0[...] + route[:, 3:4] * buf1[...]
    o_ref[...] = _layer_norm(ALPHA * x_ref[...] + y, g_ref[...], b_ref[...])


def _combine_ln(ys, pos, x, route, g, b, tm):
    m = x.shape[0]
    pos3 = pos.reshape(m // tm, 1, 2 * tm)
    vec = pl.BlockSpec((1, D_MODEL), lambda i: (0, 0))
    return pl.pallas_call(
        _combine_ln_kernel,
        grid=(m // tm,),
        in_specs=[pl.BlockSpec((1, 1, 2 * tm), lambda i: (i, 0, 0), memory_space=pltpu.SMEM),
                  pl.BlockSpec(memory_space=pl.ANY),
                  pl.BlockSpec((tm, D_MODEL), lambda i: (i, 0)),
                  pl.BlockSpec((tm, LANES), lambda i: (i, 0)), vec, vec],
        out_specs=pl.BlockSpec((tm, D_MODEL), lambda i: (i, 0)),
        out_shape=jax.ShapeDtypeStruct((m, D_MODEL), F32),
        scratch_shapes=[pltpu.VMEM((tm, D_MODEL), F32), pltpu.VMEM((tm, D_MODEL), F32), pltpu.SemaphoreType.DMA(())],
        compiler_params=_params("arbitrary"),
        name="moe_combine_ln",
    )(pos3, ys, x, route, g, b)


def _rope_tables(seq):
    def angles(dim):
        inv_freq = 1.0 / (ROPE_THETA ** (jnp.arange(0, dim, 2, dtype=F32) / dim))
        ang = jnp.arange(seq, dtype=F32)[:, None] * inv_freq[None, :]
        return jnp.cos(ang), jnp.sin(ang)

    c128, s128 = angles(HEAD_DIM)
    cos_f = jnp.concatenate([c128, c128], axis=-1)
    sin_f = jnp.concatenate([-s128, s128], axis=-1)
    c64, s64 = angles(MLA_ROPE_DIM)
    z32 = jnp.zeros_like(c64)
    cos_t = jnp.concatenate([c64, c64, z32, z32], axis=-1)
    sin_a = jnp.concatenate([-s64, z32, z32, z32], axis=-1)
    sin_b = jnp.concatenate([z32, s64, z32, z32], axis=-1)
    return cos_f, sin_f, cos_t, sin_a, sin_b, c64.T, s64.T


def _pad_cols(a, width):
    return jnp.pad(a, ((0, 0), (0, width - a.shape[1])))


def _row(v):
    return v.reshape(1, -1).astype(F32)


def _even_layer(x, xb, batch, seq, tables, w_in, q_norm, w_q_b, kv_norm, w_kv_b, w_out, ln1_g, ln1_b,
                w_gate, w_up, w_down, ln2_g, ln2_b):
    cos_f, sin_f, cos_t, sin_a, sin_b, cos_tt, sin_tt = tables
    tm = min(512, seq)
    wa = jnp.concatenate([w_in[:, OFF_CKV:OFF_KROPE], _pad_cols(w_in[:, OFF_KROPE:OFF_DQ], LANES),
                          _pad_cols(w_in[:, OFF_CQ:OFF_CKV], MLA_Q_RANK_PAD)], axis=1).astype(BF16)
    qg = _pad_cols(_row(q_norm), MLA_Q_RANK_PAD)
    wq = jnp.pad(w_q_b.reshape(MLA_Q_RANK, MLA_HEADS, MLA_QK_DIM),
                 ((0, MLA_Q_RANK_PAD - MLA_Q_RANK), (0, 0), (0, MLA_QK_PAD - MLA_QK_DIM)))
    wqt = wq.reshape(MLA_Q_RANK_PAD, MLA_HEADS * MLA_QK_PAD).T.astype(BF16)
    wkv3 = w_kv_b.reshape(MLA_KV_RANK, MLA_HEADS, MLA_NOPE_DIM + MLA_V_DIM)
    wk = wkv3[:, :, :MLA_NOPE_DIM].reshape(MLA_KV_RANK, -1).astype(BF16)
    wvt = wkv3[:, :, MLA_NOPE_DIM:].reshape(MLA_KV_RANK, -1).T.astype(BF16)
    xb3 = xb.reshape(batch, seq, D_MODEL)
    qt_mla, k_mla, vt_mla = _mla_prep(xb3, wa, _row(kv_norm), qg, wqt, wk, wvt, cos_t, sin_a, sin_b,
                                      cos_tt, sin_tt, tm)
    tq = min(FLASH_TQ, seq)
    o_mla = _flash(qt_mla, k_mla, vt_mla, None, heads=MLA_HEADS, dq=MLA_QK_PAD, dk=MLA_QK_PAD, dv=MLA_V_DIM,
                   tq=tq, tk=min(FLASH_TK, tq))
    o_mla = o_mla.reshape(batch * seq, -1)

    dqkv = _dqkv(xb, w_in[:, OFF_DQ:].astype(BF16), cos_f, sin_f, seq, tm)
    outs, lses = [], []
    for g, (window, dil) in enumerate(DIL_PATTERNS):
        assert window == DIL_SPAN * dil
        o_g, lse_g = _dilated_group(dqkv, g, dil, batch, seq, min(256, seq // dil))
        outs.append(o_g)
        lses.append(lse_g)
    o_dil = _dil_merge(outs, lses, tm)

    n_mla = MLA_HEADS * MLA_V_DIM
    wo = w_out.astype(BF16)
    x1, x1b = _proj_ln([o_mla, o_dil], [wo[:n_mla], wo[n_mla:]], x, _row(ln1_g), _row(ln1_b), tm)
    hmid = _ffn_up(x1b, w_gate, w_up, tm, 512)
    return _ffn_down_ln(hmid, w_down.astype(BF16), x1, _row(ln2_g), _row(ln2_b), tm, 1408)


def _odd_layer(x, xb, batch, seq, w_qkv, w_f, b_f, w_out, ln1_g, ln1_b, router_w, router_b,
               exp_w_gate, exp_w_up, exp_w_down, ln2_g, ln2_b):
    m = batch * seq
    tm = min(512, seq)
    xb3 = xb.reshape(batch, seq, D_MODEL)
    wb = w_qkv.astype(BF16)
    qt = _proj_t(xb3, wb[:, :FOX_WIDTH].T, HEAD_DIM ** -0.5 * LOG2E, tm, 1024)
    k = _proj(xb, wb[:, FOX_WIDTH:2 * FOX_WIDTH], tm, 1024).reshape(batch, seq, FOX_WIDTH)
    vt = _proj_t(xb3, wb[:, 2 * FOX_WIDTH:].T, 1.0, tm, 1024)
    c = _fgate(xb3, _pad_cols(w_f, LANES).astype(BF16), _pad_cols(_row(b_f), LANES), min(256, seq))
    c_t = jnp.transpose(c[:, :, :FOX_HEADS], (0, 2, 1)).reshape(batch, FOX_HEADS, 1, seq)
    tq = min(FLASH_TQ, seq)
    o = _flash(qt, k, vt, c_t, heads=FOX_HEADS, dq=HEAD_DIM, dk=HEAD_DIM, dv=HEAD_DIM, tq=tq, tk=min(FLASH_TK, tq))
    x1, x1p = _proj_ln([o.reshape(m, -1)], [w_out.astype(BF16)], x, _row(ln1_g), _row(ln1_b), tm, packed=True)

    rb = jnp.full((1, LANES), NEG, F32).at[0, :N_EXPERTS].set(router_b.astype(F32))
    route, counts = _router(x1, _pad_cols(router_w, LANES), rb, min(256, seq))
    tile = 512
    n_tiles = (2 * m) // tile + N_EXPERTS
    cnt = counts[0, :N_EXPERTS].astype(jnp.int32)
    tiles_per = (cnt + tile - 1) // tile
    tile_end = jnp.cumsum(tiles_per)
    offset = (tile_end - tiles_per) * tile
    idx = route[:, 0:2].astype(jnp.int32)
    pos = (offset[idx] + route[:, 4:6].astype(jnp.int32)).reshape(-1)
    n_used = tile_end[-1:]
    tile_ids = jnp.arange(n_tiles, dtype=jnp.int32)
    tile_expert = jnp.minimum(jnp.sum((tile_end[None, :] <= tile_ids[:, None]).astype(jnp.int32), axis=1),
                              N_EXPERTS - 1)
    td = min(256, seq)
    xs = _dispatch(x1p, pos, n_tiles * tile, td)
    hs = _moe_up(xs, exp_w_gate, exp_w_up, tile_expert, n_used, tile, 512)
    ys = _moe_down(hs, exp_w_down, tile_expert, n_used, tile, 512)
    return _combine_ln(ys, pos, x1, route, _row(ln2_g), _row(ln2_b), td)


def kernel(x, ev_w_in, ev_q_norm, ev_w_q_b, ev_kv_norm, ev_w_kv_b, ev_w_out, ev_ln1_g, ev_ln1_b, ev_ffn_w_gate, ev_ffn_w_up, ev_ffn_w_down, ev_ln2_g, ev_ln2_b, od_w_qkv, od_w_f, od_b_f, od_w_out, od_ln1_g, od_ln1_b, od_router_w, od_router_b, od_exp_w_gate, od_exp_w_up, od_exp_w_down, od_ln2_g, od_ln2_b):
    batch, seq, _ = x.shape
    tables = _rope_tables(seq)
    h = x.reshape(batch * seq, D_MODEL)
    hb = h.astype(BF16)
    for layer in range(DEPTH):
        i = layer // 2
        if layer % 2 == 0:
            h, hb = _even_layer(h, hb, batch, seq, tables, ev_w_in[i], ev_q_norm[i], ev_w_q_b[i], ev_kv_norm[i],
                                ev_w_kv_b[i], ev_w_out[i], ev_ln1_g[i], ev_ln1_b[i], ev_ffn_w_gate[i],
                                ev_ffn_w_up[i], ev_ffn_w_down[i], ev_ln2_g[i], ev_ln2_b[i])
        else:
            h = _odd_layer(h, hb, batch, seq, od_w_qkv[i], od_w_f[i], od_b_f[i], od_w_out[i], od_ln1_g[i],
                           od_ln1_b[i], od_router_w[i], od_router_b[i], od_exp_w_gate[i], od_exp_w_up[i],
                           od_exp_w_down[i], od_ln2_g[i], od_ln2_b[i])
            hb = h.astype(BF16)
    return h.reshape(batch, seq, D_MODEL)
```

```python
import functools

import jax
import jax.numpy as jnp
from jax import lax
from jax.experimental import pallas as pl
from jax.experimental.pallas import tpu as pltpu

F32 = jnp.float32
BF16 = jnp.bfloat16

D_MODEL = 2048
HEAD_DIM = 128
LANES = 128
ROPE_THETA = 10000.0
LN_EPS = 1e-5
RMS_EPS = 1e-6

MLA_HEADS = 10
MLA_Q_RANK = 448
MLA_Q_RANK_PAD = 512
MLA_KV_RANK = 128
MLA_NOPE_DIM = 128
MLA_ROPE_DIM = 64
MLA_V_DIM = 128
MLA_QK_DIM = MLA_NOPE_DIM + MLA_ROPE_DIM
MLA_QK_PAD = 256

DIL_PATTERNS = ((128, 1), (512, 4), (2048, 16))
DIL_GROUPS = 3
DIL_HEADS = 6
DIL_WIDTH = DIL_HEADS * HEAD_DIM
DIL_SPAN = 128

OFF_CQ = 0
OFF_CKV = OFF_CQ + MLA_Q_RANK
OFF_KROPE = OFF_CKV + MLA_KV_RANK
OFF_DQ = OFF_KROPE + MLA_ROPE_DIM
OFF_DK = OFF_DQ + DIL_GROUPS * DIL_WIDTH
OFF_DV = OFF_DK + DIL_WIDTH
W_IN_COLS = OFF_DV + DIL_WIDTH

FOX_HEADS = 16
FOX_WIDTH = FOX_HEADS * HEAD_DIM

D_FF = 5632
N_EXPERTS = 8
DEPTH = 2
ALPHA = (2.0 * DEPTH) ** 0.25

NEG = -1e30
LOG2E = 1.4426950408889634
FLASH_TQ = 1024
FLASH_TK = 1024
DMA_UNROLL = 8
VMEM_LIMIT = 56 * 1024 * 1024


def _params(*sem, vmem=VMEM_LIMIT):
    return pltpu.CompilerParams(dimension_semantics=sem, vmem_limit_bytes=vmem)


def _dot(a, b):
    return jnp.dot(a, b, preferred_element_type=F32)


def _dot_nt(a, b):
    return lax.dot_general(a, b, (((1,), (1,)), ((), ())), preferred_element_type=F32)


def _layer_norm(y, g, b):
    mu = jnp.mean(y, axis=-1, keepdims=True)
    d = y - mu
    var = jnp.mean(d * d, axis=-1, keepdims=True)
    return d * lax.rsqrt(var + LN_EPS) * g + b


def _rope128(x, cos_f, sin_f):
    return x * cos_f + pltpu.roll(x, 64, 1) * sin_f


def _rope64(x, cos_t, sin_a, sin_b):
    return x * cos_t + pltpu.roll(x, 96, 1) * sin_a + pltpu.roll(x, 32, 1) * sin_b


def _mla_prep_kernel(x_ref, wa_ref, kvg_ref, qg_ref, wqt_ref, wk_ref, wvt_ref, cos_ref, sa_ref, sb_ref,
                     ct_ref, st_ref, qt_ref, k_ref, vt_ref):
    h = _dot(x_ref[...], wa_ref[...])
    ckv = h[:, 0:128]
    kr = h[:, 128:256]
    cq = h[:, 256:768]
    ckv_n = (ckv * lax.rsqrt(jnp.mean(ckv * ckv, axis=-1, keepdims=True) + RMS_EPS) * kvg_ref[...]).astype(BF16)
    cq_ms = jnp.sum(cq * cq, axis=-1, keepdims=True) * (1.0 / MLA_Q_RANK)
    cq_n = (cq * lax.rsqrt(cq_ms + RMS_EPS) * qg_ref[...]).astype(BF16)
    kr_r = _rope64(kr, cos_ref[...], sa_ref[...], sb_ref[...]).astype(BF16)
    scale = MLA_QK_DIM ** -0.5 * LOG2E
    qt = _dot_nt(wqt_ref[...], cq_n)
    c, s = ct_ref[...], st_ref[...]
    k_nope = _dot(ckv_n, wk_ref[...])
    for hd in range(MLA_HEADS):
        o = hd * MLA_QK_PAD
        qt_ref[o:o + 128, :] = (qt[o:o + 128] * scale).astype(BF16)
        x1, x2 = qt[o + 128:o + 160], qt[o + 160:o + 192]
        qt_ref[o + 128:o + 160, :] = ((x1 * c - x2 * s) * scale).astype(BF16)
        qt_ref[o + 160:o + 192, :] = ((x2 * c + x1 * s) * scale).astype(BF16)
        qt_ref[o + 192:o + 256, :] = jnp.zeros((64, qt.shape[1]), BF16)
        k_ref[:, o:o + 128] = k_nope[:, hd * 128:(hd + 1) * 128].astype(BF16)
        k_ref[:, o + 128:o + 256] = kr_r
    vt_ref[...] = _dot_nt(wvt_ref[...], ckv_n).astype(BF16)


def _mla_prep(xb3, wa, kvg, qg, wqt, wk, wvt, cos_t, sin_a, sin_b, cos_tt, sin_tt, tm):
    b, s, _ = xb3.shape
    full = lambda shape: pl.BlockSpec(shape, lambda bi, i: (0, 0))
    tab = pl.BlockSpec((tm, LANES), lambda bi, i: (i, 0))
    tab_t = pl.BlockSpec((MLA_ROPE_DIM // 2, tm), lambda bi, i: (0, i))
    wide = MLA_HEADS * MLA_QK_PAD
    vw = MLA_HEADS * MLA_V_DIM
    return pl.pallas_call(
        _mla_prep_kernel,
        grid=(b, s // tm),
        in_specs=[pl.BlockSpec((None, tm, D_MODEL), lambda bi, i: (bi, i, 0)), full(wa.shape), full(kvg.shape),
                  full(qg.shape), full(wqt.shape), full(wk.shape), full(wvt.shape), tab, tab, tab, tab_t, tab_t],
        out_specs=[pl.BlockSpec((None, wide, tm), lambda bi, i: (bi, 0, i)),
                   pl.BlockSpec((None, tm, wide), lambda bi, i: (bi, i, 0)),
                   pl.BlockSpec((None, vw, tm), lambda bi, i: (bi, 0, i))],
        out_shape=[jax.ShapeDtypeStruct((b, wide, s), BF16), jax.ShapeDtypeStruct((b, s, wide), BF16),
                   jax.ShapeDtypeStruct((b, vw, s), BF16)],
        compiler_params=_params("parallel", "parallel"),
        name="mla_prep",
    )(xb3, wa, kvg, qg, wqt, wk, wvt, cos_t, sin_a, sin_b, cos_tt, sin_tt)


def _dqkv_kernel(x_ref, w_ref, cos_ref, sin_ref, o_ref):
    j = pl.program_id(0)
    h = _dot(x_ref[...], w_ref[...])

    @pl.when(j < 4)
    def _():
        cos_f, sin_f = cos_ref[...], sin_ref[...]
        sc = jnp.where(j < 3, HEAD_DIM ** -0.5, 1.0).astype(F32)
        for hd in range(DIL_HEADS):
            sl = slice(hd * 128, (hd + 1) * 128)
            o_ref[:, sl] = (_rope128(h[:, sl], cos_f, sin_f) * sc).astype(BF16)

    @pl.when(j == 4)
    def _():
        o_ref[...] = h.astype(BF16)


def _dqkv(xb, wd, cos_f, sin_f, seq, tm):
    m = xb.shape[0]
    nrow = seq // tm
    n_col = wd.shape[1] // DIL_WIDTH
    tab = pl.BlockSpec((tm, LANES), lambda j, i: (i % nrow, 0))
    return pl.pallas_call(
        _dqkv_kernel,
        grid=(n_col, m // tm),
        in_specs=[pl.BlockSpec((tm, D_MODEL), lambda j, i: (i, 0)),
                  pl.BlockSpec((D_MODEL, DIL_WIDTH), lambda j, i: (0, j)), tab, tab],
        out_specs=pl.BlockSpec((tm, DIL_WIDTH), lambda j, i: (i, j)),
        out_shape=jax.ShapeDtypeStruct((m, wd.shape[1]), BF16),
        compiler_params=_params("parallel", "parallel"),
        name="dil_qkv",
    )(xb, wd, cos_f, sin_f)


def _flash_kernel(*refs, tq, tk, use_c):
    assert tq == tk
    if use_c:
        qt_ref, k_ref, vt_ref, c_ref, o_ref, acc_ref, s0_ref, s1_ref, kaug_ref = refs
    else:
        qt_ref, k_ref, vt_ref, o_ref, acc_ref, s0_ref, s1_ref = refs
    i = pl.program_id(2)
    seq = k_ref.shape[0]
    if use_c:
        @pl.when(i == 0)
        def _():
            row = lax.broadcasted_iota(jnp.int32, (LANES, tk), 0)

            def build(j, _):
                start = pl.multiple_of(j * tk, tk)
                neg = -LOG2E * c_ref[:, pl.ds(start, tk)]
                hi = neg.astype(BF16).astype(F32)
                mid = (neg - hi).astype(BF16).astype(F32)
                lo = neg - hi - mid
                blk = jnp.where(row == 0, hi, jnp.where(row == 1, mid, jnp.where(row == 2, lo, 0.0)))
                kaug_ref[pl.ds(start, tk), :] = blk.T.astype(BF16)
                return 0

            lax.fori_loop(0, seq // tk, build, 0)

    q = qt_ref[...]
    if use_c:
        ones = (lax.broadcasted_iota(jnp.int32, (LANES, tq), 0) < 3).astype(BF16)
        q = jnp.concatenate([q, ones], axis=0)
    acc_ref[...] = jnp.zeros_like(acc_ref)

    def scores(j, s_ref):
        start = pl.multiple_of(j * tk, tk)
        kt = k_ref[pl.ds(start, tk), :]
        if use_c:
            kt = jnp.concatenate([kt, kaug_ref[pl.ds(start, tk), :]], axis=1)
        s_ref[...] = _dot(kt, q)

    def update(j, s_ref, stats, masked):
        m, l = stats
        s = s_ref[...]
        if masked:
            on_diag = lax.broadcasted_iota(jnp.int32, (tk, tq), 0) <= lax.broadcasted_iota(jnp.int32, (tk, tq), 1)
            s = jnp.where(on_diag, s, NEG)
        m_new = jnp.maximum(m, jnp.max(s, axis=0, keepdims=True))
        a = jnp.exp2(m - m_new)
        p = jnp.exp2(s - m_new)
        l = a * l + jnp.sum(p, axis=0, keepdims=True)
        start = pl.multiple_of(j * tk, tk)
        acc_ref[...] = a * acc_ref[...] + _dot(vt_ref[:, pl.ds(start, tk)], p.astype(BF16))
        return m_new, l

    scores(0, s0_ref)

    def pair(jj, stats):
        scores(2 * jj + 1, s1_ref)
        stats = update(2 * jj, s0_ref, stats, False)
        scores(2 * jj + 2, s0_ref)
        return update(2 * jj + 1, s1_ref, stats, False)

    init = (jnp.full((1, tq), NEG, F32), jnp.zeros((1, tq), F32))
    stats = lax.fori_loop(0, i // 2, pair, init)

    def odd_tail(stats):
        scores(i, s1_ref)
        stats = update(i - 1, s0_ref, stats, False)
        return update(i, s1_ref, stats, True)

    def even_tail(stats):
        return update(i, s0_ref, stats, True)

    _, l = lax.cond(i % 2 == 1, odd_tail, even_tail, stats)
    o_ref[...] = (acc_ref[...] / l).T.astype(o_ref.dtype)


def _flash(qt, k, vt, c, *, heads, dq, dk, dv, tq, tk):
    b, s, _ = k.shape
    use_c = c is not None
    in_specs = [pl.BlockSpec((None, dq, tq), lambda bi, h, i: (bi, h, i)),
                pl.BlockSpec((None, s, dk), lambda bi, h, i: (bi, 0, h)),
                pl.BlockSpec((None, dv, s), lambda bi, h, i: (bi, h, 0))]
    args = [qt, k, vt]
    scratch = [pltpu.VMEM((dv, tq), F32), pltpu.VMEM((tk, tq), F32), pltpu.VMEM((tk, tq), F32)]
    if use_c:
        in_specs.append(pl.BlockSpec((None, None, 1, s), lambda bi, h, i: (bi, h, 0, 0)))
        args.append(c)
        scratch.append(pltpu.VMEM((s, LANES), BF16))
    return pl.pallas_call(
        functools.partial(_flash_kernel, tq=tq, tk=tk, use_c=use_c),
        grid=(b, heads, s // tq),
        in_specs=in_specs,
        out_specs=pl.BlockSpec((None, tq, dv), lambda bi, h, i: (bi, i, h)),
        out_shape=jax.ShapeDtypeStruct((b, s, heads * dv), BF16),
        scratch_shapes=scratch,
        compiler_params=_params("parallel", "parallel", "arbitrary"),
        name="flash_fox" if use_c else "flash_mla",
    )(*args)


def _dilated_kernel(q_ref, kc_ref, kp_ref, vc_ref, vp_ref, o_ref, lse_ref, *, tu):
    i = pl.program_id(2)
    nk = DIL_SPAN + tu
    a = lax.broadcasted_iota(jnp.int32, (tu, nk), 0)
    c = lax.broadcasted_iota(jnp.int32, (tu, nk), 1)
    back = a + DIL_SPAN - c
    first = jnp.where(i > 0, 0, DIL_SPAN)
    bias = jnp.where(back >= 0, jnp.where(back <= DIL_SPAN, jnp.where(c >= first, 0.0, NEG), NEG), NEG)
    lane = lax.broadcasted_iota(jnp.int32, (tu, LANES), 1)
    lse_all = jnp.zeros((tu, LANES), F32)
    for hd in range(DIL_HEADS):
        sl = slice(hd * 128, (hd + 1) * 128)
        k = jnp.concatenate([kp_ref[tu - DIL_SPAN:, sl], kc_ref[:, sl]], axis=0)
        v = jnp.concatenate([vp_ref[tu - DIL_SPAN:, sl], vc_ref[:, sl]], axis=0)
        s = _dot_nt(q_ref[:, sl], k) + bias
        m = jnp.max(s, axis=-1, keepdims=True)
        p = jnp.exp(s - m)
        l = jnp.sum(p, axis=-1, keepdims=True)
        o_ref[:, sl] = _dot(p.astype(BF16), v) / l
        lse_all = jnp.where(lane == hd, m + jnp.log(l), lse_all)
    lse_ref[...] = lse_all


def _dilated_group(dqkv, g, dil, batch, seq, tu):
    su = seq // dil
    nblk = dqkv.shape[1] // DIL_WIDTH
    view = dqkv.reshape(batch, su, dil * dqkv.shape[1])
    cur = lambda blk: pl.BlockSpec((None, tu, DIL_WIDTH), lambda b, r, i: (b, i, r * nblk + blk))
    prev = lambda blk: pl.BlockSpec((None, tu, DIL_WIDTH), lambda b, r, i: (b, jnp.maximum(i - 1, 0), r * nblk + blk))
    o, lse = pl.pallas_call(
        functools.partial(_dilated_kernel, tu=tu),
        grid=(batch, dil, su // tu),
        in_specs=[cur(g), cur(3), prev(3), cur(4), prev(4)],
        out_specs=[pl.BlockSpec((None, tu, DIL_WIDTH), lambda b, r, i: (b, i, r)),
                   pl.BlockSpec((None, tu, LANES), lambda b, r, i: (b, i, r))],
        out_shape=[jax.ShapeDtypeStruct((batch, su, dil * DIL_WIDTH), F32),
                   jax.ShapeDtypeStruct((batch, su, dil * LANES), F32)],
        compiler_params=_params("parallel", "parallel", "arbitrary"),
        name=f"dilated_{dil}",
    )(view, view, view, view, view)
    return o.reshape(batch * seq, DIL_WIDTH), lse.reshape(batch * seq, LANES)


def _dil_merge_kernel(o0, o1, o2, l0, l1, l2, out_ref):
    ls = [l0[...], l1[...], l2[...]]
    mx = jnp.maximum(jnp.maximum(ls[0], ls[1]), ls[2])
    es = [jnp.exp(x - mx) for x in ls]
    den = es[0] + es[1] + es[2]
    ws = [e / den for e in es]
    os_ = [o0, o1, o2]
    for hd in range(DIL_HEADS):
        sl = slice(hd * 128, (hd + 1) * 128)
        acc = ws[0][:, hd:hd + 1] * os_[0][:, sl]
        for g in (1, 2):
            acc = acc + ws[g][:, hd:hd + 1] * os_[g][:, sl]
        out_ref[:, sl] = acc.astype(BF16)


def _dil_merge(outs, lses, tm):
    m = outs[0].shape[0]
    ob = pl.BlockSpec((tm, DIL_WIDTH), lambda i: (i, 0))
    lb = pl.BlockSpec((tm, LANES), lambda i: (i, 0))
    return pl.pallas_call(
        _dil_merge_kernel,
        grid=(m // tm,),
        in_specs=[ob, ob, ob, lb, lb, lb],
        out_specs=ob,
        out_shape=jax.ShapeDtypeStruct((m, DIL_WIDTH), BF16),
        compiler_params=_params("parallel"),
        name="dil_merge",
    )(*outs, *lses)


def _pack_halves(x):
    bits = lax.bitcast_convert_type(x.astype(BF16).astype(F32), jnp.uint32)
    n = x.shape[1] // 2
    return (bits[:, :n] >> 16) | (bits[:, n:] & jnp.uint32(0xFFFF0000))


def _unpack_halves(w):
    lo = lax.bitcast_convert_type(w << 16, F32)
    hi = lax.bitcast_convert_type(w & jnp.uint32(0xFFFF0000), F32)
    return jnp.concatenate([lo, hi], axis=1).astype(BF16)


def _proj_ln_kernel(*refs, n_lhs, packed):
    lhs = refs[:n_lhs]
    ws = refs[n_lhs:2 * n_lhs]
    x_ref, g_ref, b_ref, o_ref, ob_ref = refs[2 * n_lhs:]
    y = _dot(lhs[0][...], ws[0][...])
    for a, w in zip(lhs[1:], ws[1:]):
        y = y + _dot(a[...], w[...])
    out = _layer_norm(ALPHA * x_ref[...] + y, g_ref[...], b_ref[...])
    o_ref[...] = out
    ob_ref[...] = _pack_halves(out) if packed else out.astype(BF16)


def _proj_ln(lhs, ws, x, g, b, tm, packed=False):
    m = x.shape[0]
    row = lambda width: pl.BlockSpec((tm, width), lambda i: (i, 0))
    full = lambda shape: pl.BlockSpec(shape, lambda i: (0, 0))
    second = ((m, D_MODEL // 2), jnp.uint32) if packed else ((m, D_MODEL), BF16)
    return pl.pallas_call(
        functools.partial(_proj_ln_kernel, n_lhs=len(lhs), packed=packed),
        grid=(m // tm,),
        in_specs=[row(a.shape[1]) for a in lhs] + [full(w.shape) for w in ws]
        + [row(D_MODEL), full(g.shape), full(b.shape)],
        out_specs=[row(D_MODEL), row(second[0][1])],
        out_shape=[jax.ShapeDtypeStruct((m, D_MODEL), F32), jax.ShapeDtypeStruct(*second)],
        compiler_params=_params("parallel"),
        name="proj_ln",
    )(*lhs, *ws, x, g, b)


def _silu_mul(g, u):
    return g * (1.0 / (1.0 + jnp.exp(-g))) * u


def _ffn_up_kernel(x_ref, wg_ref, wu_ref, o_ref, wgb_ref, wub_ref):
    @pl.when(pl.program_id(1) == 0)
    def _():
        wgb_ref[...] = wg_ref[...].astype(BF16)
        wub_ref[...] = wu_ref[...].astype(BF16)

    x = x_ref[...]
    o_ref[...] = _silu_mul(_dot(x, wgb_ref[...]), _dot(x, wub_ref[...])).astype(BF16)


def _ffn_up(xb, wg, wu, tm, tn):
    m = xb.shape[0]
    return pl.pallas_call(
        _ffn_up_kernel,
        grid=(D_FF // tn, m // tm),
        in_specs=[pl.BlockSpec((tm, D_MODEL), lambda j, i: (i, 0)),
                  pl.BlockSpec((D_MODEL, tn), lambda j, i: (0, j)),
                  pl.BlockSpec((D_MODEL, tn), lambda j, i: (0, j))],
        out_specs=pl.BlockSpec((tm, tn), lambda j, i: (i, j)),
        out_shape=jax.ShapeDtypeStruct((m, D_FF), BF16),
        scratch_shapes=[pltpu.VMEM((D_MODEL, tn), BF16), pltpu.VMEM((D_MODEL, tn), BF16)],
        compiler_params=_params("arbitrary", "arbitrary"),
        name="ffn_up",
    )(xb, wg, wu)


def _ffn_down_ln_kernel(h_ref, w_ref, x_ref, g_ref, b_ref, o_ref, ob_ref, acc_ref):
    k = pl.program_id(1)

    @pl.when(k == 0)
    def _():
        acc_ref[...] = jnp.zeros_like(acc_ref)

    acc_ref[...] += _dot(h_ref[...], w_ref[...])

    @pl.when(k == pl.num_programs(1) - 1)
    def _():
        out = _layer_norm(ALPHA * x_ref[...] + acc_ref[...], g_ref[...], b_ref[...])
        o_ref[...] = out
        ob_ref[...] = out.astype(BF16)


def _ffn_down_ln(h, wd, x, g, b, tm, tk):
    m = x.shape[0]
    row = pl.BlockSpec((tm, D_MODEL), lambda i, k: (i, 0))
    vec = pl.BlockSpec((1, D_MODEL), lambda i, k: (0, 0))
    return pl.pallas_call(
        _ffn_down_ln_kernel,
        grid=(m // tm, D_FF // tk),
        in_specs=[pl.BlockSpec((tm, tk), lambda i, k: (i, k)), pl.BlockSpec((tk, D_MODEL), lambda i, k: (k, 0)),
                  row, vec, vec],
        out_specs=[row, row],
        out_shape=[jax.ShapeDtypeStruct((m, D_MODEL), F32), jax.ShapeDtypeStruct((m, D_MODEL), BF16)],
        scratch_shapes=[pltpu.VMEM((tm, D_MODEL), F32)],
        compiler_params=_params("parallel", "arbitrary"),
        name="ffn_down_ln",
    )(h, wd, x, g, b)


def _proj_kernel(x_ref, w_ref, o_ref):
    o_ref[...] = _dot(x_ref[...], w_ref[...]).astype(BF16)


def _proj(xb, w, tm, tn):
    m = xb.shape[0]
    n = w.shape[1]
    return pl.pallas_call(
        _proj_kernel,
        grid=(n // tn, m // tm),
        in_specs=[pl.BlockSpec((tm, D_MODEL), lambda j, i: (i, 0)), pl.BlockSpec((D_MODEL, tn), lambda j, i: (0, j))],
        out_specs=pl.BlockSpec((tm, tn), lambda j, i: (i, j)),
        out_shape=jax.ShapeDtypeStruct((m, n), BF16),
        compiler_params=_params("parallel", "parallel"),
        name="proj",
    )(xb, w)


def _proj_t_kernel(w_ref, x_ref, o_ref, *, scale):
    o_ref[...] = (_dot_nt(w_ref[...], x_ref[...]) * scale).astype(BF16)


def _proj_t(xb3, wt, scale, tm, tn):
    b, s, _ = xb3.shape
    n = wt.shape[0]
    return pl.pallas_call(
        functools.partial(_proj_t_kernel, scale=scale),
        grid=(n // tn, b, s // tm),
        in_specs=[pl.BlockSpec((tn, D_MODEL), lambda j, bi, i: (j, 0)),
                  pl.BlockSpec((None, tm, D_MODEL), lambda j, bi, i: (bi, i, 0))],
        out_specs=pl.BlockSpec((None, tn, tm), lambda j, bi, i: (bi, j, i)),
        out_shape=jax.ShapeDtypeStruct((b, n, s), BF16),
        compiler_params=_params("parallel", "parallel", "parallel"),
        name="proj_t",
    )(wt, xb3)


def _fgate_kernel(x_ref, w_ref, b_ref, c_ref, carry_ref):
    @pl.when(pl.program_id(1) == 0)
    def _():
        carry_ref[...] = jnp.zeros_like(carry_ref)

    z = _dot(x_ref[...], w_ref[...]) + b_ref[...]
    log_f = jnp.minimum(z, 0.0) - jnp.log(1.0 + jnp.exp(-jnp.abs(z)))
    tm = z.shape[0]
    tri = (lax.broadcasted_iota(jnp.int32, (tm, tm), 1) <= lax.broadcasted_iota(jnp.int32, (tm, tm), 0)).astype(F32)
    c = jnp.dot(tri, log_f, preferred_element_type=F32, precision=lax.Precision.HIGHEST) + carry_ref[...]
    c_ref[...] = c
    carry_ref[...] = c[tm - 1:tm, :]


def _fgate(xb3, wf, bf, tm):
    b, s, _ = xb3.shape
    return pl.pallas_call(
        _fgate_kernel,
        grid=(b, s // tm),
        in_specs=[pl.BlockSpec((None, tm, D_MODEL), lambda bi, i: (bi, i, 0)),
                  pl.BlockSpec(wf.shape, lambda bi, i: (0, 0)), pl.BlockSpec(bf.shape, lambda bi, i: (0, 0))],
        out_specs=pl.BlockSpec((None, tm, LANES), lambda bi, i: (bi, i, 0)),
        out_shape=jax.ShapeDtypeStruct((b, s, LANES), F32),
        scratch_shapes=[pltpu.VMEM((1, LANES), F32)],
        compiler_params=_params("parallel", "arbitrary"),
        name="fox_gate",
    )(xb3, wf, bf)


def _router_kernel(x_ref, w_ref, b_ref, route_ref, cnt_ref, carry_ref):
    @pl.when(pl.program_id(0) == 0)
    def _():
        carry_ref[...] = jnp.zeros_like(carry_ref)

    logits = jnp.dot(x_ref[...], w_ref[...], preferred_element_type=F32, precision=lax.Precision.HIGHEST) + b_ref[...]
    tm = logits.shape[0]
    lane = lax.broadcasted_iota(jnp.int32, (tm, LANES), 1)
    l1 = jnp.max(logits, axis=-1, keepdims=True)
    i1 = jnp.min(jnp.where(logits == l1, lane, LANES), axis=-1, keepdims=True)
    rest = jnp.where(lane == i1, NEG, logits)
    l2 = jnp.max(rest, axis=-1, keepdims=True)
    i2 = jnp.min(jnp.where(rest == l2, lane, LANES), axis=-1, keepdims=True)
    e = jnp.exp(l2 - l1)
    w1 = 1.0 / (1.0 + e)
    w2 = e / (1.0 + e)
    hot1 = (lane == i1).astype(F32)
    hot2 = (lane == i2).astype(F32)
    cnt = hot1 + hot2
    strict = (lax.broadcasted_iota(jnp.int32, (tm, tm), 1) < lax.broadcasted_iota(jnp.int32, (tm, tm), 0)).astype(BF16)
    before = _dot(strict, cnt.astype(BF16)) + carry_ref[...]
    r1 = jnp.sum(before * hot1, axis=-1, keepdims=True)
    r2 = jnp.sum(before * hot2, axis=-1, keepdims=True)
    vals = (i1.astype(F32), i2.astype(F32), w1, w2, r1, r2)
    route = jnp.zeros((tm, LANES), F32)
    for idx, val in enumerate(vals):
        route = jnp.where(lane == idx, val, route)
    route_ref[...] = route
    total = carry_ref[...] + jnp.sum(cnt, axis=0, keepdims=True)
    carry_ref[...] = total
    cnt_ref[...] = jnp.broadcast_to(total, cnt_ref.shape)


def _router(x, rw, rb, tm):
    m = x.shape[0]
    return pl.pallas_call(
        _router_kernel,
        grid=(m // tm,),
        in_specs=[pl.BlockSpec((tm, D_MODEL), lambda i: (i, 0)), pl.BlockSpec(rw.shape, lambda i: (0, 0)),
                  pl.BlockSpec(rb.shape, lambda i: (0, 0))],
        out_specs=[pl.BlockSpec((tm, LANES), lambda i: (i, 0)), pl.BlockSpec((8, LANES), lambda i: (0, 0))],
        out_shape=[jax.ShapeDtypeStruct((m, LANES), F32), jax.ShapeDtypeStruct((8, LANES), F32)],
        scratch_shapes=[pltpu.VMEM((1, LANES), F32)],
        compiler_params=_params("arbitrary"),
        name="moe_router",
    )(x, rw, rb)


def _dispatch_kernel(pos_ref, x_ref, zeros_ref, xs_ref, sem):
    del zeros_ref
    tm = x_ref.shape[0]

    def copy(r, k):
        return pltpu.make_async_copy(x_ref.at[pl.ds(r, 1), :], xs_ref.at[pl.ds(pos_ref[0, 0, 2 * r + k], 1), :], sem)

    def start(r, _):
        copy(r, 0).start()
        copy(r, 1).start()
        return 0

    def wait(r, _):
        copy(r, 0).wait()
        copy(r, 1).wait()
        return 0

    lax.fori_loop(0, tm, start, 0, unroll=DMA_UNROLL)
    lax.fori_loop(0, tm, wait, 0, unroll=DMA_UNROLL)


def _dispatch(x, pos, n_rows, tm):
    m, width = x.shape
    pos3 = pos.reshape(m // tm, 1, 2 * tm)
    return pl.pallas_call(
        _dispatch_kernel,
        grid=(m // tm,),
        in_specs=[pl.BlockSpec((1, 1, 2 * tm), lambda i: (i, 0, 0), memory_space=pltpu.SMEM),
                  pl.BlockSpec((tm, width), lambda i: (i, 0)),
                  pl.BlockSpec(memory_space=pl.ANY)],
        out_specs=pl.BlockSpec(memory_space=pl.ANY),
        out_shape=jax.ShapeDtypeStruct((n_rows, width), x.dtype),
        scratch_shapes=[pltpu.SemaphoreType.DMA(())],
        input_output_aliases={2: 0},
        compiler_params=_params("arbitrary"),
        name="moe_dispatch",
    )(pos3, x, jnp.zeros((n_rows, width), x.dtype))


def _moe_up_kernel(te_ref, nu_ref, x_ref, wg_ref, wu_ref, o_ref, wgb_ref, wub_ref):
    t = pl.program_id(1)
    live = t < nu_ref[0]
    fresh = jnp.logical_or(t == 0, te_ref[t] != te_ref[jnp.maximum(t - 1, 0)])

    @pl.when(jnp.logical_and(live, fresh))
    def _():
        wgb_ref[...] = wg_ref[...].astype(BF16)
        wub_ref[...] = wu_ref[...].astype(BF16)

    @pl.when(live)
    def _():
        x = _unpack_halves(x_ref[...])
        o_ref[...] = _silu_mul(_dot(x, wgb_ref[...]), _dot(x, wub_ref[...])).astype(BF16)

    @pl.when(jnp.logical_not(live))
    def _():
        o_ref[...] = jnp.zeros_like(o_ref)


def _moe_up(xs, wg, wu, tile_expert, n_used, tm, tn):
    p = xs.shape[0]
    live = lambda t, nu: jnp.minimum(t, nu[0] - 1)
    return pl.pallas_call(
        _moe_up_kernel,
        grid_spec=pltpu.PrefetchScalarGridSpec(
            num_scalar_prefetch=2,
            grid=(D_FF // tn, p // tm),
            in_specs=[pl.BlockSpec((tm, D_MODEL // 2), lambda j, t, te, nu: (live(t, nu), 0)),
                      pl.BlockSpec((None, D_MODEL, tn), lambda j, t, te, nu: (te[live(t, nu)], 0, j)),
                      pl.BlockSpec((None, D_MODEL, tn), lambda j, t, te, nu: (te[live(t, nu)], 0, j))],
            out_specs=pl.BlockSpec((tm, tn), lambda j, t, te, nu: (t, j)),
            scratch_shapes=[pltpu.VMEM((D_MODEL, tn), BF16), pltpu.VMEM((D_MODEL, tn), BF16)],
        ),
        out_shape=jax.ShapeDtypeStruct((p, D_FF), BF16),
        compiler_params=_params("arbitrary", "arbitrary"),
        name="moe_up",
    )(tile_expert, n_used, xs, wg, wu)


def _moe_down_kernel(te_ref, nu_ref, h_ref, w_ref, o_ref, wb_ref):
    t = pl.program_id(1)
    live = t < nu_ref[0]
    fresh = jnp.logical_or(t == 0, te_ref[t] != te_ref[jnp.maximum(t - 1, 0)])

    @pl.when(jnp.logical_and(live, fresh))
    def _():
        wb_ref[...] = w_ref[...].astype(BF16)

    @pl.when(live)
    def _():
        o_ref[...] = _dot(h_ref[...], wb_ref[...])

    @pl.when(jnp.logical_not(live))
    def _():
        o_ref[...] = jnp.zeros_like(o_ref)


def _moe_down(hs, wd, tile_expert, n_used, tm, tn):
    p = hs.shape[0]
    live = lambda t, nu: jnp.minimum(t, nu[0] - 1)
    return pl.pallas_call(
        _moe_down_kernel,
        grid_spec=pltpu.PrefetchScalarGridSpec(
            num_scalar_prefetch=2,
            grid=(D_MODEL // tn, p // tm),
            in_specs=[pl.BlockSpec((tm, D_FF), lambda j, t, te, nu: (live(t, nu), 0)),
                      pl.BlockSpec((None, D_FF, tn), lambda j, t, te, nu: (te[live(t, nu)], 0, j))],
            out_specs=pl.BlockSpec((tm, tn), lambda j, t, te, nu: (t, j)),
            scratch_shapes=[pltpu.VMEM((D_FF, tn), BF16)],
        ),
        out_shape=jax.ShapeDtypeStruct((p, D_MODEL), F32),
        compiler_params=_params("arbitrary", "arbitrary"),
        name="moe_down",
    )(tile_expert, n_used, hs, wd)


def _combine_ln_kernel(pos_ref, ys_ref, x_ref, route_ref, g_ref, b_ref, o_ref, buf0, buf1, sem):
    tm = x_ref.shape[0]

    def copies(r):
        c0 = pltpu.make_async_copy(ys_ref.at[pl.ds(pos_ref[0, 0, 2 * r], 1), :], buf0.at[pl.ds(r, 1), :], sem)
        c1 = pltpu.make_async_copy(ys_ref.at[pl.ds(pos_ref[0, 0, 2 * r + 1], 1), :], buf1.at[pl.ds(r, 1), :], sem)
        return c0, c1

    def start(r, _):
        c0, c1 = copies(r)
        c0.start()
        c1.start()
        return 0

    def wait(r, _):
        c0, c1 = copies(r)
        c0.wait()
        c1.wait()
        return 0

    lax.fori_loop(0, tm, start, 0, unroll=DMA_UNROLL)
    lax.fori_loop(0, tm, wait, 0, unroll=DMA_UNROLL)
    route = route_ref[...]
    y = route[:, 2:3] * buf0[...] + route[:, 3:4] * buf1[...]
    o_ref[...] = _layer_norm(ALPHA * x_ref[...] + y, g_ref[...], b_ref[...])


def _combine_ln(ys, pos, x, route, g, b, tm):
    m = x.shape[0]
    pos3 = pos.reshape(m // tm, 1, 2 * tm)
    vec = pl.BlockSpec((1, D_MODEL), lambda i: (0, 0))
    return pl.pallas_call(
        _combine_ln_kernel,
        grid=(m // tm,),
        in_specs=[pl.BlockSpec((1, 1, 2 * tm), lambda i: (i, 0, 0), memory_space=pltpu.SMEM),
                  pl.BlockSpec(memory_space=pl.ANY),
                  pl.BlockSpec((tm, D_MODEL), lambda i: (i, 0)),
                  pl.BlockSpec((tm, LANES), lambda i: (i, 0)), vec, vec],
        out_specs=pl.BlockSpec((tm, D_MODEL), lambda i: (i, 0)),
        out_shape=jax.ShapeDtypeStruct((m, D_MODEL), F32),
        scratch_shapes=[pltpu.VMEM((tm, D_MODEL), F32), pltpu.VMEM((tm, D_MODEL), F32), pltpu.SemaphoreType.DMA(())],
        compiler_params=_params("arbitrary"),
        name="moe_combine_ln",
    )(pos3, ys, x, route, g, b)


def _rope_tables(seq):
    def angles(dim):
        inv_freq = 1.0 / (ROPE_THETA ** (jnp.arange(0, dim, 2, dtype=F32) / dim))
        ang = jnp.arange(seq, dtype=F32)[:, None] * inv_freq[None, :]
        return jnp.cos(ang), jnp.sin(ang)

    c128, s128 = angles(HEAD_DIM)
    cos_f = jnp.concatenate([c128, c128], axis=-1)
    sin_f = jnp.concatenate([-s128, s128], axis=-1)
    c64, s64 = angles(MLA_ROPE_DIM)
    z32 = jnp.zeros_like(c64)
    cos_t = jnp.concatenate([c64, c64, z32, z32], axis=-1)
    sin_a = jnp.concatenate([-s64, z32, z32, z32], axis=-1)
    sin_b = jnp.concatenate([z32, s64, z32, z32], axis=-1)
    return cos_f, sin_f, cos_t, sin_a, sin_b, c64.T, s64.T


def _pad_cols(a, width):
    return jnp.pad(a, ((0, 0), (0, width - a.shape[1])))


def _row(v):
    return v.reshape(1, -1).astype(F32)


def _even_layer(x, xb, batch, seq, tables, w_in, q_norm, w_q_b, kv_norm, w_kv_b, w_out, ln1_g, ln1_b,
                w_gate, w_up, w_down, ln2_g, ln2_b):
    cos_f, sin_f, cos_t, sin_a, sin_b, cos_tt, sin_tt = tables
    tm = min(512, seq)
    wa = jnp.concatenate([w_in[:, OFF_CKV:OFF_KROPE], _pad_cols(w_in[:, OFF_KROPE:OFF_DQ], LANES),
                          _pad_cols(w_in[:, OFF_CQ:OFF_CKV], MLA_Q_RANK_PAD)], axis=1).astype(BF16)
    qg = _pad_cols(_row(q_norm), MLA_Q_RANK_PAD)
    wq = jnp.pad(w_q_b.reshape(MLA_Q_RANK, MLA_HEADS, MLA_QK_DIM),
                 ((0, MLA_Q_RANK_PAD - MLA_Q_RANK), (0, 0), (0, MLA_QK_PAD - MLA_QK_DIM)))
    wqt = wq.reshape(MLA_Q_RANK_PAD, MLA_HEADS * MLA_QK_PAD).T.astype(BF16)
    wkv3 = w_kv_b.reshape(MLA_KV_RANK, MLA_HEADS, MLA_NOPE_DIM + MLA_V_DIM)
    wk = wkv3[:, :, :MLA_NOPE_DIM].reshape(MLA_KV_RANK, -1).astype(BF16)
    wvt = wkv3[:, :, MLA_NOPE_DIM:].reshape(MLA_KV_RANK, -1).T.astype(BF16)
    xb3 = xb.reshape(batch, seq, D_MODEL)
    qt_mla, k_mla, vt_mla = _mla_prep(xb3, wa, _row(kv_norm), qg, wqt, wk, wvt, cos_t, sin_a, sin_b,
                                      cos_tt, sin_tt, tm)
    tq = min(FLASH_TQ, seq)
    o_mla = _flash(qt_mla, k_mla, vt_mla, None, heads=MLA_HEADS, dq=MLA_QK_PAD, dk=MLA_QK_PAD, dv=MLA_V_DIM,
                   tq=tq, tk=min(FLASH_TK, tq))
    o_mla = o_mla.reshape(batch * seq, -1)

    dqkv = _dqkv(xb, w_in[:, OFF_DQ:].astype(BF16), cos_f, sin_f, seq, tm)
    outs, lses = [], []
    for g, (window, dil) in enumerate(DIL_PATTERNS):
        assert window == DIL_SPAN * dil
        o_g, lse_g = _dilated_group(dqkv, g, dil, batch, seq, min(256, seq // dil))
        outs.append(o_g)
        lses.append(lse_g)
    o_dil = _dil_merge(outs, lses, tm)

    n_mla = MLA_HEADS * MLA_V_DIM
    wo = w_out.astype(BF16)
    x1, x1b = _proj_ln([o_mla, o_dil], [wo[:n_mla], wo[n_mla:]], x, _row(ln1_g), _row(ln1_b), tm)
    hmid = _ffn_up(x1b, w_gate, w_up, min(1024, seq), 512)
    return _ffn_down_ln(hmid, w_down.astype(BF16), x1, _row(ln2_g), _row(ln2_b), tm, 1408)


def _odd_layer(x, xb, batch, seq, w_qkv, w_f, b_f, w_out, ln1_g, ln1_b, router_w, router_b,
               exp_w_gate, exp_w_up, exp_w_down, ln2_g, ln2_b):
    m = batch * seq
    tm = min(512, seq)
    xb3 = xb.reshape(batch, seq, D_MODEL)
    wb = w_qkv.astype(BF16)
    qt = _proj_t(xb3, wb[:, :FOX_WIDTH].T, HEAD_DIM ** -0.5 * LOG2E, tm, 1024)
    k = _proj(xb, wb[:, FOX_WIDTH:2 * FOX_WIDTH], tm, 1024).reshape(batch, seq, FOX_WIDTH)
    vt = _proj_t(xb3, wb[:, 2 * FOX_WIDTH:].T, 1.0, tm, 1024)
    c = _fgate(xb3, _pad_cols(w_f, LANES).astype(BF16), _pad_cols(_row(b_f), LANES), min(256, seq))
    c_t = jnp.transpose(c[:, :, :FOX_HEADS], (0, 2, 1)).reshape(batch, FOX_HEADS, 1, seq)
    tq = min(FLASH_TQ, seq)
    o = _flash(qt, k, vt, c_t, heads=FOX_HEADS, dq=HEAD_DIM, dk=HEAD_DIM, dv=HEAD_DIM, tq=tq, tk=min(FLASH_TK, tq))
    x1, x1p = _proj_ln([o.reshape(m, -1)], [w_out.astype(BF16)], x, _row(ln1_g), _row(ln1_b), tm, packed=True)

    rb = jnp.full((1, LANES), NEG, F32).at[0, :N_EXPERTS].set(router_b.astype(F32))
    route, counts = _router(x1, _pad_cols(router_w, LANES), rb, min(256, seq))
    tile = 512
    n_tiles = (2 * m) // tile + N_EXPERTS
    cnt = counts[0, :N_EXPERTS].astype(jnp.int32)
    tiles_per = (cnt + tile - 1) // tile
    tile_end = jnp.cumsum(tiles_per)
    offset = (tile_end - tiles_per) * tile
    idx = route[:, 0:2].astype(jnp.int32)
    pos = (offset[idx] + route[:, 4:6].astype(jnp.int32)).reshape(-1)
    n_used = tile_end[-1:]
    tile_ids = jnp.arange(n_tiles, dtype=jnp.int32)
    tile_expert = jnp.minimum(jnp.sum((tile_end[None, :] <= tile_ids[:, None]).astype(jnp.int32), axis=1),
                              N_EXPERTS - 1)
    td = min(256, seq)
    xs = _dispatch(x1p, pos, n_tiles * tile, td)
    hs = _moe_up(xs, exp_w_gate, exp_w_up, tile_expert, n_used, tile, 512)
    ys = _moe_down(hs, exp_w_down, tile_expert, n_used, tile, 512)
    return _combine_ln(ys, pos, x1, route, _row(ln2_g), _row(ln2_b), td)


def kernel(x, ev_w_in, ev_q_norm, ev_w_q_b, ev_kv_norm, ev_w_kv_b, ev_w_out, ev_ln1_g, ev_ln1_b, ev_ffn_w_gate, ev_ffn_w_up, ev_ffn_w_down, ev_ln2_g, ev_ln2_b, od_w_qkv, od_w_f, od_b_f, od_w_out, od_ln1_g, od_ln1_b, od_router_w, od_router_b, od_exp_w_gate, od_exp_w_up, od_exp_w_down, od_ln2_g, od_ln2_b):
    batch, seq, _ = x.shape
    tables = _rope_tables(seq)
    h = x.reshape(batch * seq, D_MODEL)
    hb = h.astype(BF16)
    for layer in range(DEPTH):
        i = layer // 2
        if layer % 2 == 0:
            h, hb = _even_layer(h, hb, batch, seq, tables, ev_w_in[i], ev_q_norm[i], ev_w_q_b[i], ev_kv_norm[i],
                                ev_w_kv_b[i], ev_w_out[i], ev_ln1_g[i], ev_ln1_b[i], ev_ffn_w_gate[i],
                                ev_ffn_w_up[i], ev_ffn_w_down[i], ev_ln2_g[i], ev_ln2_b[i])
        else:
            h = _odd_layer(h, hb, batch, seq, od_w_qkv[i], od_w_f[i], od_b_f[i], od_w_out[i], od_ln1_g[i],
                           od_ln1_b[i], od_router_w[i], od_router_b[i], od_exp_w_gate[i], od_exp_w_up[i],
                           od_exp_w_down[i], od_ln2_g[i], od_ln2_b[i])
            hb = h.astype(BF16)
    return h.reshape(batch, seq, D_MODEL)
```

```python
import functools

import jax
import jax.numpy as jnp
from jax import lax
from jax.experimental import pallas as pl
from jax.experimental.pallas import tpu as pltpu

F32 = jnp.float32
BF16 = jnp.bfloat16

D_MODEL = 2048
HEAD_DIM = 128
LANES = 128
ROPE_THETA = 10000.0
LN_EPS = 1e-5
RMS_EPS = 1e-6

MLA_HEADS = 10
MLA_Q_RANK = 448
MLA_Q_RANK_PAD = 512
MLA_KV_RANK = 128
MLA_NOPE_DIM = 128
MLA_ROPE_DIM = 64
MLA_V_DIM = 128
MLA_QK_DIM = MLA_NOPE_DIM + MLA_ROPE_DIM
MLA_QK_PAD = 256

DIL_PATTERNS = ((128, 1), (512, 4), (2048, 16))
DIL_GROUPS = 3
DIL_HEADS = 6
DIL_WIDTH = DIL_HEADS * HEAD_DIM
DIL_SPAN = 128

OFF_CQ = 0
OFF_CKV = OFF_CQ + MLA_Q_RANK
OFF_KROPE = OFF_CKV + MLA_KV_RANK
OFF_DQ = OFF_KROPE + MLA_ROPE_DIM
OFF_DK = OFF_DQ + DIL_GROUPS * DIL_WIDTH
OFF_DV = OFF_DK + DIL_WIDTH
W_IN_COLS = OFF_DV + DIL_WIDTH

FOX_HEADS = 16
FOX_WIDTH = FOX_HEADS * HEAD_DIM

D_FF = 5632
N_EXPERTS = 8
DEPTH = 2
ALPHA = (2.0 * DEPTH) ** 0.25

NEG = -1e30
LOG2E = 1.4426950408889634
FLASH_TQ = 1024
MOE_TILE = 1024
MOE_DOWN_TILE = 512
DMA_UNROLL = 8
VMEM_LIMIT = 56 * 1024 * 1024


def _params(*sem, vmem=VMEM_LIMIT):
    return pltpu.CompilerParams(dimension_semantics=sem, vmem_limit_bytes=vmem)


def _dot(a, b):
    return jnp.dot(a, b, preferred_element_type=F32)


def _dot_nt(a, b):
    return lax.dot_general(a, b, (((1,), (1,)), ((), ())), preferred_element_type=F32)


def _layer_norm(y, g, b):
    mu = jnp.mean(y, axis=-1, keepdims=True)
    d = y - mu
    var = jnp.mean(d * d, axis=-1, keepdims=True)
    return d * lax.rsqrt(var + LN_EPS) * g + b


def _rope128(x, cos_f, sin_f):
    return x * cos_f + pltpu.roll(x, 64, 1) * sin_f


def _rope64(x, cos_t, sin_a, sin_b):
    return x * cos_t + pltpu.roll(x, 96, 1) * sin_a + pltpu.roll(x, 32, 1) * sin_b


def _mla_prep_kernel(x_ref, wa_ref, kvg_ref, qg_ref, wqt_ref, wk_ref, wvt_ref, cos_ref, sa_ref, sb_ref,
                     ct_ref, st_ref, qt_ref, k_ref, vt_ref):
    h = _dot(x_ref[...], wa_ref[...])
    ckv = h[:, 0:128]
    kr = h[:, 128:256]
    cq = h[:, 256:768]
    ckv_n = (ckv * lax.rsqrt(jnp.mean(ckv * ckv, axis=-1, keepdims=True) + RMS_EPS) * kvg_ref[...]).astype(BF16)
    cq_ms = jnp.sum(cq * cq, axis=-1, keepdims=True) * (1.0 / MLA_Q_RANK)
    cq_n = (cq * lax.rsqrt(cq_ms + RMS_EPS) * qg_ref[...]).astype(BF16)
    kr_r = _rope64(kr, cos_ref[...], sa_ref[...], sb_ref[...]).astype(BF16)
    scale = MLA_QK_DIM ** -0.5 * LOG2E
    qt = _dot_nt(wqt_ref[...], cq_n)
    c, s = ct_ref[...], st_ref[...]
    k_nope = _dot(ckv_n, wk_ref[...])
    for hd in range(MLA_HEADS):
        o = hd * MLA_QK_PAD
        qt_ref[o:o + 128, :] = (qt[o:o + 128] * scale).astype(BF16)
        x1, x2 = qt[o + 128:o + 160], qt[o + 160:o + 192]
        qt_ref[o + 128:o + 160, :] = ((x1 * c - x2 * s) * scale).astype(BF16)
        qt_ref[o + 160:o + 192, :] = ((x2 * c + x1 * s) * scale).astype(BF16)
        qt_ref[o + 192:o + 256, :] = jnp.zeros((64, qt.shape[1]), BF16)
        k_ref[:, o:o + 128] = k_nope[:, hd * 128:(hd + 1) * 128].astype(BF16)
        k_ref[:, o + 128:o + 256] = kr_r
    vt_ref[...] = _dot_nt(wvt_ref[...], ckv_n).astype(BF16)


def _mla_prep(xb3, wa, kvg, qg, wqt, wk, wvt, cos_t, sin_a, sin_b, cos_tt, sin_tt, tm):
    b, s, _ = xb3.shape
    full = lambda shape: pl.BlockSpec(shape, lambda bi, i: (0, 0))
    tab = pl.BlockSpec((tm, LANES), lambda bi, i: (i, 0))
    tab_t = pl.BlockSpec((MLA_ROPE_DIM // 2, tm), lambda bi, i: (0, i))
    wide = MLA_HEADS * MLA_QK_PAD
    vw = MLA_HEADS * MLA_V_DIM
    return pl.pallas_call(
        _mla_prep_kernel,
        grid=(b, s // tm),
        in_specs=[pl.BlockSpec((None, tm, D_MODEL), lambda bi, i: (bi, i, 0)), full(wa.shape), full(kvg.shape),
                  full(qg.shape), full(wqt.shape), full(wk.shape), full(wvt.shape), tab, tab, tab, tab_t, tab_t],
        out_specs=[pl.BlockSpec((None, wide, tm), lambda bi, i: (bi, 0, i)),
                   pl.BlockSpec((None, tm, wide), lambda bi, i: (bi, i, 0)),
                   pl.BlockSpec((None, vw, tm), lambda bi, i: (bi, 0, i))],
        out_shape=[jax.ShapeDtypeStruct((b, wide, s), BF16), jax.ShapeDtypeStruct((b, s, wide), BF16),
                   jax.ShapeDtypeStruct((b, vw, s), BF16)],
        compiler_params=_params("parallel", "parallel"),
        name="mla_prep",
    )(xb3, wa, kvg, qg, wqt, wk, wvt, cos_t, sin_a, sin_b, cos_tt, sin_tt)


def _dqkv_kernel(x_ref, w_ref, cos_ref, sin_ref, o_ref):
    j = pl.program_id(0)
    h = _dot(x_ref[...], w_ref[...])

    @pl.when(j < 4)
    def _():
        cos_f, sin_f = cos_ref[...], sin_ref[...]
        sc = jnp.where(j < 3, HEAD_DIM ** -0.5, 1.0).astype(F32)
        for hd in range(DIL_HEADS):
            sl = slice(hd * 128, (hd + 1) * 128)
            o_ref[:, sl] = (_rope128(h[:, sl], cos_f, sin_f) * sc).astype(BF16)

    @pl.when(j == 4)
    def _():
        o_ref[...] = h.astype(BF16)


def _dqkv(xb, wd, cos_f, sin_f, seq, tm):
    m = xb.shape[0]
    nrow = seq // tm
    n_col = wd.shape[1] // DIL_WIDTH
    tab = pl.BlockSpec((tm, LANES), lambda j, i: (i % nrow, 0))
    return pl.pallas_call(
        _dqkv_kernel,
        grid=(n_col, m // tm),
        in_specs=[pl.BlockSpec((tm, D_MODEL), lambda j, i: (i, 0)),
                  pl.BlockSpec((D_MODEL, DIL_WIDTH), lambda j, i: (0, j)), tab, tab],
        out_specs=pl.BlockSpec((tm, DIL_WIDTH), lambda j, i: (i, j)),
        out_shape=jax.ShapeDtypeStruct((m, wd.shape[1]), BF16),
        compiler_params=_params("parallel", "parallel"),
        name="dil_qkv",
    )(xb, wd, cos_f, sin_f)


def _flash_kernel(*refs, tq, use_c):
    if use_c:
        qt_ref, k_ref, vt_ref, c_ref, o_ref, acc_ref, s0_ref, s1_ref, kaug_ref = refs
    else:
        qt_ref, k_ref, vt_ref, o_ref, acc_ref, s0_ref, s1_ref = refs
    seq = k_ref.shape[0]
    nq = seq // tq
    if use_c:
        row = lax.broadcasted_iota(jnp.int32, (LANES, tq), 0)

        def build(j, _):
            start = pl.multiple_of(j * tq, tq)
            neg = -LOG2E * c_ref[:, pl.ds(start, tq)]
            hi = neg.astype(BF16).astype(F32)
            mid = (neg - hi).astype(BF16).astype(F32)
            lo = neg - hi - mid
            blk = jnp.where(row == 0, hi, jnp.where(row == 1, mid, jnp.where(row == 2, lo, 0.0)))
            kaug_ref[pl.ds(start, tq), :] = blk.T.astype(BF16)
            return 0

        lax.fori_loop(0, nq, build, 0)

    def scores(i, j, s_ref):
        q = qt_ref[:, pl.ds(pl.multiple_of(i * tq, tq), tq)]
        if use_c:
            ones = (lax.broadcasted_iota(jnp.int32, (LANES, tq), 0) < 3).astype(BF16)
            q = jnp.concatenate([q, ones], axis=0)
        start = pl.multiple_of(j * tq, tq)
        kt = k_ref[pl.ds(start, tq), :]
        if use_c:
            kt = jnp.concatenate([kt, kaug_ref[pl.ds(start, tq), :]], axis=1)
        s_ref[...] = _dot(kt, q)

    def update(j, s_ref, stats, masked):
        m, l = stats
        s = s_ref[...]
        if masked:
            on_diag = lax.broadcasted_iota(jnp.int32, (tq, tq), 0) <= lax.broadcasted_iota(jnp.int32, (tq, tq), 1)
            s = jnp.where(on_diag, s, NEG)
        m_new = jnp.maximum(m, jnp.max(s, axis=0, keepdims=True))
        a = jnp.exp2(m - m_new)
        p = jnp.exp2(s - m_new)
        l = a * l + jnp.sum(p, axis=0, keepdims=True)
        start = pl.multiple_of(j * tq, tq)
        acc_ref[...] = a * acc_ref[...] + _dot(vt_ref[:, pl.ds(start, tq)], p.astype(BF16))
        return m_new, l

    def query_tile(i, first, second):
        acc_ref[...] = jnp.zeros_like(acc_ref)

        def pair(jj, stats):
            scores(i, 2 * jj + 1, second)
            stats = update(2 * jj, first, stats, False)
            scores(i, 2 * jj + 2, first)
            return update(2 * jj + 1, second, stats, False)

        init = (jnp.full((1, tq), NEG, F32), jnp.zeros((1, tq), F32))
        stats = lax.fori_loop(0, i // 2, pair, init)
        nxt = jnp.minimum(i + 1, nq - 1)

        def odd_tail(stats):
            scores(i, i, second)
            stats = update(i - 1, first, stats, False)
            scores(nxt, 0, first)
            return update(i, second, stats, True)

        def even_tail(stats):
            scores(nxt, 0, second)
            return update(i, first, stats, True)

        _, l = lax.cond(i % 2 == 1, odd_tail, even_tail, stats)
        o_ref[pl.ds(pl.multiple_of(i * tq, tq), tq), :] = (acc_ref[...] / l).T.astype(o_ref.dtype)
        return 0

    scores(0, 0, s0_ref)

    def query_loop(i, _):
        return lax.cond(((i + 1) // 2) % 2 == 0, lambda: query_tile(i, s0_ref, s1_ref),
                        lambda: query_tile(i, s1_ref, s0_ref))

    lax.fori_loop(0, nq, query_loop, 0)


def _flash(qt, k, vt, c, *, heads, dq, dk, dv, tq):
    b, s, _ = k.shape
    use_c = c is not None
    in_specs = [pl.BlockSpec((None, dq, s), lambda bi, h: (bi, h, 0)),
                pl.BlockSpec((None, s, dk), lambda bi, h: (bi, 0, h)),
                pl.BlockSpec((None, dv, s), lambda bi, h: (bi, h, 0))]
    args = [qt, k, vt]
    scratch = [pltpu.VMEM((dv, tq), F32), pltpu.VMEM((tq, tq), F32), pltpu.VMEM((tq, tq), F32)]
    if use_c:
        in_specs.append(pl.BlockSpec((None, None, 1, s), lambda bi, h: (bi, h, 0, 0)))
        args.append(c)
        scratch.append(pltpu.VMEM((s, LANES), BF16))
    return pl.pallas_call(
        functools.partial(_flash_kernel, tq=tq, use_c=use_c),
        grid=(b, heads),
        in_specs=in_specs,
        out_specs=pl.BlockSpec((None, s, dv), lambda bi, h: (bi, 0, h)),
        out_shape=jax.ShapeDtypeStruct((b, s, heads * dv), BF16),
        scratch_shapes=scratch,
        compiler_params=_params("parallel", "parallel"),
        name="flash_fox" if use_c else "flash_mla",
    )(*args)


def _dilated_kernel(q_ref, kc_ref, kp_ref, vc_ref, vp_ref, o_ref, lse_ref, *, tu):
    i = pl.program_id(2)
    nk = DIL_SPAN + tu
    a = lax.broadcasted_iota(jnp.int32, (tu, nk), 0)
    c = lax.broadcasted_iota(jnp.int32, (tu, nk), 1)
    back = a + DIL_SPAN - c
    first = jnp.where(i > 0, 0, DIL_SPAN)
    bias = jnp.where(back >= 0, jnp.where(back <= DIL_SPAN, jnp.where(c >= first, 0.0, NEG), NEG), NEG)
    lane = lax.broadcasted_iota(jnp.int32, (tu, LANES), 1)
    lse_all = jnp.zeros((tu, LANES), F32)
    for hd in range(DIL_HEADS):
        sl = slice(hd * 128, (hd + 1) * 128)
        k = jnp.concatenate([kp_ref[tu - DIL_SPAN:, sl], kc_ref[:, sl]], axis=0)
        v = jnp.concatenate([vp_ref[tu - DIL_SPAN:, sl], vc_ref[:, sl]], axis=0)
        s = _dot_nt(q_ref[:, sl], k) + bias
        m = jnp.max(s, axis=-1, keepdims=True)
        p = jnp.exp(s - m)
        l = jnp.sum(p, axis=-1, keepdims=True)
        o_ref[:, sl] = _dot(p.astype(BF16), v) / l
        lse_all = jnp.where(lane == hd, m + jnp.log(l), lse_all)
    lse_ref[...] = lse_all


def _dilated_group(dqkv, g, dil, batch, seq, tu):
    su = seq // dil
    nblk = dqkv.shape[1] // DIL_WIDTH
    view = dqkv.reshape(batch, su, dil * dqkv.shape[1])
    cur = lambda blk: pl.BlockSpec((None, tu, DIL_WIDTH), lambda b, r, i: (b, i, r * nblk + blk))
    prev = lambda blk: pl.BlockSpec((None, tu, DIL_WIDTH), lambda b, r, i: (b, jnp.maximum(i - 1, 0), r * nblk + blk))
    o, lse = pl.pallas_call(
        functools.partial(_dilated_kernel, tu=tu),
        grid=(batch, dil, su // tu),
        in_specs=[cur(g), cur(3), prev(3), cur(4), prev(4)],
        out_specs=[pl.BlockSpec((None, tu, DIL_WIDTH), lambda b, r, i: (b, i, r)),
                   pl.BlockSpec((None, tu, LANES), lambda b, r, i: (b, i, r))],
        out_shape=[jax.ShapeDtypeStruct((batch, su, dil * DIL_WIDTH), F32),
                   jax.ShapeDtypeStruct((batch, su, dil * LANES), F32)],
        compiler_params=_params("parallel", "parallel", "arbitrary"),
        name=f"dilated_{dil}",
    )(view, view, view, view, view)
    return o.reshape(batch * seq, DIL_WIDTH), lse.reshape(batch * seq, LANES)


def _dil_merge_kernel(o0, o1, o2, l0, l1, l2, out_ref):
    ls = [l0[...], l1[...], l2[...]]
    mx = jnp.maximum(jnp.maximum(ls[0], ls[1]), ls[2])
    es = [jnp.exp(x - mx) for x in ls]
    den = es[0] + es[1] + es[2]
    ws = [e / den for e in es]
    os_ = [o0, o1, o2]
    for hd in range(DIL_HEADS):
        sl = slice(hd * 128, (hd + 1) * 128)
        acc = ws[0][:, hd:hd + 1] * os_[0][:, sl]
        for g in (1, 2):
            acc = acc + ws[g][:, hd:hd + 1] * os_[g][:, sl]
        out_ref[:, sl] = acc.astype(BF16)


def _dil_merge(outs, lses, tm):
    m = outs[0].shape[0]
    ob = pl.BlockSpec((tm, DIL_WIDTH), lambda i: (i, 0))
    lb = pl.BlockSpec((tm, LANES), lambda i: (i, 0))
    return pl.pallas_call(
        _dil_merge_kernel,
        grid=(m // tm,),
        in_specs=[ob, ob, ob, lb, lb, lb],
        out_specs=ob,
        out_shape=jax.ShapeDtypeStruct((m, DIL_WIDTH), BF16),
        compiler_params=_params("parallel"),
        name="dil_merge",
    )(*outs, *lses)


def _proj_ln_kernel(*refs, n_lhs, with_bf16):
    lhs = refs[:n_lhs]
    ws = refs[n_lhs:2 * n_lhs]
    x_ref, g_ref, b_ref = refs[2 * n_lhs:2 * n_lhs + 3]
    outs = refs[2 * n_lhs + 3:]
    y = _dot(lhs[0][...], ws[0][...])
    for a, w in zip(lhs[1:], ws[1:]):
        y = y + _dot(a[...], w[...])
    out = _layer_norm(ALPHA * x_ref[...] + y, g_ref[...], b_ref[...])
    outs[0][...] = out
    if with_bf16:
        outs[1][...] = out.astype(BF16)


def _proj_ln(lhs, ws, x, g, b, tm, with_bf16=True):
    m = x.shape[0]
    row = lambda width: pl.BlockSpec((tm, width), lambda i: (i, 0))
    full = lambda shape: pl.BlockSpec(shape, lambda i: (0, 0))
    dtypes = [F32, BF16] if with_bf16 else [F32]
    return pl.pallas_call(
        functools.partial(_proj_ln_kernel, n_lhs=len(lhs), with_bf16=with_bf16),
        grid=(m // tm,),
        in_specs=[row(a.shape[1]) for a in lhs] + [full(w.shape) for w in ws]
        + [row(D_MODEL), full(g.shape), full(b.shape)],
        out_specs=[row(D_MODEL) for _ in dtypes],
        out_shape=[jax.ShapeDtypeStruct((m, D_MODEL), dt) for dt in dtypes],
        compiler_params=_params("parallel"),
        name="proj_ln",
    )(*lhs, *ws, x, g, b)


def _silu_mul(g, u):
    return g * (1.0 / (1.0 + jnp.exp(-g))) * u


def _ffn_up_kernel(x_ref, wg_ref, wu_ref, o_ref, wgb_ref, wub_ref):
    @pl.when(pl.program_id(1) == 0)
    def _():
        wgb_ref[...] = wg_ref[...].astype(BF16)
        wub_ref[...] = wu_ref[...].astype(BF16)

    x = x_ref[...]
    o_ref[...] = _silu_mul(_dot(x, wgb_ref[...]), _dot(x, wub_ref[...])).astype(BF16)


def _ffn_up(xb, wg, wu, tm, tn):
    m = xb.shape[0]
    return pl.pallas_call(
        _ffn_up_kernel,
        grid=(D_FF // tn, m // tm),
        in_specs=[pl.BlockSpec((tm, D_MODEL), lambda j, i: (i, 0)),
                  pl.BlockSpec((D_MODEL, tn), lambda j, i: (0, j)),
                  pl.BlockSpec((D_MODEL, tn), lambda j, i: (0, j))],
        out_specs=pl.BlockSpec((tm, tn), lambda j, i: (i, j)),
        out_shape=jax.ShapeDtypeStruct((m, D_FF), BF16),
        scratch_shapes=[pltpu.VMEM((D_MODEL, tn), BF16), pltpu.VMEM((D_MODEL, tn), BF16)],
        compiler_params=_params("arbitrary", "arbitrary"),
        name="ffn_up",
    )(xb, wg, wu)


def _ffn_down_ln_kernel(h_ref, w_ref, x_ref, g_ref, b_ref, o_ref, ob_ref, acc_ref):
    k = pl.program_id(1)

    @pl.when(k == 0)
    def _():
        acc_ref[...] = jnp.zeros_like(acc_ref)

    acc_ref[...] += _dot(h_ref[...], w_ref[...])

    @pl.when(k == pl.num_programs(1) - 1)
    def _():
        out = _layer_norm(ALPHA * x_ref[...] + acc_ref[...], g_ref[...], b_ref[...])
        o_ref[...] = out
        ob_ref[...] = out.astype(BF16)


def _ffn_down_ln(h, wd, x, g, b, tm, tk):
    m = x.shape[0]
    row = pl.BlockSpec((tm, D_MODEL), lambda i, k: (i, 0))
    vec = pl.BlockSpec((1, D_MODEL), lambda i, k: (0, 0))
    return pl.pallas_call(
        _ffn_down_ln_kernel,
        grid=(m // tm, D_FF // tk),
        in_specs=[pl.BlockSpec((tm, tk), lambda i, k: (i, k)), pl.BlockSpec((tk, D_MODEL), lambda i, k: (k, 0)),
                  row, vec, vec],
        out_specs=[row, row],
        out_shape=[jax.ShapeDtypeStruct((m, D_MODEL), F32), jax.ShapeDtypeStruct((m, D_MODEL), BF16)],
        scratch_shapes=[pltpu.VMEM((tm, D_MODEL), F32)],
        compiler_params=_params("parallel", "arbitrary"),
        name="ffn_down_ln",
    )(h, wd, x, g, b)


def _proj_kernel(x_ref, w_ref, o_ref):
    o_ref[...] = _dot(x_ref[...], w_ref[...]).astype(BF16)


def _proj(xb, w, tm, tn):
    m = xb.shape[0]
    n = w.shape[1]
    return pl.pallas_call(
        _proj_kernel,
        grid=(n // tn, m // tm),
        in_specs=[pl.BlockSpec((tm, D_MODEL), lambda j, i: (i, 0)), pl.BlockSpec((D_MODEL, tn), lambda j, i: (0, j))],
        out_specs=pl.BlockSpec((tm, tn), lambda j, i: (i, j)),
        out_shape=jax.ShapeDtypeStruct((m, n), BF16),
        compiler_params=_params("parallel", "parallel"),
        name="proj",
    )(xb, w)


def _proj_t_kernel(w_ref, x_ref, o_ref, *, scale):
    o_ref[...] = (_dot_nt(w_ref[...], x_ref[...]) * scale).astype(BF16)


def _proj_t(xb3, wt, scale, tm, tn):
    b, s, _ = xb3.shape
    n = wt.shape[0]
    return pl.pallas_call(
        functools.partial(_proj_t_kernel, scale=scale),
        grid=(n // tn, b, s // tm),
        in_specs=[pl.BlockSpec((tn, D_MODEL), lambda j, bi, i: (j, 0)),
                  pl.BlockSpec((None, tm, D_MODEL), lambda j, bi, i: (bi, i, 0))],
        out_specs=pl.BlockSpec((None, tn, tm), lambda j, bi, i: (bi, j, i)),
        out_shape=jax.ShapeDtypeStruct((b, n, s), BF16),
        compiler_params=_params("parallel", "parallel", "parallel"),
        name="proj_t",
    )(wt, xb3)


def _fgate_kernel(x_ref, w_ref, b_ref, c_ref, carry_ref):
    @pl.when(pl.program_id(1) == 0)
    def _():
        carry_ref[...] = jnp.zeros_like(carry_ref)

    z = _dot(x_ref[...], w_ref[...]) + b_ref[...]
    log_f = jnp.minimum(z, 0.0) - jnp.log(1.0 + jnp.exp(-jnp.abs(z)))
    tm = z.shape[0]
    tri = (lax.broadcasted_iota(jnp.int32, (tm, tm), 1) <= lax.broadcasted_iota(jnp.int32, (tm, tm), 0)).astype(F32)
    c = jnp.dot(tri, log_f, preferred_element_type=F32, precision=lax.Precision.HIGHEST) + carry_ref[...]
    c_ref[...] = c
    carry_ref[...] = c[tm - 1:tm, :]


def _fgate(xb3, wf, bf, tm):
    b, s, _ = xb3.shape
    return pl.pallas_call(
        _fgate_kernel,
        grid=(b, s // tm),
        in_specs=[pl.BlockSpec((None, tm, D_MODEL), lambda bi, i: (bi, i, 0)),
                  pl.BlockSpec(wf.shape, lambda bi, i: (0, 0)), pl.BlockSpec(bf.shape, lambda bi, i: (0, 0))],
        out_specs=pl.BlockSpec((None, tm, LANES), lambda bi, i: (bi, i, 0)),
        out_shape=jax.ShapeDtypeStruct((b, s, LANES), F32),
        scratch_shapes=[pltpu.VMEM((1, LANES), F32)],
        compiler_params=_params("parallel", "arbitrary"),
        name="fox_gate",
    )(xb3, wf, bf)


def _router_kernel(x_ref, w_ref, b_ref, route_ref, cnt_ref, carry_ref):
    @pl.when(pl.program_id(0) == 0)
    def _():
        carry_ref[...] = jnp.zeros_like(carry_ref)

    logits = jnp.dot(x_ref[...], w_ref[...], preferred_element_type=F32, precision=lax.Precision.HIGHEST) + b_ref[...]
    tm = logits.shape[0]
    lane = lax.broadcasted_iota(jnp.int32, (tm, LANES), 1)
    l1 = jnp.max(logits, axis=-1, keepdims=True)
    i1 = jnp.min(jnp.where(logits == l1, lane, LANES), axis=-1, keepdims=True)
    rest = jnp.where(lane == i1, NEG, logits)
    l2 = jnp.max(rest, axis=-1, keepdims=True)
    i2 = jnp.min(jnp.where(rest == l2, lane, LANES), axis=-1, keepdims=True)
    e = jnp.exp(l2 - l1)
    w1 = 1.0 / (1.0 + e)
    w2 = e / (1.0 + e)
    hot1 = (lane == i1).astype(F32)
    hot2 = (lane == i2).astype(F32)
    cnt = hot1 + hot2
    strict = (lax.broadcasted_iota(jnp.int32, (tm, tm), 1) < lax.broadcasted_iota(jnp.int32, (tm, tm), 0)).astype(BF16)
    before = _dot(strict, cnt.astype(BF16)) + carry_ref[...]
    r1 = jnp.sum(before * hot1, axis=-1, keepdims=True)
    r2 = jnp.sum(before * hot2, axis=-1, keepdims=True)
    vals = (i1.astype(F32), i2.astype(F32), w1, w2, r1, r2)
    route = jnp.zeros((tm, LANES), F32)
    for idx, val in enumerate(vals):
        route = jnp.where(lane == idx, val, route)
    route_ref[...] = route
    total = carry_ref[...] + jnp.sum(cnt, axis=0, keepdims=True)
    carry_ref[...] = total
    cnt_ref[...] = jnp.broadcast_to(total, cnt_ref.shape)


def _router(x, rw, rb, tm):
    m = x.shape[0]
    return pl.pallas_call(
        _router_kernel,
        grid=(m // tm,),
        in_specs=[pl.BlockSpec((tm, D_MODEL), lambda i: (i, 0)), pl.BlockSpec(rw.shape, lambda i: (0, 0)),
                  pl.BlockSpec(rb.shape, lambda i: (0, 0))],
        out_specs=[pl.BlockSpec((tm, LANES), lambda i: (i, 0)), pl.BlockSpec((8, LANES), lambda i: (0, 0))],
        out_shape=[jax.ShapeDtypeStruct((m, LANES), F32), jax.ShapeDtypeStruct((8, LANES), F32)],
        scratch_shapes=[pltpu.VMEM((1, LANES), F32)],
        compiler_params=_params("arbitrary"),
        name="moe_router",
    )(x, rw, rb)


def _dispatch_kernel(pos_ref, x_ref, zeros_ref, xs_ref, sem):
    del zeros_ref
    tm = x_ref.shape[0]

    def copy(r, k):
        return pltpu.make_async_copy(x_ref.at[pl.ds(r, 1), :], xs_ref.at[pl.ds(pos_ref[0, 0, 2 * r + k], 1), :], sem)

    def start(r, _):
        copy(r, 0).start()
        copy(r, 1).start()
        return 0

    def wait(r, _):
        copy(r, 0).wait()
        copy(r, 1).wait()
        return 0

    lax.fori_loop(0, tm, start, 0, unroll=DMA_UNROLL)
    lax.fori_loop(0, tm, wait, 0, unroll=DMA_UNROLL)


def _dispatch(x, pos, n_rows, tm):
    m, width = x.shape
    pos3 = pos.reshape(m // tm, 1, 2 * tm)
    return pl.pallas_call(
        _dispatch_kernel,
        grid=(m // tm,),
        in_specs=[pl.BlockSpec((1, 1, 2 * tm), lambda i: (i, 0, 0), memory_space=pltpu.SMEM),
                  pl.BlockSpec((tm, width), lambda i: (i, 0)),
                  pl.BlockSpec(memory_space=pl.ANY)],
        out_specs=pl.BlockSpec(memory_space=pl.ANY),
        out_shape=jax.ShapeDtypeStruct((n_rows, width), x.dtype),
        scratch_shapes=[pltpu.SemaphoreType.DMA(())],
        input_output_aliases={2: 0},
        compiler_params=_params("arbitrary"),
        name="moe_dispatch",
    )(pos3, x, jnp.zeros((n_rows, width), x.dtype))


def _moe_up_kernel(te_ref, nu_ref, x_ref, wg_ref, wu_ref, o_ref, wgb_ref, wub_ref):
    t = pl.program_id(1)
    live = t < nu_ref[0]
    fresh = jnp.logical_or(t == 0, te_ref[t] != te_ref[jnp.maximum(t - 1, 0)])

    @pl.when(jnp.logical_and(live, fresh))
    def _():
        wgb_ref[...] = wg_ref[...].astype(BF16)
        wub_ref[...] = wu_ref[...].astype(BF16)

    @pl.when(live)
    def _():
        x = x_ref[...].astype(BF16)
        o_ref[...] = _silu_mul(_dot(x, wgb_ref[...]), _dot(x, wub_ref[...])).astype(BF16)

    @pl.when(jnp.logical_not(live))
    def _():
        o_ref[...] = jnp.zeros_like(o_ref)


def _moe_up(xs, wg, wu, tile_expert, n_used, tm, tn):
    p = xs.shape[0]
    live = lambda t, nu: jnp.minimum(t, nu[0] - 1)
    return pl.pallas_call(
        _moe_up_kernel,
        grid_spec=pltpu.PrefetchScalarGridSpec(
            num_scalar_prefetch=2,
            grid=(D_FF // tn, p // tm),
            in_specs=[pl.BlockSpec((tm, D_MODEL), lambda j, t, te, nu: (live(t, nu), 0)),
                      pl.BlockSpec((None, D_MODEL, tn), lambda j, t, te, nu: (te[live(t, nu)], 0, j)),
                      pl.BlockSpec((None, D_MODEL, tn), lambda j, t, te, nu: (te[live(t, nu)], 0, j))],
            out_specs=pl.BlockSpec((tm, tn), lambda j, t, te, nu: (t, j)),
            scratch_shapes=[pltpu.VMEM((D_MODEL, tn), BF16), pltpu.VMEM((D_MODEL, tn), BF16)],
        ),
        out_shape=jax.ShapeDtypeStruct((p, D_FF), BF16),
        compiler_params=_params("arbitrary", "arbitrary"),
        name="moe_up",
    )(tile_expert, n_used, xs, wg, wu)


def _moe_down_kernel(te_ref, nu_ref, h_ref, w_ref, o_ref, wb_ref):
    t = pl.program_id(1)
    live = t < nu_ref[0]
    fresh = jnp.logical_or(t == 0, te_ref[t] != te_ref[jnp.maximum(t - 1, 0)])

    @pl.when(jnp.logical_and(live, fresh))
    def _():
        wb_ref[...] = w_ref[...].astype(BF16)

    @pl.when(live)
    def _():
        o_ref[...] = _dot(h_ref[...], wb_ref[...])

    @pl.when(jnp.logical_not(live))
    def _():
        o_ref[...] = jnp.zeros_like(o_ref)


def _moe_down(hs, wd, tile_expert, n_used, tm, tn):
    p = hs.shape[0]
    live = lambda t, nu: jnp.minimum(t, nu[0] - 1)
    return pl.pallas_call(
        _moe_down_kernel,
        grid_spec=pltpu.PrefetchScalarGridSpec(
            num_scalar_prefetch=2,
            grid=(D_MODEL // tn, p // tm),
            in_specs=[pl.BlockSpec((tm, D_FF), lambda j, t, te, nu: (live(t, nu), 0)),
                      pl.BlockSpec((None, D_FF, tn), lambda j, t, te, nu: (te[live(t, nu)], 0, j))],
            out_specs=pl.BlockSpec((tm, tn), lambda j, t, te, nu: (t, j)),
            scratch_shapes=[pltpu.VMEM((D_FF, tn), BF16)],
        ),
        out_shape=jax.ShapeDtypeStruct((p, D_MODEL), F32),
        compiler_params=_params("arbitrary", "arbitrary"),
        name="moe_down",
    )(tile_expert, n_used, hs, wd)


def _combine_ln_kernel(pos_ref, ys_ref, x_ref, route_ref, g_ref, b_ref, o_ref, buf0, buf1, sem):
    tm = x_ref.shape[0]

    def copies(r):
        c0 = pltpu.make_async_copy(ys_ref.at[pl.ds(pos_ref[0, 0, 2 * r], 1), :], buf0.at[pl.ds(r, 1), :], sem)
        c1 = pltpu.make_async_copy(ys_ref.at[pl.ds(pos_ref[0, 0, 2 * r + 1], 1), :], buf1.at[pl.ds(r, 1), :], sem)
        return c0, c1

    def start(r, _):
        c0, c1 = copies(r)
        c0.start()
        c1.start()
        return 0

    def wait(r, _):
        c0, c1 = copies(r)
        c0.wait()
        c1.wait()
        return 0

    lax.fori_loop(0, tm, start, 0, unroll=DMA_UNROLL)
    lax.fori_loop(0, tm, wait, 0, unroll=DMA_UNROLL)
    route = route_ref[...]
    y = route[:, 2:3] * buf0[...] + route[:, 3:4] * buf1[...]
    o_ref[...] = _layer_norm(ALPHA * x_ref[...] + y, g_ref[...], b_ref[...])


def _combine_ln(ys, pos, x, route, g, b, tm):
    m = x.shape[0]
    pos3 = pos.reshape(m // tm, 1, 2 * tm)
    vec = pl.BlockSpec((1, D_MODEL), lambda i: (0, 0))
    return pl.pallas_call(
        _combine_ln_kernel,
        grid=(m // tm,),
        in_specs=[pl.BlockSpec((1, 1, 2 * tm), lambda i: (i, 0, 0), memory_space=pltpu.SMEM),
                  pl.BlockSpec(memory_space=pl.ANY),
                  pl.BlockSpec((tm, D_MODEL), lambda i: (i, 0)),
                  pl.BlockSpec((tm, LANES), lambda i: (i, 0)), vec, vec],
        out_specs=pl.BlockSpec((tm, D_MODEL), lambda i: (i, 0)),
        out_shape=jax.ShapeDtypeStruct((m, D_MODEL), F32),
        scratch_shapes=[pltpu.VMEM((tm, D_MODEL), F32), pltpu.VMEM((tm, D_MODEL), F32), pltpu.SemaphoreType.DMA(())],
        compiler_params=_params("arbitrary"),
        name="moe_combine_ln",
    )(pos3, ys, x, route, g, b)


def _rope_tables(seq):
    def angles(dim):
        inv_freq = 1.0 / (ROPE_THETA ** (jnp.arange(0, dim, 2, dtype=F32) / dim))
        ang = jnp.arange(seq, dtype=F32)[:, None] * inv_freq[None, :]
        return jnp.cos(ang), jnp.sin(ang)

    c128, s128 = angles(HEAD_DIM)
    cos_f = jnp.concatenate([c128, c128], axis=-1)
    sin_f = jnp.concatenate([-s128, s128], axis=-1)
    c64, s64 = angles(MLA_ROPE_DIM)
    z32 = jnp.zeros_like(c64)
    cos_t = jnp.concatenate([c64, c64, z32, z32], axis=-1)
    sin_a = jnp.concatenate([-s64, z32, z32, z32], axis=-1)
    sin_b = jnp.concatenate([z32, s64, z32, z32], axis=-1)
    return cos_f, sin_f, cos_t, sin_a, sin_b, c64.T, s64.T


def _pad_cols(a, width):
    return jnp.pad(a, ((0, 0), (0, width - a.shape[1])))


def _row(v):
    return v.reshape(1, -1).astype(F32)


def _even_layer(x, xb, batch, seq, tables, w_in, q_norm, w_q_b, kv_norm, w_kv_b, w_out, ln1_g, ln1_b,
                w_gate, w_up, w_down, ln2_g, ln2_b):
    cos_f, sin_f, cos_t, sin_a, sin_b, cos_tt, sin_tt = tables
    tm = min(512, seq)
    wa = jnp.concatenate([w_in[:, OFF_CKV:OFF_KROPE], _pad_cols(w_in[:, OFF_KROPE:OFF_DQ], LANES),
                          _pad_cols(w_in[:, OFF_CQ:OFF_CKV], MLA_Q_RANK_PAD)], axis=1).astype(BF16)
    qg = _pad_cols(_row(q_norm), MLA_Q_RANK_PAD)
    wq = jnp.pad(w_q_b.reshape(MLA_Q_RANK, MLA_HEADS, MLA_QK_DIM),
                 ((0, MLA_Q_RANK_PAD - MLA_Q_RANK), (0, 0), (0, MLA_QK_PAD - MLA_QK_DIM)))
    wqt = wq.reshape(MLA_Q_RANK_PAD, MLA_HEADS * MLA_QK_PAD).T.astype(BF16)
    wkv3 = w_kv_b.reshape(MLA_KV_RANK, MLA_HEADS, MLA_NOPE_DIM + MLA_V_DIM)
    wk = wkv3[:, :, :MLA_NOPE_DIM].reshape(MLA_KV_RANK, -1).astype(BF16)
    wvt = wkv3[:, :, MLA_NOPE_DIM:].reshape(MLA_KV_RANK, -1).T.astype(BF16)
    xb3 = xb.reshape(batch, seq, D_MODEL)
    qt_mla, k_mla, vt_mla = _mla_prep(xb3, wa, _row(kv_norm), qg, wqt, wk, wvt, cos_t, sin_a, sin_b,
                                      cos_tt, sin_tt, tm)
    tq = min(FLASH_TQ, seq)
    o_mla = _flash(qt_mla, k_mla, vt_mla, None, heads=MLA_HEADS, dq=MLA_QK_PAD, dk=MLA_QK_PAD, dv=MLA_V_DIM,
                   tq=tq)
    o_mla = o_mla.reshape(batch * seq, -1)

    dqkv = _dqkv(xb, w_in[:, OFF_DQ:].astype(BF16), cos_f, sin_f, seq, tm)
    outs, lses = [], []
    for g, (window, dil) in enumerate(DIL_PATTERNS):
        assert window == DIL_SPAN * dil
        o_g, lse_g = _dilated_group(dqkv, g, dil, batch, seq, min(256, seq // dil))
        outs.append(o_g)
        lses.append(lse_g)
    o_dil = _dil_merge(outs, lses, tm)

    n_mla = MLA_HEADS * MLA_V_DIM
    wo = w_out.astype(BF16)
    x1, x1b = _proj_ln([o_mla, o_dil], [wo[:n_mla], wo[n_mla:]], x, _row(ln1_g), _row(ln1_b), tm)
    hmid = _ffn_up(x1b, w_gate, w_up, min(1024, seq), 512)
    return _ffn_down_ln(hmid, w_down.astype(BF16), x1, _row(ln2_g), _row(ln2_b), tm, 1408)


def _odd_layer(x, xb, batch, seq, w_qkv, w_f, b_f, w_out, ln1_g, ln1_b, router_w, router_b,
               exp_w_gate, exp_w_up, exp_w_down, ln2_g, ln2_b):
    m = batch * seq
    tm = min(512, seq)
    xb3 = xb.reshape(batch, seq, D_MODEL)
    wb = w_qkv.astype(BF16)
    qt = _proj_t(xb3, wb[:, :FOX_WIDTH].T, HEAD_DIM ** -0.5 * LOG2E, tm, 1024)
    k = _proj(xb, wb[:, FOX_WIDTH:2 * FOX_WIDTH], tm, 1024).reshape(batch, seq, FOX_WIDTH)
    vt = _proj_t(xb3, wb[:, 2 * FOX_WIDTH:].T, 1.0, tm, 1024)
    c = _fgate(xb3, _pad_cols(w_f, LANES).astype(BF16), _pad_cols(_row(b_f), LANES), min(256, seq))
    c_t = jnp.transpose(c[:, :, :FOX_HEADS], (0, 2, 1)).reshape(batch, FOX_HEADS, 1, seq)
    tq = min(FLASH_TQ, seq)
    o = _flash(qt, k, vt, c_t, heads=FOX_HEADS, dq=HEAD_DIM, dk=HEAD_DIM, dv=HEAD_DIM, tq=tq)
    (x1,) = _proj_ln([o.reshape(m, -1)], [w_out.astype(BF16)], x, _row(ln1_g), _row(ln1_b), tm, with_bf16=False)

    rb = jnp.full((1, LANES), NEG, F32).at[0, :N_EXPERTS].set(router_b.astype(F32))
    route, counts = _router(x1, _pad_cols(router_w, LANES), rb, min(256, seq))
    tile = MOE_TILE
    n_tiles = (2 * m) // tile + N_EXPERTS
    cnt = counts[0, :N_EXPERTS].astype(jnp.int32)
    tiles_per = (cnt + tile - 1) // tile
    tile_end = jnp.cumsum(tiles_per)
    offset = (tile_end - tiles_per) * tile
    idx = route[:, 0:2].astype(jnp.int32)
    pos = (offset[idx] + route[:, 4:6].astype(jnp.int32)).reshape(-1)
    n_used = tile_end[-1:]
    tile_ids = jnp.arange(n_tiles, dtype=jnp.int32)
    tile_expert = jnp.minimum(jnp.sum((tile_end[None, :] <= tile_ids[:, None]).astype(jnp.int32), axis=1),
                              N_EXPERTS - 1)
    td = min(256, seq)
    xs = _dispatch(x1, pos, n_tiles * tile, td)
    hs = _moe_up(xs, exp_w_gate, exp_w_up, tile_expert, n_used, tile, 512)
    split = tile // MOE_DOWN_TILE
    ys = _moe_down(hs, exp_w_down, jnp.repeat(tile_expert, split), n_used * split, MOE_DOWN_TILE, 512)
    return _combine_ln(ys, pos, x1, route, _row(ln2_g), _row(ln2_b), td)


def kernel(x, ev_w_in, ev_q_norm, ev_w_q_b, ev_kv_norm, ev_w_kv_b, ev_w_out, ev_ln1_g, ev_ln1_b, ev_ffn_w_gate, ev_ffn_w_up, ev_ffn_w_down, ev_ln2_g, ev_ln2_b, od_w_qkv, od_w_f, od_b_f, od_w_out, od_ln1_g, od_ln1_b, od_router_w, od_router_b, od_exp_w_gate, od_exp_w_up, od_exp_w_down, od_ln2_g, od_ln2_b):
    batch, seq, _ = x.shape
    tables = _rope_tables(seq)
    h = x.reshape(batch * seq, D_MODEL)
    hb = h.astype(BF16)
    for layer in range(DEPTH):
        i = layer // 2
        if layer % 2 == 0:
            h, hb = _even_layer(h, hb, batch, seq, tables, ev_w_in[i], ev_q_norm[i], ev_w_q_b[i], ev_kv_norm[i],
                                ev_w_kv_b[i], ev_w_out[i], ev_ln1_g[i], ev_ln1_b[i], ev_ffn_w_gate[i],
                                ev_ffn_w_up[i], ev_ffn_w_down[i], ev_ln2_g[i], ev_ln2_b[i])
        else:
            h = _odd_layer(h, hb, batch, seq, od_w_qkv[i], od_w_f[i], od_b_f[i], od_w_out[i], od_ln1_g[i],
                           od_ln1_b[i], od_router_w[i], od_router_b[i], od_exp_w_gate[i], od_exp_w_up[i],
                           od_exp_w_down[i], od_ln2_g[i], od_ln2_b[i])
            hb = h.astype(BF16)
    return h.reshape(batch, seq, D_MODEL)
```

```python
import functools

import jax
import jax.numpy as jnp
from jax import lax
from jax.experimental import pallas as pl
from jax.experimental.pallas import tpu as pltpu

F32 = jnp.float32
BF16 = jnp.bfloat16

D_MODEL = 2048
HEAD_DIM = 128
LANES = 128
ROPE_THETA = 10000.0
LN_EPS = 1e-5
RMS_EPS = 1e-6

MLA_HEADS = 10
MLA_Q_RANK = 448
MLA_Q_RANK_PAD = 512
MLA_KV_RANK = 128
MLA_NOPE_DIM = 128
MLA_ROPE_DIM = 64
MLA_V_DIM = 128
MLA_QK_DIM = MLA_NOPE_DIM + MLA_ROPE_DIM
MLA_QK_PAD = 256

DIL_PATTERNS = ((128, 1), (512, 4), (2048, 16))
DIL_GROUPS = 3
DIL_HEADS = 6
DIL_WIDTH = DIL_HEADS * HEAD_DIM
DIL_SPAN = 128

OFF_CQ = 0
OFF_CKV = OFF_CQ + MLA_Q_RANK
OFF_KROPE = OFF_CKV + MLA_KV_RANK
OFF_DQ = OFF_KROPE + MLA_ROPE_DIM
OFF_DK = OFF_DQ + DIL_GROUPS * DIL_WIDTH
OFF_DV = OFF_DK + DIL_WIDTH
W_IN_COLS = OFF_DV + DIL_WIDTH

FOX_HEADS = 16
FOX_WIDTH = FOX_HEADS * HEAD_DIM

D_FF = 5632
N_EXPERTS = 8
DEPTH = 2
ALPHA = (2.0 * DEPTH) ** 0.25

NEG = -1e30
LOG2E = 1.4426950408889634
FLASH_TQ = 1024
MOE_TILE = 512
MOE_DOWN_TILE = 512
DMA_UNROLL = 8
VMEM_LIMIT = 56 * 1024 * 1024


def _params(*sem, vmem=VMEM_LIMIT):
    return pltpu.CompilerParams(dimension_semantics=sem, vmem_limit_bytes=vmem)


def _dot(a, b):
    return jnp.dot(a, b, preferred_element_type=F32)


def _dot_nt(a, b):
    return lax.dot_general(a, b, (((1,), (1,)), ((), ())), preferred_element_type=F32)


def _layer_norm(y, g, b):
    mu = jnp.mean(y, axis=-1, keepdims=True)
    d = y - mu
    var = jnp.mean(d * d, axis=-1, keepdims=True)
    return d * lax.rsqrt(var + LN_EPS) * g + b


def _rope128(x, cos_f, sin_f):
    return x * cos_f + pltpu.roll(x, 64, 1) * sin_f


def _rope64(x, cos_t, sin_a, sin_b):
    return x * cos_t + pltpu.roll(x, 96, 1) * sin_a + pltpu.roll(x, 32, 1) * sin_b


def _mla_prep_kernel(x_ref, wa_ref, kvg_ref, qg_ref, wqt_ref, wk_ref, wvt_ref, cos_ref, sa_ref, sb_ref,
                     ct_ref, st_ref, qt_ref, k_ref, vt_ref):
    h = _dot(x_ref[...], wa_ref[...])
    ckv = h[:, 0:128]
    kr = h[:, 128:256]
    cq = h[:, 256:768]
    ckv_n = (ckv * lax.rsqrt(jnp.mean(ckv * ckv, axis=-1, keepdims=True) + RMS_EPS) * kvg_ref[...]).astype(BF16)
    cq_ms = jnp.sum(cq * cq, axis=-1, keepdims=True) * (1.0 / MLA_Q_RANK)
    cq_n = (cq * lax.rsqrt(cq_ms + RMS_EPS) * qg_ref[...]).astype(BF16)
    kr_r = _rope64(kr, cos_ref[...], sa_ref[...], sb_ref[...]).astype(BF16)
    scale = MLA_QK_DIM ** -0.5 * LOG2E
    qt = _dot_nt(wqt_ref[...], cq_n)
    c, s = ct_ref[...], st_ref[...]
    k_nope = _dot(ckv_n, wk_ref[...])
    for hd in range(MLA_HEADS):
        o = hd * MLA_QK_PAD
        qt_ref[o:o + 128, :] = (qt[o:o + 128] * scale).astype(BF16)
        x1, x2 = qt[o + 128:o + 160], qt[o + 160:o + 192]
        qt_ref[o + 128:o + 160, :] = ((x1 * c - x2 * s) * scale).astype(BF16)
        qt_ref[o + 160:o + 192, :] = ((x2 * c + x1 * s) * scale).astype(BF16)
        qt_ref[o + 192:o + 256, :] = jnp.zeros((64, qt.shape[1]), BF16)
        k_ref[:, o:o + 128] = k_nope[:, hd * 128:(hd + 1) * 128].astype(BF16)
        k_ref[:, o + 128:o + 256] = kr_r
    vt_ref[...] = _dot_nt(wvt_ref[...], ckv_n).astype(BF16)


def _mla_prep(xb3, wa, kvg, qg, wqt, wk, wvt, cos_t, sin_a, sin_b, cos_tt, sin_tt, tm):
    b, s, _ = xb3.shape
    full = lambda shape: pl.BlockSpec(shape, lambda bi, i: (0, 0))
    tab = pl.BlockSpec((tm, LANES), lambda bi, i: (i, 0))
    tab_t = pl.BlockSpec((MLA_ROPE_DIM // 2, tm), lambda bi, i: (0, i))
    wide = MLA_HEADS * MLA_QK_PAD
    vw = MLA_HEADS * MLA_V_DIM
    return pl.pallas_call(
        _mla_prep_kernel,
        grid=(b, s // tm),
        in_specs=[pl.BlockSpec((None, tm, D_MODEL), lambda bi, i: (bi, i, 0)), full(wa.shape), full(kvg.shape),
                  full(qg.shape), full(wqt.shape), full(wk.shape), full(wvt.shape), tab, tab, tab, tab_t, tab_t],
        out_specs=[pl.BlockSpec((None, wide, tm), lambda bi, i: (bi, 0, i)),
                   pl.BlockSpec((None, tm, wide), lambda bi, i: (bi, i, 0)),
                   pl.BlockSpec((None, vw, tm), lambda bi, i: (bi, 0, i))],
        out_shape=[jax.ShapeDtypeStruct((b, wide, s), BF16), jax.ShapeDtypeStruct((b, s, wide), BF16),
                   jax.ShapeDtypeStruct((b, vw, s), BF16)],
        compiler_params=_params("parallel", "parallel"),
        name="mla_prep",
    )(xb3, wa, kvg, qg, wqt, wk, wvt, cos_t, sin_a, sin_b, cos_tt, sin_tt)


def _dqkv_kernel(x_ref, w_ref, cos_ref, sin_ref, o_ref):
    j = pl.program_id(0)
    h = _dot(x_ref[...], w_ref[...])

    @pl.when(j < 4)
    def _():
        cos_f, sin_f = cos_ref[...], sin_ref[...]
        sc = jnp.where(j < 3, HEAD_DIM ** -0.5, 1.0).astype(F32)
        for hd in range(DIL_HEADS):
            sl = slice(hd * 128, (hd + 1) * 128)
            o_ref[:, sl] = (_rope128(h[:, sl], cos_f, sin_f) * sc).astype(BF16)

    @pl.when(j == 4)
    def _():
        o_ref[...] = h.astype(BF16)


def _dqkv(xb, wd, cos_f, sin_f, seq, tm):
    m = xb.shape[0]
    nrow = seq // tm
    n_col = wd.shape[1] // DIL_WIDTH
    tab = pl.BlockSpec((tm, LANES), lambda j, i: (i % nrow, 0))
    return pl.pallas_call(
        _dqkv_kernel,
        grid=(n_col, m // tm),
        in_specs=[pl.BlockSpec((tm, D_MODEL), lambda j, i: (i, 0)),
                  pl.BlockSpec((D_MODEL, DIL_WIDTH), lambda j, i: (0, j)), tab, tab],
        out_specs=pl.BlockSpec((tm, DIL_WIDTH), lambda j, i: (i, j)),
        out_shape=jax.ShapeDtypeStruct((m, wd.shape[1]), BF16),
        compiler_params=_params("parallel", "parallel"),
        name="dil_qkv",
    )(xb, wd, cos_f, sin_f)


def _flash_kernel(*refs, tq, use_c):
    if use_c:
        qt_ref, k_ref, vt_ref, c_ref, o_ref, acc_ref, s0_ref, s1_ref, kaug_ref = refs
    else:
        qt_ref, k_ref, vt_ref, o_ref, acc_ref, s0_ref, s1_ref = refs
    seq = k_ref.shape[0]
    nq = seq // tq
    if use_c:
        row = lax.broadcasted_iota(jnp.int32, (LANES, tq), 0)

        def build(j, _):
            start = pl.multiple_of(j * tq, tq)
            neg = -LOG2E * c_ref[:, pl.ds(start, tq)]
            hi = neg.astype(BF16).astype(F32)
            mid = (neg - hi).astype(BF16).astype(F32)
            lo = neg - hi - mid
            blk = jnp.where(row == 0, hi, jnp.where(row == 1, mid, jnp.where(row == 2, lo, 0.0)))
            kaug_ref[pl.ds(start, tq), :] = blk.T.astype(BF16)
            return 0

        lax.fori_loop(0, nq, build, 0)

    def scores(i, j, s_ref):
        q = qt_ref[:, pl.ds(pl.multiple_of(i * tq, tq), tq)]
        if use_c:
            ones = (lax.broadcasted_iota(jnp.int32, (LANES, tq), 0) < 3).astype(BF16)
            q = jnp.concatenate([q, ones], axis=0)
        start = pl.multiple_of(j * tq, tq)
        kt = k_ref[pl.ds(start, tq), :]
        if use_c:
            kt = jnp.concatenate([kt, kaug_ref[pl.ds(start, tq), :]], axis=1)
        s_ref[...] = _dot(kt, q)

    def update(j, s_ref, stats, masked):
        m, l = stats
        s = s_ref[...]
        if masked:
            on_diag = lax.broadcasted_iota(jnp.int32, (tq, tq), 0) <= lax.broadcasted_iota(jnp.int32, (tq, tq), 1)
            s = jnp.where(on_diag, s, NEG)
        m_new = jnp.maximum(m, jnp.max(s, axis=0, keepdims=True))
        a = jnp.exp2(m - m_new)
        p = jnp.exp2(s - m_new)
        l = a * l + jnp.sum(p, axis=0, keepdims=True)
        start = pl.multiple_of(j * tq, tq)
        acc_ref[...] = a * acc_ref[...] + _dot(vt_ref[:, pl.ds(start, tq)], p.astype(BF16))
        return m_new, l

    def query_tile(i, first, second):
        acc_ref[...] = jnp.zeros_like(acc_ref)

        def pair(jj, stats):
            scores(i, 2 * jj + 1, second)
            stats = update(2 * jj, first, stats, False)
            scores(i, 2 * jj + 2, first)
            return update(2 * jj + 1, second, stats, False)

        init = (jnp.full((1, tq), NEG, F32), jnp.zeros((1, tq), F32))
        stats = lax.fori_loop(0, i // 2, pair, init)
        nxt = jnp.minimum(i + 1, nq - 1)

        def odd_tail(stats):
            scores(i, i, second)
            stats = update(i - 1, first, stats, False)
            scores(nxt, 0, first)
            return update(i, second, stats, True)

        def even_tail(stats):
            scores(nxt, 0, second)
            return update(i, first, stats, True)

        _, l = lax.cond(i % 2 == 1, odd_tail, even_tail, stats)
        o_ref[pl.ds(pl.multiple_of(i * tq, tq), tq), :] = (acc_ref[...] / l).T.astype(o_ref.dtype)
        return 0

    scores(0, 0, s0_ref)

    def query_loop(i, _):
        return lax.cond(((i + 1) // 2) % 2 == 0, lambda: query_tile(i, s0_ref, s1_ref),
                        lambda: query_tile(i, s1_ref, s0_ref))

    lax.fori_loop(0, nq, query_loop, 0)


def _flash(qt, k, vt, c, *, heads, dq, dk, dv, tq):
    b, s, _ = k.shape
    use_c = c is not None
    in_specs = [pl.BlockSpec((None, dq, s), lambda bi, h: (bi, h, 0)),
                pl.BlockSpec((None, s, dk), lambda bi, h: (bi, 0, h)),
                pl.BlockSpec((None, dv, s), lambda bi, h: (bi, h, 0))]
    args = [qt, k, vt]
    scratch = [pltpu.VMEM((dv, tq), F32), pltpu.VMEM((tq, tq), F32), pltpu.VMEM((tq, tq), F32)]
    if use_c:
        in_specs.append(pl.BlockSpec((None, None, 1, s), lambda bi, h: (bi, h, 0, 0)))
        args.append(c)
        scratch.append(pltpu.VMEM((s, LANES), BF16))
    return pl.pallas_call(
        functools.partial(_flash_kernel, tq=tq, use_c=use_c),
        grid=(b, heads),
        in_specs=in_specs,
        out_specs=pl.BlockSpec((None, s, dv), lambda bi, h: (bi, 0, h)),
        out_shape=jax.ShapeDtypeStruct((b, s, heads * dv), BF16),
        scratch_shapes=scratch,
        compiler_params=_params("parallel", "parallel"),
        name="flash_fox" if use_c else "flash_mla",
    )(*args)


def _dilated_kernel(q_ref, kc_ref, kp_ref, vc_ref, vp_ref, o_ref, lse_ref, *, tu):
    i = pl.program_id(2)
    nk = DIL_SPAN + tu
    a = lax.broadcasted_iota(jnp.int32, (tu, nk), 0)
    c = lax.broadcasted_iota(jnp.int32, (tu, nk), 1)
    back = a + DIL_SPAN - c
    first = jnp.where(i > 0, 0, DIL_SPAN)
    bias = jnp.where(back >= 0, jnp.where(back <= DIL_SPAN, jnp.where(c >= first, 0.0, NEG), NEG), NEG)
    lane = lax.broadcasted_iota(jnp.int32, (tu, LANES), 1)
    lse_all = jnp.zeros((tu, LANES), F32)
    for hd in range(DIL_HEADS):
        sl = slice(hd * 128, (hd + 1) * 128)
        k = jnp.concatenate([kp_ref[tu - DIL_SPAN:, sl], kc_ref[:, sl]], axis=0)
        v = jnp.concatenate([vp_ref[tu - DIL_SPAN:, sl], vc_ref[:, sl]], axis=0)
        s = _dot_nt(q_ref[:, sl], k) + bias
        m = jnp.max(s, axis=-1, keepdims=True)
        p = jnp.exp(s - m)
        l = jnp.sum(p, axis=-1, keepdims=True)
        o_ref[:, sl] = _dot(p.astype(BF16), v) / l
        lse_all = jnp.where(lane == hd, m + jnp.log(l), lse_all)
    lse_ref[...] = lse_all


def _dilated_group(dqkv, g, dil, batch, seq, tu):
    su = seq // dil
    nblk = dqkv.shape[1] // DIL_WIDTH
    view = dqkv.reshape(batch, su, dil * dqkv.shape[1])
    cur = lambda blk: pl.BlockSpec((None, tu, DIL_WIDTH), lambda b, r, i: (b, i, r * nblk + blk))
    prev = lambda blk: pl.BlockSpec((None, tu, DIL_WIDTH), lambda b, r, i: (b, jnp.maximum(i - 1, 0), r * nblk + blk))
    o, lse = pl.pallas_call(
        functools.partial(_dilated_kernel, tu=tu),
        grid=(batch, dil, su // tu),
        in_specs=[cur(g), cur(3), prev(3), cur(4), prev(4)],
        out_specs=[pl.BlockSpec((None, tu, DIL_WIDTH), lambda b, r, i: (b, i, r)),
                   pl.BlockSpec((None, tu, LANES), lambda b, r, i: (b, i, r))],
        out_shape=[jax.ShapeDtypeStruct((batch, su, dil * DIL_WIDTH), F32),
                   jax.ShapeDtypeStruct((batch, su, dil * LANES), F32)],
        compiler_params=_params("parallel", "parallel", "arbitrary"),
        name=f"dilated_{dil}",
    )(view, view, view, view, view)
    return o.reshape(batch * seq, DIL_WIDTH), lse.reshape(batch * seq, LANES)


def _even_out_kernel(om_ref, o0, o1, o2, l0, l1, l2, wm_ref, wd_ref, x_ref, g_ref, b_ref, o_ref, ob_ref):
    ls = [l0[...], l1[...], l2[...]]
    mx = jnp.maximum(jnp.maximum(ls[0], ls[1]), ls[2])
    es = [jnp.exp(v - mx) for v in ls]
    den = es[0] + es[1] + es[2]
    wts = [e / den for e in es]
    groups = [o0, o1, o2]
    heads = []
    for hd in range(DIL_HEADS):
        sl = slice(hd * 128, (hd + 1) * 128)
        acc = wts[0][:, hd:hd + 1] * groups[0][:, sl]
        for gi in (1, 2):
            acc = acc + wts[gi][:, hd:hd + 1] * groups[gi][:, sl]
        heads.append(acc.astype(BF16))
    o_dil = jnp.concatenate(heads, axis=1)
    y = _dot(om_ref[...], wm_ref[...]) + _dot(o_dil, wd_ref[...])
    out = _layer_norm(ALPHA * x_ref[...] + y, g_ref[...], b_ref[...])
    o_ref[...] = out
    ob_ref[...] = out.astype(BF16)


def _even_out(o_mla, outs, lses, w_mla, w_dil, x, g, b, tm):
    m = x.shape[0]
    row = lambda width: pl.BlockSpec((tm, width), lambda i: (i, 0))
    full = lambda shape: pl.BlockSpec(shape, lambda i: (0, 0))
    return pl.pallas_call(
        _even_out_kernel,
        grid=(m // tm,),
        in_specs=[row(o_mla.shape[1])] + [row(DIL_WIDTH)] * 3 + [row(LANES)] * 3
        + [full(w_mla.shape), full(w_dil.shape), row(D_MODEL), full(g.shape), full(b.shape)],
        out_specs=[row(D_MODEL), row(D_MODEL)],
        out_shape=[jax.ShapeDtypeStruct((m, D_MODEL), F32), jax.ShapeDtypeStruct((m, D_MODEL), BF16)],
        compiler_params=_params("parallel"),
        name="even_out_ln",
    )(o_mla, *outs, *lses, w_mla, w_dil, x, g, b)


def _odd_out_kernel(a_ref, w_ref, x_ref, g_ref, b_ref, rw_ref, rb_ref, o_ref, route_ref, cnt_ref, carry_ref):
    out = _layer_norm(ALPHA * x_ref[...] + _dot(a_ref[...], w_ref[...]), g_ref[...], b_ref[...])
    o_ref[...] = out

    @pl.when(pl.program_id(0) == 0)
    def _():
        carry_ref[...] = jnp.zeros_like(carry_ref)

    logits = jnp.dot(out, rw_ref[...], preferred_element_type=F32, precision=lax.Precision.HIGHEST) + rb_ref[...]
    tm = logits.shape[0]
    lane = lax.broadcasted_iota(jnp.int32, (tm, LANES), 1)
    l1 = jnp.max(logits, axis=-1, keepdims=True)
    i1 = jnp.min(jnp.where(logits == l1, lane, LANES), axis=-1, keepdims=True)
    rest = jnp.where(lane == i1, NEG, logits)
    l2 = jnp.max(rest, axis=-1, keepdims=True)
    i2 = jnp.min(jnp.where(rest == l2, lane, LANES), axis=-1, keepdims=True)
    e = jnp.exp(l2 - l1)
    w1 = 1.0 / (1.0 + e)
    w2 = e / (1.0 + e)
    hot1 = (lane == i1).astype(F32)
    hot2 = (lane == i2).astype(F32)
    cnt = hot1 + hot2
    strict = (lax.broadcasted_iota(jnp.int32, (tm, tm), 1) < lax.broadcasted_iota(jnp.int32, (tm, tm), 0)).astype(BF16)
    before = _dot(strict, cnt.astype(BF16)) + carry_ref[...]
    r1 = jnp.sum(before * hot1, axis=-1, keepdims=True)
    r2 = jnp.sum(before * hot2, axis=-1, keepdims=True)
    vals = (i1.astype(F32), i2.astype(F32), w1, w2, r1, r2)
    route = jnp.zeros((tm, LANES), F32)
    for idx, val in enumerate(vals):
        route = jnp.where(lane == idx, val, route)
    route_ref[...] = route
    total = carry_ref[...] + jnp.sum(cnt, axis=0, keepdims=True)
    carry_ref[...] = total
    cnt_ref[...] = jnp.broadcast_to(total, cnt_ref.shape)


def _odd_out(a, w, x, g, b, rw, rb, tm):
    m = x.shape[0]
    row = lambda width: pl.BlockSpec((tm, width), lambda i: (i, 0))
    full = lambda shape: pl.BlockSpec(shape, lambda i: (0, 0))
    return pl.pallas_call(
        _odd_out_kernel,
        grid=(m // tm,),
        in_specs=[row(a.shape[1]), full(w.shape), row(D_MODEL), full(g.shape), full(b.shape), full(rw.shape),
                  full(rb.shape)],
        out_specs=[row(D_MODEL), row(LANES), pl.BlockSpec((8, LANES), lambda i: (0, 0))],
        out_shape=[jax.ShapeDtypeStruct((m, D_MODEL), F32), jax.ShapeDtypeStruct((m, LANES), F32),
                   jax.ShapeDtypeStruct((8, LANES), F32)],
        scratch_shapes=[pltpu.VMEM((1, LANES), F32)],
        compiler_params=_params("arbitrary"),
        name="odd_out_ln_route",
    )(a, w, x, g, b, rw, rb)


def _silu_mul(g, u):
    return g * (1.0 / (1.0 + jnp.exp(-g))) * u


def _ffn_up_kernel(x_ref, wg_ref, wu_ref, o_ref, wgb_ref, wub_ref):
    @pl.when(pl.program_id(1) == 0)
    def _():
        wgb_ref[...] = wg_ref[...].astype(BF16)
        wub_ref[...] = wu_ref[...].astype(BF16)

    x = x_ref[...]
    o_ref[...] = _silu_mul(_dot(x, wgb_ref[...]), _dot(x, wub_ref[...])).astype(BF16)


def _ffn_up(xb, wg, wu, tm, tn):
    m = xb.shape[0]
    return pl.pallas_call(
        _ffn_up_kernel,
        grid=(D_FF // tn, m // tm),
        in_specs=[pl.BlockSpec((tm, D_MODEL), lambda j, i: (i, 0)),
                  pl.BlockSpec((D_MODEL, tn), lambda j, i: (0, j)),
                  pl.BlockSpec((D_MODEL, tn), lambda j, i: (0, j))],
        out_specs=pl.BlockSpec((tm, tn), lambda j, i: (i, j)),
        out_shape=jax.ShapeDtypeStruct((m, D_FF), BF16),
        scratch_shapes=[pltpu.VMEM((D_MODEL, tn), BF16), pltpu.VMEM((D_MODEL, tn), BF16)],
        compiler_params=_params("arbitrary", "arbitrary"),
        name="ffn_up",
    )(xb, wg, wu)


def _ffn_down_ln_kernel(h_ref, w_ref, x_ref, g_ref, b_ref, o_ref, ob_ref, acc_ref):
    k = pl.program_id(1)

    @pl.when(k == 0)
    def _():
        acc_ref[...] = jnp.zeros_like(acc_ref)

    acc_ref[...] += _dot(h_ref[...], w_ref[...])

    @pl.when(k == pl.num_programs(1) - 1)
    def _():
        out = _layer_norm(ALPHA * x_ref[...] + acc_ref[...], g_ref[...], b_ref[...])
        o_ref[...] = out
        ob_ref[...] = out.astype(BF16)


def _ffn_down_ln(h, wd, x, g, b, tm, tk):
    m = x.shape[0]
    row = pl.BlockSpec((tm, D_MODEL), lambda i, k: (i, 0))
    res = pl.BlockSpec((tm, D_MODEL), lambda i, k: (i, 0), pipeline_mode=pl.Buffered(1))
    vec = pl.BlockSpec((1, D_MODEL), lambda i, k: (0, 0))
    return pl.pallas_call(
        _ffn_down_ln_kernel,
        grid=(m // tm, D_FF // tk),
        in_specs=[pl.BlockSpec((tm, tk), lambda i, k: (i, k)), pl.BlockSpec((tk, D_MODEL), lambda i, k: (k, 0)),
                  res, vec, vec],
        out_specs=[row, row],
        out_shape=[jax.ShapeDtypeStruct((m, D_MODEL), F32), jax.ShapeDtypeStruct((m, D_MODEL), BF16)],
        scratch_shapes=[pltpu.VMEM((tm, D_MODEL), F32)],
        compiler_params=_params("parallel", "arbitrary"),
        name="ffn_down_ln",
    )(h, wd, x, g, b)


def _proj_kernel(x_ref, w_ref, o_ref):
    o_ref[...] = _dot(x_ref[...], w_ref[...]).astype(BF16)


def _proj(xb, w, tm, tn):
    m = xb.shape[0]
    n = w.shape[1]
    return pl.pallas_call(
        _proj_kernel,
        grid=(n // tn, m // tm),
        in_specs=[pl.BlockSpec((tm, D_MODEL), lambda j, i: (i, 0)), pl.BlockSpec((D_MODEL, tn), lambda j, i: (0, j))],
        out_specs=pl.BlockSpec((tm, tn), lambda j, i: (i, j)),
        out_shape=jax.ShapeDtypeStruct((m, n), BF16),
        compiler_params=_params("parallel", "parallel"),
        name="proj",
    )(xb, w)


def _proj_t_kernel(w_ref, x_ref, o_ref, *, scale):
    o_ref[...] = (_dot_nt(w_ref[...], x_ref[...]) * scale).astype(BF16)


def _proj_t(xb3, wt, scale, tm, tn):
    b, s, _ = xb3.shape
    n = wt.shape[0]
    return pl.pallas_call(
        functools.partial(_proj_t_kernel, scale=scale),
        grid=(n // tn, b, s // tm),
        in_specs=[pl.BlockSpec((tn, D_MODEL), lambda j, bi, i: (j, 0)),
                  pl.BlockSpec((None, tm, D_MODEL), lambda j, bi, i: (bi, i, 0))],
        out_specs=pl.BlockSpec((None, tn, tm), lambda j, bi, i: (bi, j, i)),
        out_shape=jax.ShapeDtypeStruct((b, n, s), BF16),
        compiler_params=_params("parallel", "parallel", "parallel"),
        name="proj_t",
    )(wt, xb3)


def _fgate_kernel(x_ref, w_ref, b_ref, c_ref, carry_ref):
    @pl.when(pl.program_id(1) == 0)
    def _():
        carry_ref[...] = jnp.zeros_like(carry_ref)

    z = _dot(x_ref[...], w_ref[...]) + b_ref[...]
    log_f = jnp.minimum(z, 0.0) - jnp.log(1.0 + jnp.exp(-jnp.abs(z)))
    tm = z.shape[0]
    tri = (lax.broadcasted_iota(jnp.int32, (tm, tm), 1) <= lax.broadcasted_iota(jnp.int32, (tm, tm), 0)).astype(F32)
    c = jnp.dot(tri, log_f, preferred_element_type=F32, precision=lax.Precision.HIGHEST) + carry_ref[...]
    c_ref[...] = c
    carry_ref[...] = c[tm - 1:tm, :]


def _fgate(xb3, wf, bf, tm):
    b, s, _ = xb3.shape
    return pl.pallas_call(
        _fgate_kernel,
        grid=(b, s // tm),
        in_specs=[pl.BlockSpec((None, tm, D_MODEL), lambda bi, i: (bi, i, 0)),
                  pl.BlockSpec(wf.shape, lambda bi, i: (0, 0)), pl.BlockSpec(bf.shape, lambda bi, i: (0, 0))],
        out_specs=pl.BlockSpec((None, tm, LANES), lambda bi, i: (bi, i, 0)),
        out_shape=jax.ShapeDtypeStruct((b, s, LANES), F32),
        scratch_shapes=[pltpu.VMEM((1, LANES), F32)],
        compiler_params=_params("parallel", "arbitrary"),
        name="fox_gate",
    )(xb3, wf, bf)


def _dispatch_kernel(pos_ref, x_ref, zeros_ref, xs_ref, sem):
    del zeros_ref
    tm = x_ref.shape[0]

    def copy(r, k):
        return pltpu.make_async_copy(x_ref.at[pl.ds(r, 1), :], xs_ref.at[pl.ds(pos_ref[0, 0, 2 * r + k], 1), :], sem)

    def start(r, _):
        copy(r, 0).start()
        copy(r, 1).start()
        return 0

    def wait(r, _):
        copy(r, 0).wait()
        copy(r, 1).wait()
        return 0

    lax.fori_loop(0, tm, start, 0, unroll=DMA_UNROLL)
    lax.fori_loop(0, tm, wait, 0, unroll=DMA_UNROLL)


def _dispatch(x, pos, n_rows, tm):
    m, width = x.shape
    pos3 = pos.reshape(m // tm, 1, 2 * tm)
    return pl.pallas_call(
        _dispatch_kernel,
        grid=(m // tm,),
        in_specs=[pl.BlockSpec((1, 1, 2 * tm), lambda i: (i, 0, 0), memory_space=pltpu.SMEM),
                  pl.BlockSpec((tm, width), lambda i: (i, 0)),
                  pl.BlockSpec(memory_space=pl.ANY)],
        out_specs=pl.BlockSpec(memory_space=pl.ANY),
        out_shape=jax.ShapeDtypeStruct((n_rows, width), x.dtype),
        scratch_shapes=[pltpu.SemaphoreType.DMA(())],
        input_output_aliases={2: 0},
        compiler_params=_params("arbitrary"),
        name="moe_dispatch",
    )(pos3, x, jnp.zeros((n_rows, width), x.dtype))


def _moe_up_kernel(te_ref, nu_ref, x_ref, wg_ref, wu_ref, o_ref, wgb_ref, wub_ref):
    t = pl.program_id(1)
    live = t < nu_ref[0]
    fresh = jnp.logical_or(t == 0, te_ref[t] != te_ref[jnp.maximum(t - 1, 0)])

    @pl.when(jnp.logical_and(live, fresh))
    def _():
        wgb_ref[...] = wg_ref[...].astype(BF16)
        wub_ref[...] = wu_ref[...].astype(BF16)

    @pl.when(live)
    def _():
        x = x_ref[...].astype(BF16)
        o_ref[...] = _silu_mul(_dot(x, wgb_ref[...]), _dot(x, wub_ref[...])).astype(BF16)

    @pl.when(jnp.logical_not(live))
    def _():
        o_ref[...] = jnp.zeros_like(o_ref)


def _moe_up(xs, wg, wu, tile_expert, n_used, tm, tn):
    p = xs.shape[0]
    live = lambda t, nu: jnp.minimum(t, nu[0] - 1)
    return pl.pallas_call(
        _moe_up_kernel,
        grid_spec=pltpu.PrefetchScalarGridSpec(
            num_scalar_prefetch=2,
            grid=(D_FF // tn, p // tm),
            in_specs=[pl.BlockSpec((tm, D_MODEL), lambda j, t, te, nu: (live(t, nu), 0)),
                      pl.BlockSpec((None, D_MODEL, tn), lambda j, t, te, nu: (te[live(t, nu)], 0, j)),
                      pl.BlockSpec((None, D_MODEL, tn), lambda j, t, te, nu: (te[live(t, nu)], 0, j))],
            out_specs=pl.BlockSpec((tm, tn), lambda j, t, te, nu: (t, j)),
            scratch_shapes=[pltpu.VMEM((D_MODEL, tn), BF16), pltpu.VMEM((D_MODEL, tn), BF16)],
        ),
        out_shape=jax.ShapeDtypeStruct((p, D_FF), BF16),
        compiler_params=_params("arbitrary", "arbitrary"),
        name="moe_up",
    )(tile_expert, n_used, xs, wg, wu)


def _moe_down_kernel(te_ref, nu_ref, h_ref, w_ref, o_ref, wb_ref):
    t = pl.program_id(1)
    live = t < nu_ref[0]
    fresh = jnp.logical_or(t == 0, te_ref[t] != te_ref[jnp.maximum(t - 1, 0)])

    @pl.when(jnp.logical_and(live, fresh))
    def _():
        wb_ref[...] = w_ref[...].astype(BF16)

    @pl.when(live)
    def _():
        o_ref[...] = _dot(h_ref[...], wb_ref[...])

    @pl.when(jnp.logical_not(live))
    def _():
        o_ref[...] = jnp.zeros_like(o_ref)


def _moe_down(hs, wd, tile_expert, n_used, tm, tn):
    p = hs.shape[0]
    live = lambda t, nu: jnp.minimum(t, nu[0] - 1)
    return pl.pallas_call(
        _moe_down_kernel,
        grid_spec=pltpu.PrefetchScalarGridSpec(
            num_scalar_prefetch=2,
            grid=(D_MODEL // tn, p // tm),
            in_specs=[pl.BlockSpec((tm, D_FF), lambda j, t, te, nu: (live(t, nu), 0)),
                      pl.BlockSpec((None, D_FF, tn), lambda j, t, te, nu: (te[live(t, nu)], 0, j))],
            out_specs=pl.BlockSpec((tm, tn), lambda j, t, te, nu: (t, j)),
            scratch_shapes=[pltpu.VMEM((D_FF, tn), BF16)],
        ),
        out_shape=jax.ShapeDtypeStruct((p, D_MODEL), F32),
        compiler_params=_params("arbitrary", "arbitrary"),
        name="moe_down",
    )(tile_expert, n_used, hs, wd)


def _combine_ln_kernel(pos_ref, ys_ref, x_ref, route_ref, g_ref, b_ref, o_ref, buf0, buf1, sem):
    tm = x_ref.shape[0]

    def copies(r):
        c0 = pltpu.make_async_copy(ys_ref.at[pl.ds(pos_ref[0, 0, 2 * r], 1), :], buf0.at[pl.ds(r, 1), :], sem)
        c1 = pltpu.make_async_copy(ys_ref.at[pl.ds(pos_ref[0, 0, 2 * r + 1], 1), :], buf1.at[pl.ds(r, 1), :], sem)
        return c0, c1

    def start(r, _):
        c0, c1 = copies(r)
        c0.start()
        c1.start()
        return 0

    def wait(r, _):
        c0, c1 = copies(r)
        c0.wait()
        c1.wait()
        return 0

    lax.fori_loop(0, tm, start, 0, unroll=DMA_UNROLL)
    lax.fori_loop(0, tm, wait, 0, unroll=DMA_UNROLL)
    route = route_ref[...]
    y = route[:, 2:3] * buf0[...] + route[:, 3:4] * buf1[...]
    o_ref[...] = _layer_norm(ALPHA * x_ref[...] + y, g_ref[...], b_ref[...])


def _combine_ln(ys, pos, x, route, g, b, tm):
    m = x.shape[0]
    pos3 = pos.reshape(m // tm, 1, 2 * tm)
    vec = pl.BlockSpec((1, D_MODEL), lambda i: (0, 0))
    return pl.pallas_call(
        _combine_ln_kernel,
        grid=(m // tm,),
        in_specs=[pl.BlockSpec((1, 1, 2 * tm), lambda i: (i, 0, 0), memory_space=pltpu.SMEM),
                  pl.BlockSpec(memory_space=pl.ANY),
                  pl.BlockSpec((tm, D_MODEL), lambda i: (i, 0)),
                  pl.BlockSpec((tm, LANES), lambda i: (i, 0)), vec, vec],
        out_specs=pl.BlockSpec((tm, D_MODEL), lambda i: (i, 0)),
        out_shape=jax.ShapeDtypeStruct((m, D_MODEL), F32),
        scratch_shapes=[pltpu.VMEM((tm, D_MODEL), F32), pltpu.VMEM((tm, D_MODEL), F32), pltpu.SemaphoreType.DMA(())],
        compiler_params=_params("arbitrary"),
        name="moe_combine_ln",
    )(pos3, ys, x, route, g, b)


def _rope_tables(seq):
    def angles(dim):
        inv_freq = 1.0 / (ROPE_THETA ** (jnp.arange(0, dim, 2, dtype=F32) / dim))
        ang = jnp.arange(seq, dtype=F32)[:, None] * inv_freq[None, :]
        return jnp.cos(ang), jnp.sin(ang)

    c128, s128 = angles(HEAD_DIM)
    cos_f = jnp.concatenate([c128, c128], axis=-1)
    sin_f = jnp.concatenate([-s128, s128], axis=-1)
    c64, s64 = angles(MLA_ROPE_DIM)
    z32 = jnp.zeros_like(c64)
    cos_t = jnp.concatenate([c64, c64, z32, z32], axis=-1)
    sin_a = jnp.concatenate([-s64, z32, z32, z32], axis=-1)
    sin_b = jnp.concatenate([z32, s64, z32, z32], axis=-1)
    return cos_f, sin_f, cos_t, sin_a, sin_b, c64.T, s64.T


def _pad_cols(a, width):
    return jnp.pad(a, ((0, 0), (0, width - a.shape[1])))


def _row(v):
    return v.reshape(1, -1).astype(F32)


def _even_layer(x, xb, batch, seq, tables, w_in, q_norm, w_q_b, kv_norm, w_kv_b, w_out, ln1_g, ln1_b,
                w_gate, w_up, w_down, ln2_g, ln2_b):
    cos_f, sin_f, cos_t, sin_a, sin_b, cos_tt, sin_tt = tables
    tm = min(512, seq)
    wa = jnp.concatenate([w_in[:, OFF_CKV:OFF_KROPE], _pad_cols(w_in[:, OFF_KROPE:OFF_DQ], LANES),
                          _pad_cols(w_in[:, OFF_CQ:OFF_CKV], MLA_Q_RANK_PAD)], axis=1).astype(BF16)
    qg = _pad_cols(_row(q_norm), MLA_Q_RANK_PAD)
    wq = jnp.pad(w_q_b.reshape(MLA_Q_RANK, MLA_HEADS, MLA_QK_DIM),
                 ((0, MLA_Q_RANK_PAD - MLA_Q_RANK), (0, 0), (0, MLA_QK_PAD - MLA_QK_DIM)))
    wqt = wq.reshape(MLA_Q_RANK_PAD, MLA_HEADS * MLA_QK_PAD).T.astype(BF16)
    wkv3 = w_kv_b.reshape(MLA_KV_RANK, MLA_HEADS, MLA_NOPE_DIM + MLA_V_DIM)
    wk = wkv3[:, :, :MLA_NOPE_DIM].reshape(MLA_KV_RANK, -1).astype(BF16)
    wvt = wkv3[:, :, MLA_NOPE_DIM:].reshape(MLA_KV_RANK, -1).T.astype(BF16)
    xb3 = xb.reshape(batch, seq, D_MODEL)
    qt_mla, k_mla, vt_mla = _mla_prep(xb3, wa, _row(kv_norm), qg, wqt, wk, wvt, cos_t, sin_a, sin_b,
                                      cos_tt, sin_tt, tm)
    tq = min(FLASH_TQ, seq)
    o_mla = _flash(qt_mla, k_mla, vt_mla, None, heads=MLA_HEADS, dq=MLA_QK_PAD, dk=MLA_QK_PAD, dv=MLA_V_DIM,
                   tq=tq)
    o_mla = o_mla.reshape(batch * seq, -1)

    dqkv = _dqkv(xb, w_in[:, OFF_DQ:].astype(BF16), cos_f, sin_f, seq, tm)
    outs, lses = [], []
    for g, (window, dil) in enumerate(DIL_PATTERNS):
        assert window == DIL_SPAN * dil
        o_g, lse_g = _dilated_group(dqkv, g, dil, batch, seq, min(256, seq // dil))
        outs.append(o_g)
        lses.append(lse_g)
    n_mla = MLA_HEADS * MLA_V_DIM
    wo = w_out.astype(BF16)
    x1, x1b = _even_out(o_mla, outs, lses, wo[:n_mla], wo[n_mla:], x, _row(ln1_g), _row(ln1_b), tm)
    hmid = _ffn_up(x1b, w_gate, w_up, min(1024, seq), 512)
    return _ffn_down_ln(hmid, w_down.astype(BF16), x1, _row(ln2_g), _row(ln2_b), min(1024, seq), 512)


def _odd_layer(x, xb, batch, seq, w_qkv, w_f, b_f, w_out, ln1_g, ln1_b, router_w, router_b,
               exp_w_gate, exp_w_up, exp_w_down, ln2_g, ln2_b):
    m = batch * seq
    tm = min(512, seq)
    xb3 = xb.reshape(batch, seq, D_MODEL)
    wb = w_qkv.astype(BF16)
    qt = _proj_t(xb3, wb[:, :FOX_WIDTH].T, HEAD_DIM ** -0.5 * LOG2E, tm, 1024)
    k = _proj(xb, wb[:, FOX_WIDTH:2 * FOX_WIDTH], tm, 1024).reshape(batch, seq, FOX_WIDTH)
    vt = _proj_t(xb3, wb[:, 2 * FOX_WIDTH:].T, 1.0, tm, 1024)
    c = _fgate(xb3, _pad_cols(w_f, LANES).astype(BF16), _pad_cols(_row(b_f), LANES), min(256, seq))
    c_t = jnp.transpose(c[:, :, :FOX_HEADS], (0, 2, 1)).reshape(batch, FOX_HEADS, 1, seq)
    tq = min(FLASH_TQ, seq)
    o = _flash(qt, k, vt, c_t, heads=FOX_HEADS, dq=HEAD_DIM, dk=HEAD_DIM, dv=HEAD_DIM, tq=tq)
    rb = jnp.full((1, LANES), NEG, F32).at[0, :N_EXPERTS].set(router_b.astype(F32))
    x1, route, counts = _odd_out(o.reshape(m, -1), w_out.astype(BF16), x, _row(ln1_g), _row(ln1_b),
                                 _pad_cols(router_w, LANES), rb, tm)
    tile = MOE_TILE
    n_tiles = (2 * m) // tile + N_EXPERTS
    cnt = counts[0, :N_EXPERTS].astype(jnp.int32)
    tiles_per = (cnt + tile - 1) // tile
    tile_end = jnp.cumsum(tiles_per)
    offset = (tile_end - tiles_per) * tile
    idx = route[:, 0:2].astype(jnp.int32)
    pos = (offset[idx] + route[:, 4:6].astype(jnp.int32)).reshape(-1)
    n_used = tile_end[-1:]
    tile_ids = jnp.arange(n_tiles, dtype=jnp.int32)
    tile_expert = jnp.minimum(jnp.sum((tile_end[None, :] <= tile_ids[:, None]).astype(jnp.int32), axis=1),
                              N_EXPERTS - 1)
    td = min(256, seq)
    xs = _dispatch(x1, pos, n_tiles * tile, td)
    hs = _moe_up(xs, exp_w_gate, exp_w_up, tile_expert, n_used, tile, 512)
    split = tile // MOE_DOWN_TILE
    ys = _moe_down(hs, exp_w_down, jnp.repeat(tile_expert, split), n_used * split, MOE_DOWN_TILE, 512)
    return _combine_ln(ys, pos, x1, route, _row(ln2_g), _row(ln2_b), td)


def kernel(x, ev_w_in, ev_q_norm, ev_w_q_b, ev_kv_norm, ev_w_kv_b, ev_w_out, ev_ln1_g, ev_ln1_b, ev_ffn_w_gate, ev_ffn_w_up, ev_ffn_w_down, ev_ln2_g, ev_ln2_b, od_w_qkv, od_w_f, od_b_f, od_w_out, od_ln1_g, od_ln1_b, od_router_w, od_router_b, od_exp_w_gate, od_exp_w_up, od_exp_w_down, od_ln2_g, od_ln2_b):
    batch, seq, _ = x.shape
    tables = _rope_tables(seq)
    h = x.reshape(batch * seq, D_MODEL)
    hb = h.astype(BF16)
    for layer in range(DEPTH):
        i = layer // 2
        if layer % 2 == 0:
            h, hb = _even_layer(h, hb, batch, seq, tables, ev_w_in[i], ev_q_norm[i], ev_w_q_b[i], ev_kv_norm[i],
                                ev_w_kv_b[i], ev_w_out[i], ev_ln1_g[i], ev_ln1_b[i], ev_ffn_w_gate[i],
                                ev_ffn_w_up[i], ev_ffn_w_down[i], ev_ln2_g[i], ev_ln2_b[i])
        else:
            h = _odd_layer(h, hb, batch, seq, od_w_qkv[i], od_w_f[i], od_b_f[i], od_w_out[i], od_ln1_g[i],
                           od_ln1_b[i], od_router_w[i], od_router_b[i], od_exp_w_gate[i], od_exp_w_up[i],
                           od_exp_w_down[i], od_ln2_g[i], od_ln2_b[i])
            hb = h.astype(BF16)
    return h.reshape(batch, seq, D_MODEL)
```

```python
import functools

import jax
import jax.numpy as jnp
from jax import lax
from jax.experimental import pallas as pl
from jax.experimental.pallas import tpu as pltpu

F32 = jnp.float32
BF16 = jnp.bfloat16

D_MODEL = 2048
HEAD_DIM = 128
LANES = 128
ROPE_THETA = 10000.0
LN_EPS = 1e-5
RMS_EPS = 1e-6

MLA_HEADS = 10
MLA_Q_RANK = 448
MLA_Q_RANK_PAD = 512
MLA_KV_RANK = 128
MLA_NOPE_DIM = 128
MLA_ROPE_DIM = 64
MLA_V_DIM = 128
MLA_QK_DIM = MLA_NOPE_DIM + MLA_ROPE_DIM
MLA_QK_PAD = 256

DIL_PATTERNS = ((128, 1), (512, 4), (2048, 16))
DIL_GROUPS = 3
DIL_HEADS = 6
DIL_WIDTH = DIL_HEADS * HEAD_DIM
DIL_SPAN = 128

OFF_CQ = 0
OFF_CKV = OFF_CQ + MLA_Q_RANK
OFF_KROPE = OFF_CKV + MLA_KV_RANK
OFF_DQ = OFF_KROPE + MLA_ROPE_DIM
OFF_DK = OFF_DQ + DIL_GROUPS * DIL_WIDTH
OFF_DV = OFF_DK + DIL_WIDTH
W_IN_COLS = OFF_DV + DIL_WIDTH

FOX_HEADS = 16
FOX_WIDTH = FOX_HEADS * HEAD_DIM

D_FF = 5632
N_EXPERTS = 8
DEPTH = 2
ALPHA = (2.0 * DEPTH) ** 0.25

NEG = -1e30
LOG2E = 1.4426950408889634
FLASH_TQ = 1024
MOE_TILE = 512
MOE_DOWN_TILE = 512
DIL_TILE = 1024
DMA_UNROLL = 8
VMEM_LIMIT = 56 * 1024 * 1024


def _params(*sem, vmem=VMEM_LIMIT):
    return pltpu.CompilerParams(dimension_semantics=sem, vmem_limit_bytes=vmem)


def _dot(a, b):
    return jnp.dot(a, b, preferred_element_type=F32)


def _dot_nt(a, b):
    return lax.dot_general(a, b, (((1,), (1,)), ((), ())), preferred_element_type=F32)


def _layer_norm(y, g, b):
    mu = jnp.mean(y, axis=-1, keepdims=True)
    d = y - mu
    var = jnp.mean(d * d, axis=-1, keepdims=True)
    return d * lax.rsqrt(var + LN_EPS) * g + b


def _rope128(x, cos_f, sin_f):
    return x * cos_f + pltpu.roll(x, 64, 1) * sin_f


def _rope64(x, cos_t, sin_a, sin_b):
    return x * cos_t + pltpu.roll(x, 96, 1) * sin_a + pltpu.roll(x, 32, 1) * sin_b


def _mla_prep_kernel(x_ref, wa_ref, kvg_ref, qg_ref, wqt_ref, wk_ref, wvt_ref, cos_ref, sa_ref, sb_ref,
                     ct_ref, st_ref, qt_ref, k_ref, vt_ref):
    h = _dot(x_ref[...], wa_ref[...])
    ckv = h[:, 0:128]
    kr = h[:, 128:256]
    cq = h[:, 256:768]
    ckv_n = (ckv * lax.rsqrt(jnp.mean(ckv * ckv, axis=-1, keepdims=True) + RMS_EPS) * kvg_ref[...]).astype(BF16)
    cq_ms = jnp.sum(cq * cq, axis=-1, keepdims=True) * (1.0 / MLA_Q_RANK)
    cq_n = (cq * lax.rsqrt(cq_ms + RMS_EPS) * qg_ref[...]).astype(BF16)
    kr_r = _rope64(kr, cos_ref[...], sa_ref[...], sb_ref[...]).astype(BF16)
    scale = MLA_QK_DIM ** -0.5 * LOG2E
    qt = _dot_nt(wqt_ref[...], cq_n)
    c, s = ct_ref[...], st_ref[...]
    k_nope = _dot(ckv_n, wk_ref[...])
    for hd in range(MLA_HEADS):
        o = hd * MLA_QK_PAD
        qt_ref[o:o + 128, :] = (qt[o:o + 128] * scale).astype(BF16)
        x1, x2 = qt[o + 128:o + 160], qt[o + 160:o + 192]
        qt_ref[o + 128:o + 160, :] = ((x1 * c - x2 * s) * scale).astype(BF16)
        qt_ref[o + 160:o + 192, :] = ((x2 * c + x1 * s) * scale).astype(BF16)
        qt_ref[o + 192:o + 256, :] = jnp.zeros((64, qt.shape[1]), BF16)
        k_ref[:, o:o + 128] = k_nope[:, hd * 128:(hd + 1) * 128].astype(BF16)
        k_ref[:, o + 128:o + 256] = kr_r
    vt_ref[...] = _dot_nt(wvt_ref[...], ckv_n).astype(BF16)


def _mla_prep(xb3, wa, kvg, qg, wqt, wk, wvt, cos_t, sin_a, sin_b, cos_tt, sin_tt, tm):
    b, s, _ = xb3.shape
    full = lambda shape: pl.BlockSpec(shape, lambda bi, i: (0, 0))
    tab = pl.BlockSpec((tm, LANES), lambda bi, i: (i, 0))
    tab_t = pl.BlockSpec((MLA_ROPE_DIM // 2, tm), lambda bi, i: (0, i))
    wide = MLA_HEADS * MLA_QK_PAD
    vw = MLA_HEADS * MLA_V_DIM
    return pl.pallas_call(
        _mla_prep_kernel,
        grid=(b, s // tm),
        in_specs=[pl.BlockSpec((None, tm, D_MODEL), lambda bi, i: (bi, i, 0)), full(wa.shape), full(kvg.shape),
                  full(qg.shape), full(wqt.shape), full(wk.shape), full(wvt.shape), tab, tab, tab, tab_t, tab_t],
        out_specs=[pl.BlockSpec((None, wide, tm), lambda bi, i: (bi, 0, i)),
                   pl.BlockSpec((None, tm, wide), lambda bi, i: (bi, i, 0)),
                   pl.BlockSpec((None, vw, tm), lambda bi, i: (bi, 0, i))],
        out_shape=[jax.ShapeDtypeStruct((b, wide, s), BF16), jax.ShapeDtypeStruct((b, s, wide), BF16),
                   jax.ShapeDtypeStruct((b, vw, s), BF16)],
        compiler_params=_params("parallel", "parallel"),
        name="mla_prep",
    )(xb3, wa, kvg, qg, wqt, wk, wvt, cos_t, sin_a, sin_b, cos_tt, sin_tt)


def _dqkv_kernel(x_ref, w_ref, cos_ref, sin_ref, o_ref):
    j = pl.program_id(0)
    h = _dot(x_ref[...], w_ref[...])

    @pl.when(j < 4)
    def _():
        cos_f, sin_f = cos_ref[...], sin_ref[...]
        sc = jnp.where(j < 3, HEAD_DIM ** -0.5, 1.0).astype(F32)
        for hd in range(DIL_HEADS):
            sl = slice(hd * 128, (hd + 1) * 128)
            o_ref[:, sl] = (_rope128(h[:, sl], cos_f, sin_f) * sc).astype(BF16)

    @pl.when(j == 4)
    def _():
        o_ref[...] = h.astype(BF16)


def _dqkv(xb, wd, cos_f, sin_f, seq, tm):
    m = xb.shape[0]
    nrow = seq // tm
    n_col = wd.shape[1] // DIL_WIDTH
    tab = pl.BlockSpec((tm, LANES), lambda j, i: (i % nrow, 0))
    return pl.pallas_call(
        _dqkv_kernel,
        grid=(n_col, m // tm),
        in_specs=[pl.BlockSpec((tm, D_MODEL), lambda j, i: (i, 0)),
                  pl.BlockSpec((D_MODEL, DIL_WIDTH), lambda j, i: (0, j)), tab, tab],
        out_specs=pl.BlockSpec((tm, DIL_WIDTH), lambda j, i: (i, j)),
        out_shape=jax.ShapeDtypeStruct((m, wd.shape[1]), BF16),
        compiler_params=_params("parallel", "parallel"),
        name="dil_qkv",
    )(xb, wd, cos_f, sin_f)


def _flash_kernel(*refs, tq, use_c):
    if use_c:
        qt_ref, k_ref, vt_ref, c_ref, o_ref, acc_ref, s0_ref, s1_ref, kaug_ref = refs
    else:
        qt_ref, k_ref, vt_ref, o_ref, acc_ref, s0_ref, s1_ref = refs
    seq = k_ref.shape[0]
    nq = seq // tq
    if use_c:
        row = lax.broadcasted_iota(jnp.int32, (LANES, tq), 0)

        def build(j, _):
            start = pl.multiple_of(j * tq, tq)
            neg = -LOG2E * c_ref[:, pl.ds(start, tq)]
            hi = neg.astype(BF16).astype(F32)
            mid = (neg - hi).astype(BF16).astype(F32)
            lo = neg - hi - mid
            blk = jnp.where(row == 0, hi, jnp.where(row == 1, mid, jnp.where(row == 2, lo, 0.0)))
            kaug_ref[pl.ds(start, tq), :] = blk.T.astype(BF16)
            return 0

        lax.fori_loop(0, nq, build, 0)

    def scores(i, j, s_ref):
        q = qt_ref[:, pl.ds(pl.multiple_of(i * tq, tq), tq)]
        if use_c:
            ones = (lax.broadcasted_iota(jnp.int32, (LANES, tq), 0) < 3).astype(BF16)
            q = jnp.concatenate([q, ones], axis=0)
        start = pl.multiple_of(j * tq, tq)
        kt = k_ref[pl.ds(start, tq), :]
        if use_c:
            kt = jnp.concatenate([kt, kaug_ref[pl.ds(start, tq), :]], axis=1)
        s_ref[...] = _dot(kt, q)

    def update(j, s_ref, stats, masked):
        m, l = stats
        s = s_ref[...]
        if masked:
            on_diag = lax.broadcasted_iota(jnp.int32, (tq, tq), 0) <= lax.broadcasted_iota(jnp.int32, (tq, tq), 1)
            s = jnp.where(on_diag, s, NEG)
        m_new = jnp.maximum(m, jnp.max(s, axis=0, keepdims=True))
        a = jnp.exp2(m - m_new)
        p = jnp.exp2(s - m_new)
        l = a * l + jnp.sum(p, axis=0, keepdims=True)
        start = pl.multiple_of(j * tq, tq)
        acc_ref[...] = a * acc_ref[...] + _dot(vt_ref[:, pl.ds(start, tq)], p.astype(BF16))
        return m_new, l

    def query_tile(i, first, second):
        acc_ref[...] = jnp.zeros_like(acc_ref)

        def pair(jj, stats):
            scores(i, 2 * jj + 1, second)
            stats = update(2 * jj, first, stats, False)
            scores(i, 2 * jj + 2, first)
            return update(2 * jj + 1, second, stats, False)

        init = (jnp.full((1, tq), NEG, F32), jnp.zeros((1, tq), F32))
        stats = lax.fori_loop(0, i // 2, pair, init)
        nxt = jnp.minimum(i + 1, nq - 1)

        def odd_tail(stats):
            scores(i, i, second)
            stats = update(i - 1, first, stats, False)
            scores(nxt, 0, first)
            return update(i, second, stats, True)

        def even_tail(stats):
            scores(nxt, 0, second)
            return update(i, first, stats, True)

        _, l = lax.cond(i % 2 == 1, odd_tail, even_tail, stats)
        o_ref[pl.ds(pl.multiple_of(i * tq, tq), tq), :] = (acc_ref[...] / l).T.astype(o_ref.dtype)
        return 0

    scores(0, 0, s0_ref)

    def query_loop(i, _):
        return lax.cond(((i + 1) // 2) % 2 == 0, lambda: query_tile(i, s0_ref, s1_ref),
                        lambda: query_tile(i, s1_ref, s0_ref))

    lax.fori_loop(0, nq, query_loop, 0)


def _flash(qt, k, vt, c, *, heads, dq, dk, dv, tq):
    b, s, _ = k.shape
    use_c = c is not None
    in_specs = [pl.BlockSpec((None, dq, s), lambda bi, h: (bi, h, 0)),
                pl.BlockSpec((None, s, dk), lambda bi, h: (bi, 0, h)),
                pl.BlockSpec((None, dv, s), lambda bi, h: (bi, h, 0))]
    args = [qt, k, vt]
    scratch = [pltpu.VMEM((dv, tq), F32), pltpu.VMEM((tq, tq), F32), pltpu.VMEM((tq, tq), F32)]
    if use_c:
        in_specs.append(pl.BlockSpec((None, None, 1, s), lambda bi, h: (bi, h, 0, 0)))
        args.append(c)
        scratch.append(pltpu.VMEM((s, LANES), BF16))
    return pl.pallas_call(
        functools.partial(_flash_kernel, tq=tq, use_c=use_c),
        grid=(b, heads),
        in_specs=in_specs,
        out_specs=pl.BlockSpec((None, s, dv), lambda bi, h: (bi, 0, h)),
        out_shape=jax.ShapeDtypeStruct((b, s, heads * dv), BF16),
        scratch_shapes=scratch,
        compiler_params=_params("parallel", "parallel"),
        name="flash_fox" if use_c else "flash_mla",
    )(*args)


def _dilated_kernel(q_ref, kc_ref, kp_ref, vc_ref, vp_ref, o_ref, lse_ref, qf, kf, vf, *, dil, tn):
    i = pl.program_id(1)
    per_class = tn // dil
    qf[...] = q_ref[...].astype(F32)
    kf[0:tn, :] = kp_ref[...].astype(F32)
    kf[tn:2 * tn, :] = kc_ref[...].astype(F32)
    vf[0:tn, :] = vp_ref[...].astype(F32)
    vf[tn:2 * tn, :] = vc_ref[...].astype(F32)
    row = lax.broadcasted_iota(jnp.int32, (DIL_SPAN, 2 * DIL_SPAN), 0)
    col = lax.broadcasted_iota(jnp.int32, (DIL_SPAN, 2 * DIL_SPAN), 1)
    back = row + DIL_SPAN - col
    in_band = jnp.where(back >= 0, jnp.where(back <= DIL_SPAN, 0.0, NEG), NEG)
    first_band = jnp.where(col >= jnp.where(i > 0, 0, DIL_SPAN), in_band, NEG)
    for r in range(dil):
        k_r = jnp.concatenate([kf[pl.ds(tn - DIL_SPAN * dil + r, DIL_SPAN, stride=dil), :],
                               kf[pl.ds(tn + r, per_class, stride=dil), :]], axis=0).astype(BF16)
        v_r = jnp.concatenate([vf[pl.ds(tn - DIL_SPAN * dil + r, DIL_SPAN, stride=dil), :],
                               vf[pl.ds(tn + r, per_class, stride=dil), :]], axis=0).astype(BF16)
        q_r = qf[pl.ds(r, per_class, stride=dil), :].astype(BF16)
        for a in range(per_class // DIL_SPAN):
            lo = a * DIL_SPAN
            s = _dot_nt(q_r[lo:lo + DIL_SPAN], k_r[lo:lo + 2 * DIL_SPAN]) + (first_band if a == 0 else in_band)
            m = jnp.max(s, axis=-1, keepdims=True)
            p = jnp.exp(s - m)
            l = jnp.sum(p, axis=-1, keepdims=True)
            rows = pl.ds(r + lo * dil, DIL_SPAN, stride=dil)
            o_ref[rows, :] = _dot(p.astype(BF16), v_r[lo:lo + 2 * DIL_SPAN]) / l
            lse_ref[rows, :] = jnp.broadcast_to(m + jnp.log(l), (DIL_SPAN, LANES))


def _dilated_group(dqkv3, g, dil, tn):
    batch, seq, _ = dqkv3.shape
    assert tn % (DIL_SPAN * dil) == 0
    blk = lambda col, prev: pl.BlockSpec(
        (None, tn, HEAD_DIM), lambda b, i, h: (b, jnp.maximum(i - 1, 0) if prev else i, col * DIL_HEADS + h))
    out = pl.BlockSpec((None, tn, HEAD_DIM), lambda b, i, h: (b, i, h))
    o, lse = pl.pallas_call(
        functools.partial(_dilated_kernel, dil=dil, tn=tn),
        grid=(batch, seq // tn, DIL_HEADS),
        in_specs=[blk(g, False), blk(3, False), blk(3, True), blk(4, False), blk(4, True)],
        out_specs=[out, out],
        out_shape=[jax.ShapeDtypeStruct((batch, seq, DIL_WIDTH), F32)] * 2,
        scratch_shapes=[pltpu.VMEM((tn, HEAD_DIM), F32), pltpu.VMEM((2 * tn, HEAD_DIM), F32),
                        pltpu.VMEM((2 * tn, HEAD_DIM), F32)],
        compiler_params=_params("parallel", "parallel", "parallel"),
        name=f"dilated_{dil}",
    )(dqkv3, dqkv3, dqkv3, dqkv3, dqkv3)
    return o.reshape(batch * seq, DIL_WIDTH), lse.reshape(batch * seq, DIL_WIDTH)


def _even_out_kernel(om_ref, o0, o1, o2, l0, l1, l2, wm_ref, wd_ref, x_ref, g_ref, b_ref, o_ref, ob_ref):
    ls = [l0[...], l1[...], l2[...]]
    mx = jnp.maximum(jnp.maximum(ls[0], ls[1]), ls[2])
    es = [jnp.exp(v - mx) for v in ls]
    den = es[0] + es[1] + es[2]
    o_dil = ((es[0] / den) * o0[...] + (es[1] / den) * o1[...] + (es[2] / den) * o2[...]).astype(BF16)
    y = _dot(om_ref[...], wm_ref[...]) + _dot(o_dil, wd_ref[...])
    out = _layer_norm(ALPHA * x_ref[...] + y, g_ref[...], b_ref[...])
    o_ref[...] = out
    ob_ref[...] = out.astype(BF16)


def _even_out(o_mla, outs, lses, w_mla, w_dil, x, g, b, tm):
    m = x.shape[0]
    row = lambda width: pl.BlockSpec((tm, width), lambda i: (i, 0))
    full = lambda shape: pl.BlockSpec(shape, lambda i: (0, 0))
    return pl.pallas_call(
        _even_out_kernel,
        grid=(m // tm,),
        in_specs=[row(o_mla.shape[1])] + [row(DIL_WIDTH)] * 6
        + [full(w_mla.shape), full(w_dil.shape), row(D_MODEL), full(g.shape), full(b.shape)],
        out_specs=[row(D_MODEL), row(D_MODEL)],
        out_shape=[jax.ShapeDtypeStruct((m, D_MODEL), F32), jax.ShapeDtypeStruct((m, D_MODEL), BF16)],
        compiler_params=_params("parallel"),
        name="even_out_ln",
    )(o_mla, *outs, *lses, w_mla, w_dil, x, g, b)


def _odd_out_kernel(a_ref, w_ref, x_ref, g_ref, b_ref, o_ref):
    o_ref[...] = _layer_norm(ALPHA * x_ref[...] + _dot(a_ref[...], w_ref[...]), g_ref[...], b_ref[...])


def _odd_out(a, w, x, g, b, tm):
    m = x.shape[0]
    row = lambda width: pl.BlockSpec((tm, width), lambda i: (i, 0))
    full = lambda shape: pl.BlockSpec(shape, lambda i: (0, 0))
    return pl.pallas_call(
        _odd_out_kernel,
        grid=(m // tm,),
        in_specs=[row(a.shape[1]), full(w.shape), row(D_MODEL), full(g.shape), full(b.shape)],
        out_specs=row(D_MODEL),
        out_shape=jax.ShapeDtypeStruct((m, D_MODEL), F32),
        compiler_params=_params("parallel"),
        name="odd_out_ln",
    )(a, w, x, g, b)


def _router_kernel(x_ref, rw_ref, rb_ref, route_ref, cnt_ref, carry_ref):
    @pl.when(pl.program_id(0) == 0)
    def _():
        carry_ref[...] = jnp.zeros_like(carry_ref)

    logits = jnp.dot(x_ref[...], rw_ref[...], preferred_element_type=F32, precision=lax.Precision.HIGHEST) + rb_ref[...]
    tm = logits.shape[0]
    lane = lax.broadcasted_iota(jnp.int32, (tm, LANES), 1)
    l1 = jnp.max(logits, axis=-1, keepdims=True)
    i1 = jnp.min(jnp.where(logits == l1, lane, LANES), axis=-1, keepdims=True)
    rest = jnp.where(lane == i1, NEG, logits)
    l2 = jnp.max(rest, axis=-1, keepdims=True)
    i2 = jnp.min(jnp.where(rest == l2, lane, LANES), axis=-1, keepdims=True)
    e = jnp.exp(l2 - l1)
    w1 = 1.0 / (1.0 + e)
    w2 = e / (1.0 + e)
    hot1 = (lane == i1).astype(F32)
    hot2 = (lane == i2).astype(F32)
    cnt = hot1 + hot2
    strict = (lax.broadcasted_iota(jnp.int32, (tm, tm), 1) < lax.broadcasted_iota(jnp.int32, (tm, tm), 0)).astype(BF16)
    before = _dot(strict, cnt.astype(BF16)) + carry_ref[...]
    r1 = jnp.sum(before * hot1, axis=-1, keepdims=True)
    r2 = jnp.sum(before * hot2, axis=-1, keepdims=True)
    vals = (i1.astype(F32), i2.astype(F32), w1, w2, r1, r2)
    route = jnp.zeros((tm, LANES), F32)
    for idx, val in enumerate(vals):
        route = jnp.where(lane == idx, val, route)
    route_ref[...] = route
    total = carry_ref[...] + jnp.sum(cnt, axis=0, keepdims=True)
    carry_ref[...] = total
    cnt_ref[...] = jnp.broadcast_to(total, cnt_ref.shape)


def _router(x, rw, rb, tm):
    m = x.shape[0]
    full = lambda shape: pl.BlockSpec(shape, lambda i: (0, 0))
    return pl.pallas_call(
        _router_kernel,
        grid=(m // tm,),
        in_specs=[pl.BlockSpec((tm, D_MODEL), lambda i: (i, 0)), full(rw.shape), full(rb.shape)],
        out_specs=[pl.BlockSpec((tm, LANES), lambda i: (i, 0)), pl.BlockSpec((8, LANES), lambda i: (0, 0))],
        out_shape=[jax.ShapeDtypeStruct((m, LANES), F32), jax.ShapeDtypeStruct((8, LANES), F32)],
        scratch_shapes=[pltpu.VMEM((1, LANES), F32)],
        compiler_params=_params("arbitrary"),
        name="moe_router",
    )(x, rw, rb)


def _silu_mul(g, u):
    return g * (1.0 / (1.0 + jnp.exp(-g))) * u


def _ffn_up_kernel(x_ref, wg_ref, wu_ref, o_ref, wgb_ref, wub_ref):
    @pl.when(pl.program_id(1) == 0)
    def _():
        wgb_ref[...] = wg_ref[...].astype(BF16)
        wub_ref[...] = wu_ref[...].astype(BF16)

    x = x_ref[...]
    o_ref[...] = _silu_mul(_dot(x, wgb_ref[...]), _dot(x, wub_ref[...])).astype(BF16)


def _ffn_up(xb, wg, wu, tm, tn):
    m = xb.shape[0]
    return pl.pallas_call(
        _ffn_up_kernel,
        grid=(D_FF // tn, m // tm),
        in_specs=[pl.BlockSpec((tm, D_MODEL), lambda j, i: (i, 0)),
                  pl.BlockSpec((D_MODEL, tn), lambda j, i: (0, j)),
                  pl.BlockSpec((D_MODEL, tn), lambda j, i: (0, j))],
        out_specs=pl.BlockSpec((tm, tn), lambda j, i: (i, j)),
        out_shape=jax.ShapeDtypeStruct((m, D_FF), BF16),
        scratch_shapes=[pltpu.VMEM((D_MODEL, tn), BF16), pltpu.VMEM((D_MODEL, tn), BF16)],
        compiler_params=_params("arbitrary", "arbitrary"),
        name="ffn_up",
    )(xb, wg, wu)


def _ffn_down_ln_kernel(h_ref, w_ref, x_ref, g_ref, b_ref, o_ref, ob_ref, acc_ref):
    k = pl.program_id(1)

    @pl.when(k == 0)
    def _():
        acc_ref[...] = jnp.zeros_like(acc_ref)

    acc_ref[...] += _dot(h_ref[...], w_ref[...])

    @pl.when(k == pl.num_programs(1) - 1)
    def _():
        out = _layer_norm(ALPHA * x_ref[...] + acc_ref[...], g_ref[...], b_ref[...])
        o_ref[...] = out
        ob_ref[...] = out.astype(BF16)


def _ffn_down_ln(h, wd, x, g, b, tm, tk):
    m = x.shape[0]
    row = pl.BlockSpec((tm, D_MODEL), lambda i, k: (i, 0))
    res = row
    vec = pl.BlockSpec((1, D_MODEL), lambda i, k: (0, 0))
    return pl.pallas_call(
        _ffn_down_ln_kernel,
        grid=(m // tm, D_FF // tk),
        in_specs=[pl.BlockSpec((tm, tk), lambda i, k: (i, k)), pl.BlockSpec((tk, D_MODEL), lambda i, k: (k, 0)),
                  res, vec, vec],
        out_specs=[row, row],
        out_shape=[jax.ShapeDtypeStruct((m, D_MODEL), F32), jax.ShapeDtypeStruct((m, D_MODEL), BF16)],
        scratch_shapes=[pltpu.VMEM((tm, D_MODEL), F32)],
        compiler_params=_params("parallel", "arbitrary"),
        name="ffn_down_ln",
    )(h, wd, x, g, b)


def _proj_kernel(x_ref, w_ref, o_ref):
    o_ref[...] = _dot(x_ref[...], w_ref[...]).astype(BF16)


def _proj(xb, w, tm, tn):
    m = xb.shape[0]
    n = w.shape[1]
    return pl.pallas_call(
        _proj_kernel,
        grid=(n // tn, m // tm),
        in_specs=[pl.BlockSpec((tm, D_MODEL), lambda j, i: (i, 0)), pl.BlockSpec((D_MODEL, tn), lambda j, i: (0, j))],
        out_specs=pl.BlockSpec((tm, tn), lambda j, i: (i, j)),
        out_shape=jax.ShapeDtypeStruct((m, n), BF16),
        compiler_params=_params("parallel", "parallel"),
        name="proj",
    )(xb, w)


def _proj_t_kernel(w_ref, x_ref, o_ref, *, scale):
    o_ref[...] = (_dot_nt(w_ref[...], x_ref[...]) * scale).astype(BF16)


def _proj_t(xb3, wt, scale, tm, tn):
    b, s, _ = xb3.shape
    n = wt.shape[0]
    return pl.pallas_call(
        functools.partial(_proj_t_kernel, scale=scale),
        grid=(n // tn, b, s // tm),
        in_specs=[pl.BlockSpec((tn, D_MODEL), lambda j, bi, i: (j, 0)),
                  pl.BlockSpec((None, tm, D_MODEL), lambda j, bi, i: (bi, i, 0))],
        out_specs=pl.BlockSpec((None, tn, tm), lambda j, bi, i: (bi, j, i)),
        out_shape=jax.ShapeDtypeStruct((b, n, s), BF16),
        compiler_params=_params("parallel", "parallel", "parallel"),
        name="proj_t",
    )(wt, xb3)


def _fgate_kernel(x_ref, w_ref, b_ref, c_ref, carry_ref):
    @pl.when(pl.program_id(1) == 0)
    def _():
        carry_ref[...] = jnp.zeros_like(carry_ref)

    z = _dot(x_ref[...], w_ref[...]) + b_ref[...]
    log_f = jnp.minimum(z, 0.0) - jnp.log(1.0 + jnp.exp(-jnp.abs(z)))
    tm = z.shape[0]
    tri = (lax.broadcasted_iota(jnp.int32, (tm, tm), 1) <= lax.broadcasted_iota(jnp.int32, (tm, tm), 0)).astype(F32)
    c = jnp.dot(tri, log_f, preferred_element_type=F32, precision=lax.Precision.HIGHEST) + carry_ref[...]
    c_ref[...] = c
    carry_ref[...] = c[tm - 1:tm, :]


def _fgate(xb3, wf, bf, tm):
    b, s, _ = xb3.shape
    return pl.pallas_call(
        _fgate_kernel,
        grid=(b, s // tm),
        in_specs=[pl.BlockSpec((None, tm, D_MODEL), lambda bi, i: (bi, i, 0)),
                  pl.BlockSpec(wf.shape, lambda bi, i: (0, 0)), pl.BlockSpec(bf.shape, lambda bi, i: (0, 0))],
        out_specs=pl.BlockSpec((None, tm, LANES), lambda bi, i: (bi, i, 0)),
        out_shape=jax.ShapeDtypeStruct((b, s, LANES), F32),
        scratch_shapes=[pltpu.VMEM((1, LANES), F32)],
        compiler_params=_params("parallel", "arbitrary"),
        name="fox_gate",
    )(xb3, wf, bf)


def _dispatch_kernel(pos_ref, x_ref, zeros_ref, xs_ref, sem):
    del zeros_ref
    tm = x_ref.shape[0]

    def copy(r, k):
        return pltpu.make_async_copy(x_ref.at[pl.ds(r, 1), :], xs_ref.at[pl.ds(pos_ref[0, 0, 2 * r + k], 1), :], sem)

    def start(r, _):
        copy(r, 0).start()
        copy(r, 1).start()
        return 0

    def wait(r, _):
        copy(r, 0).wait()
        copy(r, 1).wait()
        return 0

    lax.fori_loop(0, tm, start, 0, unroll=DMA_UNROLL)
    lax.fori_loop(0, tm, wait, 0, unroll=DMA_UNROLL)


def _dispatch(x, pos, n_rows, tm):
    m, width = x.shape
    pos3 = pos.reshape(m // tm, 1, 2 * tm)
    return pl.pallas_call(
        _dispatch_kernel,
        grid=(m // tm,),
        in_specs=[pl.BlockSpec((1, 1, 2 * tm), lambda i: (i, 0, 0), memory_space=pltpu.SMEM),
                  pl.BlockSpec((tm, width), lambda i: (i, 0)),
                  pl.BlockSpec(memory_space=pl.ANY)],
        out_specs=pl.BlockSpec(memory_space=pl.ANY),
        out_shape=jax.ShapeDtypeStruct((n_rows, width), x.dtype),
        scratch_shapes=[pltpu.SemaphoreType.DMA(())],
        input_output_aliases={2: 0},
        compiler_params=_params("arbitrary"),
        name="moe_dispatch",
    )(pos3, x, jnp.zeros((n_rows, width), x.dtype))


def _moe_up_kernel(te_ref, nu_ref, x_ref, wg_ref, wu_ref, o_ref, wgb_ref, wub_ref):
    t = pl.program_id(1)
    live = t < nu_ref[0]
    fresh = jnp.logical_or(t == 0, te_ref[t] != te_ref[jnp.maximum(t - 1, 0)])

    @pl.when(jnp.logical_and(live, fresh))
    def _():
        wgb_ref[...] = wg_ref[...].astype(BF16)
        wub_ref[...] = wu_ref[...].astype(BF16)

    @pl.when(live)
    def _():
        x = x_ref[...].astype(BF16)
        o_ref[...] = _silu_mul(_dot(x, wgb_ref[...]), _dot(x, wub_ref[...])).astype(BF16)

    @pl.when(jnp.logical_not(live))
    def _():
        o_ref[...] = jnp.zeros_like(o_ref)


def _moe_up(xs, wg, wu, tile_expert, n_used, tm, tn):
    p = xs.shape[0]
    live = lambda t, nu: jnp.minimum(t, nu[0] - 1)
    return pl.pallas_call(
        _moe_up_kernel,
        grid_spec=pltpu.PrefetchScalarGridSpec(
            num_scalar_prefetch=2,
            grid=(D_FF // tn, p // tm),
            in_specs=[pl.BlockSpec((tm, D_MODEL), lambda j, t, te, nu: (live(t, nu), 0)),
                      pl.BlockSpec((None, D_MODEL, tn), lambda j, t, te, nu: (te[live(t, nu)], 0, j)),
                      pl.BlockSpec((None, D_MODEL, tn), lambda j, t, te, nu: (te[live(t, nu)], 0, j))],
            out_specs=pl.BlockSpec((tm, tn), lambda j, t, te, nu: (t, j)),
            scratch_shapes=[pltpu.VMEM((D_MODEL, tn), BF16), pltpu.VMEM((D_MODEL, tn), BF16)],
        ),
        out_shape=jax.ShapeDtypeStruct((p, D_FF), BF16),
        compiler_params=_params("arbitrary", "arbitrary"),
        name="moe_up",
    )(tile_expert, n_used, xs, wg, wu)


def _moe_down_kernel(te_ref, nu_ref, h_ref, w_ref, o_ref, wb_ref):
    t = pl.program_id(1)
    live = t < nu_ref[0]
    fresh = jnp.logical_or(t == 0, te_ref[t] != te_ref[jnp.maximum(t - 1, 0)])

    @pl.when(jnp.logical_and(live, fresh))
    def _():
        wb_ref[...] = w_ref[...].astype(BF16)

    @pl.when(live)
    def _():
        o_ref[...] = _dot(h_ref[...], wb_ref[...])

    @pl.when(jnp.logical_not(live))
    def _():
        o_ref[...] = jnp.zeros_like(o_ref)


def _moe_down(hs, wd, tile_expert, n_used, tm, tn):
    p = hs.shape[0]
    live = lambda t, nu: jnp.minimum(t, nu[0] - 1)
    return pl.pallas_call(
        _moe_down_kernel,
        grid_spec=pltpu.PrefetchScalarGridSpec(
            num_scalar_prefetch=2,
            grid=(D_MODEL // tn, p // tm),
            in_specs=[pl.BlockSpec((tm, D_FF), lambda j, t, te, nu: (live(t, nu), 0)),
                      pl.BlockSpec((None, D_FF, tn), lambda j, t, te, nu: (te[live(t, nu)], 0, j))],
            out_specs=pl.BlockSpec((tm, tn), lambda j, t, te, nu: (t, j)),
            scratch_shapes=[pltpu.VMEM((D_FF, tn), BF16)],
        ),
        out_shape=jax.ShapeDtypeStruct((p, D_MODEL), F32),
        compiler_params=_params("arbitrary", "arbitrary"),
        name="moe_down",
    )(tile_expert, n_used, hs, wd)


def _combine_ln_kernel(pos_ref, ys_ref, x_ref, route_ref, g_ref, b_ref, o_ref, buf0, buf1, sem):
    tm = x_ref.shape[0]

    def copies(r):
        c0 = pltpu.make_async_copy(ys_ref.at[pl.ds(pos_ref[0, 0, 2 * r], 1), :], buf0.at[pl.ds(r, 1), :], sem)
        c1 = pltpu.make_async_copy(ys_ref.at[pl.ds(pos_ref[0, 0, 2 * r + 1], 1), :], buf1.at[pl.ds(r, 1), :], sem)
        return c0, c1

    def start(r, _):
        c0, c1 = copies(r)
        c0.start()
        c1.start()
        return 0

    def wait(r, _):
        c0, c1 = copies(r)
        c0.wait()
        c1.wait()
        return 0

    lax.fori_loop(0, tm, start, 0, unroll=DMA_UNROLL)
    lax.fori_loop(0, tm, wait, 0, unroll=DMA_UNROLL)
    route = route_ref[...]
    y = route[:, 2:3] * buf0[...] + route[:, 3:4] * buf1[...]
    o_ref[...] = _layer_norm(ALPHA * x_ref[...] + y, g_ref[...], b_ref[...])


def _combine_ln(ys, pos, x, route, g, b, tm):
    m = x.shape[0]
    pos3 = pos.reshape(m // tm, 1, 2 * tm)
    vec = pl.BlockSpec((1, D_MODEL), lambda i: (0, 0))
    return pl.pallas_call(
        _combine_ln_kernel,
        grid=(m // tm,),
        in_specs=[pl.BlockSpec((1, 1, 2 * tm), lambda i: (i, 0, 0), memory_space=pltpu.SMEM),
                  pl.BlockSpec(memory_space=pl.ANY),
                  pl.BlockSpec((tm, D_MODEL), lambda i: (i, 0)),
                  pl.BlockSpec((tm, LANES), lambda i: (i, 0)), vec, vec],
        out_specs=pl.BlockSpec((tm, D_MODEL), lambda i: (i, 0)),
        out_shape=jax.ShapeDtypeStruct((m, D_MODEL), F32),
        scratch_shapes=[pltpu.VMEM((tm, D_MODEL), F32), pltpu.VMEM((tm, D_MODEL), F32), pltpu.SemaphoreType.DMA(())],
        compiler_params=_params("arbitrary"),
        name="moe_combine_ln",
    )(pos3, ys, x, route, g, b)


def _rope_tables(seq):
    def angles(dim):
        inv_freq = 1.0 / (ROPE_THETA ** (jnp.arange(0, dim, 2, dtype=F32) / dim))
        ang = jnp.arange(seq, dtype=F32)[:, None] * inv_freq[None, :]
        return jnp.cos(ang), jnp.sin(ang)

    c128, s128 = angles(HEAD_DIM)
    cos_f = jnp.concatenate([c128, c128], axis=-1)
    sin_f = jnp.concatenate([-s128, s128], axis=-1)
    c64, s64 = angles(MLA_ROPE_DIM)
    z32 = jnp.zeros_like(c64)
    cos_t = jnp.concatenate([c64, c64, z32, z32], axis=-1)
    sin_a = jnp.concatenate([-s64, z32, z32, z32], axis=-1)
    sin_b = jnp.concatenate([z32, s64, z32, z32], axis=-1)
    return cos_f, sin_f, cos_t, sin_a, sin_b, c64.T, s64.T


def _pad_cols(a, width):
    return jnp.pad(a, ((0, 0), (0, width - a.shape[1])))


def _row(v):
    return v.reshape(1, -1).astype(F32)


def _even_layer(x, xb, batch, seq, tables, w_in, q_norm, w_q_b, kv_norm, w_kv_b, w_out, ln1_g, ln1_b,
                w_gate, w_up, w_down, ln2_g, ln2_b):
    cos_f, sin_f, cos_t, sin_a, sin_b, cos_tt, sin_tt = tables
    tm = min(512, seq)
    wa = jnp.concatenate([w_in[:, OFF_CKV:OFF_KROPE], _pad_cols(w_in[:, OFF_KROPE:OFF_DQ], LANES),
                          _pad_cols(w_in[:, OFF_CQ:OFF_CKV], MLA_Q_RANK_PAD)], axis=1).astype(BF16)
    qg = _pad_cols(_row(q_norm), MLA_Q_RANK_PAD)
    wq = jnp.pad(w_q_b.reshape(MLA_Q_RANK, MLA_HEADS, MLA_QK_DIM),
                 ((0, MLA_Q_RANK_PAD - MLA_Q_RANK), (0, 0), (0, MLA_QK_PAD - MLA_QK_DIM)))
    wqt = wq.reshape(MLA_Q_RANK_PAD, MLA_HEADS * MLA_QK_PAD).T.astype(BF16)
    wkv3 = w_kv_b.reshape(MLA_KV_RANK, MLA_HEADS, MLA_NOPE_DIM + MLA_V_DIM)
    wk = wkv3[:, :, :MLA_NOPE_DIM].reshape(MLA_KV_RANK, -1).astype(BF16)
    wvt = wkv3[:, :, MLA_NOPE_DIM:].reshape(MLA_KV_RANK, -1).T.astype(BF16)
    xb3 = xb.reshape(batch, seq, D_MODEL)
    qt_mla, k_mla, vt_mla = _mla_prep(xb3, wa, _row(kv_norm), qg, wqt, wk, wvt, cos_t, sin_a, sin_b,
                                      cos_tt, sin_tt, tm)
    tq = min(FLASH_TQ, seq)
    o_mla = _flash(qt_mla, k_mla, vt_mla, None, heads=MLA_HEADS, dq=MLA_QK_PAD, dk=MLA_QK_PAD, dv=MLA_V_DIM,
                   tq=tq)
    o_mla = o_mla.reshape(batch * seq, -1)

    dqkv = _dqkv(xb, w_in[:, OFF_DQ:].astype(BF16), cos_f, sin_f, seq, tm)
    dqkv3 = dqkv.reshape(batch, seq, -1)
    outs, lses = [], []
    for g, (window, dil) in enumerate(DIL_PATTERNS):
        assert window == DIL_SPAN * dil
        o_g, lse_g = _dilated_group(dqkv3, g, dil, min(seq, max(DIL_TILE, DIL_SPAN * dil)))
        outs.append(o_g)
        lses.append(lse_g)
    n_mla = MLA_HEADS * MLA_V_DIM
    wo = w_out.astype(BF16)
    x1, x1b = _even_out(o_mla, outs, lses, wo[:n_mla], wo[n_mla:], x, _row(ln1_g), _row(ln1_b), tm)
    hmid = _ffn_up(x1b, w_gate, w_up, min(1024, seq), 512)
    return _ffn_down_ln(hmid, w_down.astype(BF16), x1, _row(ln2_g), _row(ln2_b), tm, 1408)


def _odd_layer(x, xb, batch, seq, w_qkv, w_f, b_f, w_out, ln1_g, ln1_b, router_w, router_b,
               exp_w_gate, exp_w_up, exp_w_down, ln2_g, ln2_b):
    m = batch * seq
    tm = min(512, seq)
    xb3 = xb.reshape(batch, seq, D_MODEL)
    wb = w_qkv.astype(BF16)
    qt = _proj_t(xb3, wb[:, :FOX_WIDTH].T, HEAD_DIM ** -0.5 * LOG2E, tm, 1024)
    k = _proj(xb, wb[:, FOX_WIDTH:2 * FOX_WIDTH], tm, 1024).reshape(batch, seq, FOX_WIDTH)
    vt = _proj_t(xb3, wb[:, 2 * FOX_WIDTH:].T, 1.0, tm, 1024)
    c = _fgate(xb3, _pad_cols(w_f, LANES).astype(BF16), _pad_cols(_row(b_f), LANES), min(256, seq))
    c_t = jnp.transpose(c[:, :, :FOX_HEADS], (0, 2, 1)).reshape(batch, FOX_HEADS, 1, seq)
    tq = min(FLASH_TQ, seq)
    o = _flash(qt, k, vt, c_t, heads=FOX_HEADS, dq=HEAD_DIM, dk=HEAD_DIM, dv=HEAD_DIM, tq=tq)
    rb = jnp.full((1, LANES), NEG, F32).at[0, :N_EXPERTS].set(router_b.astype(F32))
    x1 = _odd_out(o.reshape(m, -1), w_out.astype(BF16), x, _row(ln1_g), _row(ln1_b), tm)
    route, counts = _router(x1, _pad_cols(router_w, LANES), rb, min(256, seq))
    tile = MOE_TILE
    n_tiles = (2 * m) // tile + N_EXPERTS
    cnt = counts[0, :N_EXPERTS].astype(jnp.int32)
    tiles_per = (cnt + tile - 1) // tile
    tile_end = jnp.cumsum(tiles_per)
    offset = (tile_end - tiles_per) * tile
    idx = route[:, 0:2].astype(jnp.int32)
    pos = (offset[idx] + route[:, 4:6].astype(jnp.int32)).reshape(-1)
    n_used = tile_end[-1:]
    tile_ids = jnp.arange(n_tiles, dtype=jnp.int32)
    tile_expert = jnp.minimum(jnp.sum((tile_end[None, :] <= tile_ids[:, None]).astype(jnp.int32), axis=1),
                              N_EXPERTS - 1)
    td = min(256, seq)
    xs = _dispatch(x1, pos, n_tiles * tile, td)
    hs = _moe_up(xs, exp_w_gate, exp_w_up, tile_expert, n_used, tile, 512)
    split = tile // MOE_DOWN_TILE
    ys = _moe_down(hs, exp_w_down, jnp.repeat(tile_expert, split), n_used * split, MOE_DOWN_TILE, 512)
    return _combine_ln(ys, pos, x1, route, _row(ln2_g), _row(ln2_b), td)


def kernel(x, ev_w_in, ev_q_norm, ev_w_q_b, ev_kv_norm, ev_w_kv_b, ev_w_out, ev_ln1_g, ev_ln1_b, ev_ffn_w_gate, ev_ffn_w_up, ev_ffn_w_down, ev_ln2_g, ev_ln2_b, od_w_qkv, od_w_f, od_b_f, od_w_out, od_ln1_g, od_ln1_b, od_router_w, od_router_b, od_exp_w_gate, od_exp_w_up, od_exp_w_down, od_ln2_g, od_ln2_b):
    batch, seq, _ = x.shape
    tables = _rope_tables(seq)
    h = x.reshape(batch * seq, D_MODEL)
    hb = h.astype(BF16)
    for layer in range(DEPTH):
        i = layer // 2
        if layer % 2 == 0:
            h, hb = _even_layer(h, hb, batch, seq, tables, ev_w_in[i], ev_q_norm[i], ev_w_q_b[i], ev_kv_norm[i],
                                ev_w_kv_b[i], ev_w_out[i], ev_ln1_g[i], ev_ln1_b[i], ev_ffn_w_gate[i],
                                ev_ffn_w_up[i], ev_ffn_w_down[i], ev_ln2_g[i], ev_ln2_b[i])
        else:
            h = _odd_layer(h, hb, batch, seq, od_w_qkv[i], od_w_f[i], od_b_f[i], od_w_out[i], od_ln1_g[i],
                           od_ln1_b[i], od_router_w[i], od_router_b[i], od_exp_w_gate[i], od_exp_w_up[i],
                           od_exp_w_down[i], od_ln2_g[i], od_ln2_b[i])
            hb = h.astype(BF16)
    return h.reshape(batch, seq, D_MODEL)
```

```python
import functools

import jax
import jax.numpy as jnp
from jax import lax
from jax.experimental import pallas as pl
from jax.experimental.pallas import tpu as pltpu

F32 = jnp.float32
BF16 = jnp.bfloat16

D_MODEL = 2048
HEAD_DIM = 128
LANES = 128
ROPE_THETA = 10000.0
LN_EPS = 1e-5
RMS_EPS = 1e-6

MLA_HEADS = 10
MLA_Q_RANK = 448
MLA_Q_RANK_PAD = 512
MLA_KV_RANK = 128
MLA_NOPE_DIM = 128
MLA_ROPE_DIM = 64
MLA_V_DIM = 128
MLA_QK_DIM = MLA_NOPE_DIM + MLA_ROPE_DIM
MLA_QK_PAD = 256

DIL_PATTERNS = ((128, 1), (512, 4), (2048, 16))
DIL_GROUPS = 3
DIL_HEADS = 6
DIL_WIDTH = DIL_HEADS * HEAD_DIM
DIL_SPAN = 128

OFF_CQ = 0
OFF_CKV = OFF_CQ + MLA_Q_RANK
OFF_KROPE = OFF_CKV + MLA_KV_RANK
OFF_DQ = OFF_KROPE + MLA_ROPE_DIM
OFF_DK = OFF_DQ + DIL_GROUPS * DIL_WIDTH
OFF_DV = OFF_DK + DIL_WIDTH
W_IN_COLS = OFF_DV + DIL_WIDTH

FOX_HEADS = 16
FOX_WIDTH = FOX_HEADS * HEAD_DIM

D_FF = 5632
N_EXPERTS = 8
DEPTH = 2
ALPHA = (2.0 * DEPTH) ** 0.25

NEG = -1e30
LOG2E = 1.4426950408889634
FLASH_TQ = 1024
MOE_TILE = 512
MOE_DOWN_TILE = 512
DIL_TILE = 1024
DMA_UNROLL = 8
VMEM_LIMIT = 56 * 1024 * 1024


def _params(*sem, vmem=VMEM_LIMIT):
    return pltpu.CompilerParams(dimension_semantics=sem, vmem_limit_bytes=vmem)


def _dot(a, b):
    return jnp.dot(a, b, preferred_element_type=F32)


def _dot_nt(a, b):
    return lax.dot_general(a, b, (((1,), (1,)), ((), ())), preferred_element_type=F32)


def _layer_norm(y, g, b):
    mu = jnp.mean(y, axis=-1, keepdims=True)
    d = y - mu
    var = jnp.mean(d * d, axis=-1, keepdims=True)
    return d * lax.rsqrt(var + LN_EPS) * g + b


def _rope128(x, cos_f, sin_f):
    return x * cos_f + pltpu.roll(x, 64, 1) * sin_f


def _rope64(x, cos_t, sin_a, sin_b):
    return x * cos_t + pltpu.roll(x, 96, 1) * sin_a + pltpu.roll(x, 32, 1) * sin_b


def _mla_prep_kernel(x_ref, wa_ref, kvg_ref, qg_ref, wqt_ref, wk_ref, wvt_ref, cos_ref, sa_ref, sb_ref,
                     ct_ref, st_ref, qt_ref, k_ref, vt_ref):
    h = _dot(x_ref[...], wa_ref[...])
    ckv = h[:, 0:128]
    kr = h[:, 128:256]
    cq = h[:, 256:768]
    ckv_n = (ckv * lax.rsqrt(jnp.mean(ckv * ckv, axis=-1, keepdims=True) + RMS_EPS) * kvg_ref[...]).astype(BF16)
    cq_ms = jnp.sum(cq * cq, axis=-1, keepdims=True) * (1.0 / MLA_Q_RANK)
    cq_n = (cq * lax.rsqrt(cq_ms + RMS_EPS) * qg_ref[...]).astype(BF16)
    kr_r = _rope64(kr, cos_ref[...], sa_ref[...], sb_ref[...]).astype(BF16)
    scale = MLA_QK_DIM ** -0.5 * LOG2E
    qt = _dot_nt(wqt_ref[...], cq_n)
    c, s = ct_ref[...], st_ref[...]
    k_nope = _dot(ckv_n, wk_ref[...])
    for hd in range(MLA_HEADS):
        o = hd * MLA_QK_PAD
        qt_ref[o:o + 128, :] = (qt[o:o + 128] * scale).astype(BF16)
        x1, x2 = qt[o + 128:o + 160], qt[o + 160:o + 192]
        qt_ref[o + 128:o + 160, :] = ((x1 * c - x2 * s) * scale).astype(BF16)
        qt_ref[o + 160:o + 192, :] = ((x2 * c + x1 * s) * scale).astype(BF16)
        qt_ref[o + 192:o + 256, :] = jnp.zeros((64, qt.shape[1]), BF16)
        k_ref[:, o:o + 128] = k_nope[:, hd * 128:(hd + 1) * 128].astype(BF16)
        k_ref[:, o + 128:o + 256] = kr_r
    vt_ref[...] = _dot_nt(wvt_ref[...], ckv_n).astype(BF16)


def _mla_prep(xb3, wa, kvg, qg, wqt, wk, wvt, cos_t, sin_a, sin_b, cos_tt, sin_tt, tm):
    b, s, _ = xb3.shape
    full = lambda shape: pl.BlockSpec(shape, lambda bi, i: (0, 0))
    tab = pl.BlockSpec((tm, LANES), lambda bi, i: (i, 0))
    tab_t = pl.BlockSpec((MLA_ROPE_DIM // 2, tm), lambda bi, i: (0, i))
    wide = MLA_HEADS * MLA_QK_PAD
    vw = MLA_HEADS * MLA_V_DIM
    return pl.pallas_call(
        _mla_prep_kernel,
        grid=(b, s // tm),
        in_specs=[pl.BlockSpec((None, tm, D_MODEL), lambda bi, i: (bi, i, 0)), full(wa.shape), full(kvg.shape),
                  full(qg.shape), full(wqt.shape), full(wk.shape), full(wvt.shape), tab, tab, tab, tab_t, tab_t],
        out_specs=[pl.BlockSpec((None, wide, tm), lambda bi, i: (bi, 0, i)),
                   pl.BlockSpec((None, tm, wide), lambda bi, i: (bi, i, 0)),
                   pl.BlockSpec((None, vw, tm), lambda bi, i: (bi, 0, i))],
        out_shape=[jax.ShapeDtypeStruct((b, wide, s), BF16), jax.ShapeDtypeStruct((b, s, wide), BF16),
                   jax.ShapeDtypeStruct((b, vw, s), BF16)],
        compiler_params=_params("parallel", "parallel"),
        name="mla_prep",
    )(xb3, wa, kvg, qg, wqt, wk, wvt, cos_t, sin_a, sin_b, cos_tt, sin_tt)


def _dqkv_kernel(x_ref, w_ref, cos_ref, sin_ref, o_ref):
    j = pl.program_id(0)
    h = _dot(x_ref[...], w_ref[...])

    @pl.when(j < 4)
    def _():
        cos_f, sin_f = cos_ref[...], sin_ref[...]
        sc = jnp.where(j < 3, HEAD_DIM ** -0.5, 1.0).astype(F32)
        for hd in range(DIL_HEADS):
            sl = slice(hd * 128, (hd + 1) * 128)
            o_ref[:, sl] = (_rope128(h[:, sl], cos_f, sin_f) * sc).astype(BF16)

    @pl.when(j == 4)
    def _():
        o_ref[...] = h.astype(BF16)


def _dqkv(xb, wd, cos_f, sin_f, seq, tm):
    m = xb.shape[0]
    nrow = seq // tm
    n_col = wd.shape[1] // DIL_WIDTH
    tab = pl.BlockSpec((tm, LANES), lambda j, i: (i % nrow, 0))
    return pl.pallas_call(
        _dqkv_kernel,
        grid=(n_col, m // tm),
        in_specs=[pl.BlockSpec((tm, D_MODEL), lambda j, i: (i, 0)),
                  pl.BlockSpec((D_MODEL, DIL_WIDTH), lambda j, i: (0, j)), tab, tab],
        out_specs=pl.BlockSpec((tm, DIL_WIDTH), lambda j, i: (i, j)),
        out_shape=jax.ShapeDtypeStruct((m, wd.shape[1]), BF16),
        compiler_params=_params("parallel", "parallel"),
        name="dil_qkv",
    )(xb, wd, cos_f, sin_f)


def _flash_kernel(*refs, tq, use_c):
    if use_c:
        qt_ref, k_ref, vt_ref, c_ref, o_ref, acc_ref, s0_ref, s1_ref, kaug_ref = refs
    else:
        qt_ref, k_ref, vt_ref, o_ref, acc_ref, s0_ref, s1_ref = refs
    seq = k_ref.shape[0]
    nq = seq // tq
    if use_c:
        row = lax.broadcasted_iota(jnp.int32, (LANES, tq), 0)

        def build(j, _):
            start = pl.multiple_of(j * tq, tq)
            neg = -LOG2E * c_ref[:, pl.ds(start, tq)]
            hi = neg.astype(BF16).astype(F32)
            mid = (neg - hi).astype(BF16).astype(F32)
            lo = neg - hi - mid
            blk = jnp.where(row == 0, hi, jnp.where(row == 1, mid, jnp.where(row == 2, lo, 0.0)))
            kaug_ref[pl.ds(start, tq), :] = blk.T.astype(BF16)
            return 0

        lax.fori_loop(0, nq, build, 0)

    def scores(i, j, s_ref):
        q = qt_ref[:, pl.ds(pl.multiple_of(i * tq, tq), tq)]
        if use_c:
            ones = (lax.broadcasted_iota(jnp.int32, (LANES, tq), 0) < 3).astype(BF16)
            q = jnp.concatenate([q, ones], axis=0)
        start = pl.multiple_of(j * tq, tq)
        kt = k_ref[pl.ds(start, tq), :]
        if use_c:
            kt = jnp.concatenate([kt, kaug_ref[pl.ds(start, tq), :]], axis=1)
        s_ref[...] = _dot(kt, q)

    def update(j, s_ref, stats, masked):
        m, l = stats
        s = s_ref[...]
        if masked:
            on_diag = lax.broadcasted_iota(jnp.int32, (tq, tq), 0) <= lax.broadcasted_iota(jnp.int32, (tq, tq), 1)
            s = jnp.where(on_diag, s, NEG)
        m_new = jnp.maximum(m, jnp.max(s, axis=0, keepdims=True))
        a = jnp.exp2(m - m_new)
        p = jnp.exp2(s - m_new)
        l = a * l + jnp.sum(p, axis=0, keepdims=True)
        start = pl.multiple_of(j * tq, tq)
        acc_ref[...] = a * acc_ref[...] + _dot(vt_ref[:, pl.ds(start, tq)], p.astype(BF16))
        return m_new, l

    def query_tile(i, first, second):
        acc_ref[...] = jnp.zeros_like(acc_ref)

        def pair(jj, stats):
            scores(i, 2 * jj + 1, second)
            stats = update(2 * jj, first, stats, False)
            scores(i, 2 * jj + 2, first)
            return update(2 * jj + 1, second, stats, False)

        init = (jnp.full((1, tq), NEG, F32), jnp.zeros((1, tq), F32))
        stats = lax.fori_loop(0, i // 2, pair, init)
        nxt = jnp.minimum(i + 1, nq - 1)

        def odd_tail(stats):
            scores(i, i, second)
            stats = update(i - 1, first, stats, False)
            scores(nxt, 0, first)
            return update(i, second, stats, True)

        def even_tail(stats):
            scores(nxt, 0, second)
            return update(i, first, stats, True)

        _, l = lax.cond(i % 2 == 1, odd_tail, even_tail, stats)
        o_ref[pl.ds(pl.multiple_of(i * tq, tq), tq), :] = (acc_ref[...] / l).T.astype(o_ref.dtype)
        return 0

    scores(0, 0, s0_ref)

    def query_loop(i, _):
        return lax.cond(((i + 1) // 2) % 2 == 0, lambda: query_tile(i, s0_ref, s1_ref),
                        lambda: query_tile(i, s1_ref, s0_ref))

    lax.fori_loop(0, nq, query_loop, 0)


def _flash(qt, k, vt, c, *, heads, dq, dk, dv, tq):
    b, s, _ = k.shape
    use_c = c is not None
    in_specs = [pl.BlockSpec((None, dq, s), lambda bi, h: (bi, h, 0)),
                pl.BlockSpec((None, s, dk), lambda bi, h: (bi, 0, h)),
                pl.BlockSpec((None, dv, s), lambda bi, h: (bi, h, 0))]
    args = [qt, k, vt]
    scratch = [pltpu.VMEM((dv, tq), F32), pltpu.VMEM((tq, tq), F32), pltpu.VMEM((tq, tq), F32)]
    if use_c:
        in_specs.append(pl.BlockSpec((None, None, 1, s), lambda bi, h: (bi, h, 0, 0)))
        args.append(c)
        scratch.append(pltpu.VMEM((s, LANES), BF16))
    return pl.pallas_call(
        functools.partial(_flash_kernel, tq=tq, use_c=use_c),
        grid=(b, heads),
        in_specs=in_specs,
        out_specs=pl.BlockSpec((None, s, dv), lambda bi, h: (bi, 0, h)),
        out_shape=jax.ShapeDtypeStruct((b, s, heads * dv), BF16),
        scratch_shapes=scratch,
        compiler_params=_params("parallel", "parallel"),
        name="flash_fox" if use_c else "flash_mla",
    )(*args)


def _dilated_kernel(q_ref, kc_ref, kp_ref, vc_ref, vp_ref, o_ref, lse_ref, qf, kf, vf, *, dil, tn):
    i = pl.program_id(1)
    per_class = tn // dil
    qf[...] = q_ref[...].astype(F32)
    kf[0:tn, :] = kp_ref[...].astype(F32)
    kf[tn:2 * tn, :] = kc_ref[...].astype(F32)
    vf[0:tn, :] = vp_ref[...].astype(F32)
    vf[tn:2 * tn, :] = vc_ref[...].astype(F32)
    row = lax.broadcasted_iota(jnp.int32, (DIL_SPAN, 2 * DIL_SPAN), 0)
    col = lax.broadcasted_iota(jnp.int32, (DIL_SPAN, 2 * DIL_SPAN), 1)
    back = row + DIL_SPAN - col
    in_band = jnp.where(back >= 0, jnp.where(back <= DIL_SPAN, 0.0, NEG), NEG)
    first_band = jnp.where(col >= jnp.where(i > 0, 0, DIL_SPAN), in_band, NEG)
    for r in range(dil):
        k_r = jnp.concatenate([kf[pl.ds(tn - DIL_SPAN * dil + r, DIL_SPAN, stride=dil), :],
                               kf[pl.ds(tn + r, per_class, stride=dil), :]], axis=0).astype(BF16)
        v_r = jnp.concatenate([vf[pl.ds(tn - DIL_SPAN * dil + r, DIL_SPAN, stride=dil), :],
                               vf[pl.ds(tn + r, per_class, stride=dil), :]], axis=0).astype(BF16)
        q_r = qf[pl.ds(r, per_class, stride=dil), :].astype(BF16)
        for a in range(per_class // DIL_SPAN):
            lo = a * DIL_SPAN
            s = _dot_nt(q_r[lo:lo + DIL_SPAN], k_r[lo:lo + 2 * DIL_SPAN]) + (first_band if a == 0 else in_band)
            m = jnp.max(s, axis=-1, keepdims=True)
            p = jnp.exp(s - m)
            l = jnp.sum(p, axis=-1, keepdims=True)
            rows = pl.ds(r + lo * dil, DIL_SPAN, stride=dil)
            o_ref[rows, :] = _dot(p.astype(BF16), v_r[lo:lo + 2 * DIL_SPAN]) / l
            lse_ref[rows, :] = jnp.broadcast_to(m + jnp.log(l), (DIL_SPAN, LANES))


def _dilated_group(dqkv3, g, dil, tn):
    batch, seq, _ = dqkv3.shape
    assert tn % (DIL_SPAN * dil) == 0
    blk = lambda col, prev: pl.BlockSpec(
        (None, tn, HEAD_DIM), lambda b, i, h: (b, jnp.maximum(i - 1, 0) if prev else i, col * DIL_HEADS + h))
    out = pl.BlockSpec((None, tn, HEAD_DIM), lambda b, i, h: (b, i, h))
    o, lse = pl.pallas_call(
        functools.partial(_dilated_kernel, dil=dil, tn=tn),
        grid=(batch, seq // tn, DIL_HEADS),
        in_specs=[blk(g, False), blk(3, False), blk(3, True), blk(4, False), blk(4, True)],
        out_specs=[out, out],
        out_shape=[jax.ShapeDtypeStruct((batch, seq, DIL_WIDTH), F32)] * 2,
        scratch_shapes=[pltpu.VMEM((tn, HEAD_DIM), F32), pltpu.VMEM((2 * tn, HEAD_DIM), F32),
                        pltpu.VMEM((2 * tn, HEAD_DIM), F32)],
        compiler_params=_params("parallel", "parallel", "parallel"),
        name=f"dilated_{dil}",
    )(dqkv3, dqkv3, dqkv3, dqkv3, dqkv3)
    return o.reshape(batch * seq, DIL_WIDTH), lse.reshape(batch * seq, DIL_WIDTH)


def _even_out_kernel(om_ref, o0, o1, o2, l0, l1, l2, wm_ref, wd_ref, x_ref, g_ref, b_ref, o_ref, ob_ref):
    ls = [l0[...], l1[...], l2[...]]
    mx = jnp.maximum(jnp.maximum(ls[0], ls[1]), ls[2])
    es = [jnp.exp(v - mx) for v in ls]
    den = es[0] + es[1] + es[2]
    o_dil = ((es[0] / den) * o0[...] + (es[1] / den) * o1[...] + (es[2] / den) * o2[...]).astype(BF16)
    y = _dot(om_ref[...], wm_ref[...]) + _dot(o_dil, wd_ref[...])
    out = _layer_norm(ALPHA * x_ref[...] + y, g_ref[...], b_ref[...])
    o_ref[...] = out
    ob_ref[...] = out.astype(BF16)


def _even_out(o_mla, outs, lses, w_mla, w_dil, x, g, b, tm):
    m = x.shape[0]
    row = lambda width: pl.BlockSpec((tm, width), lambda i: (i, 0))
    full = lambda shape: pl.BlockSpec(shape, lambda i: (0, 0))
    return pl.pallas_call(
        _even_out_kernel,
        grid=(m // tm,),
        in_specs=[row(o_mla.shape[1])] + [row(DIL_WIDTH)] * 6
        + [full(w_mla.shape), full(w_dil.shape), row(D_MODEL), full(g.shape), full(b.shape)],
        out_specs=[row(D_MODEL), row(D_MODEL)],
        out_shape=[jax.ShapeDtypeStruct((m, D_MODEL), F32), jax.ShapeDtypeStruct((m, D_MODEL), BF16)],
        compiler_params=_params("parallel"),
        name="even_out_ln",
    )(o_mla, *outs, *lses, w_mla, w_dil, x, g, b)


def _odd_out_kernel(a_ref, w_ref, x_ref, g_ref, b_ref, o_ref):
    o_ref[...] = _layer_norm(ALPHA * x_ref[...] + _dot(a_ref[...], w_ref[...]), g_ref[...], b_ref[...])


def _odd_out(a, w, x, g, b, tm):
    m = x.shape[0]
    row = lambda width: pl.BlockSpec((tm, width), lambda i: (i, 0))
    full = lambda shape: pl.BlockSpec(shape, lambda i: (0, 0))
    return pl.pallas_call(
        _odd_out_kernel,
        grid=(m // tm,),
        in_specs=[row(a.shape[1]), full(w.shape), row(D_MODEL), full(g.shape), full(b.shape)],
        out_specs=row(D_MODEL),
        out_shape=jax.ShapeDtypeStruct((m, D_MODEL), F32),
        compiler_params=_params("parallel"),
        name="odd_out_ln",
    )(a, w, x, g, b)


def _router_kernel(x_ref, rw_ref, rb_ref, route_ref, cnt_ref, carry_ref):
    @pl.when(pl.program_id(0) == 0)
    def _():
        carry_ref[...] = jnp.zeros_like(carry_ref)

    logits = jnp.dot(x_ref[...], rw_ref[...], preferred_element_type=F32, precision=lax.Precision.HIGHEST) + rb_ref[...]
    tm = logits.shape[0]
    lane = lax.broadcasted_iota(jnp.int32, (tm, LANES), 1)
    l1 = jnp.max(logits, axis=-1, keepdims=True)
    i1 = jnp.min(jnp.where(logits == l1, lane, LANES), axis=-1, keepdims=True)
    rest = jnp.where(lane == i1, NEG, logits)
    l2 = jnp.max(rest, axis=-1, keepdims=True)
    i2 = jnp.min(jnp.where(rest == l2, lane, LANES), axis=-1, keepdims=True)
    e = jnp.exp(l2 - l1)
    w1 = 1.0 / (1.0 + e)
    w2 = e / (1.0 + e)
    hot1 = (lane == i1).astype(F32)
    hot2 = (lane == i2).astype(F32)
    cnt = hot1 + hot2
    strict = (lax.broadcasted_iota(jnp.int32, (tm, tm), 1) < lax.broadcasted_iota(jnp.int32, (tm, tm), 0)).astype(BF16)
    before = _dot(strict, cnt.astype(BF16)) + carry_ref[...]
    r1 = jnp.sum(before * hot1, axis=-1, keepdims=True)
    r2 = jnp.sum(before * hot2, axis=-1, keepdims=True)
    vals = (i1.astype(F32), i2.astype(F32), w1, w2, r1, r2)
    route = jnp.zeros((tm, LANES), F32)
    for idx, val in enumerate(vals):
        route = jnp.where(lane == idx, val, route)
    route_ref[...] = route
    total = carry_ref[...] + jnp.sum(cnt, axis=0, keepdims=True)
    carry_ref[...] = total
    cnt_ref[...] = jnp.broadcast_to(total, cnt_ref.shape)


def _router(x, rw, rb, tm):
    m = x.shape[0]
    full = lambda shape: pl.BlockSpec(shape, lambda i: (0, 0))
    return pl.pallas_call(
        _router_kernel,
        grid=(m // tm,),
        in_specs=[pl.BlockSpec((tm, D_MODEL), lambda i: (i, 0)), full(rw.shape), full(rb.shape)],
        out_specs=[pl.BlockSpec((tm, LANES), lambda i: (i, 0)), pl.BlockSpec((8, LANES), lambda i: (0, 0))],
        out_shape=[jax.ShapeDtypeStruct((m, LANES), F32), jax.ShapeDtypeStruct((8, LANES), F32)],
        scratch_shapes=[pltpu.VMEM((1, LANES), F32)],
        compiler_params=_params("arbitrary"),
        name="moe_router",
    )(x, rw, rb)


def _silu_mul(g, u):
    return g * (1.0 / (1.0 + jnp.exp(-g))) * u


def _ffn_up_kernel(x_ref, wg_ref, wu_ref, o_ref, wgb_ref, wub_ref):
    @pl.when(pl.program_id(1) == 0)
    def _():
        wgb_ref[...] = wg_ref[...].astype(BF16)
        wub_ref[...] = wu_ref[...].astype(BF16)

    x = x_ref[...]
    o_ref[...] = _silu_mul(_dot(x, wgb_ref[...]), _dot(x, wub_ref[...])).astype(BF16)


def _ffn_up(xb, wg, wu, tm, tn):
    m = xb.shape[0]
    return pl.pallas_call(
        _ffn_up_kernel,
        grid=(D_FF // tn, m // tm),
        in_specs=[pl.BlockSpec((tm, D_MODEL), lambda j, i: (i, 0)),
                  pl.BlockSpec((D_MODEL, tn), lambda j, i: (0, j)),
                  pl.BlockSpec((D_MODEL, tn), lambda j, i: (0, j))],
        out_specs=pl.BlockSpec((tm, tn), lambda j, i: (i, j)),
        out_shape=jax.ShapeDtypeStruct((m, D_FF), BF16),
        scratch_shapes=[pltpu.VMEM((D_MODEL, tn), BF16), pltpu.VMEM((D_MODEL, tn), BF16)],
        compiler_params=_params("arbitrary", "arbitrary"),
        name="ffn_up",
    )(xb, wg, wu)


def _ffn_down_ln_kernel(h_ref, w_ref, x_ref, g_ref, b_ref, o_ref, ob_ref, acc_ref):
    k = pl.program_id(1)

    @pl.when(k == 0)
    def _():
        acc_ref[...] = jnp.zeros_like(acc_ref)

    acc_ref[...] += _dot(h_ref[...], w_ref[...])

    @pl.when(k == pl.num_programs(1) - 1)
    def _():
        out = _layer_norm(ALPHA * x_ref[...] + acc_ref[...], g_ref[...], b_ref[...])
        o_ref[...] = out
        ob_ref[...] = out.astype(BF16)


def _ffn_down_ln(h, wd, x, g, b, tm, tk):
    m = x.shape[0]
    row = pl.BlockSpec((tm, D_MODEL), lambda i, k: (i, 0))
    res = row
    vec = pl.BlockSpec((1, D_MODEL), lambda i, k: (0, 0))
    return pl.pallas_call(
        _ffn_down_ln_kernel,
        grid=(m // tm, D_FF // tk),
        in_specs=[pl.BlockSpec((tm, tk), lambda i, k: (i, k)), pl.BlockSpec((tk, D_MODEL), lambda i, k: (k, 0)),
                  res, vec, vec],
        out_specs=[row, row],
        out_shape=[jax.ShapeDtypeStruct((m, D_MODEL), F32), jax.ShapeDtypeStruct((m, D_MODEL), BF16)],
        scratch_shapes=[pltpu.VMEM((tm, D_MODEL), F32)],
        compiler_params=_params("parallel", "arbitrary"),
        name="ffn_down_ln",
    )(h, wd, x, g, b)


def _proj_kernel(x_ref, w_ref, o_ref):
    o_ref[...] = _dot(x_ref[...], w_ref[...]).astype(BF16)


def _proj(xb, w, tm, tn):
    m = xb.shape[0]
    n = w.shape[1]
    return pl.pallas_call(
        _proj_kernel,
        grid=(n // tn, m // tm),
        in_specs=[pl.BlockSpec((tm, D_MODEL), lambda j, i: (i, 0)), pl.BlockSpec((D_MODEL, tn), lambda j, i: (0, j))],
        out_specs=pl.BlockSpec((tm, tn), lambda j, i: (i, j)),
        out_shape=jax.ShapeDtypeStruct((m, n), BF16),
        compiler_params=_params("parallel", "parallel"),
        name="proj",
    )(xb, w)


def _proj_t_kernel(w_ref, x_ref, o_ref, *, scale):
    o_ref[...] = (_dot_nt(w_ref[...], x_ref[...]) * scale).astype(BF16)


def _proj_t(xb3, wt, scale, tm, tn):
    b, s, _ = xb3.shape
    n = wt.shape[0]
    return pl.pallas_call(
        functools.partial(_proj_t_kernel, scale=scale),
        grid=(n // tn, b, s // tm),
        in_specs=[pl.BlockSpec((tn, D_MODEL), lambda j, bi, i: (j, 0)),
                  pl.BlockSpec((None, tm, D_MODEL), lambda j, bi, i: (bi, i, 0))],
        out_specs=pl.BlockSpec((None, tn, tm), lambda j, bi, i: (bi, j, i)),
        out_shape=jax.ShapeDtypeStruct((b, n, s), BF16),
        compiler_params=_params("parallel", "parallel", "parallel"),
        name="proj_t",
    )(wt, xb3)


def _fgate_kernel(x_ref, w_ref, b_ref, c_ref, carry_ref):
    @pl.when(pl.program_id(1) == 0)
    def _():
        carry_ref[...] = jnp.zeros_like(carry_ref)

    z = _dot(x_ref[...], w_ref[...]) + b_ref[...]
    log_f = jnp.minimum(z, 0.0) - jnp.log(1.0 + jnp.exp(-jnp.abs(z)))
    tm = z.shape[0]
    tri = (lax.broadcasted_iota(jnp.int32, (tm, tm), 1) <= lax.broadcasted_iota(jnp.int32, (tm, tm), 0)).astype(F32)
    c = jnp.dot(tri, log_f, preferred_element_type=F32, precision=lax.Precision.HIGHEST) + carry_ref[...]
    c_ref[...] = c
    carry_ref[...] = c[tm - 1:tm, :]


def _fgate(xb3, wf, bf, tm):
    b, s, _ = xb3.shape
    return pl.pallas_call(
        _fgate_kernel,
        grid=(b, s // tm),
        in_specs=[pl.BlockSpec((None, tm, D_MODEL), lambda bi, i: (bi, i, 0)),
                  pl.BlockSpec(wf.shape, lambda bi, i: (0, 0)), pl.BlockSpec(bf.shape, lambda bi, i: (0, 0))],
        out_specs=pl.BlockSpec((None, tm, LANES), lambda bi, i: (bi, i, 0)),
        out_shape=jax.ShapeDtypeStruct((b, s, LANES), F32),
        scratch_shapes=[pltpu.VMEM((1, LANES), F32)],
        compiler_params=_params("parallel", "arbitrary"),
        name="fox_gate",
    )(xb3, wf, bf)


def _dispatch_kernel(pos_ref, x_ref, zeros_ref, xs_ref, sem):
    del zeros_ref
    tm = x_ref.shape[0]

    def copy(r, k):
        return pltpu.make_async_copy(x_ref.at[pl.ds(r, 1), :], xs_ref.at[pl.ds(pos_ref[0, 0, 2 * r + k], 1), :], sem)

    def start(r, _):
        copy(r, 0).start()
        copy(r, 1).start()
        return 0

    def wait(r, _):
        copy(r, 0).wait()
        copy(r, 1).wait()
        return 0

    lax.fori_loop(0, tm, start, 0, unroll=DMA_UNROLL)
    lax.fori_loop(0, tm, wait, 0, unroll=DMA_UNROLL)


def _dispatch(x, pos, n_rows, tm):
    m, width = x.shape
    pos3 = pos.reshape(m // tm, 1, 2 * tm)
    return pl.pallas_call(
        _dispatch_kernel,
        grid=(m // tm,),
        in_specs=[pl.BlockSpec((1, 1, 2 * tm), lambda i: (i, 0, 0), memory_space=pltpu.SMEM),
                  pl.BlockSpec((tm, width), lambda i: (i, 0)),
                  pl.BlockSpec(memory_space=pl.ANY)],
        out_specs=pl.BlockSpec(memory_space=pl.ANY),
        out_shape=jax.ShapeDtypeStruct((n_rows, width), x.dtype),
        scratch_shapes=[pltpu.SemaphoreType.DMA(())],
        input_output_aliases={2: 0},
        compiler_params=_params("arbitrary"),
        name="moe_dispatch",
    )(pos3, x, jnp.zeros((n_rows, width), x.dtype))


def _moe_up_kernel(te_ref, nu_ref, nx_ref, x_ref, wg_hbm, wu_hbm, o_ref, sg_ref, su_ref, wgb_ref, wub_ref, sems, *, tn):
    j = pl.program_id(0)
    t = pl.program_id(1)
    live = t < nu_ref[0]
    fresh = jnp.logical_or(t == 0, te_ref[t] != te_ref[jnp.maximum(t - 1, 0)])

    def weight_copies(e, jj):
        cols = pl.ds(pl.multiple_of(jj * tn, LANES), tn)
        return (pltpu.make_async_copy(wg_hbm.at[e, :, cols], sg_ref, sems.at[0]),
                pltpu.make_async_copy(wu_hbm.at[e, :, cols], su_ref, sems.at[1]))

    def start(e, jj):
        for c in weight_copies(e, jj):
            c.start()

    @pl.when(jnp.logical_and(j == 0, t == 0))
    def _():
        start(te_ref[0], 0)

    @pl.when(jnp.logical_and(live, fresh))
    def _():
        for c in weight_copies(te_ref[t], j):
            c.wait()
        def cast_rows(c, _):
            rows = pl.ds(pl.multiple_of(c * 256, 256), 256)
            wgb_ref[rows, :] = sg_ref[rows, :].astype(BF16)
            wub_ref[rows, :] = su_ref[rows, :].astype(BF16)
            return 0

        lax.fori_loop(0, D_MODEL // 256, cast_rows, 0)
        nxt = nx_ref[t]

        @pl.when(nxt >= 0)
        def _():
            start(nxt, j)

        @pl.when(jnp.logical_and(nxt < 0, j + 1 < pl.num_programs(0)))
        def _():
            start(te_ref[0], j + 1)

    @pl.when(live)
    def _():
        x = x_ref[...].astype(BF16)
        for lo in range(0, tn, 512):
            cols = slice(lo, min(lo + 512, tn))
            o_ref[:, cols] = _silu_mul(_dot(x, wgb_ref[:, cols]), _dot(x, wub_ref[:, cols])).astype(BF16)

    @pl.when(jnp.logical_not(live))
    def _():
        o_ref[...] = jnp.zeros_like(o_ref)


def _moe_up(xs, wg, wu, tile_expert, n_used, next_expert, tm, tn):
    p = xs.shape[0]
    live = lambda t, nu: jnp.minimum(t, nu[0] - 1)
    return pl.pallas_call(
        functools.partial(_moe_up_kernel, tn=tn),
        grid_spec=pltpu.PrefetchScalarGridSpec(
            num_scalar_prefetch=3,
            grid=(D_FF // tn, p // tm),
            in_specs=[pl.BlockSpec((tm, D_MODEL), lambda j, t, te, nu, nx: (live(t, nu), 0)),
                      pl.BlockSpec(memory_space=pl.ANY), pl.BlockSpec(memory_space=pl.ANY)],
            out_specs=pl.BlockSpec((tm, tn), lambda j, t, te, nu, nx: (t, j)),
            scratch_shapes=[pltpu.VMEM((D_MODEL, tn), F32), pltpu.VMEM((D_MODEL, tn), F32),
                            pltpu.VMEM((D_MODEL, tn), BF16), pltpu.VMEM((D_MODEL, tn), BF16),
                            pltpu.SemaphoreType.DMA((2,))],
        ),
        out_shape=jax.ShapeDtypeStruct((p, D_FF), BF16),
        compiler_params=_params("arbitrary", "arbitrary"),
        name="moe_up",
    )(tile_expert, n_used, next_expert, xs, wg, wu)


def _moe_down_kernel(te_ref, nu_ref, h_ref, w_ref, o_ref, wb_ref):
    t = pl.program_id(1)
    live = t < nu_ref[0]
    fresh = jnp.logical_or(t == 0, te_ref[t] != te_ref[jnp.maximum(t - 1, 0)])

    @pl.when(jnp.logical_and(live, fresh))
    def _():
        wb_ref[...] = w_ref[...].astype(BF16)

    @pl.when(live)
    def _():
        o_ref[...] = _dot(h_ref[...], wb_ref[...])

    @pl.when(jnp.logical_not(live))
    def _():
        o_ref[...] = jnp.zeros_like(o_ref)


def _moe_down(hs, wd, tile_expert, n_used, tm, tn):
    p = hs.shape[0]
    live = lambda t, nu: jnp.minimum(t, nu[0] - 1)
    return pl.pallas_call(
        _moe_down_kernel,
        grid_spec=pltpu.PrefetchScalarGridSpec(
            num_scalar_prefetch=2,
            grid=(D_MODEL // tn, p // tm),
            in_specs=[pl.BlockSpec((tm, D_FF), lambda j, t, te, nu: (live(t, nu), 0)),
                      pl.BlockSpec((None, D_FF, tn), lambda j, t, te, nu: (te[live(t, nu)], 0, j))],
            out_specs=pl.BlockSpec((tm, tn), lambda j, t, te, nu: (t, j)),
            scratch_shapes=[pltpu.VMEM((D_FF, tn), BF16)],
        ),
        out_shape=jax.ShapeDtypeStruct((p, D_MODEL), F32),
        compiler_params=_params("arbitrary", "arbitrary"),
        name="moe_down",
    )(tile_expert, n_used, hs, wd)


def _combine_ln_kernel(pos_ref, ys_ref, x_ref, route_ref, g_ref, b_ref, o_ref, buf0, buf1, sem):
    tm = x_ref.shape[0]

    def copies(r):
        c0 = pltpu.make_async_copy(ys_ref.at[pl.ds(pos_ref[0, 0, 2 * r], 1), :], buf0.at[pl.ds(r, 1), :], sem)
        c1 = pltpu.make_async_copy(ys_ref.at[pl.ds(pos_ref[0, 0, 2 * r + 1], 1), :], buf1.at[pl.ds(r, 1), :], sem)
        return c0, c1

    def start(r, _):
        c0, c1 = copies(r)
        c0.start()
        c1.start()
        return 0

    def wait(r, _):
        c0, c1 = copies(r)
        c0.wait()
        c1.wait()
        return 0

    lax.fori_loop(0, tm, start, 0, unroll=DMA_UNROLL)
    lax.fori_loop(0, tm, wait, 0, unroll=DMA_UNROLL)
    route = route_ref[...]
    y = route[:, 2:3] * buf0[...] + route[:, 3:4] * buf1[...]
    o_ref[...] = _layer_norm(ALPHA * x_ref[...] + y, g_ref[...], b_ref[...])


def _combine_ln(ys, pos, x, route, g, b, tm):
    m = x.shape[0]
    pos3 = pos.reshape(m // tm, 1, 2 * tm)
    vec = pl.BlockSpec((1, D_MODEL), lambda i: (0, 0))
    return pl.pallas_call(
        _combine_ln_kernel,
        grid=(m // tm,),
        in_specs=[pl.BlockSpec((1, 1, 2 * tm), lambda i: (i, 0, 0), memory_space=pltpu.SMEM),
                  pl.BlockSpec(memory_space=pl.ANY),
                  pl.BlockSpec((tm, D_MODEL), lambda i: (i, 0)),
                  pl.BlockSpec((tm, LANES), lambda i: (i, 0)), vec, vec],
        out_specs=pl.BlockSpec((tm, D_MODEL), lambda i: (i, 0)),
        out_shape=jax.ShapeDtypeStruct((m, D_MODEL), F32),
        scratch_shapes=[pltpu.VMEM((tm, D_MODEL), F32), pltpu.VMEM((tm, D_MODEL), F32), pltpu.SemaphoreType.DMA(())],
        compiler_params=_params("arbitrary"),
        name="moe_combine_ln",
    )(pos3, ys, x, route, g, b)


def _rope_tables(seq):
    def angles(dim):
        inv_freq = 1.0 / (ROPE_THETA ** (jnp.arange(0, dim, 2, dtype=F32) / dim))
        ang = jnp.arange(seq, dtype=F32)[:, None] * inv_freq[None, :]
        return jnp.cos(ang), jnp.sin(ang)

    c128, s128 = angles(HEAD_DIM)
    cos_f = jnp.concatenate([c128, c128], axis=-1)
    sin_f = jnp.concatenate([-s128, s128], axis=-1)
    c64, s64 = angles(MLA_ROPE_DIM)
    z32 = jnp.zeros_like(c64)
    cos_t = jnp.concatenate([c64, c64, z32, z32], axis=-1)
    sin_a = jnp.concatenate([-s64, z32, z32, z32], axis=-1)
    sin_b = jnp.concatenate([z32, s64, z32, z32], axis=-1)
    return cos_f, sin_f, cos_t, sin_a, sin_b, c64.T, s64.T


def _pad_cols(a, width):
    return jnp.pad(a, ((0, 0), (0, width - a.shape[1])))


def _row(v):
    return v.reshape(1, -1).astype(F32)


def _even_layer(x, xb, batch, seq, tables, w_in, q_norm, w_q_b, kv_norm, w_kv_b, w_out, ln1_g, ln1_b,
                w_gate, w_up, w_down, ln2_g, ln2_b):
    cos_f, sin_f, cos_t, sin_a, sin_b, cos_tt, sin_tt = tables
    tm = min(512, seq)
    wa = jnp.concatenate([w_in[:, OFF_CKV:OFF_KROPE], _pad_cols(w_in[:, OFF_KROPE:OFF_DQ], LANES),
                          _pad_cols(w_in[:, OFF_CQ:OFF_CKV], MLA_Q_RANK_PAD)], axis=1).astype(BF16)
    qg = _pad_cols(_row(q_norm), MLA_Q_RANK_PAD)
    wq = jnp.pad(w_q_b.reshape(MLA_Q_RANK, MLA_HEADS, MLA_QK_DIM),
                 ((0, MLA_Q_RANK_PAD - MLA_Q_RANK), (0, 0), (0, MLA_QK_PAD - MLA_QK_DIM)))
    wqt = wq.reshape(MLA_Q_RANK_PAD, MLA_HEADS * MLA_QK_PAD).T.astype(BF16)
    wkv3 = w_kv_b.reshape(MLA_KV_RANK, MLA_HEADS, MLA_NOPE_DIM + MLA_V_DIM)
    wk = wkv3[:, :, :MLA_NOPE_DIM].reshape(MLA_KV_RANK, -1).astype(BF16)
    wvt = wkv3[:, :, MLA_NOPE_DIM:].reshape(MLA_KV_RANK, -1).T.astype(BF16)
    xb3 = xb.reshape(batch, seq, D_MODEL)
    qt_mla, k_mla, vt_mla = _mla_prep(xb3, wa, _row(kv_norm), qg, wqt, wk, wvt, cos_t, sin_a, sin_b,
                                      cos_tt, sin_tt, tm)
    tq = min(FLASH_TQ, seq)
    o_mla = _flash(qt_mla, k_mla, vt_mla, None, heads=MLA_HEADS, dq=MLA_QK_PAD, dk=MLA_QK_PAD, dv=MLA_V_DIM,
                   tq=tq)
    o_mla = o_mla.reshape(batch * seq, -1)

    dqkv = _dqkv(xb, w_in[:, OFF_DQ:].astype(BF16), cos_f, sin_f, seq, tm)
    dqkv3 = dqkv.reshape(batch, seq, -1)
    outs, lses = [], []
    for g, (window, dil) in enumerate(DIL_PATTERNS):
        assert window == DIL_SPAN * dil
        o_g, lse_g = _dilated_group(dqkv3, g, dil, min(seq, max(DIL_TILE, DIL_SPAN * dil)))
        outs.append(o_g)
        lses.append(lse_g)
    n_mla = MLA_HEADS * MLA_V_DIM
    wo = w_out.astype(BF16)
    x1, x1b = _even_out(o_mla, outs, lses, wo[:n_mla], wo[n_mla:], x, _row(ln1_g), _row(ln1_b), tm)
    hmid = _ffn_up(x1b, w_gate, w_up, min(1024, seq), 512)
    return _ffn_down_ln(hmid, w_down.astype(BF16), x1, _row(ln2_g), _row(ln2_b), tm, 1408)


def _odd_layer(x, xb, batch, seq, w_qkv, w_f, b_f, w_out, ln1_g, ln1_b, router_w, router_b,
               exp_w_gate, exp_w_up, exp_w_down, ln2_g, ln2_b):
    m = batch * seq
    tm = min(512, seq)
    xb3 = xb.reshape(batch, seq, D_MODEL)
    wb = w_qkv.astype(BF16)
    qt = _proj_t(xb3, wb[:, :FOX_WIDTH].T, HEAD_DIM ** -0.5 * LOG2E, tm, 1024)
    k = _proj(xb, wb[:, FOX_WIDTH:2 * FOX_WIDTH], tm, 1024).reshape(batch, seq, FOX_WIDTH)
    vt = _proj_t(xb3, wb[:, 2 * FOX_WIDTH:].T, 1.0, tm, 1024)
    c = _fgate(xb3, _pad_cols(w_f, LANES).astype(BF16), _pad_cols(_row(b_f), LANES), min(256, seq))
    c_t = jnp.transpose(c[:, :, :FOX_HEADS], (0, 2, 1)).reshape(batch, FOX_HEADS, 1, seq)
    tq = min(FLASH_TQ, seq)
    o = _flash(qt, k, vt, c_t, heads=FOX_HEADS, dq=HEAD_DIM, dk=HEAD_DIM, dv=HEAD_DIM, tq=tq)
    rb = jnp.full((1, LANES), NEG, F32).at[0, :N_EXPERTS].set(router_b.astype(F32))
    x1 = _odd_out(o.reshape(m, -1), w_out.astype(BF16), x, _row(ln1_g), _row(ln1_b), tm)
    route, counts = _router(x1, _pad_cols(router_w, LANES), rb, min(256, seq))
    tile = MOE_TILE
    n_tiles = (2 * m) // tile + N_EXPERTS
    cnt = counts[0, :N_EXPERTS].astype(jnp.int32)
    tiles_per = (cnt + tile - 1) // tile
    tile_end = jnp.cumsum(tiles_per)
    offset = (tile_end - tiles_per) * tile
    idx = route[:, 0:2].astype(jnp.int32)
    pos = (offset[idx] + route[:, 4:6].astype(jnp.int32)).reshape(-1)
    n_used = tile_end[-1:]
    tile_ids = jnp.arange(n_tiles, dtype=jnp.int32)
    tile_expert = jnp.minimum(jnp.sum((tile_end[None, :] <= tile_ids[:, None]).astype(jnp.int32), axis=1),
                              N_EXPERTS - 1)
    td = min(256, seq)
    xs = _dispatch(x1, pos, n_tiles * tile, td)
    experts = jnp.arange(N_EXPERTS, dtype=jnp.int32)
    later = jnp.where((experts[None, :] > experts[:, None]) & (tiles_per[None, :] > 0), experts[None, :], N_EXPERTS)
    next_nonempty = jnp.min(later, axis=1)
    next_expert = jnp.where(next_nonempty < N_EXPERTS, next_nonempty, -1)[tile_expert].astype(jnp.int32)
    hs = _moe_up(xs, exp_w_gate, exp_w_up, tile_expert, n_used, next_expert, tile, 1408)
    split = tile // MOE_DOWN_TILE
    ys = _moe_down(hs, exp_w_down, jnp.repeat(tile_expert, split), n_used * split, MOE_DOWN_TILE, 512)
    return _combine_ln(ys, pos, x1, route, _row(ln2_g), _row(ln2_b), td)


def kernel(x, ev_w_in, ev_q_norm, ev_w_q_b, ev_kv_norm, ev_w_kv_b, ev_w_out, ev_ln1_g, ev_ln1_b, ev_ffn_w_gate, ev_ffn_w_up, ev_ffn_w_down, ev_ln2_g, ev_ln2_b, od_w_qkv, od_w_f, od_b_f, od_w_out, od_ln1_g, od_ln1_b, od_router_w, od_router_b, od_exp_w_gate, od_exp_w_up, od_exp_w_down, od_ln2_g, od_ln2_b):
    batch, seq, _ = x.shape
    tables = _rope_tables(seq)
    h = x.reshape(batch * seq, D_MODEL)
    hb = h.astype(BF16)
    for layer in range(DEPTH):
        i = layer // 2
        if layer % 2 == 0:
            h, hb = _even_layer(h, hb, batch, seq, tables, ev_w_in[i], ev_q_norm[i], ev_w_q_b[i], ev_kv_norm[i],
                                ev_w_kv_b[i], ev_w_out[i], ev_ln1_g[i], ev_ln1_b[i], ev_ffn_w_gate[i],
                                ev_ffn_w_up[i], ev_ffn_w_down[i], ev_ln2_g[i], ev_ln2_b[i])
        else:
            h = _odd_layer(h, hb, batch, seq, od_w_qkv[i], od_w_f[i], od_b_f[i], od_w_out[i], od_ln1_g[i],
                           od_ln1_b[i], od_router_w[i], od_router_b[i], od_exp_w_gate[i], od_exp_w_up[i],
                           od_exp_w_down[i], od_ln2_g[i], od_ln2_b[i])
            hb = h.astype(BF16)
    return h.reshape(batch, seq, D_MODEL)
```

```python
import functools

import jax
import jax.numpy as jnp
from jax import lax
from jax.experimental import pallas as pl
from jax.experimental.pallas import tpu as pltpu

F32 = jnp.float32
BF16 = jnp.bfloat16

D_MODEL = 2048
HEAD_DIM = 128
LANES = 128
MXU_COLS = 256
ROPE_THETA = 10000.0
LN_EPS = 1e-5
RMS_EPS = 1e-6

MLA_HEADS = 10
MLA_Q_RANK = 448
MLA_Q_RANK_PAD = 512
MLA_KV_RANK = 128
MLA_NOPE_DIM = 128
MLA_ROPE_DIM = 64
MLA_V_DIM = 128
MLA_QK_DIM = MLA_NOPE_DIM + MLA_ROPE_DIM
MLA_QK_PAD = 256

DIL_PATTERNS = ((128, 1), (512, 4), (2048, 16))
DIL_GROUPS = 3
DIL_HEADS = 6
DIL_WIDTH = DIL_HEADS * HEAD_DIM
DIL_SPAN = 128

OFF_CQ = 0
OFF_CKV = OFF_CQ + MLA_Q_RANK
OFF_KROPE = OFF_CKV + MLA_KV_RANK
OFF_DQ = OFF_KROPE + MLA_ROPE_DIM
OFF_DK = OFF_DQ + DIL_GROUPS * DIL_WIDTH
OFF_DV = OFF_DK + DIL_WIDTH
W_IN_COLS = OFF_DV + DIL_WIDTH

FOX_HEADS = 16
FOX_WIDTH = FOX_HEADS * HEAD_DIM

D_FF = 5632
N_EXPERTS = 8
DEPTH = 2
ALPHA = (2.0 * DEPTH) ** 0.25

NEG = -1e30
LOG2E = 1.4426950408889634
FLASH_TQ = 1024
MOE_TILE = 512
MOE_DOWN_TILE = 512
DIL_TILE = 1024
DMA_UNROLL = 8
VMEM_LIMIT = 56 * 1024 * 1024


def _params(*sem, vmem=VMEM_LIMIT):
    return pltpu.CompilerParams(dimension_semantics=sem, vmem_limit_bytes=vmem)


def _dot(a, b):
    return jnp.dot(a, b, preferred_element_type=F32)


def _dot_nt(a, b):
    return lax.dot_general(a, b, (((1,), (1,)), ((), ())), preferred_element_type=F32)


def _layer_norm(y, g, b):
    mu = jnp.mean(y, axis=-1, keepdims=True)
    d = y - mu
    var = jnp.mean(d * d, axis=-1, keepdims=True)
    return d * lax.rsqrt(var + LN_EPS) * g + b


def _rope128(x, cos_f, sin_f):
    return x * cos_f + pltpu.roll(x, 64, 1) * sin_f


def _rope64(x, cos_t, sin_a, sin_b):
    return x * cos_t + pltpu.roll(x, 96, 1) * sin_a + pltpu.roll(x, 32, 1) * sin_b


def _mla_prep_kernel(x_ref, wa_ref, kvg_ref, qg_ref, wqt_ref, wk_ref, wvt_ref, cos_ref, sa_ref, sb_ref,
                     ct_ref, st_ref, qt_ref, k_ref, vt_ref):
    h = _dot(x_ref[...], wa_ref[...])
    ckv = h[:, 0:128]
    kr = h[:, 128:256]
    cq = h[:, 256:768]
    ckv_n = (ckv * lax.rsqrt(jnp.mean(ckv * ckv, axis=-1, keepdims=True) + RMS_EPS) * kvg_ref[...]).astype(BF16)
    cq_ms = jnp.sum(cq * cq, axis=-1, keepdims=True) * (1.0 / MLA_Q_RANK)
    cq_n = (cq * lax.rsqrt(cq_ms + RMS_EPS) * qg_ref[...]).astype(BF16)
    kr_r = _rope64(kr, cos_ref[...], sa_ref[...], sb_ref[...]).astype(BF16)
    scale = MLA_QK_DIM ** -0.5 * LOG2E
    qt = _dot_nt(wqt_ref[...], cq_n)
    c, s = ct_ref[...], st_ref[...]
    k_nope = _dot(ckv_n, wk_ref[...])
    for hd in range(MLA_HEADS):
        o = hd * MLA_QK_PAD
        qt_ref[o:o + 128, :] = (qt[o:o + 128] * scale).astype(BF16)
        x1, x2 = qt[o + 128:o + 160], qt[o + 160:o + 192]
        qt_ref[o + 128:o + 160, :] = ((x1 * c - x2 * s) * scale).astype(BF16)
        qt_ref[o + 160:o + 192, :] = ((x2 * c + x1 * s) * scale).astype(BF16)
        qt_ref[o + 192:o + 256, :] = jnp.zeros((64, qt.shape[1]), BF16)
        k_ref[:, o:o + 128] = k_nope[:, hd * 128:(hd + 1) * 128].astype(BF16)
        k_ref[:, o + 128:o + 256] = kr_r
    vt_ref[...] = _dot_nt(wvt_ref[...], ckv_n).astype(BF16)


def _mla_prep(xb3, wa, kvg, qg, wqt, wk, wvt, cos_t, sin_a, sin_b, cos_tt, sin_tt, tm):
    b, s, _ = xb3.shape
    full = lambda shape: pl.BlockSpec(shape, lambda bi, i: (0, 0))
    tab = pl.BlockSpec((tm, LANES), lambda bi, i: (i, 0))
    tab_t = pl.BlockSpec((MLA_ROPE_DIM // 2, tm), lambda bi, i: (0, i))
    wide = MLA_HEADS * MLA_QK_PAD
    vw = MLA_HEADS * MLA_V_DIM
    return pl.pallas_call(
        _mla_prep_kernel,
        grid=(b, s // tm),
        in_specs=[pl.BlockSpec((None, tm, D_MODEL), lambda bi, i: (bi, i, 0)), full(wa.shape), full(kvg.shape),
                  full(qg.shape), full(wqt.shape), full(wk.shape), full(wvt.shape), tab, tab, tab, tab_t, tab_t],
        out_specs=[pl.BlockSpec((None, wide, tm), lambda bi, i: (bi, 0, i)),
                   pl.BlockSpec((None, tm, wide), lambda bi, i: (bi, i, 0)),
                   pl.BlockSpec((None, vw, tm), lambda bi, i: (bi, 0, i))],
        out_shape=[jax.ShapeDtypeStruct((b, wide, s), BF16), jax.ShapeDtypeStruct((b, s, wide), BF16),
                   jax.ShapeDtypeStruct((b, vw, s), BF16)],
        compiler_params=_params("parallel", "parallel"),
        name="mla_prep",
    )(xb3, wa, kvg, qg, wqt, wk, wvt, cos_t, sin_a, sin_b, cos_tt, sin_tt)


def _dqkv_kernel(x_ref, w_ref, cos_ref, sin_ref, o_ref):
    j = pl.program_id(0)
    h = _dot(x_ref[...], w_ref[...])

    @pl.when(j < 4)
    def _():
        cos_f, sin_f = cos_ref[...], sin_ref[...]
        sc = jnp.where(j < 3, HEAD_DIM ** -0.5, 1.0).astype(F32)
        for hd in range(DIL_HEADS):
            sl = slice(hd * 128, (hd + 1) * 128)
            o_ref[:, sl] = (_rope128(h[:, sl], cos_f, sin_f) * sc).astype(BF16)

    @pl.when(j == 4)
    def _():
        o_ref[...] = h.astype(BF16)


def _dqkv(xb, wd, cos_f, sin_f, seq, tm):
    m = xb.shape[0]
    nrow = seq // tm
    n_col = wd.shape[1] // DIL_WIDTH
    tab = pl.BlockSpec((tm, LANES), lambda j, i: (i % nrow, 0))
    return pl.pallas_call(
        _dqkv_kernel,
        grid=(n_col, m // tm),
        in_specs=[pl.BlockSpec((tm, D_MODEL), lambda j, i: (i, 0)),
                  pl.BlockSpec((D_MODEL, DIL_WIDTH), lambda j, i: (0, j)), tab, tab],
        out_specs=pl.BlockSpec((tm, DIL_WIDTH), lambda j, i: (i, j)),
        out_shape=jax.ShapeDtypeStruct((m, wd.shape[1]), BF16),
        compiler_params=_params("parallel", "parallel"),
        name="dil_qkv",
    )(xb, wd, cos_f, sin_f)


def _flash_kernel(*refs, tq, use_c):
    if use_c:
        qt_ref, k_ref, vt_ref, c_ref, o_ref, acc_ref, s0_ref, s1_ref, kaug_ref = refs
    else:
        qt_ref, k_ref, vt_ref, o_ref, acc_ref, s0_ref, s1_ref = refs
    seq = k_ref.shape[0]
    nq = seq // tq
    if use_c:
        row = lax.broadcasted_iota(jnp.int32, (LANES, tq), 0)

        def build(j, _):
            start = pl.multiple_of(j * tq, tq)
            neg = -LOG2E * c_ref[:, pl.ds(start, tq)]
            hi = neg.astype(BF16).astype(F32)
            mid = (neg - hi).astype(BF16).astype(F32)
            lo = neg - hi - mid
            blk = jnp.where(row == 0, hi, jnp.where(row == 1, mid, jnp.where(row == 2, lo, 0.0)))
            kaug_ref[pl.ds(start, tq), :] = blk.T.astype(BF16)
            return 0

        lax.fori_loop(0, nq, build, 0)

    def scores(i, j, s_ref):
        q = qt_ref[:, pl.ds(pl.multiple_of(i * tq, tq), tq)]
        if use_c:
            ones = (lax.broadcasted_iota(jnp.int32, (LANES, tq), 0) < 3).astype(BF16)
            q = jnp.concatenate([q, ones], axis=0)
        start = pl.multiple_of(j * tq, tq)
        kt = k_ref[pl.ds(start, tq), :]
        if use_c:
            kt = jnp.concatenate([kt, kaug_ref[pl.ds(start, tq), :]], axis=1)
        s_ref[...] = _dot(kt, q)

    def update(j, s_ref, stats, masked):
        m, l = stats
        s = s_ref[...]
        if masked:
            on_diag = lax.broadcasted_iota(jnp.int32, (tq, tq), 0) <= lax.broadcasted_iota(jnp.int32, (tq, tq), 1)
            s = jnp.where(on_diag, s, NEG)
        m_new = jnp.maximum(m, jnp.max(s, axis=0, keepdims=True))
        a = jnp.exp2(m - m_new)
        p = jnp.exp2(s - m_new)
        l = a * l + jnp.sum(p, axis=0, keepdims=True)
        start = pl.multiple_of(j * tq, tq)
        acc_ref[...] = a * acc_ref[...] + _dot(vt_ref[:, pl.ds(start, tq)], p.astype(BF16))
        return m_new, l

    def query_tile(i, first, second):
        acc_ref[...] = jnp.zeros_like(acc_ref)

        def pair(jj, stats):
            scores(i, 2 * jj + 1, second)
            stats = update(2 * jj, first, stats, False)
            scores(i, 2 * jj + 2, first)
            return update(2 * jj + 1, second, stats, False)

        init = (jnp.full((1, tq), NEG, F32), jnp.zeros((1, tq), F32))
        stats = lax.fori_loop(0, i // 2, pair, init)
        nxt = jnp.minimum(i + 1, nq - 1)

        def odd_tail(stats):
            scores(i, i, second)
            stats = update(i - 1, first, stats, False)
            scores(nxt, 0, first)
            return update(i, second, stats, True)

        def even_tail(stats):
            scores(nxt, 0, second)
            return update(i, first, stats, True)

        _, l = lax.cond(i % 2 == 1, odd_tail, even_tail, stats)
        o_ref[pl.ds(pl.multiple_of(i * tq, tq), tq), :] = (acc_ref[...] / l).T.astype(o_ref.dtype)
        return 0

    scores(0, 0, s0_ref)

    def query_loop(i, _):
        return lax.cond(((i + 1) // 2) % 2 == 0, lambda: query_tile(i, s0_ref, s1_ref),
                        lambda: query_tile(i, s1_ref, s0_ref))

    lax.fori_loop(0, nq, query_loop, 0)


def _flash(qt, k, vt, c, *, heads, dq, dk, dv, tq):
    b, s, _ = k.shape
    use_c = c is not None
    in_specs = [pl.BlockSpec((None, dq, s), lambda bi, h: (bi, h, 0)),
                pl.BlockSpec((None, s, dk), lambda bi, h: (bi, 0, h)),
                pl.BlockSpec((None, dv, s), lambda bi, h: (bi, h, 0))]
    args = [qt, k, vt]
    scratch = [pltpu.VMEM((dv, tq), F32), pltpu.VMEM((tq, tq), F32), pltpu.VMEM((tq, tq), F32)]
    if use_c:
        in_specs.append(pl.BlockSpec((None, None, 1, s), lambda bi, h: (bi, h, 0, 0)))
        args.append(c)
        scratch.append(pltpu.VMEM((s, LANES), BF16))
    return pl.pallas_call(
        functools.partial(_flash_kernel, tq=tq, use_c=use_c),
        grid=(b, heads),
        in_specs=in_specs,
        out_specs=pl.BlockSpec((None, s, dv), lambda bi, h: (bi, 0, h)),
        out_shape=jax.ShapeDtypeStruct((b, s, heads * dv), BF16),
        scratch_shapes=scratch,
        compiler_params=_params("parallel", "parallel"),
        name="flash_fox" if use_c else "flash_mla",
    )(*args)


def _dilated_kernel(q_ref, kc_ref, kp_ref, vc_ref, vp_ref, o_ref, lse_ref, qf, kf, vf, *, dil, tn):
    i = pl.program_id(1)
    per_class = tn // dil
    qf[...] = q_ref[...].astype(F32)
    kf[0:tn, :] = kp_ref[...].astype(F32)
    kf[tn:2 * tn, :] = kc_ref[...].astype(F32)
    vf[0:tn, :] = vp_ref[...].astype(F32)
    vf[tn:2 * tn, :] = vc_ref[...].astype(F32)
    row = lax.broadcasted_iota(jnp.int32, (DIL_SPAN, 2 * DIL_SPAN), 0)
    col = lax.broadcasted_iota(jnp.int32, (DIL_SPAN, 2 * DIL_SPAN), 1)
    back = row + DIL_SPAN - col
    in_band = jnp.where(back >= 0, jnp.where(back <= DIL_SPAN, 0.0, NEG), NEG)
    first_band = jnp.where(col >= jnp.where(i > 0, 0, DIL_SPAN), in_band, NEG)
    for r in range(dil):
        k_r = jnp.concatenate([kf[pl.ds(tn - DIL_SPAN * dil + r, DIL_SPAN, stride=dil), :],
                               kf[pl.ds(tn + r, per_class, stride=dil), :]], axis=0).astype(BF16)
        v_r = jnp.concatenate([vf[pl.ds(tn - DIL_SPAN * dil + r, DIL_SPAN, stride=dil), :],
                               vf[pl.ds(tn + r, per_class, stride=dil), :]], axis=0).astype(BF16)
        q_r = qf[pl.ds(r, per_class, stride=dil), :].astype(BF16)
        for a in range(per_class // DIL_SPAN):
            lo = a * DIL_SPAN
            s = _dot_nt(q_r[lo:lo + DIL_SPAN], k_r[lo:lo + 2 * DIL_SPAN]) + (first_band if a == 0 else in_band)
            m = jnp.max(s, axis=-1, keepdims=True)
            p = jnp.exp(s - m)
            l = jnp.sum(p, axis=-1, keepdims=True)
            rows = pl.ds(r + lo * dil, DIL_SPAN, stride=dil)
            o_ref[rows, :] = _dot(p.astype(BF16), v_r[lo:lo + 2 * DIL_SPAN]) / l
            lse_ref[rows, :] = jnp.broadcast_to(m + jnp.log(l), (DIL_SPAN, LANES))


def _dilated_group(dqkv3, g, dil, tn):
    batch, seq, _ = dqkv3.shape
    assert tn % (DIL_SPAN * dil) == 0
    blk = lambda col, prev: pl.BlockSpec(
        (None, tn, HEAD_DIM), lambda b, i, h: (b, jnp.maximum(i - 1, 0) if prev else i, col * DIL_HEADS + h))
    out = pl.BlockSpec((None, tn, HEAD_DIM), lambda b, i, h: (b, i, h))
    o, lse = pl.pallas_call(
        functools.partial(_dilated_kernel, dil=dil, tn=tn),
        grid=(batch, seq // tn, DIL_HEADS),
        in_specs=[blk(g, False), blk(3, False), blk(3, True), blk(4, False), blk(4, True)],
        out_specs=[out, out],
        out_shape=[jax.ShapeDtypeStruct((batch, seq, DIL_WIDTH), F32)] * 2,
        scratch_shapes=[pltpu.VMEM((tn, HEAD_DIM), F32), pltpu.VMEM((2 * tn, HEAD_DIM), F32),
                        pltpu.VMEM((2 * tn, HEAD_DIM), F32)],
        compiler_params=_params("parallel", "parallel", "parallel"),
        name=f"dilated_{dil}",
    )(dqkv3, dqkv3, dqkv3, dqkv3, dqkv3)
    return o.reshape(batch * seq, DIL_WIDTH), lse.reshape(batch * seq, DIL_WIDTH)


def _even_out_kernel(om_ref, o0, o1, o2, l0, l1, l2, wm_ref, wd_ref, x_ref, g_ref, b_ref, o_ref, ob_ref):
    ls = [l0[...], l1[...], l2[...]]
    mx = jnp.maximum(jnp.maximum(ls[0], ls[1]), ls[2])
    es = [jnp.exp(v - mx) for v in ls]
    den = es[0] + es[1] + es[2]
    o_dil = ((es[0] / den) * o0[...] + (es[1] / den) * o1[...] + (es[2] / den) * o2[...]).astype(BF16)
    y = _dot(om_ref[...], wm_ref[...]) + _dot(o_dil, wd_ref[...])
    out = _layer_norm(ALPHA * x_ref[...] + y, g_ref[...], b_ref[...])
    o_ref[...] = out
    ob_ref[...] = out.astype(BF16)


def _even_out(o_mla, outs, lses, w_mla, w_dil, x, g, b, tm):
    m = x.shape[0]
    row = lambda width: pl.BlockSpec((tm, width), lambda i: (i, 0))
    full = lambda shape: pl.BlockSpec(shape, lambda i: (0, 0))
    return pl.pallas_call(
        _even_out_kernel,
        grid=(m // tm,),
        in_specs=[row(o_mla.shape[1])] + [row(DIL_WIDTH)] * 6
        + [full(w_mla.shape), full(w_dil.shape), row(D_MODEL), full(g.shape), full(b.shape)],
        out_specs=[row(D_MODEL), row(D_MODEL)],
        out_shape=[jax.ShapeDtypeStruct((m, D_MODEL), F32), jax.ShapeDtypeStruct((m, D_MODEL), BF16)],
        compiler_params=_params("parallel"),
        name="even_out_ln",
    )(o_mla, *outs, *lses, w_mla, w_dil, x, g, b)


def _odd_out_kernel(a_ref, w_ref, x_ref, g_ref, b_ref, o_ref):
    o_ref[...] = _layer_norm(ALPHA * x_ref[...] + _dot(a_ref[...], w_ref[...]), g_ref[...], b_ref[...])


def _odd_out(a, w, x, g, b, tm):
    m = x.shape[0]
    row = lambda width: pl.BlockSpec((tm, width), lambda i: (i, 0))
    full = lambda shape: pl.BlockSpec(shape, lambda i: (0, 0))
    return pl.pallas_call(
        _odd_out_kernel,
        grid=(m // tm,),
        in_specs=[row(a.shape[1]), full(w.shape), row(D_MODEL), full(g.shape), full(b.shape)],
        out_specs=row(D_MODEL),
        out_shape=jax.ShapeDtypeStruct((m, D_MODEL), F32),
        compiler_params=_params("parallel"),
        name="odd_out_ln",
    )(a, w, x, g, b)


def _router_kernel(x_ref, rwh_ref, rwl_ref, rb_ref, route_ref, cnt_ref, carry_ref):
    @pl.when(pl.program_id(0) == 0)
    def _():
        carry_ref[...] = jnp.zeros_like(carry_ref)

    x = x_ref[...]
    xh = x.astype(BF16)
    xl = (x - xh.astype(F32)).astype(BF16)
    logits = _dot(xh, rwh_ref[...]) + (_dot(xh, rwl_ref[...]) + _dot(xl, rwh_ref[...])) + rb_ref[...]
    tm = logits.shape[0]
    lane = lax.broadcasted_iota(jnp.int32, (tm, LANES), 1)
    l1 = jnp.max(logits, axis=-1, keepdims=True)
    i1 = jnp.min(jnp.where(logits == l1, lane, LANES), axis=-1, keepdims=True)
    rest = jnp.where(lane == i1, NEG, logits)
    l2 = jnp.max(rest, axis=-1, keepdims=True)
    i2 = jnp.min(jnp.where(rest == l2, lane, LANES), axis=-1, keepdims=True)
    e = jnp.exp(l2 - l1)
    w1 = 1.0 / (1.0 + e)
    w2 = e / (1.0 + e)
    hot1 = (lane == i1).astype(F32)
    hot2 = (lane == i2).astype(F32)
    cnt = hot1 + hot2
    strict = (lax.broadcasted_iota(jnp.int32, (tm, tm), 1) < lax.broadcasted_iota(jnp.int32, (tm, tm), 0)).astype(BF16)
    before = _dot(strict, cnt.astype(BF16)) + carry_ref[...]
    r1 = jnp.sum(before * hot1, axis=-1, keepdims=True)
    r2 = jnp.sum(before * hot2, axis=-1, keepdims=True)
    vals = (i1.astype(F32), i2.astype(F32), w1, w2, r1, r2)
    route = jnp.zeros((tm, LANES), F32)
    for idx, val in enumerate(vals):
        route = jnp.where(lane == idx, val, route)
    route_ref[...] = route
    total = carry_ref[...] + jnp.sum(cnt, axis=0, keepdims=True)
    carry_ref[...] = total
    cnt_ref[...] = jnp.broadcast_to(total, cnt_ref.shape)


def _router(x, rw, rb, tm):
    m = x.shape[0]
    full = lambda shape: pl.BlockSpec(shape, lambda i: (0, 0))
    rw_hi = rw.astype(BF16)
    rw_lo = (rw - rw_hi.astype(F32)).astype(BF16)
    return pl.pallas_call(
        _router_kernel,
        grid=(m // tm,),
        in_specs=[pl.BlockSpec((tm, D_MODEL), lambda i: (i, 0)), full(rw.shape), full(rw.shape), full(rb.shape)],
        out_specs=[pl.BlockSpec((tm, LANES), lambda i: (i, 0)), pl.BlockSpec((8, LANES), lambda i: (0, 0))],
        out_shape=[jax.ShapeDtypeStruct((m, LANES), F32), jax.ShapeDtypeStruct((8, LANES), F32)],
        scratch_shapes=[pltpu.VMEM((1, LANES), F32)],
        compiler_params=_params("arbitrary"),
        name="moe_router",
    )(x, rw_hi, rw_lo, rb)


def _silu_mul(g, u):
    return g * (1.0 / (1.0 + jnp.exp(-g))) * u


def _ffn_up_kernel(x_ref, wg_ref, wu_ref, o_ref, wgb_ref, wub_ref):
    @pl.when(pl.program_id(1) == 0)
    def _():
        wgb_ref[...] = wg_ref[...].astype(BF16)
        wub_ref[...] = wu_ref[...].astype(BF16)

    x = x_ref[...]
    o_ref[...] = _silu_mul(_dot(x, wgb_ref[...]), _dot(x, wub_ref[...])).astype(BF16)


def _ffn_up(xb, wg, wu, tm, tn):
    m = xb.shape[0]
    return pl.pallas_call(
        _ffn_up_kernel,
        grid=(D_FF // tn, m // tm),
        in_specs=[pl.BlockSpec((tm, D_MODEL), lambda j, i: (i, 0)),
                  pl.BlockSpec((D_MODEL, tn), lambda j, i: (0, j)),
                  pl.BlockSpec((D_MODEL, tn), lambda j, i: (0, j))],
        out_specs=pl.BlockSpec((tm, tn), lambda j, i: (i, j)),
        out_shape=jax.ShapeDtypeStruct((m, D_FF), BF16),
        scratch_shapes=[pltpu.VMEM((D_MODEL, tn), BF16), pltpu.VMEM((D_MODEL, tn), BF16)],
        compiler_params=_params("arbitrary", "arbitrary"),
        name="ffn_up",
    )(xb, wg, wu)


def _ffn_down_ln_kernel(h_ref, w_ref, x_ref, g_ref, b_ref, o_ref, ob_ref, acc_ref):
    k = pl.program_id(1)

    @pl.when(k == 0)
    def _():
        acc_ref[...] = jnp.zeros_like(acc_ref)

    acc_ref[...] += _dot(h_ref[...], w_ref[...])

    @pl.when(k == pl.num_programs(1) - 1)
    def _():
        out = _layer_norm(ALPHA * x_ref[...] + acc_ref[...], g_ref[...], b_ref[...])
        o_ref[...] = out
        ob_ref[...] = out.astype(BF16)


def _ffn_down_ln(h, wd, x, g, b, tm, tk):
    m = x.shape[0]
    row = pl.BlockSpec((tm, D_MODEL), lambda i, k: (i, 0))
    res = row
    vec = pl.BlockSpec((1, D_MODEL), lambda i, k: (0, 0))
    return pl.pallas_call(
        _ffn_down_ln_kernel,
        grid=(m // tm, D_FF // tk),
        in_specs=[pl.BlockSpec((tm, tk), lambda i, k: (i, k)), pl.BlockSpec((tk, D_MODEL), lambda i, k: (k, 0)),
                  res, vec, vec],
        out_specs=[row, row],
        out_shape=[jax.ShapeDtypeStruct((m, D_MODEL), F32), jax.ShapeDtypeStruct((m, D_MODEL), BF16)],
        scratch_shapes=[pltpu.VMEM((tm, D_MODEL), F32)],
        compiler_params=_params("parallel", "arbitrary"),
        name="ffn_down_ln",
    )(h, wd, x, g, b)


def _proj_kernel(x_ref, w_ref, o_ref):
    o_ref[...] = _dot(x_ref[...], w_ref[...]).astype(BF16)


def _proj(xb, w, tm, tn):
    m = xb.shape[0]
    n = w.shape[1]
    return pl.pallas_call(
        _proj_kernel,
        grid=(n // tn, m // tm),
        in_specs=[pl.BlockSpec((tm, D_MODEL), lambda j, i: (i, 0)), pl.BlockSpec((D_MODEL, tn), lambda j, i: (0, j))],
        out_specs=pl.BlockSpec((tm, tn), lambda j, i: (i, j)),
        out_shape=jax.ShapeDtypeStruct((m, n), BF16),
        compiler_params=_params("parallel", "parallel"),
        name="proj",
    )(xb, w)


def _proj_t_kernel(w_ref, x_ref, o_ref, *, scale):
    o_ref[...] = (_dot_nt(w_ref[...], x_ref[...]) * scale).astype(BF16)


def _proj_t(xb3, wt, scale, tm, tn):
    b, s, _ = xb3.shape
    n = wt.shape[0]
    return pl.pallas_call(
        functools.partial(_proj_t_kernel, scale=scale),
        grid=(n // tn, b, s // tm),
        in_specs=[pl.BlockSpec((tn, D_MODEL), lambda j, bi, i: (j, 0)),
                  pl.BlockSpec((None, tm, D_MODEL), lambda j, bi, i: (bi, i, 0))],
        out_specs=pl.BlockSpec((None, tn, tm), lambda j, bi, i: (bi, j, i)),
        out_shape=jax.ShapeDtypeStruct((b, n, s), BF16),
        compiler_params=_params("parallel", "parallel", "parallel"),
        name="proj_t",
    )(wt, xb3)


def _fgate_kernel(x_ref, w_ref, b_ref, c_ref, carry_ref):
    @pl.when(pl.program_id(1) == 0)
    def _():
        carry_ref[...] = jnp.zeros_like(carry_ref)

    z = _dot(x_ref[...], w_ref[...]) + b_ref[...]
    log_f = jnp.minimum(z, 0.0) - jnp.log(1.0 + jnp.exp(-jnp.abs(z)))
    tm = z.shape[0]
    tri = (lax.broadcasted_iota(jnp.int32, (tm, tm), 1) <= lax.broadcasted_iota(jnp.int32, (tm, tm), 0)).astype(F32)
    c = jnp.dot(tri, log_f, preferred_element_type=F32, precision=lax.Precision.HIGHEST) + carry_ref[...]
    c_ref[...] = c
    carry_ref[...] = c[tm - 1:tm, :]


def _fgate(xb3, wf, bf, tm):
    b, s, _ = xb3.shape
    return pl.pallas_call(
        _fgate_kernel,
        grid=(b, s // tm),
        in_specs=[pl.BlockSpec((None, tm, D_MODEL), lambda bi, i: (bi, i, 0)),
                  pl.BlockSpec(wf.shape, lambda bi, i: (0, 0)), pl.BlockSpec(bf.shape, lambda bi, i: (0, 0))],
        out_specs=pl.BlockSpec((None, tm, LANES), lambda bi, i: (bi, i, 0)),
        out_shape=jax.ShapeDtypeStruct((b, s, LANES), F32),
        scratch_shapes=[pltpu.VMEM((1, LANES), F32)],
        compiler_params=_params("parallel", "arbitrary"),
        name="fox_gate",
    )(xb3, wf, bf)


def _dispatch_kernel(pos_ref, x_ref, zeros_ref, xs_ref, sem):
    del zeros_ref
    tm = x_ref.shape[0]

    def copy(r, k):
        return pltpu.make_async_copy(x_ref.at[pl.ds(r, 1), :], xs_ref.at[pl.ds(pos_ref[0, 0, 2 * r + k], 1), :], sem)

    def start(r, _):
        copy(r, 0).start()
        copy(r, 1).start()
        return 0

    def wait(r, _):
        copy(r, 0).wait()
        copy(r, 1).wait()
        return 0

    lax.fori_loop(0, tm, start, 0, unroll=DMA_UNROLL)
    lax.fori_loop(0, tm, wait, 0, unroll=DMA_UNROLL)


def _dispatch(x, pos, n_rows, tm):
    m, width = x.shape
    pos3 = pos.reshape(m // tm, 1, 2 * tm)
    return pl.pallas_call(
        _dispatch_kernel,
        grid=(m // tm,),
        in_specs=[pl.BlockSpec((1, 1, 2 * tm), lambda i: (i, 0, 0), memory_space=pltpu.SMEM),
                  pl.BlockSpec((tm, width), lambda i: (i, 0)),
                  pl.BlockSpec(memory_space=pl.ANY)],
        out_specs=pl.BlockSpec(memory_space=pl.ANY),
        out_shape=jax.ShapeDtypeStruct((n_rows, width), x.dtype),
        scratch_shapes=[pltpu.SemaphoreType.DMA(())],
        input_output_aliases={2: 0},
        compiler_params=_params("arbitrary"),
        name="moe_dispatch",
    )(pos3, x, jnp.zeros((n_rows, width), x.dtype))


def _moe_up_kernel(te_ref, nu_ref, nx_ref, x_ref, wg_hbm, wu_hbm, o_ref, sg_ref, su_ref, wb_ref, sems, *, tn):
    groups = [(lo, min(MXU_COLS, tn - lo)) for lo in range(0, tn, MXU_COLS)]
    j = pl.program_id(0)
    t = pl.program_id(1)
    live = t < nu_ref[0]
    fresh = jnp.logical_or(t == 0, te_ref[t] != te_ref[jnp.maximum(t - 1, 0)])

    def weight_copies(e, jj):
        cols = pl.ds(pl.multiple_of(jj * tn, LANES), tn)
        return (pltpu.make_async_copy(wg_hbm.at[e, :, cols], sg_ref, sems.at[0]),
                pltpu.make_async_copy(wu_hbm.at[e, :, cols], su_ref, sems.at[1]))

    def start(e, jj):
        for c in weight_copies(e, jj):
            c.start()

    @pl.when(jnp.logical_and(j == 0, t == 0))
    def _():
        start(te_ref[0], 0)

    @pl.when(jnp.logical_and(live, fresh))
    def _():
        for c in weight_copies(te_ref[t], j):
            c.wait()
        def cast_rows(c, _):
            rows = pl.ds(pl.multiple_of(c * 256, 256), 256)
            for lo, width in groups:
                wb_ref[rows, 2 * lo:2 * lo + width] = sg_ref[rows, lo:lo + width].astype(BF16)
                wb_ref[rows, 2 * lo + width:2 * lo + 2 * width] = su_ref[rows, lo:lo + width].astype(BF16)
            return 0

        lax.fori_loop(0, D_MODEL // 256, cast_rows, 0)
        nxt = nx_ref[t]

        @pl.when(nxt >= 0)
        def _():
            start(nxt, j)

        @pl.when(jnp.logical_and(nxt < 0, j + 1 < pl.num_programs(0)))
        def _():
            start(te_ref[0], j + 1)

    @pl.when(live)
    def _():
        x = x_ref[...].astype(BF16)
        for lo, width in groups:
            gu = _dot(x, wb_ref[:, 2 * lo:2 * lo + 2 * width])
            o_ref[:, lo:lo + width] = _silu_mul(gu[:, :width], gu[:, width:]).astype(BF16)

    @pl.when(jnp.logical_not(live))
    def _():
        o_ref[...] = jnp.zeros_like(o_ref)


def _moe_up(xs, wg, wu, tile_expert, n_used, next_expert, tm, tn):
    p = xs.shape[0]
    live = lambda t, nu: jnp.minimum(t, nu[0] - 1)
    return pl.pallas_call(
        functools.partial(_moe_up_kernel, tn=tn),
        grid_spec=pltpu.PrefetchScalarGridSpec(
            num_scalar_prefetch=3,
            grid=(D_FF // tn, p // tm),
            in_specs=[pl.BlockSpec((tm, D_MODEL), lambda j, t, te, nu, nx: (live(t, nu), 0)),
                      pl.BlockSpec(memory_space=pl.ANY), pl.BlockSpec(memory_space=pl.ANY)],
            out_specs=pl.BlockSpec((tm, tn), lambda j, t, te, nu, nx: (t, j)),
            scratch_shapes=[pltpu.VMEM((D_MODEL, tn), F32), pltpu.VMEM((D_MODEL, tn), F32),
                            pltpu.VMEM((D_MODEL, 2 * tn), BF16), pltpu.SemaphoreType.DMA((2,))],
        ),
        out_shape=jax.ShapeDtypeStruct((p, D_FF), BF16),
        compiler_params=_params("arbitrary", "arbitrary"),
        name="moe_up",
    )(tile_expert, n_used, next_expert, xs, wg, wu)


def _moe_down_kernel(te_ref, nu_ref, h_ref, w_ref, o_ref, wb_ref):
    t = pl.program_id(1)
    live = t < nu_ref[0]
    fresh = jnp.logical_or(t == 0, te_ref[t] != te_ref[jnp.maximum(t - 1, 0)])

    @pl.when(jnp.logical_and(live, fresh))
    def _():
        wb_ref[...] = w_ref[...].astype(BF16)

    @pl.when(live)
    def _():
        o_ref[...] = _dot(h_ref[...], wb_ref[...])

    @pl.when(jnp.logical_not(live))
    def _():
        o_ref[...] = jnp.zeros_like(o_ref)


def _moe_down(hs, wd, tile_expert, n_used, tm, tn):
    p = hs.shape[0]
    live = lambda t, nu: jnp.minimum(t, nu[0] - 1)
    return pl.pallas_call(
        _moe_down_kernel,
        grid_spec=pltpu.PrefetchScalarGridSpec(
            num_scalar_prefetch=2,
            grid=(D_MODEL // tn, p // tm),
            in_specs=[pl.BlockSpec((tm, D_FF), lambda j, t, te, nu: (live(t, nu), 0)),
                      pl.BlockSpec((None, D_FF, tn), lambda j, t, te, nu: (te[live(t, nu)], 0, j))],
            out_specs=pl.BlockSpec((tm, tn), lambda j, t, te, nu: (t, j)),
            scratch_shapes=[pltpu.VMEM((D_FF, tn), BF16)],
        ),
        out_shape=jax.ShapeDtypeStruct((p, D_MODEL), F32),
        compiler_params=_params("arbitrary", "arbitrary"),
        name="moe_down",
    )(tile_expert, n_used, hs, wd)


def _combine_ln_kernel(pos_ref, ys_ref, x_ref, route_ref, g_ref, b_ref, o_ref, buf0, buf1, sem):
    tm = x_ref.shape[0]

    def copies(r):
        c0 = pltpu.make_async_copy(ys_ref.at[pl.ds(pos_ref[0, 0, 2 * r], 1), :], buf0.at[pl.ds(r, 1), :], sem)
        c1 = pltpu.make_async_copy(ys_ref.at[pl.ds(pos_ref[0, 0, 2 * r + 1], 1), :], buf1.at[pl.ds(r, 1), :], sem)
        return c0, c1

    def start(r, _):
        c0, c1 = copies(r)
        c0.start()
        c1.start()
        return 0

    def wait(r, _):
        c0, c1 = copies(r)
        c0.wait()
        c1.wait()
        return 0

    lax.fori_loop(0, tm, start, 0, unroll=DMA_UNROLL)
    lax.fori_loop(0, tm, wait, 0, unroll=DMA_UNROLL)
    route = route_ref[...]
    y = route[:, 2:3] * buf0[...] + route[:, 3:4] * buf1[...]
    o_ref[...] = _layer_norm(ALPHA * x_ref[...] + y, g_ref[...], b_ref[...])


def _combine_ln(ys, pos, x, route, g, b, tm):
    m = x.shape[0]
    pos3 = pos.reshape(m // tm, 1, 2 * tm)
    vec = pl.BlockSpec((1, D_MODEL), lambda i: (0, 0))
    return pl.pallas_call(
        _combine_ln_kernel,
        grid=(m // tm,),
        in_specs=[pl.BlockSpec((1, 1, 2 * tm), lambda i: (i, 0, 0), memory_space=pltpu.SMEM),
                  pl.BlockSpec(memory_space=pl.ANY),
                  pl.BlockSpec((tm, D_MODEL), lambda i: (i, 0)),
                  pl.BlockSpec((tm, LANES), lambda i: (i, 0)), vec, vec],
        out_specs=pl.BlockSpec((tm, D_MODEL), lambda i: (i, 0)),
        out_shape=jax.ShapeDtypeStruct((m, D_MODEL), F32),
        scratch_shapes=[pltpu.VMEM((tm, D_MODEL), F32), pltpu.VMEM((tm, D_MODEL), F32), pltpu.SemaphoreType.DMA(())],
        compiler_params=_params("arbitrary"),
        name="moe_combine_ln",
    )(pos3, ys, x, route, g, b)


def _rope_tables(seq):
    def angles(dim):
        inv_freq = 1.0 / (ROPE_THETA ** (jnp.arange(0, dim, 2, dtype=F32) / dim))
        ang = jnp.arange(seq, dtype=F32)[:, None] * inv_freq[None, :]
        return jnp.cos(ang), jnp.sin(ang)

    c128, s128 = angles(HEAD_DIM)
    cos_f = jnp.concatenate([c128, c128], axis=-1)
    sin_f = jnp.concatenate([-s128, s128], axis=-1)
    c64, s64 = angles(MLA_ROPE_DIM)
    z32 = jnp.zeros_like(c64)
    cos_t = jnp.concatenate([c64, c64, z32, z32], axis=-1)
    sin_a = jnp.concatenate([-s64, z32, z32, z32], axis=-1)
    sin_b = jnp.concatenate([z32, s64, z32, z32], axis=-1)
    return cos_f, sin_f, cos_t, sin_a, sin_b, c64.T, s64.T


def _pad_cols(a, width):
    return jnp.pad(a, ((0, 0), (0, width - a.shape[1])))


def _row(v):
    return v.reshape(1, -1).astype(F32)


def _even_layer(x, xb, batch, seq, tables, w_in, q_norm, w_q_b, kv_norm, w_kv_b, w_out, ln1_g, ln1_b,
                w_gate, w_up, w_down, ln2_g, ln2_b):
    cos_f, sin_f, cos_t, sin_a, sin_b, cos_tt, sin_tt = tables
    tm = min(512, seq)
    wa = jnp.concatenate([w_in[:, OFF_CKV:OFF_KROPE], _pad_cols(w_in[:, OFF_KROPE:OFF_DQ], LANES),
                          _pad_cols(w_in[:, OFF_CQ:OFF_CKV], MLA_Q_RANK_PAD)], axis=1).astype(BF16)
    qg = _pad_cols(_row(q_norm), MLA_Q_RANK_PAD)
    wq = jnp.pad(w_q_b.reshape(MLA_Q_RANK, MLA_HEADS, MLA_QK_DIM),
                 ((0, MLA_Q_RANK_PAD - MLA_Q_RANK), (0, 0), (0, MLA_QK_PAD - MLA_QK_DIM)))
    wqt = wq.reshape(MLA_Q_RANK_PAD, MLA_HEADS * MLA_QK_PAD).T.astype(BF16)
    wkv3 = w_kv_b.reshape(MLA_KV_RANK, MLA_HEADS, MLA_NOPE_DIM + MLA_V_DIM)
    wk = wkv3[:, :, :MLA_NOPE_DIM].reshape(MLA_KV_RANK, -1).astype(BF16)
    wvt = wkv3[:, :, MLA_NOPE_DIM:].reshape(MLA_KV_RANK, -1).T.astype(BF16)
    xb3 = xb.reshape(batch, seq, D_MODEL)
    qt_mla, k_mla, vt_mla = _mla_prep(xb3, wa, _row(kv_norm), qg, wqt, wk, wvt, cos_t, sin_a, sin_b,
                                      cos_tt, sin_tt, tm)
    tq = min(FLASH_TQ, seq)
    o_mla = _flash(qt_mla, k_mla, vt_mla, None, heads=MLA_HEADS, dq=MLA_QK_PAD, dk=MLA_QK_PAD, dv=MLA_V_DIM,
                   tq=tq)
    o_mla = o_mla.reshape(batch * seq, -1)

    dqkv = _dqkv(xb, w_in[:, OFF_DQ:].astype(BF16), cos_f, sin_f, seq, tm)
    dqkv3 = dqkv.reshape(batch, seq, -1)
    outs, lses = [], []
    for g, (window, dil) in enumerate(DIL_PATTERNS):
        assert window == DIL_SPAN * dil
        o_g, lse_g = _dilated_group(dqkv3, g, dil, min(seq, max(DIL_TILE, DIL_SPAN * dil)))
        outs.append(o_g)
        lses.append(lse_g)
    n_mla = MLA_HEADS * MLA_V_DIM
    wo = w_out.astype(BF16)
    x1, x1b = _even_out(o_mla, outs, lses, wo[:n_mla], wo[n_mla:], x, _row(ln1_g), _row(ln1_b), tm)
    hmid = _ffn_up(x1b, w_gate, w_up, min(1024, seq), 512)
    return _ffn_down_ln(hmid, w_down.astype(BF16), x1, _row(ln2_g), _row(ln2_b), tm, 1408)


def _odd_layer(x, xb, batch, seq, w_qkv, w_f, b_f, w_out, ln1_g, ln1_b, router_w, router_b,
               exp_w_gate, exp_w_up, exp_w_down, ln2_g, ln2_b):
    m = batch * seq
    tm = min(512, seq)
    xb3 = xb.reshape(batch, seq, D_MODEL)
    wb = w_qkv.astype(BF16)
    qt = _proj_t(xb3, wb[:, :FOX_WIDTH].T, HEAD_DIM ** -0.5 * LOG2E, tm, 1024)
    k = _proj(xb, wb[:, FOX_WIDTH:2 * FOX_WIDTH], tm, 1024).reshape(batch, seq, FOX_WIDTH)
    vt = _proj_t(xb3, wb[:, 2 * FOX_WIDTH:].T, 1.0, tm, 1024)
    c = _fgate(xb3, _pad_cols(w_f, LANES).astype(BF16), _pad_cols(_row(b_f), LANES), min(256, seq))
    c_t = jnp.transpose(c[:, :, :FOX_HEADS], (0, 2, 1)).reshape(batch, FOX_HEADS, 1, seq)
    tq = min(FLASH_TQ, seq)
    o = _flash(qt, k, vt, c_t, heads=FOX_HEADS, dq=HEAD_DIM, dk=HEAD_DIM, dv=HEAD_DIM, tq=tq)
    rb = jnp.full((1, LANES), NEG, F32).at[0, :N_EXPERTS].set(router_b.astype(F32))
    x1 = _odd_out(o.reshape(m, -1), w_out.astype(BF16), x, _row(ln1_g), _row(ln1_b), tm)
    route, counts = _router(x1, _pad_cols(router_w, LANES), rb, min(256, seq))
    tile = MOE_TILE
    n_tiles = (2 * m) // tile + N_EXPERTS
    cnt = counts[0, :N_EXPERTS].astype(jnp.int32)
    tiles_per = (cnt + tile - 1) // tile
    tile_end = jnp.cumsum(tiles_per)
    offset = (tile_end - tiles_per) * tile
    idx = route[:, 0:2].astype(jnp.int32)
    pos = (offset[idx] + route[:, 4:6].astype(jnp.int32)).reshape(-1)
    n_used = tile_end[-1:]
    tile_ids = jnp.arange(n_tiles, dtype=jnp.int32)
    tile_expert = jnp.minimum(jnp.sum((tile_end[None, :] <= tile_ids[:, None]).astype(jnp.int32), axis=1),
                              N_EXPERTS - 1)
    td = min(256, seq)
    xs = _dispatch(x1, pos, n_tiles * tile, td)
    experts = jnp.arange(N_EXPERTS, dtype=jnp.int32)
    later = jnp.where((experts[None, :] > experts[:, None]) & (tiles_per[None, :] > 0), experts[None, :], N_EXPERTS)
    next_nonempty = jnp.min(later, axis=1)
    next_expert = jnp.where(next_nonempty < N_EXPERTS, next_nonempty, -1)[tile_expert].astype(jnp.int32)
    hs = _moe_up(xs, exp_w_gate, exp_w_up, tile_expert, n_used, next_expert, tile, 1408)
    split = tile // MOE_DOWN_TILE
    ys = _moe_down(hs, exp_w_down, jnp.repeat(tile_expert, split), n_used * split, MOE_DOWN_TILE, 512)
    return _combine_ln(ys, pos, x1, route, _row(ln2_g), _row(ln2_b), td)


def kernel(x, ev_w_in, ev_q_norm, ev_w_q_b, ev_kv_norm, ev_w_kv_b, ev_w_out, ev_ln1_g, ev_ln1_b, ev_ffn_w_gate, ev_ffn_w_up, ev_ffn_w_down, ev_ln2_g, ev_ln2_b, od_w_qkv, od_w_f, od_b_f, od_w_out, od_ln1_g, od_ln1_b, od_router_w, od_router_b, od_exp_w_gate, od_exp_w_up, od_exp_w_down, od_ln2_g, od_ln2_b):
    batch, seq, _ = x.shape
    tables = _rope_tables(seq)
    h = x.reshape(batch * seq, D_MODEL)
    hb = h.astype(BF16)
    for layer in range(DEPTH):
        i = layer // 2
        if layer % 2 == 0:
            h, hb = _even_layer(h, hb, batch, seq, tables, ev_w_in[i], ev_q_norm[i], ev_w_q_b[i], ev_kv_norm[i],
                                ev_w_kv_b[i], ev_w_out[i], ev_ln1_g[i], ev_ln1_b[i], ev_ffn_w_gate[i],
                                ev_ffn_w_up[i], ev_ffn_w_down[i], ev_ln2_g[i], ev_ln2_b[i])
        else:
            h = _odd_layer(h, hb, batch, seq, od_w_qkv[i], od_w_f[i], od_b_f[i], od_w_out[i], od_ln1_g[i],
                           od_ln1_b[i], od_router_w[i], od_router_b[i], od_exp_w_gate[i], od_exp_w_up[i],
                           od_exp_w_down[i], od_ln2_g[i], od_ln2_b[i])
            hb = h.astype(BF16)
    return h.reshape(batch, seq, D_MODEL)
```

```python
import functools

import jax
import jax.numpy as jnp
from jax import lax
from jax.experimental import pallas as pl
from jax.experimental.pallas import tpu as pltpu

F32 = jnp.float32
BF16 = jnp.bfloat16

D_MODEL = 2048
HEAD_DIM = 128
LANES = 128
MXU_COLS = 256
ROPE_THETA = 10000.0
LN_EPS = 1e-5
RMS_EPS = 1e-6

MLA_HEADS = 10
MLA_Q_RANK = 448
MLA_Q_RANK_PAD = 512
MLA_KV_RANK = 128
MLA_NOPE_DIM = 128
MLA_ROPE_DIM = 64
MLA_V_DIM = 128
MLA_QK_DIM = MLA_NOPE_DIM + MLA_ROPE_DIM
MLA_QK_PAD = 256

DIL_PATTERNS = ((128, 1), (512, 4), (2048, 16))
DIL_GROUPS = 3
DIL_HEADS = 6
DIL_WIDTH = DIL_HEADS * HEAD_DIM
DIL_SPAN = 128

OFF_CQ = 0
OFF_CKV = OFF_CQ + MLA_Q_RANK
OFF_KROPE = OFF_CKV + MLA_KV_RANK
OFF_DQ = OFF_KROPE + MLA_ROPE_DIM
OFF_DK = OFF_DQ + DIL_GROUPS * DIL_WIDTH
OFF_DV = OFF_DK + DIL_WIDTH
W_IN_COLS = OFF_DV + DIL_WIDTH

FOX_HEADS = 16
FOX_WIDTH = FOX_HEADS * HEAD_DIM

D_FF = 5632
N_EXPERTS = 8
DEPTH = 2
ALPHA = (2.0 * DEPTH) ** 0.25

NEG = -1e30
LOG2E = 1.4426950408889634
FLASH_TQ = 1024
MOE_TILE = 512
MOE_DOWN_TILE = 512
DIL_TILE = 1024
DMA_UNROLL = 8
VMEM_LIMIT = 56 * 1024 * 1024


def _params(*sem, vmem=VMEM_LIMIT):
    return pltpu.CompilerParams(dimension_semantics=sem, vmem_limit_bytes=vmem)


def _dot(a, b):
    return jnp.dot(a, b, preferred_element_type=F32)


def _dot_nt(a, b):
    return lax.dot_general(a, b, (((1,), (1,)), ((), ())), preferred_element_type=F32)


def _layer_norm(y, g, b):
    mu = jnp.mean(y, axis=-1, keepdims=True)
    d = y - mu
    var = jnp.mean(d * d, axis=-1, keepdims=True)
    return d * lax.rsqrt(var + LN_EPS) * g + b


def _rope128(x, cos_f, sin_f):
    return x * cos_f + pltpu.roll(x, 64, 1) * sin_f


def _rope64(x, cos_t, sin_a, sin_b):
    return x * cos_t + pltpu.roll(x, 96, 1) * sin_a + pltpu.roll(x, 32, 1) * sin_b


def _mla_prep_kernel(x_ref, wa_ref, kvg_ref, qg_ref, wqt_ref, wk_ref, wvt_ref, cos_ref, sa_ref, sb_ref,
                     ct_ref, st_ref, qt_ref, k_ref, vt_ref, xb_ref):
    xb = x_ref[...].astype(BF16)
    xb_ref[...] = xb
    h = _dot(xb, wa_ref[...])
    ckv = h[:, 0:128]
    kr = h[:, 128:256]
    cq = h[:, 256:768]
    ckv_n = (ckv * lax.rsqrt(jnp.mean(ckv * ckv, axis=-1, keepdims=True) + RMS_EPS) * kvg_ref[...]).astype(BF16)
    cq_ms = jnp.sum(cq * cq, axis=-1, keepdims=True) * (1.0 / MLA_Q_RANK)
    cq_n = (cq * lax.rsqrt(cq_ms + RMS_EPS) * qg_ref[...]).astype(BF16)
    kr_r = _rope64(kr, cos_ref[...], sa_ref[...], sb_ref[...]).astype(BF16)
    scale = MLA_QK_DIM ** -0.5 * LOG2E
    qt = _dot_nt(wqt_ref[...], cq_n)
    c, s = ct_ref[...], st_ref[...]
    k_nope = _dot(ckv_n, wk_ref[...])
    for hd in range(MLA_HEADS):
        o = hd * MLA_QK_PAD
        qt_ref[o:o + 128, :] = (qt[o:o + 128] * scale).astype(BF16)
        x1, x2 = qt[o + 128:o + 160], qt[o + 160:o + 192]
        qt_ref[o + 128:o + 160, :] = ((x1 * c - x2 * s) * scale).astype(BF16)
        qt_ref[o + 160:o + 192, :] = ((x2 * c + x1 * s) * scale).astype(BF16)
        qt_ref[o + 192:o + 256, :] = jnp.zeros((64, qt.shape[1]), BF16)
        k_ref[:, o:o + 128] = k_nope[:, hd * 128:(hd + 1) * 128].astype(BF16)
        k_ref[:, o + 128:o + 256] = kr_r
    vt_ref[...] = _dot_nt(wvt_ref[...], ckv_n).astype(BF16)


def _mla_prep(x3, wa, kvg, qg, wqt, wk, wvt, cos_t, sin_a, sin_b, cos_tt, sin_tt, tm):
    b, s, _ = x3.shape
    full = lambda shape: pl.BlockSpec(shape, lambda bi, i: (0, 0))
    tab = pl.BlockSpec((tm, LANES), lambda bi, i: (i, 0))
    tab_t = pl.BlockSpec((MLA_ROPE_DIM // 2, tm), lambda bi, i: (0, i))
    wide = MLA_HEADS * MLA_QK_PAD
    vw = MLA_HEADS * MLA_V_DIM
    return pl.pallas_call(
        _mla_prep_kernel,
        grid=(b, s // tm),
        in_specs=[pl.BlockSpec((None, tm, D_MODEL), lambda bi, i: (bi, i, 0)), full(wa.shape), full(kvg.shape),
                  full(qg.shape), full(wqt.shape), full(wk.shape), full(wvt.shape), tab, tab, tab, tab_t, tab_t],
        out_specs=[pl.BlockSpec((None, wide, tm), lambda bi, i: (bi, 0, i)),
                   pl.BlockSpec((None, tm, wide), lambda bi, i: (bi, i, 0)),
                   pl.BlockSpec((None, vw, tm), lambda bi, i: (bi, 0, i)),
                   pl.BlockSpec((None, tm, D_MODEL), lambda bi, i: (bi, i, 0))],
        out_shape=[jax.ShapeDtypeStruct((b, wide, s), BF16), jax.ShapeDtypeStruct((b, s, wide), BF16),
                   jax.ShapeDtypeStruct((b, vw, s), BF16), jax.ShapeDtypeStruct((b, s, D_MODEL), BF16)],
        compiler_params=_params("parallel", "parallel"),
        name="mla_prep",
    )(x3, wa, kvg, qg, wqt, wk, wvt, cos_t, sin_a, sin_b, cos_tt, sin_tt)


def _dqkv_kernel(x_ref, w_ref, cos_ref, sin_ref, o_ref):
    j = pl.program_id(0)
    h = _dot(x_ref[...], w_ref[...])

    @pl.when(j < 4)
    def _():
        cos_f, sin_f = cos_ref[...], sin_ref[...]
        sc = jnp.where(j < 3, HEAD_DIM ** -0.5, 1.0).astype(F32)
        for hd in range(DIL_HEADS):
            sl = slice(hd * 128, (hd + 1) * 128)
            o_ref[:, sl] = (_rope128(h[:, sl], cos_f, sin_f) * sc).astype(BF16)

    @pl.when(j == 4)
    def _():
        o_ref[...] = h.astype(BF16)


def _dqkv(xb, wd, cos_f, sin_f, seq, tm):
    m = xb.shape[0]
    nrow = seq // tm
    n_col = wd.shape[1] // DIL_WIDTH
    tab = pl.BlockSpec((tm, LANES), lambda j, i: (i % nrow, 0))
    return pl.pallas_call(
        _dqkv_kernel,
        grid=(n_col, m // tm),
        in_specs=[pl.BlockSpec((tm, D_MODEL), lambda j, i: (i, 0)),
                  pl.BlockSpec((D_MODEL, DIL_WIDTH), lambda j, i: (0, j)), tab, tab],
        out_specs=pl.BlockSpec((tm, DIL_WIDTH), lambda j, i: (i, j)),
        out_shape=jax.ShapeDtypeStruct((m, wd.shape[1]), BF16),
        compiler_params=_params("parallel", "parallel"),
        name="dil_qkv",
    )(xb, wd, cos_f, sin_f)


def _flash_kernel(*refs, tq, use_c):
    if use_c:
        qt_ref, k_ref, vt_ref, c_ref, o_ref, acc_ref, s0_ref, s1_ref, kaug_ref = refs
    else:
        qt_ref, k_ref, vt_ref, o_ref, acc_ref, s0_ref, s1_ref = refs
    seq = k_ref.shape[0]
    nq = seq // tq
    if use_c:
        row = lax.broadcasted_iota(jnp.int32, (LANES, tq), 0)

        def build(j, _):
            start = pl.multiple_of(j * tq, tq)
            neg = -LOG2E * c_ref[:, pl.ds(start, tq)]
            hi = neg.astype(BF16).astype(F32)
            mid = (neg - hi).astype(BF16).astype(F32)
            lo = neg - hi - mid
            blk = jnp.where(row == 0, hi, jnp.where(row == 1, mid, jnp.where(row == 2, lo, 0.0)))
            kaug_ref[pl.ds(start, tq), :] = blk.T.astype(BF16)
            return 0

        lax.fori_loop(0, nq, build, 0)

    def scores(i, j, s_ref):
        q = qt_ref[:, pl.ds(pl.multiple_of(i * tq, tq), tq)]
        if use_c:
            ones = (lax.broadcasted_iota(jnp.int32, (LANES, tq), 0) < 3).astype(BF16)
            q = jnp.concatenate([q, ones], axis=0)
        start = pl.multiple_of(j * tq, tq)
        kt = k_ref[pl.ds(start, tq), :]
        if use_c:
            kt = jnp.concatenate([kt, kaug_ref[pl.ds(start, tq), :]], axis=1)
        s_ref[...] = _dot(kt, q)

    def update(j, s_ref, stats, masked):
        m, l = stats
        s = s_ref[...]
        if masked:
            on_diag = lax.broadcasted_iota(jnp.int32, (tq, tq), 0) <= lax.broadcasted_iota(jnp.int32, (tq, tq), 1)
            s = jnp.where(on_diag, s, NEG)
        m_new = jnp.maximum(m, jnp.max(s, axis=0, keepdims=True))
        a = jnp.exp2(m - m_new)
        p = jnp.exp2(s - m_new)
        l = a * l + jnp.sum(p, axis=0, keepdims=True)
        start = pl.multiple_of(j * tq, tq)
        acc_ref[...] = a * acc_ref[...] + _dot(vt_ref[:, pl.ds(start, tq)], p.astype(BF16))
        return m_new, l

    def query_tile(i, first, second):
        acc_ref[...] = jnp.zeros_like(acc_ref)

        def pair(jj, stats):
            scores(i, 2 * jj + 1, second)
            stats = update(2 * jj, first, stats, False)
            scores(i, 2 * jj + 2, first)
            return update(2 * jj + 1, second, stats, False)

        init = (jnp.full((1, tq), NEG, F32), jnp.zeros((1, tq), F32))
        stats = lax.fori_loop(0, i // 2, pair, init)
        nxt = jnp.minimum(i + 1, nq - 1)

        def odd_tail(stats):
            scores(i, i, second)
            stats = update(i - 1, first, stats, False)
            scores(nxt, 0, first)
            return update(i, second, stats, True)

        def even_tail(stats):
            scores(nxt, 0, second)
            return update(i, first, stats, True)

        _, l = lax.cond(i % 2 == 1, odd_tail, even_tail, stats)
        o_ref[pl.ds(pl.multiple_of(i * tq, tq), tq), :] = (acc_ref[...] / l).T.astype(o_ref.dtype)
        return 0

    scores(0, 0, s0_ref)

    def query_loop(i, _):
        return lax.cond(((i + 1) // 2) % 2 == 0, lambda: query_tile(i, s0_ref, s1_ref),
                        lambda: query_tile(i, s1_ref, s0_ref))

    lax.fori_loop(0, nq, query_loop, 0)


def _flash(qt, k, vt, c, *, heads, dq, dk, dv, tq):
    b, s, _ = k.shape
    use_c = c is not None
    in_specs = [pl.BlockSpec((None, dq, s), lambda bi, h: (bi, h, 0)),
                pl.BlockSpec((None, s, dk), lambda bi, h: (bi, 0, h)),
                pl.BlockSpec((None, dv, s), lambda bi, h: (bi, h, 0))]
    args = [qt, k, vt]
    scratch = [pltpu.VMEM((dv, tq), F32), pltpu.VMEM((tq, tq), F32), pltpu.VMEM((tq, tq), F32)]
    if use_c:
        in_specs.append(pl.BlockSpec((None, None, 1, s), lambda bi, h: (bi, h, 0, 0)))
        args.append(c)
        scratch.append(pltpu.VMEM((s, LANES), BF16))
    return pl.pallas_call(
        functools.partial(_flash_kernel, tq=tq, use_c=use_c),
        grid=(b, heads),
        in_specs=in_specs,
        out_specs=pl.BlockSpec((None, s, dv), lambda bi, h: (bi, 0, h)),
        out_shape=jax.ShapeDtypeStruct((b, s, heads * dv), BF16),
        scratch_shapes=scratch,
        compiler_params=_params("parallel", "parallel"),
        name="flash_fox" if use_c else "flash_mla",
    )(*args)


def _dilated_kernel(q_ref, kc_ref, kp_ref, vc_ref, vp_ref, o_ref, lse_ref, qf, kf, vf, *, dil, tn):
    i = pl.program_id(1)
    per_class = tn // dil
    qf[...] = q_ref[...].astype(F32)
    kf[0:tn, :] = kp_ref[...].astype(F32)
    kf[tn:2 * tn, :] = kc_ref[...].astype(F32)
    vf[0:tn, :] = vp_ref[...].astype(F32)
    vf[tn:2 * tn, :] = vc_ref[...].astype(F32)
    row = lax.broadcasted_iota(jnp.int32, (DIL_SPAN, 2 * DIL_SPAN), 0)
    col = lax.broadcasted_iota(jnp.int32, (DIL_SPAN, 2 * DIL_SPAN), 1)
    back = row + DIL_SPAN - col
    in_band = jnp.where(back >= 0, jnp.where(back <= DIL_SPAN, 0.0, NEG), NEG)
    first_band = jnp.where(col >= jnp.where(i > 0, 0, DIL_SPAN), in_band, NEG)
    for r in range(dil):
        k_r = jnp.concatenate([kf[pl.ds(tn - DIL_SPAN * dil + r, DIL_SPAN, stride=dil), :],
                               kf[pl.ds(tn + r, per_class, stride=dil), :]], axis=0).astype(BF16)
        v_r = jnp.concatenate([vf[pl.ds(tn - DIL_SPAN * dil + r, DIL_SPAN, stride=dil), :],
                               vf[pl.ds(tn + r, per_class, stride=dil), :]], axis=0).astype(BF16)
        q_r = qf[pl.ds(r, per_class, stride=dil), :].astype(BF16)
        for a in range(per_class // DIL_SPAN):
            lo = a * DIL_SPAN
            s = _dot_nt(q_r[lo:lo + DIL_SPAN], k_r[lo:lo + 2 * DIL_SPAN]) + (first_band if a == 0 else in_band)
            m = jnp.max(s, axis=-1, keepdims=True)
            p = jnp.exp(s - m)
            l = jnp.sum(p, axis=-1, keepdims=True)
            rows = pl.ds(r + lo * dil, DIL_SPAN, stride=dil)
            o_ref[rows, :] = _dot(p.astype(BF16), v_r[lo:lo + 2 * DIL_SPAN]) / l
            lse_ref[rows, :] = jnp.broadcast_to(m + jnp.log(l), (DIL_SPAN, LANES))


def _dilated_group(dqkv3, g, dil, tn):
    batch, seq, _ = dqkv3.shape
    assert tn % (DIL_SPAN * dil) == 0
    blk = lambda col, prev: pl.BlockSpec(
        (None, tn, HEAD_DIM), lambda b, i, h: (b, jnp.maximum(i - 1, 0) if prev else i, col * DIL_HEADS + h))
    out = pl.BlockSpec((None, tn, HEAD_DIM), lambda b, i, h: (b, i, h))
    o, lse = pl.pallas_call(
        functools.partial(_dilated_kernel, dil=dil, tn=tn),
        grid=(batch, seq // tn, DIL_HEADS),
        in_specs=[blk(g, False), blk(3, False), blk(3, True), blk(4, False), blk(4, True)],
        out_specs=[out, out],
        out_shape=[jax.ShapeDtypeStruct((batch, seq, DIL_WIDTH), F32)] * 2,
        scratch_shapes=[pltpu.VMEM((tn, HEAD_DIM), F32), pltpu.VMEM((2 * tn, HEAD_DIM), F32),
                        pltpu.VMEM((2 * tn, HEAD_DIM), F32)],
        compiler_params=_params("parallel", "parallel", "parallel"),
        name=f"dilated_{dil}",
    )(dqkv3, dqkv3, dqkv3, dqkv3, dqkv3)
    return o.reshape(batch * seq, DIL_WIDTH), lse.reshape(batch * seq, DIL_WIDTH)


def _even_out_kernel(om_ref, o0, o1, o2, l0, l1, l2, wm_ref, wd_ref, x_ref, g_ref, b_ref, o_ref, ob_ref):
    ls = [l0[...], l1[...], l2[...]]
    mx = jnp.maximum(jnp.maximum(ls[0], ls[1]), ls[2])
    es = [jnp.exp(v - mx) for v in ls]
    den = es[0] + es[1] + es[2]
    o_dil = ((es[0] / den) * o0[...] + (es[1] / den) * o1[...] + (es[2] / den) * o2[...]).astype(BF16)
    y = _dot(om_ref[...], wm_ref[...]) + _dot(o_dil, wd_ref[...])
    out = _layer_norm(ALPHA * x_ref[...] + y, g_ref[...], b_ref[...])
    o_ref[...] = out
    ob_ref[...] = out.astype(BF16)


def _even_out(o_mla, outs, lses, w_mla, w_dil, x, g, b, tm):
    m = x.shape[0]
    row = lambda width: pl.BlockSpec((tm, width), lambda i: (i, 0))
    full = lambda shape: pl.BlockSpec(shape, lambda i: (0, 0))
    return pl.pallas_call(
        _even_out_kernel,
        grid=(m // tm,),
        in_specs=[row(o_mla.shape[1])] + [row(DIL_WIDTH)] * 6
        + [full(w_mla.shape), full(w_dil.shape), row(D_MODEL), full(g.shape), full(b.shape)],
        out_specs=[row(D_MODEL), row(D_MODEL)],
        out_shape=[jax.ShapeDtypeStruct((m, D_MODEL), F32), jax.ShapeDtypeStruct((m, D_MODEL), BF16)],
        compiler_params=_params("parallel"),
        name="even_out_ln",
    )(o_mla, *outs, *lses, w_mla, w_dil, x, g, b)


def _odd_out_kernel(a_ref, w_ref, x_ref, g_ref, b_ref, o_ref):
    o_ref[...] = _layer_norm(ALPHA * x_ref[...] + _dot(a_ref[...], w_ref[...]), g_ref[...], b_ref[...])


def _odd_out(a, w, x, g, b, tm):
    m = x.shape[0]
    row = lambda width: pl.BlockSpec((tm, width), lambda i: (i, 0))
    full = lambda shape: pl.BlockSpec(shape, lambda i: (0, 0))
    return pl.pallas_call(
        _odd_out_kernel,
        grid=(m // tm,),
        in_specs=[row(a.shape[1]), full(w.shape), row(D_MODEL), full(g.shape), full(b.shape)],
        out_specs=row(D_MODEL),
        out_shape=jax.ShapeDtypeStruct((m, D_MODEL), F32),
        compiler_params=_params("parallel"),
        name="odd_out_ln",
    )(a, w, x, g, b)


def _router_kernel(x_ref, rwh_ref, rwl_ref, rb_ref, route_ref, cnt_ref, carry_ref):
    @pl.when(pl.program_id(0) == 0)
    def _():
        carry_ref[...] = jnp.zeros_like(carry_ref)

    x = x_ref[...]
    xh = x.astype(BF16)
    xl = (x - xh.astype(F32)).astype(BF16)
    logits = _dot(xh, rwh_ref[...]) + (_dot(xh, rwl_ref[...]) + _dot(xl, rwh_ref[...])) + rb_ref[...]
    tm = logits.shape[0]
    lane = lax.broadcasted_iota(jnp.int32, (tm, LANES), 1)
    l1 = jnp.max(logits, axis=-1, keepdims=True)
    i1 = jnp.min(jnp.where(logits == l1, lane, LANES), axis=-1, keepdims=True)
    rest = jnp.where(lane == i1, NEG, logits)
    l2 = jnp.max(rest, axis=-1, keepdims=True)
    i2 = jnp.min(jnp.where(rest == l2, lane, LANES), axis=-1, keepdims=True)
    e = jnp.exp(l2 - l1)
    w1 = 1.0 / (1.0 + e)
    w2 = e / (1.0 + e)
    hot1 = (lane == i1).astype(F32)
    hot2 = (lane == i2).astype(F32)
    cnt = hot1 + hot2
    strict = (lax.broadcasted_iota(jnp.int32, (tm, tm), 1) < lax.broadcasted_iota(jnp.int32, (tm, tm), 0)).astype(BF16)
    before = _dot(strict, cnt.astype(BF16)) + carry_ref[...]
    r1 = jnp.sum(before * hot1, axis=-1, keepdims=True)
    r2 = jnp.sum(before * hot2, axis=-1, keepdims=True)
    vals = (i1.astype(F32), i2.astype(F32), w1, w2, r1, r2)
    route = jnp.zeros((tm, LANES), F32)
    for idx, val in enumerate(vals):
        route = jnp.where(lane == idx, val, route)
    route_ref[...] = route
    total = carry_ref[...] + jnp.sum(cnt, axis=0, keepdims=True)
    carry_ref[...] = total
    cnt_ref[...] = jnp.broadcast_to(total, cnt_ref.shape)


def _router(x, rw, rb, tm):
    m = x.shape[0]
    full = lambda shape: pl.BlockSpec(shape, lambda i: (0, 0))
    rw_hi = rw.astype(BF16)
    rw_lo = (rw - rw_hi.astype(F32)).astype(BF16)
    return pl.pallas_call(
        _router_kernel,
        grid=(m // tm,),
        in_specs=[pl.BlockSpec((tm, D_MODEL), lambda i: (i, 0)), full(rw.shape), full(rw.shape), full(rb.shape)],
        out_specs=[pl.BlockSpec((tm, LANES), lambda i: (i, 0)), pl.BlockSpec((8, LANES), lambda i: (0, 0))],
        out_shape=[jax.ShapeDtypeStruct((m, LANES), F32), jax.ShapeDtypeStruct((8, LANES), F32)],
        scratch_shapes=[pltpu.VMEM((1, LANES), F32)],
        compiler_params=_params("arbitrary"),
        name="moe_router",
    )(x, rw_hi, rw_lo, rb)


def _silu_mul(g, u):
    return g * (1.0 / (1.0 + jnp.exp(-g))) * u


def _ffn_up_kernel(x_ref, wg_ref, wu_ref, o_ref, wgb_ref, wub_ref):
    @pl.when(pl.program_id(1) == 0)
    def _():
        wgb_ref[...] = wg_ref[...].astype(BF16)
        wub_ref[...] = wu_ref[...].astype(BF16)

    x = x_ref[...]
    o_ref[...] = _silu_mul(_dot(x, wgb_ref[...]), _dot(x, wub_ref[...])).astype(BF16)


def _ffn_up(xb, wg, wu, tm, tn):
    m = xb.shape[0]
    return pl.pallas_call(
        _ffn_up_kernel,
        grid=(D_FF // tn, m // tm),
        in_specs=[pl.BlockSpec((tm, D_MODEL), lambda j, i: (i, 0)),
                  pl.BlockSpec((D_MODEL, tn), lambda j, i: (0, j)),
                  pl.BlockSpec((D_MODEL, tn), lambda j, i: (0, j))],
        out_specs=pl.BlockSpec((tm, tn), lambda j, i: (i, j)),
        out_shape=jax.ShapeDtypeStruct((m, D_FF), BF16),
        scratch_shapes=[pltpu.VMEM((D_MODEL, tn), BF16), pltpu.VMEM((D_MODEL, tn), BF16)],
        compiler_params=_params("arbitrary", "arbitrary"),
        name="ffn_up",
    )(xb, wg, wu)


def _ffn_down_ln_kernel(h_ref, w_ref, x_ref, g_ref, b_ref, o_ref, ob_ref, acc_ref):
    k = pl.program_id(1)

    @pl.when(k == 0)
    def _():
        acc_ref[...] = jnp.zeros_like(acc_ref)

    acc_ref[...] += _dot(h_ref[...], w_ref[...])

    @pl.when(k == pl.num_programs(1) - 1)
    def _():
        out = _layer_norm(ALPHA * x_ref[...] + acc_ref[...], g_ref[...], b_ref[...])
        o_ref[...] = out
        ob_ref[...] = out.astype(BF16)


def _ffn_down_ln(h, wd, x, g, b, tm, tk):
    m = x.shape[0]
    row = pl.BlockSpec((tm, D_MODEL), lambda i, k: (i, 0))
    res = row
    vec = pl.BlockSpec((1, D_MODEL), lambda i, k: (0, 0))
    return pl.pallas_call(
        _ffn_down_ln_kernel,
        grid=(m // tm, D_FF // tk),
        in_specs=[pl.BlockSpec((tm, tk), lambda i, k: (i, k)), pl.BlockSpec((tk, D_MODEL), lambda i, k: (k, 0)),
                  res, vec, vec],
        out_specs=[row, row],
        out_shape=[jax.ShapeDtypeStruct((m, D_MODEL), F32), jax.ShapeDtypeStruct((m, D_MODEL), BF16)],
        scratch_shapes=[pltpu.VMEM((tm, D_MODEL), F32)],
        compiler_params=_params("parallel", "arbitrary"),
        name="ffn_down_ln",
    )(h, wd, x, g, b)


def _proj_kernel(x_ref, w_ref, o_ref):
    o_ref[...] = _dot(x_ref[...], w_ref[...]).astype(BF16)


def _proj(xb, w, tm, tn):
    m = xb.shape[0]
    n = w.shape[1]
    return pl.pallas_call(
        _proj_kernel,
        grid=(n // tn, m // tm),
        in_specs=[pl.BlockSpec((tm, D_MODEL), lambda j, i: (i, 0)), pl.BlockSpec((D_MODEL, tn), lambda j, i: (0, j))],
        out_specs=pl.BlockSpec((tm, tn), lambda j, i: (i, j)),
        out_shape=jax.ShapeDtypeStruct((m, n), BF16),
        compiler_params=_params("parallel", "parallel"),
        name="proj",
    )(xb, w)


def _proj_t_kernel(w_ref, x_ref, o_ref, *, scale):
    o_ref[...] = (_dot_nt(w_ref[...], x_ref[...]) * scale).astype(BF16)


def _proj_t(xb3, wt, scale, tm, tn):
    b, s, _ = xb3.shape
    n = wt.shape[0]
    return pl.pallas_call(
        functools.partial(_proj_t_kernel, scale=scale),
        grid=(n // tn, b, s // tm),
        in_specs=[pl.BlockSpec((tn, D_MODEL), lambda j, bi, i: (j, 0)),
                  pl.BlockSpec((None, tm, D_MODEL), lambda j, bi, i: (bi, i, 0))],
        out_specs=pl.BlockSpec((None, tn, tm), lambda j, bi, i: (bi, j, i)),
        out_shape=jax.ShapeDtypeStruct((b, n, s), BF16),
        compiler_params=_params("parallel", "parallel", "parallel"),
        name="proj_t",
    )(wt, xb3)


def _fgate_kernel(x_ref, w_ref, b_ref, c_ref, carry_ref):
    @pl.when(pl.program_id(1) == 0)
    def _():
        carry_ref[...] = jnp.zeros_like(carry_ref)

    z = _dot(x_ref[...], w_ref[...]) + b_ref[...]
    log_f = jnp.minimum(z, 0.0) - jnp.log(1.0 + jnp.exp(-jnp.abs(z)))
    tm = z.shape[0]
    tri = (lax.broadcasted_iota(jnp.int32, (tm, tm), 1) <= lax.broadcasted_iota(jnp.int32, (tm, tm), 0)).astype(F32)
    c = jnp.dot(tri, log_f, preferred_element_type=F32, precision=lax.Precision.HIGHEST) + carry_ref[...]
    c_ref[...] = c
    carry_ref[...] = c[tm - 1:tm, :]


def _fgate(xb3, wf, bf, tm):
    b, s, _ = xb3.shape
    return pl.pallas_call(
        _fgate_kernel,
        grid=(b, s // tm),
        in_specs=[pl.BlockSpec((None, tm, D_MODEL), lambda bi, i: (bi, i, 0)),
                  pl.BlockSpec(wf.shape, lambda bi, i: (0, 0)), pl.BlockSpec(bf.shape, lambda bi, i: (0, 0))],
        out_specs=pl.BlockSpec((None, tm, LANES), lambda bi, i: (bi, i, 0)),
        out_shape=jax.ShapeDtypeStruct((b, s, LANES), F32),
        scratch_shapes=[pltpu.VMEM((1, LANES), F32)],
        compiler_params=_params("parallel", "arbitrary"),
        name="fox_gate",
    )(xb3, wf, bf)


def _dispatch_kernel(fill_ref, pos_ref, x_ref, xs_ref, zero_ref, sem, zero_sem):
    tm = x_ref.shape[0]

    @pl.when(pl.program_id(0) == 0)
    def _():
        zero_ref[...] = jnp.zeros_like(zero_ref)
        rows = zero_ref.shape[0]

        def fill(f):
            start = pl.multiple_of(jnp.maximum(fill_ref[f], 0), rows)
            return pltpu.make_async_copy(zero_ref, xs_ref.at[pl.ds(start, rows), :], zero_sem)

        for f in range(fill_ref.shape[0]):
            @pl.when(fill_ref[f] >= 0)
            def _():
                fill(f).start()

        for f in range(fill_ref.shape[0]):
            @pl.when(fill_ref[f] >= 0)
            def _():
                fill(f).wait()

    def copy(r, k):
        return pltpu.make_async_copy(x_ref.at[pl.ds(r, 1), :], xs_ref.at[pl.ds(pos_ref[0, 0, 2 * r + k], 1), :], sem)

    def start(r, _):
        copy(r, 0).start()
        copy(r, 1).start()
        return 0

    def wait(r, _):
        copy(r, 0).wait()
        copy(r, 1).wait()
        return 0

    lax.fori_loop(0, tm, start, 0, unroll=DMA_UNROLL)
    lax.fori_loop(0, tm, wait, 0, unroll=DMA_UNROLL)


def _dispatch(x, pos, fill_rows, n_rows, tile, tm):
    m, width = x.shape
    pos3 = pos.reshape(m // tm, 1, 2 * tm)
    return pl.pallas_call(
        _dispatch_kernel,
        grid_spec=pltpu.PrefetchScalarGridSpec(
            num_scalar_prefetch=1,
            grid=(m // tm,),
            in_specs=[pl.BlockSpec((1, 1, 2 * tm), lambda i, fill: (i, 0, 0), memory_space=pltpu.SMEM),
                      pl.BlockSpec((tm, width), lambda i, fill: (i, 0))],
            out_specs=pl.BlockSpec(memory_space=pl.ANY),
            scratch_shapes=[pltpu.VMEM((tile, width), x.dtype), pltpu.SemaphoreType.DMA(()),
                            pltpu.SemaphoreType.DMA(())],
        ),
        out_shape=jax.ShapeDtypeStruct((n_rows, width), x.dtype),
        compiler_params=_params("arbitrary"),
        name="moe_dispatch",
    )(fill_rows, pos3, x)


def _moe_up_kernel(te_ref, nu_ref, nx_ref, x_ref, wg_hbm, wu_hbm, o_ref, sg_ref, su_ref, wb_ref, sems, *, tn):
    groups = [(lo, min(MXU_COLS, tn - lo)) for lo in range(0, tn, MXU_COLS)]
    j = pl.program_id(0)
    t = pl.program_id(1)
    live = t < nu_ref[0]
    fresh = jnp.logical_or(t == 0, te_ref[t] != te_ref[jnp.maximum(t - 1, 0)])

    def weight_copies(e, jj):
        cols = pl.ds(pl.multiple_of(jj * tn, LANES), tn)
        return (pltpu.make_async_copy(wg_hbm.at[e, :, cols], sg_ref, sems.at[0]),
                pltpu.make_async_copy(wu_hbm.at[e, :, cols], su_ref, sems.at[1]))

    def start(e, jj):
        for c in weight_copies(e, jj):
            c.start()

    @pl.when(jnp.logical_and(j == 0, t == 0))
    def _():
        start(te_ref[0], 0)

    @pl.when(jnp.logical_and(live, fresh))
    def _():
        for c in weight_copies(te_ref[t], j):
            c.wait()
        def cast_rows(c, _):
            rows = pl.ds(pl.multiple_of(c * 256, 256), 256)
            for lo, width in groups:
                wb_ref[rows, 2 * lo:2 * lo + width] = sg_ref[rows, lo:lo + width].astype(BF16)
                wb_ref[rows, 2 * lo + width:2 * lo + 2 * width] = su_ref[rows, lo:lo + width].astype(BF16)
            return 0

        lax.fori_loop(0, D_MODEL // 256, cast_rows, 0)
        nxt = nx_ref[t]

        @pl.when(nxt >= 0)
        def _():
            start(nxt, j)

        @pl.when(jnp.logical_and(nxt < 0, j + 1 < pl.num_programs(0)))
        def _():
            start(te_ref[0], j + 1)

    @pl.when(live)
    def _():
        x = x_ref[...].astype(BF16)
        for lo, width in groups:
            gu = _dot(x, wb_ref[:, 2 * lo:2 * lo + 2 * width])
            o_ref[:, lo:lo + width] = _silu_mul(gu[:, :width], gu[:, width:]).astype(BF16)

    @pl.when(jnp.logical_not(live))
    def _():
        o_ref[...] = jnp.zeros_like(o_ref)


def _moe_up(xs, wg, wu, tile_expert, n_used, next_expert, tm, tn):
    p = xs.shape[0]
    live = lambda t, nu: jnp.minimum(t, nu[0] - 1)
    return pl.pallas_call(
        functools.partial(_moe_up_kernel, tn=tn),
        grid_spec=pltpu.PrefetchScalarGridSpec(
            num_scalar_prefetch=3,
            grid=(D_FF // tn, p // tm),
            in_specs=[pl.BlockSpec((tm, D_MODEL), lambda j, t, te, nu, nx: (live(t, nu), 0)),
                      pl.BlockSpec(memory_space=pl.ANY), pl.BlockSpec(memory_space=pl.ANY)],
            out_specs=pl.BlockSpec((tm, tn), lambda j, t, te, nu, nx: (t, j)),
            scratch_shapes=[pltpu.VMEM((D_MODEL, tn), F32), pltpu.VMEM((D_MODEL, tn), F32),
                            pltpu.VMEM((D_MODEL, 2 * tn), BF16), pltpu.SemaphoreType.DMA((2,))],
        ),
        out_shape=jax.ShapeDtypeStruct((p, D_FF), BF16),
        compiler_params=_params("arbitrary", "arbitrary"),
        name="moe_up",
    )(tile_expert, n_used, next_expert, xs, wg, wu)


def _moe_down_kernel(te_ref, nu_ref, h_ref, w_ref, o_ref, wb_ref):
    t = pl.program_id(1)
    live = t < nu_ref[0]
    fresh = jnp.logical_or(t == 0, te_ref[t] != te_ref[jnp.maximum(t - 1, 0)])

    @pl.when(jnp.logical_and(live, fresh))
    def _():
        wb_ref[...] = w_ref[...].astype(BF16)

    @pl.when(live)
    def _():
        o_ref[...] = _dot(h_ref[...], wb_ref[...])

    @pl.when(jnp.logical_not(live))
    def _():
        o_ref[...] = jnp.zeros_like(o_ref)


def _moe_down(hs, wd, tile_expert, n_used, tm, tn):
    p = hs.shape[0]
    live = lambda t, nu: jnp.minimum(t, nu[0] - 1)
    return pl.pallas_call(
        _moe_down_kernel,
        grid_spec=pltpu.PrefetchScalarGridSpec(
            num_scalar_prefetch=2,
            grid=(D_MODEL // tn, p // tm),
            in_specs=[pl.BlockSpec((tm, D_FF), lambda j, t, te, nu: (live(t, nu), 0)),
                      pl.BlockSpec((None, D_FF, tn), lambda j, t, te, nu: (te[live(t, nu)], 0, j))],
            out_specs=pl.BlockSpec((tm, tn), lambda j, t, te, nu: (t, j)),
            scratch_shapes=[pltpu.VMEM((D_FF, tn), BF16)],
        ),
        out_shape=jax.ShapeDtypeStruct((p, D_MODEL), F32),
        compiler_params=_params("arbitrary", "arbitrary"),
        name="moe_down",
    )(tile_expert, n_used, hs, wd)


def _combine_ln_kernel(pos_ref, ys_ref, x_ref, route_ref, g_ref, b_ref, o_ref, buf0, buf1, sem):
    tm = x_ref.shape[0]

    def copies(r):
        c0 = pltpu.make_async_copy(ys_ref.at[pl.ds(pos_ref[0, 0, 2 * r], 1), :], buf0.at[pl.ds(r, 1), :], sem)
        c1 = pltpu.make_async_copy(ys_ref.at[pl.ds(pos_ref[0, 0, 2 * r + 1], 1), :], buf1.at[pl.ds(r, 1), :], sem)
        return c0, c1

    def start(r, _):
        c0, c1 = copies(r)
        c0.start()
        c1.start()
        return 0

    def wait(r, _):
        c0, c1 = copies(r)
        c0.wait()
        c1.wait()
        return 0

    lax.fori_loop(0, tm, start, 0, unroll=DMA_UNROLL)
    lax.fori_loop(0, tm, wait, 0, unroll=DMA_UNROLL)
    route = route_ref[...]
    y = route[:, 2:3] * buf0[...] + route[:, 3:4] * buf1[...]
    o_ref[...] = _layer_norm(ALPHA * x_ref[...] + y, g_ref[...], b_ref[...])


def _combine_ln(ys, pos, x, route, g, b, tm):
    m = x.shape[0]
    pos3 = pos.reshape(m // tm, 1, 2 * tm)
    vec = pl.BlockSpec((1, D_MODEL), lambda i: (0, 0))
    return pl.pallas_call(
        _combine_ln_kernel,
        grid=(m // tm,),
        in_specs=[pl.BlockSpec((1, 1, 2 * tm), lambda i: (i, 0, 0), memory_space=pltpu.SMEM),
                  pl.BlockSpec(memory_space=pl.ANY),
                  pl.BlockSpec((tm, D_MODEL), lambda i: (i, 0)),
                  pl.BlockSpec((tm, LANES), lambda i: (i, 0)), vec, vec],
        out_specs=pl.BlockSpec((tm, D_MODEL), lambda i: (i, 0)),
        out_shape=jax.ShapeDtypeStruct((m, D_MODEL), F32),
        scratch_shapes=[pltpu.VMEM((tm, D_MODEL), F32), pltpu.VMEM((tm, D_MODEL), F32), pltpu.SemaphoreType.DMA(())],
        compiler_params=_params("arbitrary"),
        name="moe_combine_ln",
    )(pos3, ys, x, route, g, b)


def _rope_tables(seq):
    def angles(dim):
        inv_freq = 1.0 / (ROPE_THETA ** (jnp.arange(0, dim, 2, dtype=F32) / dim))
        ang = jnp.arange(seq, dtype=F32)[:, None] * inv_freq[None, :]
        return jnp.cos(ang), jnp.sin(ang)

    c128, s128 = angles(HEAD_DIM)
    cos_f = jnp.concatenate([c128, c128], axis=-1)
    sin_f = jnp.concatenate([-s128, s128], axis=-1)
    c64, s64 = angles(MLA_ROPE_DIM)
    z32 = jnp.zeros_like(c64)
    cos_t = jnp.concatenate([c64, c64, z32, z32], axis=-1)
    sin_a = jnp.concatenate([-s64, z32, z32, z32], axis=-1)
    sin_b = jnp.concatenate([z32, s64, z32, z32], axis=-1)
    return cos_f, sin_f, cos_t, sin_a, sin_b, c64.T, s64.T


def _pad_cols(a, width):
    return jnp.pad(a, ((0, 0), (0, width - a.shape[1])))


def _row(v):
    return v.reshape(1, -1).astype(F32)


def _even_layer(x, batch, seq, tables, w_in, q_norm, w_q_b, kv_norm, w_kv_b, w_out, ln1_g, ln1_b,
                w_gate, w_up, w_down, ln2_g, ln2_b):
    cos_f, sin_f, cos_t, sin_a, sin_b, cos_tt, sin_tt = tables
    tm = min(512, seq)
    wa = jnp.concatenate([w_in[:, OFF_CKV:OFF_KROPE], _pad_cols(w_in[:, OFF_KROPE:OFF_DQ], LANES),
                          _pad_cols(w_in[:, OFF_CQ:OFF_CKV], MLA_Q_RANK_PAD)], axis=1).astype(BF16)
    qg = _pad_cols(_row(q_norm), MLA_Q_RANK_PAD)
    wq = jnp.pad(w_q_b.reshape(MLA_Q_RANK, MLA_HEADS, MLA_QK_DIM),
                 ((0, MLA_Q_RANK_PAD - MLA_Q_RANK), (0, 0), (0, MLA_QK_PAD - MLA_QK_DIM)))
    wqt = wq.reshape(MLA_Q_RANK_PAD, MLA_HEADS * MLA_QK_PAD).T.astype(BF16)
    wkv3 = w_kv_b.reshape(MLA_KV_RANK, MLA_HEADS, MLA_NOPE_DIM + MLA_V_DIM)
    wk = wkv3[:, :, :MLA_NOPE_DIM].reshape(MLA_KV_RANK, -1).astype(BF16)
    wvt = wkv3[:, :, MLA_NOPE_DIM:].reshape(MLA_KV_RANK, -1).T.astype(BF16)
    qt_mla, k_mla, vt_mla, xb3 = _mla_prep(x.reshape(batch, seq, D_MODEL), wa, _row(kv_norm), qg, wqt, wk, wvt,
                                           cos_t, sin_a, sin_b, cos_tt, sin_tt, tm)
    xb = xb3.reshape(batch * seq, D_MODEL)
    tq = min(FLASH_TQ, seq)
    o_mla = _flash(qt_mla, k_mla, vt_mla, None, heads=MLA_HEADS, dq=MLA_QK_PAD, dk=MLA_QK_PAD, dv=MLA_V_DIM,
                   tq=tq)
    o_mla = o_mla.reshape(batch * seq, -1)

    dqkv = _dqkv(xb, w_in[:, OFF_DQ:].astype(BF16), cos_f, sin_f, seq, tm)
    dqkv3 = dqkv.reshape(batch, seq, -1)
    outs, lses = [], []
    for g, (window, dil) in enumerate(DIL_PATTERNS):
        assert window == DIL_SPAN * dil
        o_g, lse_g = _dilated_group(dqkv3, g, dil, min(seq, max(DIL_TILE, DIL_SPAN * dil)))
        outs.append(o_g)
        lses.append(lse_g)
    n_mla = MLA_HEADS * MLA_V_DIM
    wo = w_out.astype(BF16)
    x1, x1b = _even_out(o_mla, outs, lses, wo[:n_mla], wo[n_mla:], x, _row(ln1_g), _row(ln1_b), tm)
    hmid = _ffn_up(x1b, w_gate, w_up, min(1024, seq), 512)
    return _ffn_down_ln(hmid, w_down.astype(BF16), x1, _row(ln2_g), _row(ln2_b), tm, 1408)


def _odd_layer(x, xb, batch, seq, w_qkv, w_f, b_f, w_out, ln1_g, ln1_b, router_w, router_b,
               exp_w_gate, exp_w_up, exp_w_down, ln2_g, ln2_b):
    m = batch * seq
    tm = min(512, seq)
    xb3 = xb.reshape(batch, seq, D_MODEL)
    wb = w_qkv.astype(BF16)
    qt = _proj_t(xb3, wb[:, :FOX_WIDTH].T, HEAD_DIM ** -0.5 * LOG2E, tm, 1024)
    k = _proj(xb, wb[:, FOX_WIDTH:2 * FOX_WIDTH], tm, 1024).reshape(batch, seq, FOX_WIDTH)
    vt = _proj_t(xb3, wb[:, 2 * FOX_WIDTH:].T, 1.0, tm, 1024)
    c = _fgate(xb3, _pad_cols(w_f, LANES).astype(BF16), _pad_cols(_row(b_f), LANES), min(256, seq))
    c_t = jnp.transpose(c[:, :, :FOX_HEADS], (0, 2, 1)).reshape(batch, FOX_HEADS, 1, seq)
    tq = min(FLASH_TQ, seq)
    o = _flash(qt, k, vt, c_t, heads=FOX_HEADS, dq=HEAD_DIM, dk=HEAD_DIM, dv=HEAD_DIM, tq=tq)
    rb = jnp.full((1, LANES), NEG, F32).at[0, :N_EXPERTS].set(router_b.astype(F32))
    x1 = _odd_out(o.reshape(m, -1), w_out.astype(BF16), x, _row(ln1_g), _row(ln1_b), tm)
    route, counts = _router(x1, _pad_cols(router_w, LANES), rb, min(256, seq))
    tile = MOE_TILE
    n_tiles = (2 * m) // tile + N_EXPERTS
    cnt = counts[0, :N_EXPERTS].astype(jnp.int32)
    tiles_per = (cnt + tile - 1) // tile
    tile_end = jnp.cumsum(tiles_per)
    offset = (tile_end - tiles_per) * tile
    idx = route[:, 0:2].astype(jnp.int32)
    pos = (offset[idx] + route[:, 4:6].astype(jnp.int32)).reshape(-1)
    n_used = tile_end[-1:]
    tile_ids = jnp.arange(n_tiles, dtype=jnp.int32)
    tile_expert = jnp.minimum(jnp.sum((tile_end[None, :] <= tile_ids[:, None]).astype(jnp.int32), axis=1),
                              N_EXPERTS - 1)
    td = min(256, seq)
    experts = jnp.arange(N_EXPERTS, dtype=jnp.int32)
    last_tile = jnp.where(tiles_per > 0, tile_end - 1, -1)
    unused = n_used[0] + experts
    unused = jnp.where(unused < n_tiles, unused, -1)
    fill_tiles = jnp.concatenate([last_tile, unused])
    fill_rows = jnp.where(fill_tiles >= 0, fill_tiles * tile, -1).astype(jnp.int32)
    xs = _dispatch(x1, pos, fill_rows, n_tiles * tile, tile, td)
    later = jnp.where((experts[None, :] > experts[:, None]) & (tiles_per[None, :] > 0), experts[None, :], N_EXPERTS)
    next_nonempty = jnp.min(later, axis=1)
    next_expert = jnp.where(next_nonempty < N_EXPERTS, next_nonempty, -1)[tile_expert].astype(jnp.int32)
    hs = _moe_up(xs, exp_w_gate, exp_w_up, tile_expert, n_used, next_expert, tile, 1408)
    split = tile // MOE_DOWN_TILE
    ys = _moe_down(hs, exp_w_down, jnp.repeat(tile_expert, split), n_used * split, MOE_DOWN_TILE, 512)
    return _combine_ln(ys, pos, x1, route, _row(ln2_g), _row(ln2_b), td)


def kernel(x, ev_w_in, ev_q_norm, ev_w_q_b, ev_kv_norm, ev_w_kv_b, ev_w_out, ev_ln1_g, ev_ln1_b, ev_ffn_w_gate, ev_ffn_w_up, ev_ffn_w_down, ev_ln2_g, ev_ln2_b, od_w_qkv, od_w_f, od_b_f, od_w_out, od_ln1_g, od_ln1_b, od_router_w, od_router_b, od_exp_w_gate, od_exp_w_up, od_exp_w_down, od_ln2_g, od_ln2_b):
    batch, seq, _ = x.shape
    tables = _rope_tables(seq)
    h = x.reshape(batch * seq, D_MODEL)
    hb = None
    for layer in range(DEPTH):
        i = layer // 2
        if layer % 2 == 0:
            h, hb = _even_layer(h, batch, seq, tables, ev_w_in[i], ev_q_norm[i], ev_w_q_b[i], ev_kv_norm[i],
                                ev_w_kv_b[i], ev_w_out[i], ev_ln1_g[i], ev_ln1_b[i], ev_ffn_w_gate[i],
                                ev_ffn_w_up[i], ev_ffn_w_down[i], ev_ln2_g[i], ev_ln2_b[i])
        else:
            h = _odd_layer(h, hb, batch, seq, od_w_qkv[i], od_w_f[i], od_b_f[i], od_w_out[i], od_ln1_g[i],
                           od_ln1_b[i], od_router_w[i], od_router_b[i], od_exp_w_gate[i], od_exp_w_up[i],
                           od_exp_w_down[i], od_ln2_g[i], od_ln2_b[i])
    return h.reshape(batch, seq, D_MODEL)
```

```python
import functools

import jax
import jax.numpy as jnp
from jax import lax
from jax.experimental import pallas as pl
from jax.experimental.pallas import tpu as pltpu

F32 = jnp.float32
BF16 = jnp.bfloat16

D_MODEL = 2048
HEAD_DIM = 128
LANES = 128
MXU_COLS = 256
ROPE_THETA = 10000.0
LN_EPS = 1e-5
RMS_EPS = 1e-6

MLA_HEADS = 10
MLA_Q_RANK = 448
MLA_Q_RANK_PAD = 512
MLA_KV_RANK = 128
MLA_NOPE_DIM = 128
MLA_ROPE_DIM = 64
MLA_V_DIM = 128
MLA_QK_DIM = MLA_NOPE_DIM + MLA_ROPE_DIM
MLA_QK_PAD = 256

DIL_PATTERNS = ((128, 1), (512, 4), (2048, 16))
DIL_GROUPS = 3
DIL_HEADS = 6
DIL_WIDTH = DIL_HEADS * HEAD_DIM
DIL_SPAN = 128

OFF_CQ = 0
OFF_CKV = OFF_CQ + MLA_Q_RANK
OFF_KROPE = OFF_CKV + MLA_KV_RANK
OFF_DQ = OFF_KROPE + MLA_ROPE_DIM
OFF_DK = OFF_DQ + DIL_GROUPS * DIL_WIDTH
OFF_DV = OFF_DK + DIL_WIDTH
W_IN_COLS = OFF_DV + DIL_WIDTH

FOX_HEADS = 16
FOX_WIDTH = FOX_HEADS * HEAD_DIM

D_FF = 5632
N_EXPERTS = 8
DEPTH = 2
ALPHA = (2.0 * DEPTH) ** 0.25

NEG = -1e30
LOG2E = 1.4426950408889634
FLASH_TQ = 1024
MOE_TILE = 512
MOE_DOWN_TILE = 512
DIL_TILE = 1024
DMA_UNROLL = 8
VMEM_LIMIT = 56 * 1024 * 1024


def _params(*sem, vmem=VMEM_LIMIT):
    return pltpu.CompilerParams(dimension_semantics=sem, vmem_limit_bytes=vmem)


def _dot(a, b):
    return jnp.dot(a, b, preferred_element_type=F32)


def _dot_nt(a, b):
    return lax.dot_general(a, b, (((1,), (1,)), ((), ())), preferred_element_type=F32)


def _layer_norm(y, g, b):
    mu = jnp.mean(y, axis=-1, keepdims=True)
    d = y - mu
    var = jnp.mean(d * d, axis=-1, keepdims=True)
    return d * lax.rsqrt(var + LN_EPS) * g + b


def _rope128(x, cos_f, sin_f):
    return x * cos_f + pltpu.roll(x, 64, 1) * sin_f


def _rope64(x, cos_t, sin_a, sin_b):
    return x * cos_t + pltpu.roll(x, 96, 1) * sin_a + pltpu.roll(x, 32, 1) * sin_b


def _mla_prep_kernel(x_ref, wa_ref, kvg_ref, qg_ref, wqt_ref, wk_ref, wvt_ref, cos_ref, sa_ref, sb_ref,
                     ct_ref, st_ref, qt_ref, k_ref, vt_ref, xb_ref):
    xb = x_ref[...].astype(BF16)
    xb_ref[...] = xb
    h = _dot(xb, wa_ref[...])
    ckv = h[:, 0:128]
    kr = h[:, 128:256]
    cq = h[:, 256:768]
    ckv_n = (ckv * lax.rsqrt(jnp.mean(ckv * ckv, axis=-1, keepdims=True) + RMS_EPS) * kvg_ref[...]).astype(BF16)
    cq_ms = jnp.sum(cq * cq, axis=-1, keepdims=True) * (1.0 / MLA_Q_RANK)
    cq_n = (cq * lax.rsqrt(cq_ms + RMS_EPS) * qg_ref[...]).astype(BF16)
    kr_r = _rope64(kr, cos_ref[...], sa_ref[...], sb_ref[...]).astype(BF16)
    scale = MLA_QK_DIM ** -0.5 * LOG2E
    qt = _dot_nt(wqt_ref[...], cq_n)
    c, s = ct_ref[...], st_ref[...]
    k_nope = _dot(ckv_n, wk_ref[...])
    for hd in range(MLA_HEADS):
        o = hd * MLA_QK_PAD
        qt_ref[o:o + 128, :] = (qt[o:o + 128] * scale).astype(BF16)
        x1, x2 = qt[o + 128:o + 160], qt[o + 160:o + 192]
        qt_ref[o + 128:o + 160, :] = ((x1 * c - x2 * s) * scale).astype(BF16)
        qt_ref[o + 160:o + 192, :] = ((x2 * c + x1 * s) * scale).astype(BF16)
        qt_ref[o + 192:o + 256, :] = jnp.zeros((64, qt.shape[1]), BF16)
        k_ref[:, o:o + 128] = k_nope[:, hd * 128:(hd + 1) * 128].astype(BF16)
        k_ref[:, o + 128:o + 256] = kr_r
    vt_ref[...] = _dot_nt(wvt_ref[...], ckv_n).astype(BF16)


def _mla_prep(x3, wa, kvg, qg, wqt, wk, wvt, cos_t, sin_a, sin_b, cos_tt, sin_tt, tm):
    b, s, _ = x3.shape
    full = lambda shape: pl.BlockSpec(shape, lambda bi, i: (0, 0))
    tab = pl.BlockSpec((tm, LANES), lambda bi, i: (i, 0))
    tab_t = pl.BlockSpec((MLA_ROPE_DIM // 2, tm), lambda bi, i: (0, i))
    wide = MLA_HEADS * MLA_QK_PAD
    vw = MLA_HEADS * MLA_V_DIM
    return pl.pallas_call(
        _mla_prep_kernel,
        grid=(b, s // tm),
        in_specs=[pl.BlockSpec((None, tm, D_MODEL), lambda bi, i: (bi, i, 0)), full(wa.shape), full(kvg.shape),
                  full(qg.shape), full(wqt.shape), full(wk.shape), full(wvt.shape), tab, tab, tab, tab_t, tab_t],
        out_specs=[pl.BlockSpec((None, wide, tm), lambda bi, i: (bi, 0, i)),
                   pl.BlockSpec((None, tm, wide), lambda bi, i: (bi, i, 0)),
                   pl.BlockSpec((None, vw, tm), lambda bi, i: (bi, 0, i)),
                   pl.BlockSpec((None, tm, D_MODEL), lambda bi, i: (bi, i, 0))],
        out_shape=[jax.ShapeDtypeStruct((b, wide, s), BF16), jax.ShapeDtypeStruct((b, s, wide), BF16),
                   jax.ShapeDtypeStruct((b, vw, s), BF16), jax.ShapeDtypeStruct((b, s, D_MODEL), BF16)],
        compiler_params=_params("parallel", "parallel"),
        name="mla_prep",
    )(x3, wa, kvg, qg, wqt, wk, wvt, cos_t, sin_a, sin_b, cos_tt, sin_tt)


def _dqkv_kernel(x_ref, w_ref, cos_ref, sin_ref, o_ref):
    j = pl.program_id(0)
    h = _dot(x_ref[...], w_ref[...])

    @pl.when(j < 4)
    def _():
        cos_f, sin_f = cos_ref[...], sin_ref[...]
        sc = jnp.where(j < 3, HEAD_DIM ** -0.5, 1.0).astype(F32)
        for hd in range(DIL_HEADS):
            sl = slice(hd * 128, (hd + 1) * 128)
            o_ref[:, sl] = (_rope128(h[:, sl], cos_f, sin_f) * sc).astype(BF16)

    @pl.when(j == 4)
    def _():
        o_ref[...] = h.astype(BF16)


def _dqkv(xb, wd, cos_f, sin_f, seq, tm):
    m = xb.shape[0]
    nrow = seq // tm
    n_col = wd.shape[1] // DIL_WIDTH
    tab = pl.BlockSpec((tm, LANES), lambda j, i: (i % nrow, 0))
    return pl.pallas_call(
        _dqkv_kernel,
        grid=(n_col, m // tm),
        in_specs=[pl.BlockSpec((tm, D_MODEL), lambda j, i: (i, 0)),
                  pl.BlockSpec((D_MODEL, DIL_WIDTH), lambda j, i: (0, j)), tab, tab],
        out_specs=pl.BlockSpec((tm, DIL_WIDTH), lambda j, i: (i, j)),
        out_shape=jax.ShapeDtypeStruct((m, wd.shape[1]), BF16),
        compiler_params=_params("parallel", "parallel"),
        name="dil_qkv",
    )(xb, wd, cos_f, sin_f)


def _flash_kernel(*refs, tq, use_c):
    if use_c:
        qt_ref, k_ref, vt_ref, c_ref, o_ref, acc_ref, s0_ref, s1_ref, kaug_ref = refs
    else:
        qt_ref, k_ref, vt_ref, o_ref, acc_ref, s0_ref, s1_ref = refs
    seq = k_ref.shape[0]
    nq = seq // tq
    if use_c:
        row = lax.broadcasted_iota(jnp.int32, (LANES, tq), 0)

        def build(j, _):
            start = pl.multiple_of(j * tq, tq)
            neg = -LOG2E * c_ref[:, pl.ds(start, tq)]
            hi = neg.astype(BF16).astype(F32)
            mid = (neg - hi).astype(BF16).astype(F32)
            lo = neg - hi - mid
            blk = jnp.where(row == 0, hi, jnp.where(row == 1, mid, jnp.where(row == 2, lo, 0.0)))
            kaug_ref[pl.ds(start, tq), :] = blk.T.astype(BF16)
            return 0

        lax.fori_loop(0, nq, build, 0)

    def scores(i, j, s_ref):
        q = qt_ref[:, pl.ds(pl.multiple_of(i * tq, tq), tq)]
        if use_c:
            ones = (lax.broadcasted_iota(jnp.int32, (LANES, tq), 0) < 3).astype(BF16)
            q = jnp.concatenate([q, ones], axis=0)
        start = pl.multiple_of(j * tq, tq)
        kt = k_ref[pl.ds(start, tq), :]
        if use_c:
            kt = jnp.concatenate([kt, kaug_ref[pl.ds(start, tq), :]], axis=1)
        s_ref[...] = _dot(kt, q)

    def update(j, s_ref, stats):
        m, l = stats
        s = s_ref[...]
        m_new = jnp.maximum(m, jnp.max(s, axis=0, keepdims=True))
        a = jnp.exp2(m - m_new)
        p = jnp.exp2(s - m_new)
        l = a * l + jnp.sum(p, axis=0, keepdims=True)
        start = pl.multiple_of(j * tq, tq)
        acc_ref[...] = a * acc_ref[...] + _dot(vt_ref[:, pl.ds(start, tq)], p.astype(BF16))
        return m_new, l

    half = tq // 2

    def scores_diag(i, s_ref):
        q = qt_ref[:, pl.ds(pl.multiple_of(i * tq, tq), tq)]
        if use_c:
            ones = (lax.broadcasted_iota(jnp.int32, (LANES, tq), 0) < 3).astype(BF16)
            q = jnp.concatenate([q, ones], axis=0)
        start = pl.multiple_of(i * tq, tq)
        kt = k_ref[pl.ds(start, tq), :]
        if use_c:
            kt = jnp.concatenate([kt, kaug_ref[pl.ds(start, tq), :]], axis=1)
        s_ref[0:half, 0:half] = _dot(kt[0:half], q[:, 0:half])
        s_ref[:, half:tq] = _dot(kt, q[:, half:tq])

    def update_diag(j, s_ref, stats):
        m, l = stats
        start = pl.multiple_of(j * tq, tq)
        parts = []
        for lo, nkeys in ((0, half), (half, tq)):
            cols = slice(lo, lo + half)
            key = lax.broadcasted_iota(jnp.int32, (nkeys, half), 0)
            qry = lax.broadcasted_iota(jnp.int32, (nkeys, half), 1) + lo
            s = jnp.where(key <= qry, s_ref[0:nkeys, cols], NEG)
            m_new = jnp.maximum(m[:, cols], jnp.max(s, axis=0, keepdims=True))
            a = jnp.exp2(m[:, cols] - m_new)
            p = jnp.exp2(s - m_new)
            parts.append((m_new, a * l[:, cols] + jnp.sum(p, axis=0, keepdims=True)))
            acc_ref[:, cols] = a * acc_ref[:, cols] + _dot(vt_ref[:, pl.ds(start, nkeys)], p.astype(BF16))
        return tuple(jnp.concatenate([parts[0][n], parts[1][n]], axis=1) for n in range(2))

    def query_tile(i, first, second):
        acc_ref[...] = jnp.zeros_like(acc_ref)

        def pair(jj, stats):
            scores(i, 2 * jj + 1, second)
            stats = update(2 * jj, first, stats)
            scores(i, 2 * jj + 2, first)
            return update(2 * jj + 1, second, stats)

        init = (jnp.full((1, tq), NEG, F32), jnp.zeros((1, tq), F32))
        stats = lax.fori_loop(0, i // 2, pair, init)
        nxt = jnp.minimum(i + 1, nq - 1)

        def odd_tail(stats):
            scores_diag(i, second)
            stats = update(i - 1, first, stats)
            scores(nxt, 0, first)
            return update_diag(i, second, stats)

        def even_tail(stats):
            scores(nxt, 0, second)
            return update_diag(i, first, stats)

        _, l = lax.cond(i % 2 == 1, odd_tail, even_tail, stats)
        o_ref[pl.ds(pl.multiple_of(i * tq, tq), tq), :] = (acc_ref[...] / l).T.astype(o_ref.dtype)
        return 0

    scores(0, 0, s0_ref)

    def query_loop(i, _):
        return lax.cond(((i + 1) // 2) % 2 == 0, lambda: query_tile(i, s0_ref, s1_ref),
                        lambda: query_tile(i, s1_ref, s0_ref))

    lax.fori_loop(0, nq, query_loop, 0)


def _flash(qt, k, vt, c, *, heads, dq, dk, dv, tq):
    b, s, _ = k.shape
    use_c = c is not None
    in_specs = [pl.BlockSpec((None, dq, s), lambda bi, h: (bi, h, 0)),
                pl.BlockSpec((None, s, dk), lambda bi, h: (bi, 0, h)),
                pl.BlockSpec((None, dv, s), lambda bi, h: (bi, h, 0))]
    args = [qt, k, vt]
    scratch = [pltpu.VMEM((dv, tq), F32), pltpu.VMEM((tq, tq), F32), pltpu.VMEM((tq, tq), F32)]
    if use_c:
        in_specs.append(pl.BlockSpec((None, None, 1, s), lambda bi, h: (bi, h, 0, 0)))
        args.append(c)
        scratch.append(pltpu.VMEM((s, LANES), BF16))
    return pl.pallas_call(
        functools.partial(_flash_kernel, tq=tq, use_c=use_c),
        grid=(b, heads),
        in_specs=in_specs,
        out_specs=pl.BlockSpec((None, s, dv), lambda bi, h: (bi, 0, h)),
        out_shape=jax.ShapeDtypeStruct((b, s, heads * dv), BF16),
        scratch_shapes=scratch,
        compiler_params=_params("parallel", "parallel"),
        name="flash_fox" if use_c else "flash_mla",
    )(*args)


def _dilated_kernel(q_ref, kc_ref, kp_ref, vc_ref, vp_ref, o_ref, lse_ref, qf, kf, vf, *, dil, tn):
    i = pl.program_id(1)
    per_class = tn // dil
    qf[...] = q_ref[...].astype(F32)
    kf[0:tn, :] = kp_ref[...].astype(F32)
    kf[tn:2 * tn, :] = kc_ref[...].astype(F32)
    vf[0:tn, :] = vp_ref[...].astype(F32)
    vf[tn:2 * tn, :] = vc_ref[...].astype(F32)
    row = lax.broadcasted_iota(jnp.int32, (DIL_SPAN, 2 * DIL_SPAN), 0)
    col = lax.broadcasted_iota(jnp.int32, (DIL_SPAN, 2 * DIL_SPAN), 1)
    back = row + DIL_SPAN - col
    in_band = jnp.where(back >= 0, jnp.where(back <= DIL_SPAN, 0.0, NEG), NEG)
    first_band = jnp.where(col >= jnp.where(i > 0, 0, DIL_SPAN), in_band, NEG)
    for r in range(dil):
        k_r = jnp.concatenate([kf[pl.ds(tn - DIL_SPAN * dil + r, DIL_SPAN, stride=dil), :],
                               kf[pl.ds(tn + r, per_class, stride=dil), :]], axis=0).astype(BF16)
        v_r = jnp.concatenate([vf[pl.ds(tn - DIL_SPAN * dil + r, DIL_SPAN, stride=dil), :],
                               vf[pl.ds(tn + r, per_class, stride=dil), :]], axis=0).astype(BF16)
        q_r = qf[pl.ds(r, per_class, stride=dil), :].astype(BF16)
        for a in range(per_class // DIL_SPAN):
            lo = a * DIL_SPAN
            s = _dot_nt(q_r[lo:lo + DIL_SPAN], k_r[lo:lo + 2 * DIL_SPAN]) + (first_band if a == 0 else in_band)
            m = jnp.max(s, axis=-1, keepdims=True)
            p = jnp.exp(s - m)
            l = jnp.sum(p, axis=-1, keepdims=True)
            rows = pl.ds(r + lo * dil, DIL_SPAN, stride=dil)
            o_ref[rows, :] = _dot(p.astype(BF16), v_r[lo:lo + 2 * DIL_SPAN]) / l
            lse_ref[rows, :] = jnp.broadcast_to(m + jnp.log(l), (DIL_SPAN, LANES))


def _dilated_group(dqkv3, g, dil, tn):
    batch, seq, _ = dqkv3.shape
    assert tn % (DIL_SPAN * dil) == 0
    blk = lambda col, prev: pl.BlockSpec(
        (None, tn, HEAD_DIM), lambda b, i, h: (b, jnp.maximum(i - 1, 0) if prev else i, col * DIL_HEADS + h))
    out = pl.BlockSpec((None, tn, HEAD_DIM), lambda b, i, h: (b, i, h))
    o, lse = pl.pallas_call(
        functools.partial(_dilated_kernel, dil=dil, tn=tn),
        grid=(batch, seq // tn, DIL_HEADS),
        in_specs=[blk(g, False), blk(3, False), blk(3, True), blk(4, False), blk(4, True)],
        out_specs=[out, out],
        out_shape=[jax.ShapeDtypeStruct((batch, seq, DIL_WIDTH), F32)] * 2,
        scratch_shapes=[pltpu.VMEM((tn, HEAD_DIM), F32), pltpu.VMEM((2 * tn, HEAD_DIM), F32),
                        pltpu.VMEM((2 * tn, HEAD_DIM), F32)],
        compiler_params=_params("parallel", "parallel", "parallel"),
        name=f"dilated_{dil}",
    )(dqkv3, dqkv3, dqkv3, dqkv3, dqkv3)
    return o.reshape(batch * seq, DIL_WIDTH), lse.reshape(batch * seq, DIL_WIDTH)


def _even_out_kernel(om_ref, o0, o1, o2, l0, l1, l2, wm_ref, wd_ref, x_ref, g_ref, b_ref, o_ref, ob_ref):
    ls = [l0[...], l1[...], l2[...]]
    mx = jnp.maximum(jnp.maximum(ls[0], ls[1]), ls[2])
    es = [jnp.exp(v - mx) for v in ls]
    den = es[0] + es[1] + es[2]
    o_dil = ((es[0] / den) * o0[...] + (es[1] / den) * o1[...] + (es[2] / den) * o2[...]).astype(BF16)
    y = _dot(om_ref[...], wm_ref[...]) + _dot(o_dil, wd_ref[...])
    out = _layer_norm(ALPHA * x_ref[...] + y, g_ref[...], b_ref[...])
    o_ref[...] = out
    ob_ref[...] = out.astype(BF16)


def _even_out(o_mla, outs, lses, w_mla, w_dil, x, g, b, tm):
    m = x.shape[0]
    row = lambda width: pl.BlockSpec((tm, width), lambda i: (i, 0))
    full = lambda shape: pl.BlockSpec(shape, lambda i: (0, 0))
    return pl.pallas_call(
        _even_out_kernel,
        grid=(m // tm,),
        in_specs=[row(o_mla.shape[1])] + [row(DIL_WIDTH)] * 6
        + [full(w_mla.shape), full(w_dil.shape), row(D_MODEL), full(g.shape), full(b.shape)],
        out_specs=[row(D_MODEL), row(D_MODEL)],
        out_shape=[jax.ShapeDtypeStruct((m, D_MODEL), F32), jax.ShapeDtypeStruct((m, D_MODEL), BF16)],
        compiler_params=_params("parallel"),
        name="even_out_ln",
    )(o_mla, *outs, *lses, w_mla, w_dil, x, g, b)


def _odd_out_kernel(a_ref, w_ref, x_ref, g_ref, b_ref, o_ref):
    o_ref[...] = _layer_norm(ALPHA * x_ref[...] + _dot(a_ref[...], w_ref[...]), g_ref[...], b_ref[...])


def _odd_out(a, w, x, g, b, tm):
    m = x.shape[0]
    row = lambda width: pl.BlockSpec((tm, width), lambda i: (i, 0))
    full = lambda shape: pl.BlockSpec(shape, lambda i: (0, 0))
    return pl.pallas_call(
        _odd_out_kernel,
        grid=(m // tm,),
        in_specs=[row(a.shape[1]), full(w.shape), row(D_MODEL), full(g.shape), full(b.shape)],
        out_specs=row(D_MODEL),
        out_shape=jax.ShapeDtypeStruct((m, D_MODEL), F32),
        compiler_params=_params("parallel"),
        name="odd_out_ln",
    )(a, w, x, g, b)


def _router_kernel(x_ref, rwh_ref, rwl_ref, rb_ref, route_ref, cnt_ref, carry_ref):
    @pl.when(pl.program_id(0) == 0)
    def _():
        carry_ref[...] = jnp.zeros_like(carry_ref)

    x = x_ref[...]
    xh = x.astype(BF16)
    xl = (x - xh.astype(F32)).astype(BF16)
    logits = _dot(xh, rwh_ref[...]) + (_dot(xh, rwl_ref[...]) + _dot(xl, rwh_ref[...])) + rb_ref[...]
    tm = logits.shape[0]
    lane = lax.broadcasted_iota(jnp.int32, (tm, LANES), 1)
    l1 = jnp.max(logits, axis=-1, keepdims=True)
    i1 = jnp.min(jnp.where(logits == l1, lane, LANES), axis=-1, keepdims=True)
    rest = jnp.where(lane == i1, NEG, logits)
    l2 = jnp.max(rest, axis=-1, keepdims=True)
    i2 = jnp.min(jnp.where(rest == l2, lane, LANES), axis=-1, keepdims=True)
    e = jnp.exp(l2 - l1)
    w1 = 1.0 / (1.0 + e)
    w2 = e / (1.0 + e)
    hot1 = (lane == i1).astype(F32)
    hot2 = (lane == i2).astype(F32)
    cnt = hot1 + hot2
    strict = (lax.broadcasted_iota(jnp.int32, (tm, tm), 1) < lax.broadcasted_iota(jnp.int32, (tm, tm), 0)).astype(BF16)
    before = _dot(strict, cnt.astype(BF16)) + carry_ref[...]
    r1 = jnp.sum(before * hot1, axis=-1, keepdims=True)
    r2 = jnp.sum(before * hot2, axis=-1, keepdims=True)
    vals = (i1.astype(F32), i2.astype(F32), w1, w2, r1, r2)
    route = jnp.zeros((tm, LANES), F32)
    for idx, val in enumerate(vals):
        route = jnp.where(lane == idx, val, route)
    route_ref[...] = route
    total = carry_ref[...] + jnp.sum(cnt, axis=0, keepdims=True)
    carry_ref[...] = total
    cnt_ref[...] = jnp.broadcast_to(total, cnt_ref.shape)


def _router(x, rw, rb, tm):
    m = x.shape[0]
    full = lambda shape: pl.BlockSpec(shape, lambda i: (0, 0))
    rw_hi = rw.astype(BF16)
    rw_lo = (rw - rw_hi.astype(F32)).astype(BF16)
    return pl.pallas_call(
        _router_kernel,
        grid=(m // tm,),
        in_specs=[pl.BlockSpec((tm, D_MODEL), lambda i: (i, 0)), full(rw.shape), full(rw.shape), full(rb.shape)],
        out_specs=[pl.BlockSpec((tm, LANES), lambda i: (i, 0)), pl.BlockSpec((8, LANES), lambda i: (0, 0))],
        out_shape=[jax.ShapeDtypeStruct((m, LANES), F32), jax.ShapeDtypeStruct((8, LANES), F32)],
        scratch_shapes=[pltpu.VMEM((1, LANES), F32)],
        compiler_params=_params("arbitrary"),
        name="moe_router",
    )(x, rw_hi, rw_lo, rb)


def _silu_mul(g, u):
    return g * (1.0 / (1.0 + jnp.exp(-g))) * u


def _ffn_up_kernel(x_ref, wg_ref, wu_ref, o_ref, wgb_ref, wub_ref):
    @pl.when(pl.program_id(1) == 0)
    def _():
        wgb_ref[...] = wg_ref[...].astype(BF16)
        wub_ref[...] = wu_ref[...].astype(BF16)

    x = x_ref[...]
    o_ref[...] = _silu_mul(_dot(x, wgb_ref[...]), _dot(x, wub_ref[...])).astype(BF16)


def _ffn_up(xb, wg, wu, tm, tn):
    m = xb.shape[0]
    return pl.pallas_call(
        _ffn_up_kernel,
        grid=(D_FF // tn, m // tm),
        in_specs=[pl.BlockSpec((tm, D_MODEL), lambda j, i: (i, 0)),
                  pl.BlockSpec((D_MODEL, tn), lambda j, i: (0, j)),
                  pl.BlockSpec((D_MODEL, tn), lambda j, i: (0, j))],
        out_specs=pl.BlockSpec((tm, tn), lambda j, i: (i, j)),
        out_shape=jax.ShapeDtypeStruct((m, D_FF), BF16),
        scratch_shapes=[pltpu.VMEM((D_MODEL, tn), BF16), pltpu.VMEM((D_MODEL, tn), BF16)],
        compiler_params=_params("arbitrary", "arbitrary"),
        name="ffn_up",
    )(xb, wg, wu)


def _ffn_down_ln_kernel(h_ref, w_ref, x_ref, g_ref, b_ref, o_ref, ob_ref, acc_ref):
    k = pl.program_id(1)

    @pl.when(k == 0)
    def _():
        acc_ref[...] = jnp.zeros_like(acc_ref)

    acc_ref[...] += _dot(h_ref[...], w_ref[...])

    @pl.when(k == pl.num_programs(1) - 1)
    def _():
        out = _layer_norm(ALPHA * x_ref[...] + acc_ref[...], g_ref[...], b_ref[...])
        o_ref[...] = out
        ob_ref[...] = out.astype(BF16)


def _ffn_down_ln(h, wd, x, g, b, tm, tk):
    m = x.shape[0]
    row = pl.BlockSpec((tm, D_MODEL), lambda i, k: (i, 0))
    res = row
    vec = pl.BlockSpec((1, D_MODEL), lambda i, k: (0, 0))
    return pl.pallas_call(
        _ffn_down_ln_kernel,
        grid=(m // tm, D_FF // tk),
        in_specs=[pl.BlockSpec((tm, tk), lambda i, k: (i, k)), pl.BlockSpec((tk, D_MODEL), lambda i, k: (k, 0)),
                  res, vec, vec],
        out_specs=[row, row],
        out_shape=[jax.ShapeDtypeStruct((m, D_MODEL), F32), jax.ShapeDtypeStruct((m, D_MODEL), BF16)],
        scratch_shapes=[pltpu.VMEM((tm, D_MODEL), F32)],
        compiler_params=_params("parallel", "arbitrary"),
        name="ffn_down_ln",
    )(h, wd, x, g, b)


def _proj_kernel(x_ref, w_ref, o_ref):
    o_ref[...] = _dot(x_ref[...], w_ref[...]).astype(BF16)


def _proj(xb, w, tm, tn):
    m = xb.shape[0]
    n = w.shape[1]
    return pl.pallas_call(
        _proj_kernel,
        grid=(n // tn, m // tm),
        in_specs=[pl.BlockSpec((tm, D_MODEL), lambda j, i: (i, 0)), pl.BlockSpec((D_MODEL, tn), lambda j, i: (0, j))],
        out_specs=pl.BlockSpec((tm, tn), lambda j, i: (i, j)),
        out_shape=jax.ShapeDtypeStruct((m, n), BF16),
        compiler_params=_params("parallel", "parallel"),
        name="proj",
    )(xb, w)


def _proj_t_kernel(w_ref, x_ref, o_ref, *, scale):
    o_ref[...] = (_dot_nt(w_ref[...], x_ref[...]) * scale).astype(BF16)


def _proj_t(xb3, wt, scale, tm, tn):
    b, s, _ = xb3.shape
    n = wt.shape[0]
    return pl.pallas_call(
        functools.partial(_proj_t_kernel, scale=scale),
        grid=(n // tn, b, s // tm),
        in_specs=[pl.BlockSpec((tn, D_MODEL), lambda j, bi, i: (j, 0)),
                  pl.BlockSpec((None, tm, D_MODEL), lambda j, bi, i: (bi, i, 0))],
        out_specs=pl.BlockSpec((None, tn, tm), lambda j, bi, i: (bi, j, i)),
        out_shape=jax.ShapeDtypeStruct((b, n, s), BF16),
        compiler_params=_params("parallel", "parallel", "parallel"),
        name="proj_t",
    )(wt, xb3)


def _fgate_kernel(x_ref, w_ref, b_ref, c_ref, carry_ref):
    @pl.when(pl.program_id(1) == 0)
    def _():
        carry_ref[...] = jnp.zeros_like(carry_ref)

    z = _dot(x_ref[...], w_ref[...]) + b_ref[...]
    log_f = jnp.minimum(z, 0.0) - jnp.log(1.0 + jnp.exp(-jnp.abs(z)))
    tm = z.shape[0]
    tri = (lax.broadcasted_iota(jnp.int32, (tm, tm), 1) <= lax.broadcasted_iota(jnp.int32, (tm, tm), 0)).astype(F32)
    c = jnp.dot(tri, log_f, preferred_element_type=F32, precision=lax.Precision.HIGHEST) + carry_ref[...]
    c_ref[...] = c
    carry_ref[...] = c[tm - 1:tm, :]


def _fgate(xb3, wf, bf, tm):
    b, s, _ = xb3.shape
    return pl.pallas_call(
        _fgate_kernel,
        grid=(b, s // tm),
        in_specs=[pl.BlockSpec((None, tm, D_MODEL), lambda bi, i: (bi, i, 0)),
                  pl.BlockSpec(wf.shape, lambda bi, i: (0, 0)), pl.BlockSpec(bf.shape, lambda bi, i: (0, 0))],
        out_specs=pl.BlockSpec((None, tm, LANES), lambda bi, i: (bi, i, 0)),
        out_shape=jax.ShapeDtypeStruct((b, s, LANES), F32),
        scratch_shapes=[pltpu.VMEM((1, LANES), F32)],
        compiler_params=_params("parallel", "arbitrary"),
        name="fox_gate",
    )(xb3, wf, bf)


def _dispatch_kernel(fill_ref, pos_ref, x_ref, xs_ref, zero_ref, sem, zero_sem):
    tm = x_ref.shape[0]

    @pl.when(pl.program_id(0) == 0)
    def _():
        zero_ref[...] = jnp.zeros_like(zero_ref)
        rows = zero_ref.shape[0]

        def fill(f):
            start = pl.multiple_of(jnp.maximum(fill_ref[f], 0), rows)
            return pltpu.make_async_copy(zero_ref, xs_ref.at[pl.ds(start, rows), :], zero_sem)

        for f in range(fill_ref.shape[0]):
            @pl.when(fill_ref[f] >= 0)
            def _():
                fill(f).start()

        for f in range(fill_ref.shape[0]):
            @pl.when(fill_ref[f] >= 0)
            def _():
                fill(f).wait()

    def copy(r, k):
        return pltpu.make_async_copy(x_ref.at[pl.ds(r, 1), :], xs_ref.at[pl.ds(pos_ref[0, 0, 2 * r + k], 1), :], sem)

    def start(r, _):
        copy(r, 0).start()
        copy(r, 1).start()
        return 0

    def wait(r, _):
        copy(r, 0).wait()
        copy(r, 1).wait()
        return 0

    lax.fori_loop(0, tm, start, 0, unroll=DMA_UNROLL)
    lax.fori_loop(0, tm, wait, 0, unroll=DMA_UNROLL)


def _dispatch(x, pos, fill_rows, n_rows, tile, tm):
    m, width = x.shape
    pos3 = pos.reshape(m // tm, 1, 2 * tm)
    return pl.pallas_call(
        _dispatch_kernel,
        grid_spec=pltpu.PrefetchScalarGridSpec(
            num_scalar_prefetch=1,
            grid=(m // tm,),
            in_specs=[pl.BlockSpec((1, 1, 2 * tm), lambda i, fill: (i, 0, 0), memory_space=pltpu.SMEM),
                      pl.BlockSpec((tm, width), lambda i, fill: (i, 0))],
            out_specs=pl.BlockSpec(memory_space=pl.ANY),
            scratch_shapes=[pltpu.VMEM((tile, width), x.dtype), pltpu.SemaphoreType.DMA(()),
                            pltpu.SemaphoreType.DMA(())],
        ),
        out_shape=jax.ShapeDtypeStruct((n_rows, width), x.dtype),
        compiler_params=_params("arbitrary"),
        name="moe_dispatch",
    )(fill_rows, pos3, x)


def _moe_up_kernel(te_ref, nu_ref, nx_ref, x_ref, wg_hbm, wu_hbm, o_ref, sg_ref, su_ref, wb_ref, sems, *, tn):
    groups = [(lo, min(MXU_COLS, tn - lo)) for lo in range(0, tn, MXU_COLS)]
    j = pl.program_id(0)
    t = pl.program_id(1)
    live = t < nu_ref[0]
    fresh = jnp.logical_or(t == 0, te_ref[t] != te_ref[jnp.maximum(t - 1, 0)])

    def weight_copies(e, jj):
        cols = pl.ds(pl.multiple_of(jj * tn, LANES), tn)
        return (pltpu.make_async_copy(wg_hbm.at[e, :, cols], sg_ref, sems.at[0]),
                pltpu.make_async_copy(wu_hbm.at[e, :, cols], su_ref, sems.at[1]))

    def start(e, jj):
        for c in weight_copies(e, jj):
            c.start()

    @pl.when(jnp.logical_and(j == 0, t == 0))
    def _():
        start(te_ref[0], 0)

    @pl.when(jnp.logical_and(live, fresh))
    def _():
        for c in weight_copies(te_ref[t], j):
            c.wait()
        def cast_rows(c, _):
            rows = pl.ds(pl.multiple_of(c * 256, 256), 256)
            for lo, width in groups:
                wb_ref[rows, 2 * lo:2 * lo + width] = sg_ref[rows, lo:lo + width].astype(BF16)
                wb_ref[rows, 2 * lo + width:2 * lo + 2 * width] = su_ref[rows, lo:lo + width].astype(BF16)
            return 0

        lax.fori_loop(0, D_MODEL // 256, cast_rows, 0)
        nxt = nx_ref[t]

        @pl.when(nxt >= 0)
        def _():
            start(nxt, j)

        @pl.when(jnp.logical_and(nxt < 0, j + 1 < pl.num_programs(0)))
        def _():
            start(te_ref[0], j + 1)

    @pl.when(live)
    def _():
        x = x_ref[...].astype(BF16)
        for lo, width in groups:
            gu = _dot(x, wb_ref[:, 2 * lo:2 * lo + 2 * width])
            o_ref[:, lo:lo + width] = _silu_mul(gu[:, :width], gu[:, width:]).astype(BF16)

    @pl.when(jnp.logical_not(live))
    def _():
        o_ref[...] = jnp.zeros_like(o_ref)


def _moe_up(xs, wg, wu, tile_expert, n_used, next_expert, tm, tn):
    p = xs.shape[0]
    live = lambda t, nu: jnp.minimum(t, nu[0] - 1)
    return pl.pallas_call(
        functools.partial(_moe_up_kernel, tn=tn),
        grid_spec=pltpu.PrefetchScalarGridSpec(
            num_scalar_prefetch=3,
            grid=(D_FF // tn, p // tm),
            in_specs=[pl.BlockSpec((tm, D_MODEL), lambda j, t, te, nu, nx: (live(t, nu), 0)),
                      pl.BlockSpec(memory_space=pl.ANY), pl.BlockSpec(memory_space=pl.ANY)],
            out_specs=pl.BlockSpec((tm, tn), lambda j, t, te, nu, nx: (t, j)),
            scratch_shapes=[pltpu.VMEM((D_MODEL, tn), F32), pltpu.VMEM((D_MODEL, tn), F32),
                            pltpu.VMEM((D_MODEL, 2 * tn), BF16), pltpu.SemaphoreType.DMA((2,))],
        ),
        out_shape=jax.ShapeDtypeStruct((p, D_FF), BF16),
        compiler_params=_params("arbitrary", "arbitrary"),
        name="moe_up",
    )(tile_expert, n_used, next_expert, xs, wg, wu)


def _moe_down_kernel(te_ref, nu_ref, h_ref, w_ref, o_ref, wb_ref):
    t = pl.program_id(1)
    live = t < nu_ref[0]
    fresh = jnp.logical_or(t == 0, te_ref[t] != te_ref[jnp.maximum(t - 1, 0)])

    @pl.when(jnp.logical_and(live, fresh))
    def _():
        wb_ref[...] = w_ref[...].astype(BF16)

    @pl.when(live)
    def _():
        o_ref[...] = _dot(h_ref[...], wb_ref[...])

    @pl.when(jnp.logical_not(live))
    def _():
        o_ref[...] = jnp.zeros_like(o_ref)


def _moe_down(hs, wd, tile_expert, n_used, tm, tn):
    p = hs.shape[0]
    live = lambda t, nu: jnp.minimum(t, nu[0] - 1)
    return pl.pallas_call(
        _moe_down_kernel,
        grid_spec=pltpu.PrefetchScalarGridSpec(
            num_scalar_prefetch=2,
            grid=(D_MODEL // tn, p // tm),
            in_specs=[pl.BlockSpec((tm, D_FF), lambda j, t, te, nu: (live(t, nu), 0)),
                      pl.BlockSpec((None, D_FF, tn), lambda j, t, te, nu: (te[live(t, nu)], 0, j))],
            out_specs=pl.BlockSpec((tm, tn), lambda j, t, te, nu: (t, j)),
            scratch_shapes=[pltpu.VMEM((D_FF, tn), BF16)],
        ),
        out_shape=jax.ShapeDtypeStruct((p, D_MODEL), F32),
        compiler_params=_params("arbitrary", "arbitrary"),
        name="moe_down",
    )(tile_expert, n_used, hs, wd)


def _combine_ln_kernel(pos_ref, ys_ref, x_ref, route_ref, g_ref, b_ref, o_ref, buf0, buf1, sem):
    tm = x_ref.shape[0]

    def copies(r):
        c0 = pltpu.make_async_copy(ys_ref.at[pl.ds(pos_ref[0, 0, 2 * r], 1), :], buf0.at[pl.ds(r, 1), :], sem)
        c1 = pltpu.make_async_copy(ys_ref.at[pl.ds(pos_ref[0, 0, 2 * r + 1], 1), :], buf1.at[pl.ds(r, 1), :], sem)
        return c0, c1

    def start(r, _):
        c0, c1 = copies(r)
        c0.start()
        c1.start()
        return 0

    def wait(r, _):
        c0, c1 = copies(r)
        c0.wait()
        c1.wait()
        return 0

    lax.fori_loop(0, tm, start, 0, unroll=DMA_UNROLL)
    lax.fori_loop(0, tm, wait, 0, unroll=DMA_UNROLL)
    route = route_ref[...]
    y = route[:, 2:3] * buf0[...] + route[:, 3:4] * buf1[...]
    o_ref[...] = _layer_norm(ALPHA * x_ref[...] + y, g_ref[...], b_ref[...])


def _combine_ln(ys, pos, x, route, g, b, tm):
    m = x.shape[0]
    pos3 = pos.reshape(m // tm, 1, 2 * tm)
    vec = pl.BlockSpec((1, D_MODEL), lambda i: (0, 0))
    return pl.pallas_call(
        _combine_ln_kernel,
        grid=(m // tm,),
        in_specs=[pl.BlockSpec((1, 1, 2 * tm), lambda i: (i, 0, 0), memory_space=pltpu.SMEM),
                  pl.BlockSpec(memory_space=pl.ANY),
                  pl.BlockSpec((tm, D_MODEL), lambda i: (i, 0)),
                  pl.BlockSpec((tm, LANES), lambda i: (i, 0)), vec, vec],
        out_specs=pl.BlockSpec((tm, D_MODEL), lambda i: (i, 0)),
        out_shape=jax.ShapeDtypeStruct((m, D_MODEL), F32),
        scratch_shapes=[pltpu.VMEM((tm, D_MODEL), F32), pltpu.VMEM((tm, D_MODEL), F32), pltpu.SemaphoreType.DMA(())],
        compiler_params=_params("arbitrary"),
        name="moe_combine_ln",
    )(pos3, ys, x, route, g, b)


def _rope_tables(seq):
    def angles(dim):
        inv_freq = 1.0 / (ROPE_THETA ** (jnp.arange(0, dim, 2, dtype=F32) / dim))
        ang = jnp.arange(seq, dtype=F32)[:, None] * inv_freq[None, :]
        return jnp.cos(ang), jnp.sin(ang)

    c128, s128 = angles(HEAD_DIM)
    cos_f = jnp.concatenate([c128, c128], axis=-1)
    sin_f = jnp.concatenate([-s128, s128], axis=-1)
    c64, s64 = angles(MLA_ROPE_DIM)
    z32 = jnp.zeros_like(c64)
    cos_t = jnp.concatenate([c64, c64, z32, z32], axis=-1)
    sin_a = jnp.concatenate([-s64, z32, z32, z32], axis=-1)
    sin_b = jnp.concatenate([z32, s64, z32, z32], axis=-1)
    return cos_f, sin_f, cos_t, sin_a, sin_b, c64.T, s64.T


def _pad_cols(a, width):
    return jnp.pad(a, ((0, 0), (0, width - a.shape[1])))


def _row(v):
    return v.reshape(1, -1).astype(F32)


def _even_layer(x, batch, seq, tables, w_in, q_norm, w_q_b, kv_norm, w_kv_b, w_out, ln1_g, ln1_b,
                w_gate, w_up, w_down, ln2_g, ln2_b):
    cos_f, sin_f, cos_t, sin_a, sin_b, cos_tt, sin_tt = tables
    tm = min(512, seq)
    wa = jnp.concatenate([w_in[:, OFF_CKV:OFF_KROPE], _pad_cols(w_in[:, OFF_KROPE:OFF_DQ], LANES),
                          _pad_cols(w_in[:, OFF_CQ:OFF_CKV], MLA_Q_RANK_PAD)], axis=1).astype(BF16)
    qg = _pad_cols(_row(q_norm), MLA_Q_RANK_PAD)
    wq = jnp.pad(w_q_b.reshape(MLA_Q_RANK, MLA_HEADS, MLA_QK_DIM),
                 ((0, MLA_Q_RANK_PAD - MLA_Q_RANK), (0, 0), (0, MLA_QK_PAD - MLA_QK_DIM)))
    wqt = wq.reshape(MLA_Q_RANK_PAD, MLA_HEADS * MLA_QK_PAD).T.astype(BF16)
    wkv3 = w_kv_b.reshape(MLA_KV_RANK, MLA_HEADS, MLA_NOPE_DIM + MLA_V_DIM)
    wk = wkv3[:, :, :MLA_NOPE_DIM].reshape(MLA_KV_RANK, -1).astype(BF16)
    wvt = wkv3[:, :, MLA_NOPE_DIM:].reshape(MLA_KV_RANK, -1).T.astype(BF16)
    qt_mla, k_mla, vt_mla, xb3 = _mla_prep(x.reshape(batch, seq, D_MODEL), wa, _row(kv_norm), qg, wqt, wk, wvt,
                                           cos_t, sin_a, sin_b, cos_tt, sin_tt, tm)
    xb = xb3.reshape(batch * seq, D_MODEL)
    tq = min(FLASH_TQ, seq)
    o_mla = _flash(qt_mla, k_mla, vt_mla, None, heads=MLA_HEADS, dq=MLA_QK_PAD, dk=MLA_QK_PAD, dv=MLA_V_DIM,
                   tq=tq)
    o_mla = o_mla.reshape(batch * seq, -1)

    dqkv = _dqkv(xb, w_in[:, OFF_DQ:].astype(BF16), cos_f, sin_f, seq, min(1024, seq))
    dqkv3 = dqkv.reshape(batch, seq, -1)
    outs, lses = [], []
    for g, (window, dil) in enumerate(DIL_PATTERNS):
        assert window == DIL_SPAN * dil
        o_g, lse_g = _dilated_group(dqkv3, g, dil, min(seq, max(DIL_TILE, DIL_SPAN * dil)))
        outs.append(o_g)
        lses.append(lse_g)
    n_mla = MLA_HEADS * MLA_V_DIM
    wo = w_out.astype(BF16)
    x1, x1b = _even_out(o_mla, outs, lses, wo[:n_mla], wo[n_mla:], x, _row(ln1_g), _row(ln1_b), tm)
    hmid = _ffn_up(x1b, w_gate, w_up, min(1024, seq), 512)
    return _ffn_down_ln(hmid, w_down.astype(BF16), x1, _row(ln2_g), _row(ln2_b), tm, 1408)


def _odd_layer(x, xb, batch, seq, w_qkv, w_f, b_f, w_out, ln1_g, ln1_b, router_w, router_b,
               exp_w_gate, exp_w_up, exp_w_down, ln2_g, ln2_b):
    m = batch * seq
    tm = min(512, seq)
    xb3 = xb.reshape(batch, seq, D_MODEL)
    wb = w_qkv.astype(BF16)
    qt = _proj_t(xb3, wb[:, :FOX_WIDTH].T, HEAD_DIM ** -0.5 * LOG2E, tm, 1024)
    k = _proj(xb, wb[:, FOX_WIDTH:2 * FOX_WIDTH], tm, 1024).reshape(batch, seq, FOX_WIDTH)
    vt = _proj_t(xb3, wb[:, 2 * FOX_WIDTH:].T, 1.0, tm, 1024)
    c = _fgate(xb3, _pad_cols(w_f, LANES).astype(BF16), _pad_cols(_row(b_f), LANES), min(256, seq))
    c_t = jnp.transpose(c[:, :, :FOX_HEADS], (0, 2, 1)).reshape(batch, FOX_HEADS, 1, seq)
    tq = min(FLASH_TQ, seq)
    o = _flash(qt, k, vt, c_t, heads=FOX_HEADS, dq=HEAD_DIM, dk=HEAD_DIM, dv=HEAD_DIM, tq=tq)
    rb = jnp.full((1, LANES), NEG, F32).at[0, :N_EXPERTS].set(router_b.astype(F32))
    x1 = _odd_out(o.reshape(m, -1), w_out.astype(BF16), x, _row(ln1_g), _row(ln1_b), tm)
    route, counts = _router(x1, _pad_cols(router_w, LANES), rb, min(256, seq))
    tile = MOE_TILE
    n_tiles = (2 * m) // tile + N_EXPERTS
    cnt = counts[0, :N_EXPERTS].astype(jnp.int32)
    tiles_per = (cnt + tile - 1) // tile
    tile_end = jnp.cumsum(tiles_per)
    offset = (tile_end - tiles_per) * tile
    idx = route[:, 0:2].astype(jnp.int32)
    pos = (offset[idx] + route[:, 4:6].astype(jnp.int32)).reshape(-1)
    n_used = tile_end[-1:]
    tile_ids = jnp.arange(n_tiles, dtype=jnp.int32)
    tile_expert = jnp.minimum(jnp.sum((tile_end[None, :] <= tile_ids[:, None]).astype(jnp.int32), axis=1),
                              N_EXPERTS - 1)
    td = min(256, seq)
    experts = jnp.arange(N_EXPERTS, dtype=jnp.int32)
    last_tile = jnp.where(tiles_per > 0, tile_end - 1, -1)
    unused = n_used[0] + experts
    unused = jnp.where(unused < n_tiles, unused, -1)
    fill_tiles = jnp.concatenate([last_tile, unused])
    fill_rows = jnp.where(fill_tiles >= 0, fill_tiles * tile, -1).astype(jnp.int32)
    xs = _dispatch(x1, pos, fill_rows, n_tiles * tile, tile, td)
    later = jnp.where((experts[None, :] > experts[:, None]) & (tiles_per[None, :] > 0), experts[None, :], N_EXPERTS)
    next_nonempty = jnp.min(later, axis=1)
    next_expert = jnp.where(next_nonempty < N_EXPERTS, next_nonempty, -1)[tile_expert].astype(jnp.int32)
    hs = _moe_up(xs, exp_w_gate, exp_w_up, tile_expert, n_used, next_expert, tile, 1408)
    split = tile // MOE_DOWN_TILE
    ys = _moe_down(hs, exp_w_down, jnp.repeat(tile_expert, split), n_used * split, MOE_DOWN_TILE, 512)
    return _combine_ln(ys, pos, x1, route, _row(ln2_g), _row(ln2_b), td)


def kernel(x, ev_w_in, ev_q_norm, ev_w_q_b, ev_kv_norm, ev_w_kv_b, ev_w_out, ev_ln1_g, ev_ln1_b, ev_ffn_w_gate, ev_ffn_w_up, ev_ffn_w_down, ev_ln2_g, ev_ln2_b, od_w_qkv, od_w_f, od_b_f, od_w_out, od_ln1_g, od_ln1_b, od_router_w, od_router_b, od_exp_w_gate, od_exp_w_up, od_exp_w_down, od_ln2_g, od_ln2_b):
    batch, seq, _ = x.shape
    tables = _rope_tables(seq)
    h = x.reshape(batch * seq, D_MODEL)
    hb = None
    for layer in range(DEPTH):
        i = layer // 2
        if layer % 2 == 0:
            h, hb = _even_layer(h, batch, seq, tables, ev_w_in[i], ev_q_norm[i], ev_w_q_b[i], ev_kv_norm[i],
                                ev_w_kv_b[i], ev_w_out[i], ev_ln1_g[i], ev_ln1_b[i], ev_ffn_w_gate[i],
                                ev_ffn_w_up[i], ev_ffn_w_down[i], ev_ln2_g[i], ev_ln2_b[i])
        else:
            h = _odd_layer(h, hb, batch, seq, od_w_qkv[i], od_w_f[i], od_b_f[i], od_w_out[i], od_ln1_g[i],
                           od_ln1_b[i], od_router_w[i], od_router_b[i], od_exp_w_gate[i], od_exp_w_up[i],
                           od_exp_w_down[i], od_ln2_g[i], od_ln2_b[i])
    return h.reshape(batch, seq, D_MODEL)
```

```python
import functools

import jax
import jax.numpy as jnp
from jax import lax
from jax.experimental import pallas as pl
from jax.experimental.pallas import tpu as pltpu

F32 = jnp.float32
BF16 = jnp.bfloat16

D_MODEL = 2048
HEAD_DIM = 128
LANES = 128
MXU_COLS = 256
ROPE_THETA = 10000.0
LN_EPS = 1e-5
RMS_EPS = 1e-6

MLA_HEADS = 10
MLA_Q_RANK = 448
MLA_Q_RANK_PAD = 512
MLA_KV_RANK = 128
MLA_NOPE_DIM = 128
MLA_ROPE_DIM = 64
MLA_V_DIM = 128
MLA_QK_DIM = MLA_NOPE_DIM + MLA_ROPE_DIM
MLA_QK_PAD = 256

DIL_PATTERNS = ((128, 1), (512, 4), (2048, 16))
DIL_GROUPS = 3
DIL_HEADS = 6
DIL_WIDTH = DIL_HEADS * HEAD_DIM
DIL_SPAN = 128

OFF_CQ = 0
OFF_CKV = OFF_CQ + MLA_Q_RANK
OFF_KROPE = OFF_CKV + MLA_KV_RANK
OFF_DQ = OFF_KROPE + MLA_ROPE_DIM
OFF_DK = OFF_DQ + DIL_GROUPS * DIL_WIDTH
OFF_DV = OFF_DK + DIL_WIDTH
W_IN_COLS = OFF_DV + DIL_WIDTH

FOX_HEADS = 16
FOX_WIDTH = FOX_HEADS * HEAD_DIM

D_FF = 5632
N_EXPERTS = 8
DEPTH = 2
ALPHA = (2.0 * DEPTH) ** 0.25

NEG = -1e30
LOG2E = 1.4426950408889634
FLASH_TQ = 1024
MOE_TILE = 512
MOE_DOWN_TILE = 512
DIL_TILE = 1024
DMA_UNROLL = 8
VMEM_LIMIT = 56 * 1024 * 1024


def _params(*sem, vmem=VMEM_LIMIT):
    return pltpu.CompilerParams(dimension_semantics=sem, vmem_limit_bytes=vmem)


def _dot(a, b):
    return jnp.dot(a, b, preferred_element_type=F32)


def _dot_nt(a, b):
    return lax.dot_general(a, b, (((1,), (1,)), ((), ())), preferred_element_type=F32)


def _layer_norm(y, g, b):
    mu = jnp.mean(y, axis=-1, keepdims=True)
    d = y - mu
    var = jnp.mean(d * d, axis=-1, keepdims=True)
    return d * lax.rsqrt(var + LN_EPS) * g + b


def _rope128(x, cos_f, sin_f):
    return x * cos_f + pltpu.roll(x, 64, 1) * sin_f


def _rope64(x, cos_t, sin_a, sin_b):
    return x * cos_t + pltpu.roll(x, 96, 1) * sin_a + pltpu.roll(x, 32, 1) * sin_b


def _mla_prep_kernel(x_ref, wa_ref, kvg_ref, qg_ref, wqt_ref, wk_ref, wvt_ref, cos_ref, sa_ref, sb_ref,
                     ct_ref, st_ref, qt_ref, k_ref, vt_ref, xb_ref):
    xb = x_ref[...].astype(BF16)
    xb_ref[...] = xb
    h = _dot(xb, wa_ref[...])
    ckv = h[:, 0:128]
    kr = h[:, 128:256]
    cq = h[:, 256:768]
    ckv_n = (ckv * lax.rsqrt(jnp.mean(ckv * ckv, axis=-1, keepdims=True) + RMS_EPS) * kvg_ref[...]).astype(BF16)
    cq_ms = jnp.sum(cq * cq, axis=-1, keepdims=True) * (1.0 / MLA_Q_RANK)
    cq_n = (cq * lax.rsqrt(cq_ms + RMS_EPS) * qg_ref[...]).astype(BF16)
    kr_r = _rope64(kr, cos_ref[...], sa_ref[...], sb_ref[...]).astype(BF16)
    scale = MLA_QK_DIM ** -0.5 * LOG2E
    qt = _dot_nt(wqt_ref[...], cq_n)
    c, s = ct_ref[...], st_ref[...]
    k_nope = _dot(ckv_n, wk_ref[...])
    for hd in range(MLA_HEADS):
        o = hd * MLA_QK_PAD
        qt_ref[o:o + 128, :] = (qt[o:o + 128] * scale).astype(BF16)
        x1, x2 = qt[o + 128:o + 160], qt[o + 160:o + 192]
        qt_ref[o + 128:o + 160, :] = ((x1 * c - x2 * s) * scale).astype(BF16)
        qt_ref[o + 160:o + 192, :] = ((x2 * c + x1 * s) * scale).astype(BF16)
        qt_ref[o + 192:o + 256, :] = jnp.zeros((64, qt.shape[1]), BF16)
        k_ref[:, o:o + 128] = k_nope[:, hd * 128:(hd + 1) * 128].astype(BF16)
        k_ref[:, o + 128:o + 256] = kr_r
    vt_ref[...] = _dot_nt(wvt_ref[...], ckv_n).astype(BF16)


def _mla_prep(x3, wa, kvg, qg, wqt, wk, wvt, cos_t, sin_a, sin_b, cos_tt, sin_tt, tm):
    b, s, _ = x3.shape
    full = lambda shape: pl.BlockSpec(shape, lambda bi, i: (0, 0))
    tab = pl.BlockSpec((tm, LANES), lambda bi, i: (i, 0))
    tab_t = pl.BlockSpec((MLA_ROPE_DIM // 2, tm), lambda bi, i: (0, i))
    wide = MLA_HEADS * MLA_QK_PAD
    vw = MLA_HEADS * MLA_V_DIM
    return pl.pallas_call(
        _mla_prep_kernel,
        grid=(b, s // tm),
        in_specs=[pl.BlockSpec((None, tm, D_MODEL), lambda bi, i: (bi, i, 0)), full(wa.shape), full(kvg.shape),
                  full(qg.shape), full(wqt.shape), full(wk.shape), full(wvt.shape), tab, tab, tab, tab_t, tab_t],
        out_specs=[pl.BlockSpec((None, wide, tm), lambda bi, i: (bi, 0, i)),
                   pl.BlockSpec((None, tm, wide), lambda bi, i: (bi, i, 0)),
                   pl.BlockSpec((None, vw, tm), lambda bi, i: (bi, 0, i)),
                   pl.BlockSpec((None, tm, D_MODEL), lambda bi, i: (bi, i, 0))],
        out_shape=[jax.ShapeDtypeStruct((b, wide, s), BF16), jax.ShapeDtypeStruct((b, s, wide), BF16),
                   jax.ShapeDtypeStruct((b, vw, s), BF16), jax.ShapeDtypeStruct((b, s, D_MODEL), BF16)],
        compiler_params=_params("parallel", "parallel"),
        name="mla_prep",
    )(x3, wa, kvg, qg, wqt, wk, wvt, cos_t, sin_a, sin_b, cos_tt, sin_tt)


def _dqkv_kernel(x_ref, w_ref, cos_ref, sin_ref, o_ref):
    j = pl.program_id(0)
    h = _dot(x_ref[...], w_ref[...])

    @pl.when(j < 4)
    def _():
        cos_f, sin_f = cos_ref[...], sin_ref[...]
        sc = jnp.where(j < 3, HEAD_DIM ** -0.5, 1.0).astype(F32)
        for hd in range(DIL_HEADS):
            sl = slice(hd * 128, (hd + 1) * 128)
            o_ref[:, sl] = (_rope128(h[:, sl], cos_f, sin_f) * sc).astype(BF16)

    @pl.when(j == 4)
    def _():
        o_ref[...] = h.astype(BF16)


def _dqkv(xb, wd, cos_f, sin_f, seq, tm):
    m = xb.shape[0]
    nrow = seq // tm
    n_col = wd.shape[1] // DIL_WIDTH
    tab = pl.BlockSpec((tm, LANES), lambda j, i: (i % nrow, 0))
    return pl.pallas_call(
        _dqkv_kernel,
        grid=(n_col, m // tm),
        in_specs=[pl.BlockSpec((tm, D_MODEL), lambda j, i: (i, 0)),
                  pl.BlockSpec((D_MODEL, DIL_WIDTH), lambda j, i: (0, j)), tab, tab],
        out_specs=pl.BlockSpec((tm, DIL_WIDTH), lambda j, i: (i, j)),
        out_shape=jax.ShapeDtypeStruct((m, wd.shape[1]), BF16),
        compiler_params=_params("parallel", "parallel"),
        name="dil_qkv",
    )(xb, wd, cos_f, sin_f)


def _flash_kernel(*refs, tq, use_c):
    if use_c:
        qt_ref, k_ref, vt_ref, c_ref, o_ref, acc_ref, s0_ref, s1_ref, kaug_ref = refs
    else:
        qt_ref, k_ref, vt_ref, o_ref, acc_ref, s0_ref, s1_ref = refs
    seq = k_ref.shape[0]
    nq = seq // tq
    if use_c:
        row = lax.broadcasted_iota(jnp.int32, (LANES, tq), 0)

        def build(j, _):
            start = pl.multiple_of(j * tq, tq)
            neg = -LOG2E * c_ref[:, pl.ds(start, tq)]
            hi = neg.astype(BF16).astype(F32)
            mid = (neg - hi).astype(BF16).astype(F32)
            lo = neg - hi - mid
            blk = jnp.where(row == 0, hi, jnp.where(row == 1, mid, jnp.where(row == 2, lo, 0.0)))
            kaug_ref[pl.ds(start, tq), :] = blk.T.astype(BF16)
            return 0

        lax.fori_loop(0, nq, build, 0)

    def scores(i, j, s_ref):
        q = qt_ref[:, pl.ds(pl.multiple_of(i * tq, tq), tq)]
        if use_c:
            ones = (lax.broadcasted_iota(jnp.int32, (LANES, tq), 0) < 3).astype(BF16)
            q = jnp.concatenate([q, ones], axis=0)
        start = pl.multiple_of(j * tq, tq)
        kt = k_ref[pl.ds(start, tq), :]
        if use_c:
            kt = jnp.concatenate([kt, kaug_ref[pl.ds(start, tq), :]], axis=1)
        s_ref[...] = _dot(kt, q)

    def update(j, s_ref, stats):
        m, l = stats
        s = s_ref[...]
        m_new = jnp.maximum(m, jnp.max(s, axis=0, keepdims=True))
        a = jnp.exp2(m - m_new)
        p = jnp.exp2(s - m_new)
        l = a * l + jnp.sum(p, axis=0, keepdims=True)
        start = pl.multiple_of(j * tq, tq)
        acc_ref[...] = a * acc_ref[...] + _dot(vt_ref[:, pl.ds(start, tq)], p.astype(BF16))
        return m_new, l

    half = tq // 2

    def scores_diag(i, s_ref):
        q = qt_ref[:, pl.ds(pl.multiple_of(i * tq, tq), tq)]
        if use_c:
            ones = (lax.broadcasted_iota(jnp.int32, (LANES, tq), 0) < 3).astype(BF16)
            q = jnp.concatenate([q, ones], axis=0)
        start = pl.multiple_of(i * tq, tq)
        kt = k_ref[pl.ds(start, tq), :]
        if use_c:
            kt = jnp.concatenate([kt, kaug_ref[pl.ds(start, tq), :]], axis=1)
        s_ref[0:half, 0:half] = _dot(kt[0:half], q[:, 0:half])
        s_ref[:, half:tq] = _dot(kt, q[:, half:tq])

    def update_diag(j, s_ref, stats):
        m, l = stats
        start = pl.multiple_of(j * tq, tq)
        parts = []
        for lo, nkeys in ((0, half), (half, tq)):
            cols = slice(lo, lo + half)
            key = lax.broadcasted_iota(jnp.int32, (nkeys, half), 0)
            qry = lax.broadcasted_iota(jnp.int32, (nkeys, half), 1) + lo
            s = jnp.where(key <= qry, s_ref[0:nkeys, cols], NEG)
            m_new = jnp.maximum(m[:, cols], jnp.max(s, axis=0, keepdims=True))
            a = jnp.exp2(m[:, cols] - m_new)
            p = jnp.exp2(s - m_new)
            parts.append((m_new, a * l[:, cols] + jnp.sum(p, axis=0, keepdims=True)))
            acc_ref[:, cols] = a * acc_ref[:, cols] + _dot(vt_ref[:, pl.ds(start, nkeys)], p.astype(BF16))
        return tuple(jnp.concatenate([parts[0][n], parts[1][n]], axis=1) for n in range(2))

    def query_tile(i, first, second):
        acc_ref[...] = jnp.zeros_like(acc_ref)

        def pair(jj, stats):
            scores(i, 2 * jj + 1, second)
            stats = update(2 * jj, first, stats)
            scores(i, 2 * jj + 2, first)
            return update(2 * jj + 1, second, stats)

        init = (jnp.full((1, tq), NEG, F32), jnp.zeros((1, tq), F32))
        stats = lax.fori_loop(0, i // 2, pair, init)
        nxt = jnp.minimum(i + 1, nq - 1)

        def odd_tail(stats):
            scores_diag(i, second)
            stats = update(i - 1, first, stats)
            scores(nxt, 0, first)
            return update_diag(i, second, stats)

        def even_tail(stats):
            scores(nxt, 0, second)
            return update_diag(i, first, stats)

        _, l = lax.cond(i % 2 == 1, odd_tail, even_tail, stats)
        o_ref[pl.ds(pl.multiple_of(i * tq, tq), tq), :] = (acc_ref[...] / l).T.astype(o_ref.dtype)
        return 0

    scores(0, 0, s0_ref)

    def query_loop(i, _):
        return lax.cond(((i + 1) // 2) % 2 == 0, lambda: query_tile(i, s0_ref, s1_ref),
                        lambda: query_tile(i, s1_ref, s0_ref))

    lax.fori_loop(0, nq, query_loop, 0)


def _flash(qt, k, vt, c, *, heads, dq, dk, dv, tq):
    b, s, _ = k.shape
    use_c = c is not None
    in_specs = [pl.BlockSpec((None, dq, s), lambda bi, h: (bi, h, 0)),
                pl.BlockSpec((None, s, dk), lambda bi, h: (bi, 0, h)),
                pl.BlockSpec((None, dv, s), lambda bi, h: (bi, h, 0))]
    args = [qt, k, vt]
    scratch = [pltpu.VMEM((dv, tq), F32), pltpu.VMEM((tq, tq), F32), pltpu.VMEM((tq, tq), F32)]
    if use_c:
        in_specs.append(pl.BlockSpec((None, None, 1, s), lambda bi, h: (bi, h, 0, 0)))
        args.append(c)
        scratch.append(pltpu.VMEM((s, LANES), BF16))
    return pl.pallas_call(
        functools.partial(_flash_kernel, tq=tq, use_c=use_c),
        grid=(b, heads),
        in_specs=in_specs,
        out_specs=pl.BlockSpec((None, s, dv), lambda bi, h: (bi, 0, h)),
        out_shape=jax.ShapeDtypeStruct((b, s, heads * dv), BF16),
        scratch_shapes=scratch,
        compiler_params=_params("parallel", "parallel"),
        name="flash_fox" if use_c else "flash_mla",
    )(*args)


def _dilated_kernel(q_ref, kc_ref, kp_ref, vc_ref, vp_ref, o_ref, lse_ref, qf, kf, vf, *, dil, tn):
    i = pl.program_id(1)
    per_class = tn // dil
    qf[...] = q_ref[...].astype(F32)
    kf[0:tn, :] = kp_ref[...].astype(F32)
    kf[tn:2 * tn, :] = kc_ref[...].astype(F32)
    vf[0:tn, :] = vp_ref[...].astype(F32)
    vf[tn:2 * tn, :] = vc_ref[...].astype(F32)
    row = lax.broadcasted_iota(jnp.int32, (DIL_SPAN, 2 * DIL_SPAN), 0)
    col = lax.broadcasted_iota(jnp.int32, (DIL_SPAN, 2 * DIL_SPAN), 1)
    back = row + DIL_SPAN - col
    in_band = jnp.where(back >= 0, jnp.where(back <= DIL_SPAN, 0.0, NEG), NEG)
    first_band = jnp.where(col >= jnp.where(i > 0, 0, DIL_SPAN), in_band, NEG)
    for r in range(dil):
        k_r = jnp.concatenate([kf[pl.ds(tn - DIL_SPAN * dil + r, DIL_SPAN, stride=dil), :],
                               kf[pl.ds(tn + r, per_class, stride=dil), :]], axis=0).astype(BF16)
        v_r = jnp.concatenate([vf[pl.ds(tn - DIL_SPAN * dil + r, DIL_SPAN, stride=dil), :],
                               vf[pl.ds(tn + r, per_class, stride=dil), :]], axis=0).astype(BF16)
        q_r = qf[pl.ds(r, per_class, stride=dil), :].astype(BF16)
        for a in range(per_class // DIL_SPAN):
            lo = a * DIL_SPAN
            s = _dot_nt(q_r[lo:lo + DIL_SPAN], k_r[lo:lo + 2 * DIL_SPAN]) + (first_band if a == 0 else in_band)
            m = jnp.max(s, axis=-1, keepdims=True)
            p = jnp.exp(s - m)
            l = jnp.sum(p, axis=-1, keepdims=True)
            rows = pl.ds(r + lo * dil, DIL_SPAN, stride=dil)
            o_ref[rows, :] = _dot(p.astype(BF16), v_r[lo:lo + 2 * DIL_SPAN]) / l
            lse_ref[rows, :] = jnp.broadcast_to(m + jnp.log(l), (DIL_SPAN, LANES))


def _dilated_group(dqkv3, g, dil, tn):
    batch, seq, _ = dqkv3.shape
    assert tn % (DIL_SPAN * dil) == 0
    blk = lambda col, prev: pl.BlockSpec(
        (None, tn, HEAD_DIM), lambda b, i, h: (b, jnp.maximum(i - 1, 0) if prev else i, col * DIL_HEADS + h))
    out = pl.BlockSpec((None, tn, HEAD_DIM), lambda b, i, h: (b, i, h))
    o, lse = pl.pallas_call(
        functools.partial(_dilated_kernel, dil=dil, tn=tn),
        grid=(batch, seq // tn, DIL_HEADS),
        in_specs=[blk(g, False), blk(3, False), blk(3, True), blk(4, False), blk(4, True)],
        out_specs=[out, out],
        out_shape=[jax.ShapeDtypeStruct((batch, seq, DIL_WIDTH), F32)] * 2,
        scratch_shapes=[pltpu.VMEM((tn, HEAD_DIM), F32), pltpu.VMEM((2 * tn, HEAD_DIM), F32),
                        pltpu.VMEM((2 * tn, HEAD_DIM), F32)],
        compiler_params=_params("parallel", "parallel", "parallel"),
        name=f"dilated_{dil}",
    )(dqkv3, dqkv3, dqkv3, dqkv3, dqkv3)
    return o.reshape(batch * seq, DIL_WIDTH), lse.reshape(batch * seq, DIL_WIDTH)


def _even_out_kernel(om_ref, o0, o1, o2, l0, l1, l2, wm_ref, wd_ref, x_ref, g_ref, b_ref, o_ref, ob_ref):
    ls = [l0[...], l1[...], l2[...]]
    mx = jnp.maximum(jnp.maximum(ls[0], ls[1]), ls[2])
    es = [jnp.exp(v - mx) for v in ls]
    den = es[0] + es[1] + es[2]
    o_dil = ((es[0] / den) * o0[...] + (es[1] / den) * o1[...] + (es[2] / den) * o2[...]).astype(BF16)
    y = _dot(om_ref[...], wm_ref[...]) + _dot(o_dil, wd_ref[...])
    out = _layer_norm(ALPHA * x_ref[...] + y, g_ref[...], b_ref[...])
    o_ref[...] = out
    ob_ref[...] = out.astype(BF16)


def _even_out(o_mla, outs, lses, w_mla, w_dil, x, g, b, tm):
    m = x.shape[0]
    row = lambda width: pl.BlockSpec((tm, width), lambda i: (i, 0))
    full = lambda shape: pl.BlockSpec(shape, lambda i: (0, 0))
    return pl.pallas_call(
        _even_out_kernel,
        grid=(m // tm,),
        in_specs=[row(o_mla.shape[1])] + [row(DIL_WIDTH)] * 6
        + [full(w_mla.shape), full(w_dil.shape), row(D_MODEL), full(g.shape), full(b.shape)],
        out_specs=[row(D_MODEL), row(D_MODEL)],
        out_shape=[jax.ShapeDtypeStruct((m, D_MODEL), F32), jax.ShapeDtypeStruct((m, D_MODEL), BF16)],
        compiler_params=_params("parallel"),
        name="even_out_ln",
    )(o_mla, *outs, *lses, w_mla, w_dil, x, g, b)


def _odd_out_kernel(a_ref, w_ref, x_ref, g_ref, b_ref, o_ref):
    o_ref[...] = _layer_norm(ALPHA * x_ref[...] + _dot(a_ref[...], w_ref[...]), g_ref[...], b_ref[...])


def _odd_out(a, w, x, g, b, tm):
    m = x.shape[0]
    row = lambda width: pl.BlockSpec((tm, width), lambda i: (i, 0))
    full = lambda shape: pl.BlockSpec(shape, lambda i: (0, 0))
    return pl.pallas_call(
        _odd_out_kernel,
        grid=(m // tm,),
        in_specs=[row(a.shape[1]), full(w.shape), row(D_MODEL), full(g.shape), full(b.shape)],
        out_specs=row(D_MODEL),
        out_shape=jax.ShapeDtypeStruct((m, D_MODEL), F32),
        compiler_params=_params("parallel"),
        name="odd_out_ln",
    )(a, w, x, g, b)


def _router_kernel(x_ref, rwh_ref, rwl_ref, rb_ref, route_ref, cnt_ref, carry_ref):
    @pl.when(pl.program_id(0) == 0)
    def _():
        carry_ref[...] = jnp.zeros_like(carry_ref)

    x = x_ref[...]
    xh = x.astype(BF16)
    xl = (x - xh.astype(F32)).astype(BF16)
    logits = _dot(xh, rwh_ref[...]) + (_dot(xh, rwl_ref[...]) + _dot(xl, rwh_ref[...])) + rb_ref[...]
    tm = logits.shape[0]
    lane = lax.broadcasted_iota(jnp.int32, (tm, LANES), 1)
    l1 = jnp.max(logits, axis=-1, keepdims=True)
    i1 = jnp.min(jnp.where(logits == l1, lane, LANES), axis=-1, keepdims=True)
    rest = jnp.where(lane == i1, NEG, logits)
    l2 = jnp.max(rest, axis=-1, keepdims=True)
    i2 = jnp.min(jnp.where(rest == l2, lane, LANES), axis=-1, keepdims=True)
    e = jnp.exp(l2 - l1)
    w1 = 1.0 / (1.0 + e)
    w2 = e / (1.0 + e)
    hot1 = (lane == i1).astype(F32)
    hot2 = (lane == i2).astype(F32)
    cnt = hot1 + hot2
    strict = (lax.broadcasted_iota(jnp.int32, (tm, tm), 1) < lax.broadcasted_iota(jnp.int32, (tm, tm), 0)).astype(BF16)
    before = _dot(strict, cnt.astype(BF16)) + carry_ref[...]
    r1 = jnp.sum(before * hot1, axis=-1, keepdims=True)
    r2 = jnp.sum(before * hot2, axis=-1, keepdims=True)
    vals = (i1.astype(F32), i2.astype(F32), w1, w2, r1, r2)
    route = jnp.zeros((tm, LANES), F32)
    for idx, val in enumerate(vals):
        route = jnp.where(lane == idx, val, route)
    route_ref[...] = route
    total = carry_ref[...] + jnp.sum(cnt, axis=0, keepdims=True)
    carry_ref[...] = total
    cnt_ref[...] = jnp.broadcast_to(total, cnt_ref.shape)


def _router(x, rw, rb, tm):
    m = x.shape[0]
    full = lambda shape: pl.BlockSpec(shape, lambda i: (0, 0))
    rw_hi = rw.astype(BF16)
    rw_lo = (rw - rw_hi.astype(F32)).astype(BF16)
    return pl.pallas_call(
        _router_kernel,
        grid=(m // tm,),
        in_specs=[pl.BlockSpec((tm, D_MODEL), lambda i: (i, 0)), full(rw.shape), full(rw.shape), full(rb.shape)],
        out_specs=[pl.BlockSpec((tm, LANES), lambda i: (i, 0)), pl.BlockSpec((8, LANES), lambda i: (0, 0))],
        out_shape=[jax.ShapeDtypeStruct((m, LANES), F32), jax.ShapeDtypeStruct((8, LANES), F32)],
        scratch_shapes=[pltpu.VMEM((1, LANES), F32)],
        compiler_params=_params("arbitrary"),
        name="moe_router",
    )(x, rw_hi, rw_lo, rb)


def _silu_mul(g, u):
    return g * (1.0 / (1.0 + jnp.exp(-g))) * u


def _ffn_up_kernel(x_ref, wg_ref, wu_ref, o_ref, wgb_ref, wub_ref):
    @pl.when(pl.program_id(1) == 0)
    def _():
        wgb_ref[...] = wg_ref[...].astype(BF16)
        wub_ref[...] = wu_ref[...].astype(BF16)

    x = x_ref[...]
    o_ref[...] = _silu_mul(_dot(x, wgb_ref[...]), _dot(x, wub_ref[...])).astype(BF16)


def _ffn_up(xb, wg, wu, tm, tn):
    m = xb.shape[0]
    return pl.pallas_call(
        _ffn_up_kernel,
        grid=(D_FF // tn, m // tm),
        in_specs=[pl.BlockSpec((tm, D_MODEL), lambda j, i: (i, 0)),
                  pl.BlockSpec((D_MODEL, tn), lambda j, i: (0, j)),
                  pl.BlockSpec((D_MODEL, tn), lambda j, i: (0, j))],
        out_specs=pl.BlockSpec((tm, tn), lambda j, i: (i, j)),
        out_shape=jax.ShapeDtypeStruct((m, D_FF), BF16),
        scratch_shapes=[pltpu.VMEM((D_MODEL, tn), BF16), pltpu.VMEM((D_MODEL, tn), BF16)],
        compiler_params=_params("arbitrary", "arbitrary"),
        name="ffn_up",
    )(xb, wg, wu)


def _ffn_down_ln_kernel(h_ref, w_ref, x_ref, g_ref, b_ref, o_ref, ob_ref, acc_ref):
    k = pl.program_id(1)

    @pl.when(k == 0)
    def _():
        acc_ref[...] = jnp.zeros_like(acc_ref)

    acc_ref[...] += _dot(h_ref[...], w_ref[...])

    @pl.when(k == pl.num_programs(1) - 1)
    def _():
        out = _layer_norm(ALPHA * x_ref[...] + acc_ref[...], g_ref[...], b_ref[...])
        o_ref[...] = out
        ob_ref[...] = out.astype(BF16)


def _ffn_down_ln(h, wd, x, g, b, tm, tk):
    m = x.shape[0]
    row = pl.BlockSpec((tm, D_MODEL), lambda i, k: (i, 0))
    res = row
    vec = pl.BlockSpec((1, D_MODEL), lambda i, k: (0, 0))
    return pl.pallas_call(
        _ffn_down_ln_kernel,
        grid=(m // tm, D_FF // tk),
        in_specs=[pl.BlockSpec((tm, tk), lambda i, k: (i, k)), pl.BlockSpec((tk, D_MODEL), lambda i, k: (k, 0)),
                  res, vec, vec],
        out_specs=[row, row],
        out_shape=[jax.ShapeDtypeStruct((m, D_MODEL), F32), jax.ShapeDtypeStruct((m, D_MODEL), BF16)],
        scratch_shapes=[pltpu.VMEM((tm, D_MODEL), F32)],
        compiler_params=_params("parallel", "arbitrary"),
        name="ffn_down_ln",
    )(h, wd, x, g, b)


def _proj_kernel(x_ref, w_ref, o_ref):
    o_ref[...] = _dot(x_ref[...], w_ref[...]).astype(BF16)


def _proj(xb, w, tm, tn):
    m = xb.shape[0]
    n = w.shape[1]
    return pl.pallas_call(
        _proj_kernel,
        grid=(n // tn, m // tm),
        in_specs=[pl.BlockSpec((tm, D_MODEL), lambda j, i: (i, 0)), pl.BlockSpec((D_MODEL, tn), lambda j, i: (0, j))],
        out_specs=pl.BlockSpec((tm, tn), lambda j, i: (i, j)),
        out_shape=jax.ShapeDtypeStruct((m, n), BF16),
        compiler_params=_params("parallel", "parallel"),
        name="proj",
    )(xb, w)


def _proj_t_kernel(w_ref, x_ref, o_ref, *, scale):
    o_ref[...] = (_dot_nt(w_ref[...], x_ref[...]) * scale).astype(BF16)


def _proj_t(xb3, wt, scale, tm, tn):
    b, s, _ = xb3.shape
    n = wt.shape[0]
    return pl.pallas_call(
        functools.partial(_proj_t_kernel, scale=scale),
        grid=(n // tn, b, s // tm),
        in_specs=[pl.BlockSpec((tn, D_MODEL), lambda j, bi, i: (j, 0)),
                  pl.BlockSpec((None, tm, D_MODEL), lambda j, bi, i: (bi, i, 0))],
        out_specs=pl.BlockSpec((None, tn, tm), lambda j, bi, i: (bi, j, i)),
        out_shape=jax.ShapeDtypeStruct((b, n, s), BF16),
        compiler_params=_params("parallel", "parallel", "parallel"),
        name="proj_t",
    )(wt, xb3)


def _fgate_kernel(x_ref, w_ref, b_ref, c_ref, carry_ref):
    @pl.when(pl.program_id(1) == 0)
    def _():
        carry_ref[...] = jnp.zeros_like(carry_ref)

    z = _dot(x_ref[...], w_ref[...]) + b_ref[...]
    log_f = jnp.minimum(z, 0.0) - jnp.log(1.0 + jnp.exp(-jnp.abs(z)))
    tm = z.shape[0]
    tri = (lax.broadcasted_iota(jnp.int32, (tm, tm), 1) <= lax.broadcasted_iota(jnp.int32, (tm, tm), 0)).astype(F32)
    c = jnp.dot(tri, log_f, preferred_element_type=F32, precision=lax.Precision.HIGHEST) + carry_ref[...]
    c_ref[...] = c
    carry_ref[...] = c[tm - 1:tm, :]


def _fgate(xb3, wf, bf, tm):
    b, s, _ = xb3.shape
    return pl.pallas_call(
        _fgate_kernel,
        grid=(b, s // tm),
        in_specs=[pl.BlockSpec((None, tm, D_MODEL), lambda bi, i: (bi, i, 0)),
                  pl.BlockSpec(wf.shape, lambda bi, i: (0, 0)), pl.BlockSpec(bf.shape, lambda bi, i: (0, 0))],
        out_specs=pl.BlockSpec((None, tm, LANES), lambda bi, i: (bi, i, 0)),
        out_shape=jax.ShapeDtypeStruct((b, s, LANES), F32),
        scratch_shapes=[pltpu.VMEM((1, LANES), F32)],
        compiler_params=_params("parallel", "arbitrary"),
        name="fox_gate",
    )(xb3, wf, bf)


def _dispatch_kernel(fill_ref, pos_ref, x_ref, xs_ref, zero_ref, sem, zero_sem):
    tm = x_ref.shape[0]

    @pl.when(pl.program_id(0) == 0)
    def _():
        zero_ref[...] = jnp.zeros_like(zero_ref)
        rows = zero_ref.shape[0]

        def fill(f):
            start = pl.multiple_of(jnp.maximum(fill_ref[f], 0), rows)
            return pltpu.make_async_copy(zero_ref, xs_ref.at[pl.ds(start, rows), :], zero_sem)

        for f in range(fill_ref.shape[0]):
            @pl.when(fill_ref[f] >= 0)
            def _():
                fill(f).start()

        for f in range(fill_ref.shape[0]):
            @pl.when(fill_ref[f] >= 0)
            def _():
                fill(f).wait()

    def copy(r, k):
        return pltpu.make_async_copy(x_ref.at[pl.ds(r, 1), :], xs_ref.at[pl.ds(pos_ref[0, 0, 2 * r + k], 1), :], sem)

    def start(r, _):
        copy(r, 0).start()
        copy(r, 1).start()
        return 0

    def wait(r, _):
        copy(r, 0).wait()
        copy(r, 1).wait()
        return 0

    lax.fori_loop(0, tm, start, 0, unroll=DMA_UNROLL)
    lax.fori_loop(0, tm, wait, 0, unroll=DMA_UNROLL)


def _dispatch(x, pos, fill_rows, n_rows, tile, tm):
    m, width = x.shape
    pos3 = pos.reshape(m // tm, 1, 2 * tm)
    return pl.pallas_call(
        _dispatch_kernel,
        grid_spec=pltpu.PrefetchScalarGridSpec(
            num_scalar_prefetch=1,
            grid=(m // tm,),
            in_specs=[pl.BlockSpec((1, 1, 2 * tm), lambda i, fill: (i, 0, 0), memory_space=pltpu.SMEM),
                      pl.BlockSpec((tm, width), lambda i, fill: (i, 0))],
            out_specs=pl.BlockSpec(memory_space=pl.ANY),
            scratch_shapes=[pltpu.VMEM((tile, width), x.dtype), pltpu.SemaphoreType.DMA(()),
                            pltpu.SemaphoreType.DMA(())],
        ),
        out_shape=jax.ShapeDtypeStruct((n_rows, width), x.dtype),
        compiler_params=_params("arbitrary"),
        name="moe_dispatch",
    )(fill_rows, pos3, x)


def _moe_up_kernel(te_ref, nu_ref, nx_ref, x_ref, wg_hbm, wu_hbm, o_ref, sg_ref, su_ref, wb_ref, sems, *, tn):
    groups = [(lo, min(MXU_COLS, tn - lo)) for lo in range(0, tn, MXU_COLS)]
    j = pl.program_id(0)
    t = pl.program_id(1)
    live = t < nu_ref[0]
    fresh = jnp.logical_or(t == 0, te_ref[t] != te_ref[jnp.maximum(t - 1, 0)])

    def weight_copies(e, jj):
        cols = pl.ds(pl.multiple_of(jj * tn, LANES), tn)
        return (pltpu.make_async_copy(wg_hbm.at[e, :, cols], sg_ref, sems.at[0]),
                pltpu.make_async_copy(wu_hbm.at[e, :, cols], su_ref, sems.at[1]))

    def start(e, jj):
        for c in weight_copies(e, jj):
            c.start()

    @pl.when(jnp.logical_and(j == 0, t == 0))
    def _():
        start(te_ref[0], 0)

    @pl.when(jnp.logical_and(live, fresh))
    def _():
        for c in weight_copies(te_ref[t], j):
            c.wait()
        def cast_rows(c, _):
            rows = pl.ds(pl.multiple_of(c * 256, 256), 256)
            for lo, width in groups:
                wb_ref[rows, 2 * lo:2 * lo + width] = sg_ref[rows, lo:lo + width].astype(BF16)
                wb_ref[rows, 2 * lo + width:2 * lo + 2 * width] = su_ref[rows, lo:lo + width].astype(BF16)
            return 0

        lax.fori_loop(0, D_MODEL // 256, cast_rows, 0)
        nxt = nx_ref[t]

        @pl.when(nxt >= 0)
        def _():
            start(nxt, j)

        @pl.when(jnp.logical_and(nxt < 0, j + 1 < pl.num_programs(0)))
        def _():
            start(te_ref[0], j + 1)

    @pl.when(live)
    def _():
        x = x_ref[...].astype(BF16)
        for lo, width in groups:
            gu = _dot(x, wb_ref[:, 2 * lo:2 * lo + 2 * width])
            o_ref[:, lo:lo + width] = _silu_mul(gu[:, :width], gu[:, width:]).astype(BF16)

    @pl.when(jnp.logical_not(live))
    def _():
        o_ref[...] = jnp.zeros_like(o_ref)


def _moe_up(xs, wg, wu, tile_expert, n_used, next_expert, tm, tn):
    p = xs.shape[0]
    live = lambda t, nu: jnp.minimum(t, nu[0] - 1)
    return pl.pallas_call(
        functools.partial(_moe_up_kernel, tn=tn),
        grid_spec=pltpu.PrefetchScalarGridSpec(
            num_scalar_prefetch=3,
            grid=(D_FF // tn, p // tm),
            in_specs=[pl.BlockSpec((tm, D_MODEL), lambda j, t, te, nu, nx: (live(t, nu), 0)),
                      pl.BlockSpec(memory_space=pl.ANY), pl.BlockSpec(memory_space=pl.ANY)],
            out_specs=pl.BlockSpec((tm, tn), lambda j, t, te, nu, nx: (t, j)),
            scratch_shapes=[pltpu.VMEM((D_MODEL, tn), F32), pltpu.VMEM((D_MODEL, tn), F32),
                            pltpu.VMEM((D_MODEL, 2 * tn), BF16), pltpu.SemaphoreType.DMA((2,))],
        ),
        out_shape=jax.ShapeDtypeStruct((p, D_FF), BF16),
        compiler_params=_params("arbitrary", "arbitrary"),
        name="moe_up",
    )(tile_expert, n_used, next_expert, xs, wg, wu)


def _moe_down_kernel(te_ref, nu_ref, h_ref, w_ref, o_ref, wb_ref):
    t = pl.program_id(1)
    live = t < nu_ref[0]
    fresh = jnp.logical_or(t == 0, te_ref[t] != te_ref[jnp.maximum(t - 1, 0)])

    @pl.when(jnp.logical_and(live, fresh))
    def _():
        wb_ref[...] = w_ref[...].astype(BF16)

    @pl.when(live)
    def _():
        o_ref[...] = _dot(h_ref[...], wb_ref[...])

    @pl.when(jnp.logical_not(live))
    def _():
        o_ref[...] = jnp.zeros_like(o_ref)


def _moe_down(hs, wd, tile_expert, n_used, tm, tn):
    p = hs.shape[0]
    live = lambda t, nu: jnp.minimum(t, nu[0] - 1)
    return pl.pallas_call(
        _moe_down_kernel,
        grid_spec=pltpu.PrefetchScalarGridSpec(
            num_scalar_prefetch=2,
            grid=(D_MODEL // tn, p // tm),
            in_specs=[pl.BlockSpec((tm, D_FF), lambda j, t, te, nu: (live(t, nu), 0)),
                      pl.BlockSpec((None, D_FF, tn), lambda j, t, te, nu: (te[live(t, nu)], 0, j))],
            out_specs=pl.BlockSpec((tm, tn), lambda j, t, te, nu: (t, j)),
            scratch_shapes=[pltpu.VMEM((D_FF, tn), BF16)],
        ),
        out_shape=jax.ShapeDtypeStruct((p, D_MODEL), F32),
        compiler_params=_params("arbitrary", "arbitrary"),
        name="moe_down",
    )(tile_expert, n_used, hs, wd)


def _combine_ln_kernel(pos_ref, ys_ref, x_ref, route_ref, g_ref, b_ref, o_ref, buf0, buf1, sem):
    tm = x_ref.shape[0]

    def copies(r):
        c0 = pltpu.make_async_copy(ys_ref.at[pl.ds(pos_ref[0, 0, 2 * r], 1), :], buf0.at[pl.ds(r, 1), :], sem)
        c1 = pltpu.make_async_copy(ys_ref.at[pl.ds(pos_ref[0, 0, 2 * r + 1], 1), :], buf1.at[pl.ds(r, 1), :], sem)
        return c0, c1

    def start(r, _):
        c0, c1 = copies(r)
        c0.start()
        c1.start()
        return 0

    def wait(r, _):
        c0, c1 = copies(r)
        c0.wait()
        c1.wait()
        return 0

    lax.fori_loop(0, tm, start, 0, unroll=DMA_UNROLL)
    lax.fori_loop(0, tm, wait, 0, unroll=DMA_UNROLL)
    route = route_ref[...]
    y = route[:, 2:3] * buf0[...] + route[:, 3:4] * buf1[...]
    o_ref[...] = _layer_norm(ALPHA * x_ref[...] + y, g_ref[...], b_ref[...])


def _combine_ln(ys, pos, x, route, g, b, tm):
    m = x.shape[0]
    pos3 = pos.reshape(m // tm, 1, 2 * tm)
    vec = pl.BlockSpec((1, D_MODEL), lambda i: (0, 0))
    return pl.pallas_call(
        _combine_ln_kernel,
        grid=(m // tm,),
        in_specs=[pl.BlockSpec((1, 1, 2 * tm), lambda i: (i, 0, 0), memory_space=pltpu.SMEM),
                  pl.BlockSpec(memory_space=pl.ANY),
                  pl.BlockSpec((tm, D_MODEL), lambda i: (i, 0)),
                  pl.BlockSpec((tm, LANES), lambda i: (i, 0)), vec, vec],
        out_specs=pl.BlockSpec((tm, D_MODEL), lambda i: (i, 0)),
        out_shape=jax.ShapeDtypeStruct((m, D_MODEL), F32),
        scratch_shapes=[pltpu.VMEM((tm, D_MODEL), F32), pltpu.VMEM((tm, D_MODEL), F32), pltpu.SemaphoreType.DMA(())],
        compiler_params=_params("arbitrary"),
        name="moe_combine_ln",
    )(pos3, ys, x, route, g, b)


def _rope_tables(seq):
    def angles(dim):
        inv_freq = 1.0 / (ROPE_THETA ** (jnp.arange(0, dim, 2, dtype=F32) / dim))
        ang = jnp.arange(seq, dtype=F32)[:, None] * inv_freq[None, :]
        return jnp.cos(ang), jnp.sin(ang)

    c128, s128 = angles(HEAD_DIM)
    cos_f = jnp.concatenate([c128, c128], axis=-1)
    sin_f = jnp.concatenate([-s128, s128], axis=-1)
    c64, s64 = angles(MLA_ROPE_DIM)
    z32 = jnp.zeros_like(c64)
    cos_t = jnp.concatenate([c64, c64, z32, z32], axis=-1)
    sin_a = jnp.concatenate([-s64, z32, z32, z32], axis=-1)
    sin_b = jnp.concatenate([z32, s64, z32, z32], axis=-1)
    return cos_f, sin_f, cos_t, sin_a, sin_b, c64.T, s64.T


def _pad_cols(a, width):
    return jnp.pad(a, ((0, 0), (0, width - a.shape[1])))


def _row(v):
    return v.reshape(1, -1).astype(F32)


def _even_layer(x, batch, seq, tables, w_in, q_norm, w_q_b, kv_norm, w_kv_b, w_out, ln1_g, ln1_b,
                w_gate, w_up, w_down, ln2_g, ln2_b):
    cos_f, sin_f, cos_t, sin_a, sin_b, cos_tt, sin_tt = tables
    tm = min(512, seq)
    wa = jnp.concatenate([w_in[:, OFF_CKV:OFF_KROPE], _pad_cols(w_in[:, OFF_KROPE:OFF_DQ], LANES),
                          _pad_cols(w_in[:, OFF_CQ:OFF_CKV], MLA_Q_RANK_PAD)], axis=1).astype(BF16)
    qg = _pad_cols(_row(q_norm), MLA_Q_RANK_PAD)
    wq = jnp.pad(w_q_b.reshape(MLA_Q_RANK, MLA_HEADS, MLA_QK_DIM),
                 ((0, MLA_Q_RANK_PAD - MLA_Q_RANK), (0, 0), (0, MLA_QK_PAD - MLA_QK_DIM)))
    wqt = wq.reshape(MLA_Q_RANK_PAD, MLA_HEADS * MLA_QK_PAD).T.astype(BF16)
    wkv3 = w_kv_b.reshape(MLA_KV_RANK, MLA_HEADS, MLA_NOPE_DIM + MLA_V_DIM)
    wk = wkv3[:, :, :MLA_NOPE_DIM].reshape(MLA_KV_RANK, -1).astype(BF16)
    wvt = wkv3[:, :, MLA_NOPE_DIM:].reshape(MLA_KV_RANK, -1).T.astype(BF16)
    qt_mla, k_mla, vt_mla, xb3 = _mla_prep(x.reshape(batch, seq, D_MODEL), wa, _row(kv_norm), qg, wqt, wk, wvt,
                                           cos_t, sin_a, sin_b, cos_tt, sin_tt, tm)
    xb = xb3.reshape(batch * seq, D_MODEL)
    tq = min(FLASH_TQ, seq)
    o_mla = _flash(qt_mla, k_mla, vt_mla, None, heads=MLA_HEADS, dq=MLA_QK_PAD, dk=MLA_QK_PAD, dv=MLA_V_DIM,
                   tq=tq)
    o_mla = o_mla.reshape(batch * seq, -1)

    dqkv = _dqkv(xb, w_in[:, OFF_DQ:].astype(BF16), cos_f, sin_f, seq, min(1024, seq))
    dqkv3 = dqkv.reshape(batch, seq, -1)
    outs, lses = [], []
    for g, (window, dil) in enumerate(DIL_PATTERNS):
        assert window == DIL_SPAN * dil
        o_g, lse_g = _dilated_group(dqkv3, g, dil, min(seq, max(DIL_TILE, DIL_SPAN * dil)))
        outs.append(o_g)
        lses.append(lse_g)
    n_mla = MLA_HEADS * MLA_V_DIM
    wo = w_out.astype(BF16)
    x1, x1b = _even_out(o_mla, outs, lses, wo[:n_mla], wo[n_mla:], x, _row(ln1_g), _row(ln1_b), tm)
    hmid = _ffn_up(x1b, w_gate, w_up, min(1024, seq), 512)
    return _ffn_down_ln(hmid, w_down.astype(BF16), x1, _row(ln2_g), _row(ln2_b), tm, 1408)


def _odd_layer(x, xb, batch, seq, w_qkv, w_f, b_f, w_out, ln1_g, ln1_b, router_w, router_b,
               exp_w_gate, exp_w_up, exp_w_down, ln2_g, ln2_b):
    m = batch * seq
    tm = min(512, seq)
    xb3 = xb.reshape(batch, seq, D_MODEL)
    wb = w_qkv.astype(BF16)
    tp = min(1024, seq)
    qt = _proj_t(xb3, wb[:, :FOX_WIDTH].T, HEAD_DIM ** -0.5 * LOG2E, tp, 1024)
    k = _proj(xb, wb[:, FOX_WIDTH:2 * FOX_WIDTH], tp, 1024).reshape(batch, seq, FOX_WIDTH)
    vt = _proj_t(xb3, wb[:, 2 * FOX_WIDTH:].T, 1.0, tp, 1024)
    c = _fgate(xb3, _pad_cols(w_f, LANES).astype(BF16), _pad_cols(_row(b_f), LANES), min(256, seq))
    c_t = jnp.transpose(c[:, :, :FOX_HEADS], (0, 2, 1)).reshape(batch, FOX_HEADS, 1, seq)
    tq = min(FLASH_TQ, seq)
    o = _flash(qt, k, vt, c_t, heads=FOX_HEADS, dq=HEAD_DIM, dk=HEAD_DIM, dv=HEAD_DIM, tq=tq)
    rb = jnp.full((1, LANES), NEG, F32).at[0, :N_EXPERTS].set(router_b.astype(F32))
    x1 = _odd_out(o.reshape(m, -1), w_out.astype(BF16), x, _row(ln1_g), _row(ln1_b), tm)
    route, counts = _router(x1, _pad_cols(router_w, LANES), rb, tm)
    tile = MOE_TILE
    n_tiles = (2 * m) // tile + N_EXPERTS
    cnt = counts[0, :N_EXPERTS].astype(jnp.int32)
    tiles_per = (cnt + tile - 1) // tile
    tile_end = jnp.cumsum(tiles_per)
    offset = (tile_end - tiles_per) * tile
    idx = route[:, 0:2].astype(jnp.int32)
    pos = (offset[idx] + route[:, 4:6].astype(jnp.int32)).reshape(-1)
    n_used = tile_end[-1:]
    tile_ids = jnp.arange(n_tiles, dtype=jnp.int32)
    tile_expert = jnp.minimum(jnp.sum((tile_end[None, :] <= tile_ids[:, None]).astype(jnp.int32), axis=1),
                              N_EXPERTS - 1)
    td = tm
    experts = jnp.arange(N_EXPERTS, dtype=jnp.int32)
    last_tile = jnp.where(tiles_per > 0, tile_end - 1, -1)
    unused = n_used[0] + experts
    unused = jnp.where(unused < n_tiles, unused, -1)
    fill_tiles = jnp.concatenate([last_tile, unused])
    fill_rows = jnp.where(fill_tiles >= 0, fill_tiles * tile, -1).astype(jnp.int32)
    xs = _dispatch(x1, pos, fill_rows, n_tiles * tile, tile, td)
    later = jnp.where((experts[None, :] > experts[:, None]) & (tiles_per[None, :] > 0), experts[None, :], N_EXPERTS)
    next_nonempty = jnp.min(later, axis=1)
    next_expert = jnp.where(next_nonempty < N_EXPERTS, next_nonempty, -1)[tile_expert].astype(jnp.int32)
    hs = _moe_up(xs, exp_w_gate, exp_w_up, tile_expert, n_used, next_expert, tile, 1408)
    split = tile // MOE_DOWN_TILE
    ys = _moe_down(hs, exp_w_down, jnp.repeat(tile_expert, split), n_used * split, MOE_DOWN_TILE, 512)
    return _combine_ln(ys, pos, x1, route, _row(ln2_g), _row(ln2_b), td)


def kernel(x, ev_w_in, ev_q_norm, ev_w_q_b, ev_kv_norm, ev_w_kv_b, ev_w_out, ev_ln1_g, ev_ln1_b, ev_ffn_w_gate, ev_ffn_w_up, ev_ffn_w_down, ev_ln2_g, ev_ln2_b, od_w_qkv, od_w_f, od_b_f, od_w_out, od_ln1_g, od_ln1_b, od_router_w, od_router_b, od_exp_w_gate, od_exp_w_up, od_exp_w_down, od_ln2_g, od_ln2_b):
    batch, seq, _ = x.shape
    tables = _rope_tables(seq)
    h = x.reshape(batch * seq, D_MODEL)
    hb = None
    for layer in range(DEPTH):
        i = layer // 2
        if layer % 2 == 0:
            h, hb = _even_layer(h, batch, seq, tables, ev_w_in[i], ev_q_norm[i], ev_w_q_b[i], ev_kv_norm[i],
                                ev_w_kv_b[i], ev_w_out[i], ev_ln1_g[i], ev_ln1_b[i], ev_ffn_w_gate[i],
                                ev_ffn_w_up[i], ev_ffn_w_down[i], ev_ln2_g[i], ev_ln2_b[i])
        else:
            h = _odd_layer(h, hb, batch, seq, od_w_qkv[i], od_w_f[i], od_b_f[i], od_w_out[i], od_ln1_g[i],
                           od_ln1_b[i], od_router_w[i], od_router_b[i], od_exp_w_gate[i], od_exp_w_up[i],
                           od_exp_w_down[i], od_ln2_g[i], od_ln2_b[i])
    return h.reshape(batch, seq, D_MODEL)
```

```python
import functools

import jax
import jax.numpy as jnp
from jax import lax
from jax.experimental import pallas as pl
from jax.experimental.pallas import tpu as pltpu

F32 = jnp.float32
BF16 = jnp.bfloat16

D_MODEL = 2048
HEAD_DIM = 128
LANES = 128
MXU_COLS = 256
ROPE_THETA = 10000.0
LN_EPS = 1e-5
RMS_EPS = 1e-6

MLA_HEADS = 10
MLA_Q_RANK = 448
MLA_Q_RANK_PAD = 512
MLA_KV_RANK = 128
MLA_NOPE_DIM = 128
MLA_ROPE_DIM = 64
MLA_V_DIM = 128
MLA_QK_DIM = MLA_NOPE_DIM + MLA_ROPE_DIM
MLA_QK_PAD = 256

DIL_PATTERNS = ((128, 1), (512, 4), (2048, 16))
DIL_GROUPS = 3
DIL_HEADS = 6
DIL_WIDTH = DIL_HEADS * HEAD_DIM
DIL_SPAN = 128

OFF_CQ = 0
OFF_CKV = OFF_CQ + MLA_Q_RANK
OFF_KROPE = OFF_CKV + MLA_KV_RANK
OFF_DQ = OFF_KROPE + MLA_ROPE_DIM
OFF_DK = OFF_DQ + DIL_GROUPS * DIL_WIDTH
OFF_DV = OFF_DK + DIL_WIDTH
W_IN_COLS = OFF_DV + DIL_WIDTH

FOX_HEADS = 16
FOX_WIDTH = FOX_HEADS * HEAD_DIM

D_FF = 5632
N_EXPERTS = 8
DEPTH = 2
ALPHA = (2.0 * DEPTH) ** 0.25

NEG = -1e30
LOG2E = 1.4426950408889634
VMEM_LIMIT = 56 * 1024 * 1024

ROW_TILE = 512
WIDE_ROW_TILE = 1024
FLASH_TQ = 1024
DIL_TILE = 2048
GATE_ROW_TILE = 512
QKV_COL_TILE = 1024
FFN_COL_TILE = 512
FFN_K_TILE = 1408
MOE_TILE = 512
MOE_COL_TILE = 1408
MOE_DOWN_COL_TILE = 512
DMA_UNROLL = 8


def _params(*sem, vmem=VMEM_LIMIT):
    return pltpu.CompilerParams(dimension_semantics=sem, vmem_limit_bytes=vmem)


def _dot(a, b):
    return jnp.dot(a, b, preferred_element_type=F32)


def _dot_nt(a, b):
    return lax.dot_general(a, b, (((1,), (1,)), ((), ())), preferred_element_type=F32)


def _layer_norm(y, g, b):
    mu = jnp.mean(y, axis=-1, keepdims=True)
    d = y - mu
    var = jnp.mean(d * d, axis=-1, keepdims=True)
    return d * lax.rsqrt(var + LN_EPS) * g + b


def _rope128(x, cos_f, sin_f):
    return x * cos_f + pltpu.roll(x, 64, 1) * sin_f


def _rope64(x, cos_t, sin_a, sin_b):
    return x * cos_t + pltpu.roll(x, 96, 1) * sin_a + pltpu.roll(x, 32, 1) * sin_b


def _mla_prep_kernel(x_ref, wa_ref, kvg_ref, qg_ref, wqt_ref, wk_ref, wvt_ref, cos_ref, sa_ref, sb_ref,
                     ct_ref, st_ref, qt_ref, k_ref, vt_ref, xb_ref):
    xb = x_ref[...].astype(BF16)
    xb_ref[...] = xb
    h = _dot(xb, wa_ref[...])
    ckv = h[:, 0:128]
    kr = h[:, 128:256]
    cq = h[:, 256:768]
    ckv_n = (ckv * lax.rsqrt(jnp.mean(ckv * ckv, axis=-1, keepdims=True) + RMS_EPS) * kvg_ref[...]).astype(BF16)
    cq_ms = jnp.sum(cq * cq, axis=-1, keepdims=True) * (1.0 / MLA_Q_RANK)
    cq_n = (cq * lax.rsqrt(cq_ms + RMS_EPS) * qg_ref[...]).astype(BF16)
    kr_r = _rope64(kr, cos_ref[...], sa_ref[...], sb_ref[...]).astype(BF16)
    scale = MLA_QK_DIM ** -0.5 * LOG2E
    qt = _dot_nt(wqt_ref[...], cq_n)
    c, s = ct_ref[...], st_ref[...]
    k_nope = _dot(ckv_n, wk_ref[...])
    for hd in range(MLA_HEADS):
        o = hd * MLA_QK_PAD
        qt_ref[o:o + 128, :] = (qt[o:o + 128] * scale).astype(BF16)
        x1, x2 = qt[o + 128:o + 160], qt[o + 160:o + 192]
        qt_ref[o + 128:o + 160, :] = ((x1 * c - x2 * s) * scale).astype(BF16)
        qt_ref[o + 160:o + 192, :] = ((x2 * c + x1 * s) * scale).astype(BF16)
        qt_ref[o + 192:o + 256, :] = jnp.zeros((64, qt.shape[1]), BF16)
        k_ref[:, o:o + 128] = k_nope[:, hd * 128:(hd + 1) * 128].astype(BF16)
        k_ref[:, o + 128:o + 256] = kr_r
    vt_ref[...] = _dot_nt(wvt_ref[...], ckv_n).astype(BF16)


def _mla_prep(x3, wa, kvg, qg, wqt, wk, wvt, cos_t, sin_a, sin_b, cos_tt, sin_tt, tm):
    b, s, _ = x3.shape
    full = lambda shape: pl.BlockSpec(shape, lambda bi, i: (0, 0))
    tab = pl.BlockSpec((tm, LANES), lambda bi, i: (i, 0))
    tab_t = pl.BlockSpec((MLA_ROPE_DIM // 2, tm), lambda bi, i: (0, i))
    wide = MLA_HEADS * MLA_QK_PAD
    vw = MLA_HEADS * MLA_V_DIM
    return pl.pallas_call(
        _mla_prep_kernel,
        grid=(b, s // tm),
        in_specs=[pl.BlockSpec((None, tm, D_MODEL), lambda bi, i: (bi, i, 0)), full(wa.shape), full(kvg.shape),
                  full(qg.shape), full(wqt.shape), full(wk.shape), full(wvt.shape), tab, tab, tab, tab_t, tab_t],
        out_specs=[pl.BlockSpec((None, wide, tm), lambda bi, i: (bi, 0, i)),
                   pl.BlockSpec((None, tm, wide), lambda bi, i: (bi, i, 0)),
                   pl.BlockSpec((None, vw, tm), lambda bi, i: (bi, 0, i)),
                   pl.BlockSpec((None, tm, D_MODEL), lambda bi, i: (bi, i, 0))],
        out_shape=[jax.ShapeDtypeStruct((b, wide, s), BF16), jax.ShapeDtypeStruct((b, s, wide), BF16),
                   jax.ShapeDtypeStruct((b, vw, s), BF16), jax.ShapeDtypeStruct((b, s, D_MODEL), BF16)],
        compiler_params=_params("parallel", "parallel"),
        name="mla_prep",
    )(x3, wa, kvg, qg, wqt, wk, wvt, cos_t, sin_a, sin_b, cos_tt, sin_tt)


def _dqkv_kernel(x_ref, w_ref, cos_ref, sin_ref, o_ref):
    j = pl.program_id(0)
    h = _dot(x_ref[...], w_ref[...])

    @pl.when(j < 4)
    def _():
        cos_f, sin_f = cos_ref[...], sin_ref[...]
        sc = jnp.where(j < 3, HEAD_DIM ** -0.5, 1.0).astype(F32)
        for hd in range(DIL_HEADS):
            sl = slice(hd * 128, (hd + 1) * 128)
            o_ref[:, sl] = (_rope128(h[:, sl], cos_f, sin_f) * sc).astype(BF16)

    @pl.when(j == 4)
    def _():
        o_ref[...] = h.astype(BF16)


def _dqkv(xb, wd, cos_f, sin_f, seq, tm):
    m = xb.shape[0]
    nrow = seq // tm
    n_col = wd.shape[1] // DIL_WIDTH
    tab = pl.BlockSpec((tm, LANES), lambda j, i: (i % nrow, 0))
    return pl.pallas_call(
        _dqkv_kernel,
        grid=(n_col, m // tm),
        in_specs=[pl.BlockSpec((tm, D_MODEL), lambda j, i: (i, 0)),
                  pl.BlockSpec((D_MODEL, DIL_WIDTH), lambda j, i: (0, j)), tab, tab],
        out_specs=pl.BlockSpec((tm, DIL_WIDTH), lambda j, i: (i, j)),
        out_shape=jax.ShapeDtypeStruct((m, wd.shape[1]), BF16),
        compiler_params=_params("parallel", "parallel"),
        name="dil_qkv",
    )(xb, wd, cos_f, sin_f)


def _flash_kernel(*refs, tq, use_c):
    if use_c:
        qt_ref, k_ref, vt_ref, c_ref, o_ref, acc_ref, s0_ref, s1_ref, kaug_ref = refs
    else:
        qt_ref, k_ref, vt_ref, o_ref, acc_ref, s0_ref, s1_ref = refs
    seq = k_ref.shape[0]
    nq = seq // tq
    if use_c:
        row = lax.broadcasted_iota(jnp.int32, (LANES, tq), 0)

        def build(j, _):
            start = pl.multiple_of(j * tq, tq)
            neg = -LOG2E * c_ref[:, pl.ds(start, tq)]
            hi = neg.astype(BF16).astype(F32)
            mid = (neg - hi).astype(BF16).astype(F32)
            lo = neg - hi - mid
            blk = jnp.where(row == 0, hi, jnp.where(row == 1, mid, jnp.where(row == 2, lo, 0.0)))
            kaug_ref[pl.ds(start, tq), :] = blk.T.astype(BF16)
            return 0

        lax.fori_loop(0, nq, build, 0)

    def scores(i, j, s_ref):
        q = qt_ref[:, pl.ds(pl.multiple_of(i * tq, tq), tq)]
        if use_c:
            ones = (lax.broadcasted_iota(jnp.int32, (LANES, tq), 0) < 3).astype(BF16)
            q = jnp.concatenate([q, ones], axis=0)
        start = pl.multiple_of(j * tq, tq)
        kt = k_ref[pl.ds(start, tq), :]
        if use_c:
            kt = jnp.concatenate([kt, kaug_ref[pl.ds(start, tq), :]], axis=1)
        s_ref[...] = _dot(kt, q)

    def update(j, s_ref, stats):
        m, l = stats
        s = s_ref[...]
        m_new = jnp.maximum(m, jnp.max(s, axis=0, keepdims=True))
        a = jnp.exp2(m - m_new)
        p = jnp.exp2(s - m_new)
        l = a * l + jnp.sum(p, axis=0, keepdims=True)
        start = pl.multiple_of(j * tq, tq)
        acc_ref[...] = a * acc_ref[...] + _dot(vt_ref[:, pl.ds(start, tq)], p.astype(BF16))
        return m_new, l

    half = tq // 2

    def scores_diag(i, s_ref):
        q = qt_ref[:, pl.ds(pl.multiple_of(i * tq, tq), tq)]
        if use_c:
            ones = (lax.broadcasted_iota(jnp.int32, (LANES, tq), 0) < 3).astype(BF16)
            q = jnp.concatenate([q, ones], axis=0)
        start = pl.multiple_of(i * tq, tq)
        kt = k_ref[pl.ds(start, tq), :]
        if use_c:
            kt = jnp.concatenate([kt, kaug_ref[pl.ds(start, tq), :]], axis=1)
        s_ref[0:half, 0:half] = _dot(kt[0:half], q[:, 0:half])
        s_ref[:, half:tq] = _dot(kt, q[:, half:tq])

    def update_diag(j, s_ref, stats):
        m, l = stats
        start = pl.multiple_of(j * tq, tq)
        parts = []
        for lo, nkeys in ((0, half), (half, tq)):
            cols = slice(lo, lo + half)
            key = lax.broadcasted_iota(jnp.int32, (nkeys, half), 0)
            qry = lax.broadcasted_iota(jnp.int32, (nkeys, half), 1) + lo
            s = jnp.where(key <= qry, s_ref[0:nkeys, cols], NEG)
            m_new = jnp.maximum(m[:, cols], jnp.max(s, axis=0, keepdims=True))
            a = jnp.exp2(m[:, cols] - m_new)
            p = jnp.exp2(s - m_new)
            parts.append((m_new, a * l[:, cols] + jnp.sum(p, axis=0, keepdims=True)))
            acc_ref[:, cols] = a * acc_ref[:, cols] + _dot(vt_ref[:, pl.ds(start, nkeys)], p.astype(BF16))
        return tuple(jnp.concatenate([parts[0][n], parts[1][n]], axis=1) for n in range(2))

    def query_tile(i, first, second):
        acc_ref[...] = jnp.zeros_like(acc_ref)

        def pair(jj, stats):
            scores(i, 2 * jj + 1, second)
            stats = update(2 * jj, first, stats)
            scores(i, 2 * jj + 2, first)
            return update(2 * jj + 1, second, stats)

        init = (jnp.full((1, tq), NEG, F32), jnp.zeros((1, tq), F32))
        stats = lax.fori_loop(0, i // 2, pair, init)
        nxt = jnp.minimum(i + 1, nq - 1)

        def odd_tail(stats):
            scores_diag(i, second)
            stats = update(i - 1, first, stats)
            scores(nxt, 0, first)
            return update_diag(i, second, stats)

        def even_tail(stats):
            scores(nxt, 0, second)
            return update_diag(i, first, stats)

        _, l = lax.cond(i % 2 == 1, odd_tail, even_tail, stats)
        o_ref[pl.ds(pl.multiple_of(i * tq, tq), tq), :] = (acc_ref[...] / l).T.astype(o_ref.dtype)
        return 0

    scores(0, 0, s0_ref)

    def query_loop(i, _):
        return lax.cond(((i + 1) // 2) % 2 == 0, lambda: query_tile(i, s0_ref, s1_ref),
                        lambda: query_tile(i, s1_ref, s0_ref))

    lax.fori_loop(0, nq, query_loop, 0)


def _flash(qt, k, vt, c, *, heads, dq, dk, dv, tq):
    b, s, _ = k.shape
    use_c = c is not None
    in_specs = [pl.BlockSpec((None, dq, s), lambda bi, h: (bi, h, 0)),
                pl.BlockSpec((None, s, dk), lambda bi, h: (bi, 0, h)),
                pl.BlockSpec((None, dv, s), lambda bi, h: (bi, h, 0))]
    args = [qt, k, vt]
    scratch = [pltpu.VMEM((dv, tq), F32), pltpu.VMEM((tq, tq), F32), pltpu.VMEM((tq, tq), F32)]
    if use_c:
        in_specs.append(pl.BlockSpec((None, None, 1, s), lambda bi, h: (bi, h, 0, 0)))
        args.append(c)
        scratch.append(pltpu.VMEM((s, LANES), BF16))
    return pl.pallas_call(
        functools.partial(_flash_kernel, tq=tq, use_c=use_c),
        grid=(b, heads),
        in_specs=in_specs,
        out_specs=pl.BlockSpec((None, s, dv), lambda bi, h: (bi, 0, h)),
        out_shape=jax.ShapeDtypeStruct((b, s, heads * dv), BF16),
        scratch_shapes=scratch,
        compiler_params=_params("parallel", "parallel"),
        name="flash_fox" if use_c else "flash_mla",
    )(*args)


def _dilated_kernel(q_ref, kc_ref, kp_ref, vc_ref, vp_ref, o_ref, lse_ref, qf, kf, vf, *, dil, tn):
    i = pl.program_id(1)
    per_class = tn // dil
    qf[...] = q_ref[...].astype(F32)
    kf[0:tn, :] = kp_ref[...].astype(F32)
    kf[tn:2 * tn, :] = kc_ref[...].astype(F32)
    vf[0:tn, :] = vp_ref[...].astype(F32)
    vf[tn:2 * tn, :] = vc_ref[...].astype(F32)
    row = lax.broadcasted_iota(jnp.int32, (DIL_SPAN, 2 * DIL_SPAN), 0)
    col = lax.broadcasted_iota(jnp.int32, (DIL_SPAN, 2 * DIL_SPAN), 1)
    back = row + DIL_SPAN - col
    in_band = jnp.where(back >= 0, jnp.where(back <= DIL_SPAN, 0.0, NEG), NEG)
    first_band = jnp.where(col >= jnp.where(i > 0, 0, DIL_SPAN), in_band, NEG)
    for r in range(dil):
        k_r = jnp.concatenate([kf[pl.ds(tn - DIL_SPAN * dil + r, DIL_SPAN, stride=dil), :],
                               kf[pl.ds(tn + r, per_class, stride=dil), :]], axis=0).astype(BF16)
        v_r = jnp.concatenate([vf[pl.ds(tn - DIL_SPAN * dil + r, DIL_SPAN, stride=dil), :],
                               vf[pl.ds(tn + r, per_class, stride=dil), :]], axis=0).astype(BF16)
        q_r = qf[pl.ds(r, per_class, stride=dil), :].astype(BF16)
        for a in range(per_class // DIL_SPAN):
            lo = a * DIL_SPAN
            s = _dot_nt(q_r[lo:lo + DIL_SPAN], k_r[lo:lo + 2 * DIL_SPAN]) + (first_band if a == 0 else in_band)
            m = jnp.max(s, axis=-1, keepdims=True)
            p = jnp.exp(s - m)
            l = jnp.sum(p, axis=-1, keepdims=True)
            rows = pl.ds(r + lo * dil, DIL_SPAN, stride=dil)
            o_ref[rows, :] = _dot(p.astype(BF16), v_r[lo:lo + 2 * DIL_SPAN]) / l
            lse_ref[rows, :] = jnp.broadcast_to(m + jnp.log(l), (DIL_SPAN, LANES))


def _dilated_group(dqkv3, g, dil, tn):
    batch, seq, _ = dqkv3.shape
    assert tn % (DIL_SPAN * dil) == 0
    blk = lambda col, prev: pl.BlockSpec(
        (None, tn, HEAD_DIM), lambda b, i, h: (b, jnp.maximum(i - 1, 0) if prev else i, col * DIL_HEADS + h))
    out = pl.BlockSpec((None, tn, HEAD_DIM), lambda b, i, h: (b, i, h))
    o, lse = pl.pallas_call(
        functools.partial(_dilated_kernel, dil=dil, tn=tn),
        grid=(batch, seq // tn, DIL_HEADS),
        in_specs=[blk(g, False), blk(3, False), blk(3, True), blk(4, False), blk(4, True)],
        out_specs=[out, out],
        out_shape=[jax.ShapeDtypeStruct((batch, seq, DIL_WIDTH), F32)] * 2,
        scratch_shapes=[pltpu.VMEM((tn, HEAD_DIM), F32), pltpu.VMEM((2 * tn, HEAD_DIM), F32),
                        pltpu.VMEM((2 * tn, HEAD_DIM), F32)],
        compiler_params=_params("parallel", "parallel", "parallel"),
        name=f"dilated_{dil}",
    )(dqkv3, dqkv3, dqkv3, dqkv3, dqkv3)
    return o.reshape(batch * seq, DIL_WIDTH), lse.reshape(batch * seq, DIL_WIDTH)


def _even_out_kernel(om_ref, o0, o1, o2, l0, l1, l2, wm_ref, wd_ref, x_ref, g_ref, b_ref, o_ref, ob_ref):
    ls = [l0[...], l1[...], l2[...]]
    mx = jnp.maximum(jnp.maximum(ls[0], ls[1]), ls[2])
    es = [jnp.exp(v - mx) for v in ls]
    den = es[0] + es[1] + es[2]
    o_dil = ((es[0] / den) * o0[...] + (es[1] / den) * o1[...] + (es[2] / den) * o2[...]).astype(BF16)
    y = _dot(om_ref[...], wm_ref[...]) + _dot(o_dil, wd_ref[...])
    out = _layer_norm(ALPHA * x_ref[...] + y, g_ref[...], b_ref[...])
    o_ref[...] = out
    ob_ref[...] = out.astype(BF16)


def _even_out(o_mla, outs, lses, w_mla, w_dil, x, g, b, tm):
    m = x.shape[0]
    row = lambda width: pl.BlockSpec((tm, width), lambda i: (i, 0))
    full = lambda shape: pl.BlockSpec(shape, lambda i: (0, 0))
    return pl.pallas_call(
        _even_out_kernel,
        grid=(m // tm,),
        in_specs=[row(o_mla.shape[1])] + [row(DIL_WIDTH)] * 6
        + [full(w_mla.shape), full(w_dil.shape), row(D_MODEL), full(g.shape), full(b.shape)],
        out_specs=[row(D_MODEL), row(D_MODEL)],
        out_shape=[jax.ShapeDtypeStruct((m, D_MODEL), F32), jax.ShapeDtypeStruct((m, D_MODEL), BF16)],
        compiler_params=_params("parallel"),
        name="even_out_ln",
    )(o_mla, *outs, *lses, w_mla, w_dil, x, g, b)


def _odd_out_kernel(a_ref, w_ref, x_ref, g_ref, b_ref, o_ref):
    o_ref[...] = _layer_norm(ALPHA * x_ref[...] + _dot(a_ref[...], w_ref[...]), g_ref[...], b_ref[...])


def _odd_out(a, w, x, g, b, tm):
    m = x.shape[0]
    row = lambda width: pl.BlockSpec((tm, width), lambda i: (i, 0))
    full = lambda shape: pl.BlockSpec(shape, lambda i: (0, 0))
    return pl.pallas_call(
        _odd_out_kernel,
        grid=(m // tm,),
        in_specs=[row(a.shape[1]), full(w.shape), row(D_MODEL), full(g.shape), full(b.shape)],
        out_specs=row(D_MODEL),
        out_shape=jax.ShapeDtypeStruct((m, D_MODEL), F32),
        compiler_params=_params("parallel"),
        name="odd_out_ln",
    )(a, w, x, g, b)


def _router_kernel(x_ref, rwh_ref, rwl_ref, rb_ref, route_ref, cnt_ref, carry_ref):
    @pl.when(pl.program_id(0) == 0)
    def _():
        carry_ref[...] = jnp.zeros_like(carry_ref)

    x = x_ref[...]
    xh = x.astype(BF16)
    xl = (x - xh.astype(F32)).astype(BF16)
    logits = _dot(xh, rwh_ref[...]) + (_dot(xh, rwl_ref[...]) + _dot(xl, rwh_ref[...])) + rb_ref[...]
    tm = logits.shape[0]
    lane = lax.broadcasted_iota(jnp.int32, (tm, LANES), 1)
    l1 = jnp.max(logits, axis=-1, keepdims=True)
    i1 = jnp.min(jnp.where(logits == l1, lane, LANES), axis=-1, keepdims=True)
    rest = jnp.where(lane == i1, NEG, logits)
    l2 = jnp.max(rest, axis=-1, keepdims=True)
    i2 = jnp.min(jnp.where(rest == l2, lane, LANES), axis=-1, keepdims=True)
    e = jnp.exp(l2 - l1)
    w1 = 1.0 / (1.0 + e)
    w2 = e / (1.0 + e)
    hot1 = (lane == i1).astype(F32)
    hot2 = (lane == i2).astype(F32)
    cnt = hot1 + hot2
    strict = (lax.broadcasted_iota(jnp.int32, (tm, tm), 1) < lax.broadcasted_iota(jnp.int32, (tm, tm), 0)).astype(BF16)
    before = _dot(strict, cnt.astype(BF16)) + carry_ref[...]
    r1 = jnp.sum(before * hot1, axis=-1, keepdims=True)
    r2 = jnp.sum(before * hot2, axis=-1, keepdims=True)
    vals = (i1.astype(F32), i2.astype(F32), w1, w2, r1, r2)
    route = jnp.zeros((tm, LANES), F32)
    for idx, val in enumerate(vals):
        route = jnp.where(lane == idx, val, route)
    route_ref[...] = route
    total = carry_ref[...] + jnp.sum(cnt, axis=0, keepdims=True)
    carry_ref[...] = total
    cnt_ref[...] = jnp.broadcast_to(total, cnt_ref.shape)


def _router(x, rw, rb, tm):
    m = x.shape[0]
    full = lambda shape: pl.BlockSpec(shape, lambda i: (0, 0))
    rw_hi = rw.astype(BF16)
    rw_lo = (rw - rw_hi.astype(F32)).astype(BF16)
    return pl.pallas_call(
        _router_kernel,
        grid=(m // tm,),
        in_specs=[pl.BlockSpec((tm, D_MODEL), lambda i: (i, 0)), full(rw.shape), full(rw.shape), full(rb.shape)],
        out_specs=[pl.BlockSpec((tm, LANES), lambda i: (i, 0)), pl.BlockSpec((8, LANES), lambda i: (0, 0))],
        out_shape=[jax.ShapeDtypeStruct((m, LANES), F32), jax.ShapeDtypeStruct((8, LANES), F32)],
        scratch_shapes=[pltpu.VMEM((1, LANES), F32)],
        compiler_params=_params("arbitrary"),
        name="moe_router",
    )(x, rw_hi, rw_lo, rb)


def _silu_mul(g, u):
    return g * (1.0 / (1.0 + jnp.exp(-g))) * u


def _ffn_up_kernel(x_ref, wg_ref, wu_ref, o_ref, wgb_ref, wub_ref):
    @pl.when(pl.program_id(1) == 0)
    def _():
        wgb_ref[...] = wg_ref[...].astype(BF16)
        wub_ref[...] = wu_ref[...].astype(BF16)

    x = x_ref[...]
    o_ref[...] = _silu_mul(_dot(x, wgb_ref[...]), _dot(x, wub_ref[...])).astype(BF16)


def _ffn_up(xb, wg, wu, tm, tn):
    m = xb.shape[0]
    return pl.pallas_call(
        _ffn_up_kernel,
        grid=(D_FF // tn, m // tm),
        in_specs=[pl.BlockSpec((tm, D_MODEL), lambda j, i: (i, 0)),
                  pl.BlockSpec((D_MODEL, tn), lambda j, i: (0, j)),
                  pl.BlockSpec((D_MODEL, tn), lambda j, i: (0, j))],
        out_specs=pl.BlockSpec((tm, tn), lambda j, i: (i, j)),
        out_shape=jax.ShapeDtypeStruct((m, D_FF), BF16),
        scratch_shapes=[pltpu.VMEM((D_MODEL, tn), BF16), pltpu.VMEM((D_MODEL, tn), BF16)],
        compiler_params=_params("arbitrary", "arbitrary"),
        name="ffn_up",
    )(xb, wg, wu)


def _ffn_down_ln_kernel(h_ref, w_ref, x_ref, g_ref, b_ref, o_ref, ob_ref, acc_ref):
    k = pl.program_id(1)

    @pl.when(k == 0)
    def _():
        acc_ref[...] = jnp.zeros_like(acc_ref)

    acc_ref[...] += _dot(h_ref[...], w_ref[...])

    @pl.when(k == pl.num_programs(1) - 1)
    def _():
        out = _layer_norm(ALPHA * x_ref[...] + acc_ref[...], g_ref[...], b_ref[...])
        o_ref[...] = out
        ob_ref[...] = out.astype(BF16)


def _ffn_down_ln(h, wd, x, g, b, tm, tk):
    m = x.shape[0]
    row = pl.BlockSpec((tm, D_MODEL), lambda i, k: (i, 0))
    vec = pl.BlockSpec((1, D_MODEL), lambda i, k: (0, 0))
    return pl.pallas_call(
        _ffn_down_ln_kernel,
        grid=(m // tm, D_FF // tk),
        in_specs=[pl.BlockSpec((tm, tk), lambda i, k: (i, k)), pl.BlockSpec((tk, D_MODEL), lambda i, k: (k, 0)),
                  row, vec, vec],
        out_specs=[row, row],
        out_shape=[jax.ShapeDtypeStruct((m, D_MODEL), F32), jax.ShapeDtypeStruct((m, D_MODEL), BF16)],
        scratch_shapes=[pltpu.VMEM((tm, D_MODEL), F32)],
        compiler_params=_params("parallel", "arbitrary"),
        name="ffn_down_ln",
    )(h, wd, x, g, b)


def _proj_kernel(x_ref, w_ref, o_ref):
    o_ref[...] = _dot(x_ref[...], w_ref[...]).astype(BF16)


def _proj(xb, w, tm, tn):
    m = xb.shape[0]
    n = w.shape[1]
    return pl.pallas_call(
        _proj_kernel,
        grid=(n // tn, m // tm),
        in_specs=[pl.BlockSpec((tm, D_MODEL), lambda j, i: (i, 0)), pl.BlockSpec((D_MODEL, tn), lambda j, i: (0, j))],
        out_specs=pl.BlockSpec((tm, tn), lambda j, i: (i, j)),
        out_shape=jax.ShapeDtypeStruct((m, n), BF16),
        compiler_params=_params("parallel", "parallel"),
        name="proj",
    )(xb, w)


def _proj_t_kernel(w_ref, x_ref, o_ref, *, scale):
    o_ref[...] = (_dot_nt(w_ref[...], x_ref[...]) * scale).astype(BF16)


def _proj_t(xb3, wt, scale, tm, tn):
    b, s, _ = xb3.shape
    n = wt.shape[0]
    return pl.pallas_call(
        functools.partial(_proj_t_kernel, scale=scale),
        grid=(n // tn, b, s // tm),
        in_specs=[pl.BlockSpec((tn, D_MODEL), lambda j, bi, i: (j, 0)),
                  pl.BlockSpec((None, tm, D_MODEL), lambda j, bi, i: (bi, i, 0))],
        out_specs=pl.BlockSpec((None, tn, tm), lambda j, bi, i: (bi, j, i)),
        out_shape=jax.ShapeDtypeStruct((b, n, s), BF16),
        compiler_params=_params("parallel", "parallel", "parallel"),
        name="proj_t",
    )(wt, xb3)


def _fgate_kernel(x_ref, w_ref, b_ref, c_ref, carry_ref):
    @pl.when(pl.program_id(1) == 0)
    def _():
        carry_ref[...] = jnp.zeros_like(carry_ref)

    z = _dot(x_ref[...], w_ref[...]) + b_ref[...]
    log_f = jnp.minimum(z, 0.0) - jnp.log(1.0 + jnp.exp(-jnp.abs(z)))
    tm = z.shape[0]
    tri = (lax.broadcasted_iota(jnp.int32, (tm, tm), 1) <= lax.broadcasted_iota(jnp.int32, (tm, tm), 0)).astype(F32)
    c = jnp.dot(tri, log_f, preferred_element_type=F32, precision=lax.Precision.HIGHEST) + carry_ref[...]
    c_ref[...] = c
    carry_ref[...] = c[tm - 1:tm, :]


def _fgate(xb3, wf, bf, tm):
    b, s, _ = xb3.shape
    return pl.pallas_call(
        _fgate_kernel,
        grid=(b, s // tm),
        in_specs=[pl.BlockSpec((None, tm, D_MODEL), lambda bi, i: (bi, i, 0)),
                  pl.BlockSpec(wf.shape, lambda bi, i: (0, 0)), pl.BlockSpec(bf.shape, lambda bi, i: (0, 0))],
        out_specs=pl.BlockSpec((None, tm, LANES), lambda bi, i: (bi, i, 0)),
        out_shape=jax.ShapeDtypeStruct((b, s, LANES), F32),
        scratch_shapes=[pltpu.VMEM((1, LANES), F32)],
        compiler_params=_params("parallel", "arbitrary"),
        name="fox_gate",
    )(xb3, wf, bf)


def _dispatch_kernel(fill_ref, pos_ref, x_ref, xs_ref, zero_ref, sem, zero_sem):
    tm = x_ref.shape[0]

    @pl.when(pl.program_id(0) == 0)
    def _():
        zero_ref[...] = jnp.zeros_like(zero_ref)
        rows = zero_ref.shape[0]

        def fill(f):
            start = pl.multiple_of(jnp.maximum(fill_ref[f], 0), rows)
            return pltpu.make_async_copy(zero_ref, xs_ref.at[pl.ds(start, rows), :], zero_sem)

        for f in range(fill_ref.shape[0]):
            @pl.when(fill_ref[f] >= 0)
            def _():
                fill(f).start()

        for f in range(fill_ref.shape[0]):
            @pl.when(fill_ref[f] >= 0)
            def _():
                fill(f).wait()

    def copy(r, k):
        return pltpu.make_async_copy(x_ref.at[pl.ds(r, 1), :], xs_ref.at[pl.ds(pos_ref[0, 0, 2 * r + k], 1), :], sem)

    def start(r, _):
        copy(r, 0).start()
        copy(r, 1).start()
        return 0

    def wait(r, _):
        copy(r, 0).wait()
        copy(r, 1).wait()
        return 0

    lax.fori_loop(0, tm, start, 0, unroll=DMA_UNROLL)
    lax.fori_loop(0, tm, wait, 0, unroll=DMA_UNROLL)


def _dispatch(x, pos, fill_rows, n_rows, tile, tm):
    m, width = x.shape
    pos3 = pos.reshape(m // tm, 1, 2 * tm)
    return pl.pallas_call(
        _dispatch_kernel,
        grid_spec=pltpu.PrefetchScalarGridSpec(
            num_scalar_prefetch=1,
            grid=(m // tm,),
            in_specs=[pl.BlockSpec((1, 1, 2 * tm), lambda i, fill: (i, 0, 0), memory_space=pltpu.SMEM),
                      pl.BlockSpec((tm, width), lambda i, fill: (i, 0))],
            out_specs=pl.BlockSpec(memory_space=pl.ANY),
            scratch_shapes=[pltpu.VMEM((tile, width), x.dtype), pltpu.SemaphoreType.DMA(()),
                            pltpu.SemaphoreType.DMA(())],
        ),
        out_shape=jax.ShapeDtypeStruct((n_rows, width), x.dtype),
        compiler_params=_params("arbitrary"),
        name="moe_dispatch",
    )(fill_rows, pos3, x)


def _moe_up_kernel(te_ref, nu_ref, nx_ref, x_ref, wg_hbm, wu_hbm, o_ref, sg_ref, su_ref, wb_ref, sems, *, tn):
    groups = [(lo, min(MXU_COLS, tn - lo)) for lo in range(0, tn, MXU_COLS)]
    j = pl.program_id(0)
    t = pl.program_id(1)
    live = t < nu_ref[0]
    fresh = jnp.logical_or(t == 0, te_ref[t] != te_ref[jnp.maximum(t - 1, 0)])

    def weight_copies(e, jj):
        cols = pl.ds(pl.multiple_of(jj * tn, LANES), tn)
        return (pltpu.make_async_copy(wg_hbm.at[e, :, cols], sg_ref, sems.at[0]),
                pltpu.make_async_copy(wu_hbm.at[e, :, cols], su_ref, sems.at[1]))

    def start(e, jj):
        for c in weight_copies(e, jj):
            c.start()

    @pl.when(jnp.logical_and(j == 0, t == 0))
    def _():
        start(te_ref[0], 0)

    @pl.when(jnp.logical_and(live, fresh))
    def _():
        for c in weight_copies(te_ref[t], j):
            c.wait()
        def cast_rows(c, _):
            rows = pl.ds(pl.multiple_of(c * 256, 256), 256)
            for lo, width in groups:
                wb_ref[rows, 2 * lo:2 * lo + width] = sg_ref[rows, lo:lo + width].astype(BF16)
                wb_ref[rows, 2 * lo + width:2 * lo + 2 * width] = su_ref[rows, lo:lo + width].astype(BF16)
            return 0

        lax.fori_loop(0, D_MODEL // 256, cast_rows, 0)
        nxt = nx_ref[t]

        @pl.when(nxt >= 0)
        def _():
            start(nxt, j)

        @pl.when(jnp.logical_and(nxt < 0, j + 1 < pl.num_programs(0)))
        def _():
            start(te_ref[0], j + 1)

    @pl.when(live)
    def _():
        x = x_ref[...].astype(BF16)
        for lo, width in groups:
            gu = _dot(x, wb_ref[:, 2 * lo:2 * lo + 2 * width])
            o_ref[:, lo:lo + width] = _silu_mul(gu[:, :width], gu[:, width:]).astype(BF16)

    @pl.when(jnp.logical_not(live))
    def _():
        o_ref[...] = jnp.zeros_like(o_ref)


def _moe_up(xs, wg, wu, tile_expert, n_used, next_expert, tm, tn):
    p = xs.shape[0]
    live = lambda t, nu: jnp.minimum(t, nu[0] - 1)
    return pl.pallas_call(
        functools.partial(_moe_up_kernel, tn=tn),
        grid_spec=pltpu.PrefetchScalarGridSpec(
            num_scalar_prefetch=3,
            grid=(D_FF // tn, p // tm),
            in_specs=[pl.BlockSpec((tm, D_MODEL), lambda j, t, te, nu, nx: (live(t, nu), 0)),
                      pl.BlockSpec(memory_space=pl.ANY), pl.BlockSpec(memory_space=pl.ANY)],
            out_specs=pl.BlockSpec((tm, tn), lambda j, t, te, nu, nx: (t, j)),
            scratch_shapes=[pltpu.VMEM((D_MODEL, tn), F32), pltpu.VMEM((D_MODEL, tn), F32),
                            pltpu.VMEM((D_MODEL, 2 * tn), BF16), pltpu.SemaphoreType.DMA((2,))],
        ),
        out_shape=jax.ShapeDtypeStruct((p, D_FF), BF16),
        compiler_params=_params("arbitrary", "arbitrary"),
        name="moe_up",
    )(tile_expert, n_used, next_expert, xs, wg, wu)


def _moe_down_kernel(te_ref, nu_ref, h_ref, w_ref, o_ref, wb_ref):
    t = pl.program_id(1)
    live = t < nu_ref[0]
    fresh = jnp.logical_or(t == 0, te_ref[t] != te_ref[jnp.maximum(t - 1, 0)])

    @pl.when(jnp.logical_and(live, fresh))
    def _():
        wb_ref[...] = w_ref[...].astype(BF16)

    @pl.when(live)
    def _():
        o_ref[...] = _dot(h_ref[...], wb_ref[...])

    @pl.when(jnp.logical_not(live))
    def _():
        o_ref[...] = jnp.zeros_like(o_ref)


def _moe_down(hs, wd, tile_expert, n_used, tm, tn):
    p = hs.shape[0]
    live = lambda t, nu: jnp.minimum(t, nu[0] - 1)
    return pl.pallas_call(
        _moe_down_kernel,
        grid_spec=pltpu.PrefetchScalarGridSpec(
            num_scalar_prefetch=2,
            grid=(D_MODEL // tn, p // tm),
            in_specs=[pl.BlockSpec((tm, D_FF), lambda j, t, te, nu: (live(t, nu), 0)),
                      pl.BlockSpec((None, D_FF, tn), lambda j, t, te, nu: (te[live(t, nu)], 0, j))],
            out_specs=pl.BlockSpec((tm, tn), lambda j, t, te, nu: (t, j)),
            scratch_shapes=[pltpu.VMEM((D_FF, tn), BF16)],
        ),
        out_shape=jax.ShapeDtypeStruct((p, D_MODEL), F32),
        compiler_params=_params("arbitrary", "arbitrary"),
        name="moe_down",
    )(tile_expert, n_used, hs, wd)


def _combine_ln_kernel(pos_ref, ys_ref, x_ref, route_ref, g_ref, b_ref, o_ref, buf0, buf1, sem):
    tm = x_ref.shape[0]

    def copies(r):
        c0 = pltpu.make_async_copy(ys_ref.at[pl.ds(pos_ref[0, 0, 2 * r], 1), :], buf0.at[pl.ds(r, 1), :], sem)
        c1 = pltpu.make_async_copy(ys_ref.at[pl.ds(pos_ref[0, 0, 2 * r + 1], 1), :], buf1.at[pl.ds(r, 1), :], sem)
        return c0, c1

    def start(r, _):
        c0, c1 = copies(r)
        c0.start()
        c1.start()
        return 0

    def wait(r, _):
        c0, c1 = copies(r)
        c0.wait()
        c1.wait()
        return 0

    lax.fori_loop(0, tm, start, 0, unroll=DMA_UNROLL)
    lax.fori_loop(0, tm, wait, 0, unroll=DMA_UNROLL)
    route = route_ref[...]
    y = route[:, 2:3] * buf0[...] + route[:, 3:4] * buf1[...]
    o_ref[...] = _layer_norm(ALPHA * x_ref[...] + y, g_ref[...], b_ref[...])


def _combine_ln(ys, pos, x, route, g, b, tm):
    m = x.shape[0]
    pos3 = pos.reshape(m // tm, 1, 2 * tm)
    vec = pl.BlockSpec((1, D_MODEL), lambda i: (0, 0))
    return pl.pallas_call(
        _combine_ln_kernel,
        grid=(m // tm,),
        in_specs=[pl.BlockSpec((1, 1, 2 * tm), lambda i: (i, 0, 0), memory_space=pltpu.SMEM),
                  pl.BlockSpec(memory_space=pl.ANY),
                  pl.BlockSpec((tm, D_MODEL), lambda i: (i, 0)),
                  pl.BlockSpec((tm, LANES), lambda i: (i, 0)), vec, vec],
        out_specs=pl.BlockSpec((tm, D_MODEL), lambda i: (i, 0)),
        out_shape=jax.ShapeDtypeStruct((m, D_MODEL), F32),
        scratch_shapes=[pltpu.VMEM((tm, D_MODEL), F32), pltpu.VMEM((tm, D_MODEL), F32), pltpu.SemaphoreType.DMA(())],
        compiler_params=_params("arbitrary"),
        name="moe_combine_ln",
    )(pos3, ys, x, route, g, b)


def _rope_tables(seq):
    def angles(dim):
        inv_freq = 1.0 / (ROPE_THETA ** (jnp.arange(0, dim, 2, dtype=F32) / dim))
        ang = jnp.arange(seq, dtype=F32)[:, None] * inv_freq[None, :]
        return jnp.cos(ang), jnp.sin(ang)

    c128, s128 = angles(HEAD_DIM)
    cos_f = jnp.concatenate([c128, c128], axis=-1)
    sin_f = jnp.concatenate([-s128, s128], axis=-1)
    c64, s64 = angles(MLA_ROPE_DIM)
    z32 = jnp.zeros_like(c64)
    cos_t = jnp.concatenate([c64, c64, z32, z32], axis=-1)
    sin_a = jnp.concatenate([-s64, z32, z32, z32], axis=-1)
    sin_b = jnp.concatenate([z32, s64, z32, z32], axis=-1)
    return cos_f, sin_f, cos_t, sin_a, sin_b, c64.T, s64.T


def _pad_cols(a, width):
    return jnp.pad(a, ((0, 0), (0, width - a.shape[1])))


def _row(v):
    return v.reshape(1, -1).astype(F32)


def _even_layer(x, batch, seq, tables, w_in, q_norm, w_q_b, kv_norm, w_kv_b, w_out, ln1_g, ln1_b,
                w_gate, w_up, w_down, ln2_g, ln2_b):
    cos_f, sin_f, cos_t, sin_a, sin_b, cos_tt, sin_tt = tables
    tm = min(ROW_TILE, seq)
    tw = min(WIDE_ROW_TILE, seq)
    wa = jnp.concatenate([w_in[:, OFF_CKV:OFF_KROPE], _pad_cols(w_in[:, OFF_KROPE:OFF_DQ], LANES),
                          _pad_cols(w_in[:, OFF_CQ:OFF_CKV], MLA_Q_RANK_PAD)], axis=1).astype(BF16)
    qg = _pad_cols(_row(q_norm), MLA_Q_RANK_PAD)
    wq = jnp.pad(w_q_b.reshape(MLA_Q_RANK, MLA_HEADS, MLA_QK_DIM),
                 ((0, MLA_Q_RANK_PAD - MLA_Q_RANK), (0, 0), (0, MLA_QK_PAD - MLA_QK_DIM)))
    wqt = wq.reshape(MLA_Q_RANK_PAD, MLA_HEADS * MLA_QK_PAD).T.astype(BF16)
    wkv3 = w_kv_b.reshape(MLA_KV_RANK, MLA_HEADS, MLA_NOPE_DIM + MLA_V_DIM)
    wk = wkv3[:, :, :MLA_NOPE_DIM].reshape(MLA_KV_RANK, -1).astype(BF16)
    wvt = wkv3[:, :, MLA_NOPE_DIM:].reshape(MLA_KV_RANK, -1).T.astype(BF16)
    qt_mla, k_mla, vt_mla, xb3 = _mla_prep(x.reshape(batch, seq, D_MODEL), wa, _row(kv_norm), qg, wqt, wk, wvt,
                                           cos_t, sin_a, sin_b, cos_tt, sin_tt, tm)
    xb = xb3.reshape(batch * seq, D_MODEL)
    tq = min(FLASH_TQ, seq)
    o_mla = _flash(qt_mla, k_mla, vt_mla, None, heads=MLA_HEADS, dq=MLA_QK_PAD, dk=MLA_QK_PAD, dv=MLA_V_DIM,
                   tq=tq)
    o_mla = o_mla.reshape(batch * seq, -1)

    dqkv = _dqkv(xb, w_in[:, OFF_DQ:].astype(BF16), cos_f, sin_f, seq, tw)
    dqkv3 = dqkv.reshape(batch, seq, -1)
    outs, lses = [], []
    for g, (window, dil) in enumerate(DIL_PATTERNS):
        assert window == DIL_SPAN * dil
        o_g, lse_g = _dilated_group(dqkv3, g, dil, min(seq, max(DIL_TILE, DIL_SPAN * dil)))
        outs.append(o_g)
        lses.append(lse_g)
    n_mla = MLA_HEADS * MLA_V_DIM
    wo = w_out.astype(BF16)
    x1, x1b = _even_out(o_mla, outs, lses, wo[:n_mla], wo[n_mla:], x, _row(ln1_g), _row(ln1_b), tm)
    hmid = _ffn_up(x1b, w_gate, w_up, tw, FFN_COL_TILE)
    return _ffn_down_ln(hmid, w_down.astype(BF16), x1, _row(ln2_g), _row(ln2_b), tm, FFN_K_TILE)


def _odd_layer(x, xb, batch, seq, w_qkv, w_f, b_f, w_out, ln1_g, ln1_b, router_w, router_b,
               exp_w_gate, exp_w_up, exp_w_down, ln2_g, ln2_b):
    m = batch * seq
    tm = min(ROW_TILE, seq)
    tw = min(WIDE_ROW_TILE, seq)
    xb3 = xb.reshape(batch, seq, D_MODEL)
    wb = w_qkv.astype(BF16)
    qt = _proj_t(xb3, wb[:, :FOX_WIDTH].T, HEAD_DIM ** -0.5 * LOG2E, tw, QKV_COL_TILE)
    k = _proj(xb, wb[:, FOX_WIDTH:2 * FOX_WIDTH], tw, QKV_COL_TILE).reshape(batch, seq, FOX_WIDTH)
    vt = _proj_t(xb3, wb[:, 2 * FOX_WIDTH:].T, 1.0, tw, QKV_COL_TILE)
    c = _fgate(xb3, _pad_cols(w_f, LANES).astype(BF16), _pad_cols(_row(b_f), LANES), min(GATE_ROW_TILE, seq))
    c_t = jnp.transpose(c[:, :, :FOX_HEADS], (0, 2, 1)).reshape(batch, FOX_HEADS, 1, seq)
    tq = min(FLASH_TQ, seq)
    o = _flash(qt, k, vt, c_t, heads=FOX_HEADS, dq=HEAD_DIM, dk=HEAD_DIM, dv=HEAD_DIM, tq=tq)
    rb = jnp.full((1, LANES), NEG, F32).at[0, :N_EXPERTS].set(router_b.astype(F32))
    x1 = _odd_out(o.reshape(m, -1), w_out.astype(BF16), x, _row(ln1_g), _row(ln1_b), tm)
    route, counts = _router(x1, _pad_cols(router_w, LANES), rb, tm)
    tile = MOE_TILE
    n_tiles = (2 * m) // tile + N_EXPERTS
    cnt = counts[0, :N_EXPERTS].astype(jnp.int32)
    tiles_per = (cnt + tile - 1) // tile
    tile_end = jnp.cumsum(tiles_per)
    offset = (tile_end - tiles_per) * tile
    idx = route[:, 0:2].astype(jnp.int32)
    pos = (offset[idx] + route[:, 4:6].astype(jnp.int32)).reshape(-1)
    n_used = tile_end[-1:]
    tile_ids = jnp.arange(n_tiles, dtype=jnp.int32)
    tile_expert = jnp.minimum(jnp.sum((tile_end[None, :] <= tile_ids[:, None]).astype(jnp.int32), axis=1),
                              N_EXPERTS - 1)
    experts = jnp.arange(N_EXPERTS, dtype=jnp.int32)
    last_tile = jnp.where(tiles_per > 0, tile_end - 1, -1)
    unused = n_used[0] + experts
    unused = jnp.where(unused < n_tiles, unused, -1)
    fill_tiles = jnp.concatenate([last_tile, unused])
    fill_rows = jnp.where(fill_tiles >= 0, fill_tiles * tile, -1).astype(jnp.int32)
    xs = _dispatch(x1, pos, fill_rows, n_tiles * tile, tile, tm)
    later = jnp.where((experts[None, :] > experts[:, None]) & (tiles_per[None, :] > 0), experts[None, :], N_EXPERTS)
    next_nonempty = jnp.min(later, axis=1)
    next_expert = jnp.where(next_nonempty < N_EXPERTS, next_nonempty, -1)[tile_expert].astype(jnp.int32)
    hs = _moe_up(xs, exp_w_gate, exp_w_up, tile_expert, n_used, next_expert, tile, MOE_COL_TILE)
    ys = _moe_down(hs, exp_w_down, tile_expert, n_used, tile, MOE_DOWN_COL_TILE)
    return _combine_ln(ys, pos, x1, route, _row(ln2_g), _row(ln2_b), tm)


def kernel(x, ev_w_in, ev_q_norm, ev_w_q_b, ev_kv_norm, ev_w_kv_b, ev_w_out, ev_ln1_g, ev_ln1_b, ev_ffn_w_gate, ev_ffn_w_up, ev_ffn_w_down, ev_ln2_g, ev_ln2_b, od_w_qkv, od_w_f, od_b_f, od_w_out, od_ln1_g, od_ln1_b, od_router_w, od_router_b, od_exp_w_gate, od_exp_w_up, od_exp_w_down, od_ln2_g, od_ln2_b):
    batch, seq, _ = x.shape
    tables = _rope_tables(seq)
    h = x.reshape(batch * seq, D_MODEL)
    hb = None
    for layer in range(DEPTH):
        i = layer // 2
        if layer % 2 == 0:
            h, hb = _even_layer(h, batch, seq, tables, ev_w_in[i], ev_q_norm[i], ev_w_q_b[i], ev_kv_norm[i],
                                ev_w_kv_b[i], ev_w_out[i], ev_ln1_g[i], ev_ln1_b[i], ev_ffn_w_gate[i],
                                ev_ffn_w_up[i], ev_ffn_w_down[i], ev_ln2_g[i], ev_ln2_b[i])
        else:
            h = _odd_layer(h, hb, batch, seq, od_w_qkv[i], od_w_f[i], od_b_f[i], od_w_out[i], od_ln1_g[i],
                           od_ln1_b[i], od_router_w[i], od_router_b[i], od_exp_w_gate[i], od_exp_w_up[i],
                           od_exp_w_down[i], od_ln2_g[i], od_ln2_b[i])
    return h.reshape(batch, seq, D_MODEL)
```

```python
import functools

import jax
import jax.numpy as jnp
from jax import lax
from jax.experimental import pallas as pl
from jax.experimental.pallas import tpu as pltpu

F32 = jnp.float32
BF16 = jnp.bfloat16

D_MODEL = 2048
HEAD_DIM = 128
LANES = 128
MXU_COLS = 256
ROPE_THETA = 10000.0
LN_EPS = 1e-5
RMS_EPS = 1e-6

MLA_HEADS = 10
MLA_Q_RANK = 448
MLA_Q_RANK_PAD = 512
MLA_KV_RANK = 128
MLA_NOPE_DIM = 128
MLA_ROPE_DIM = 64
MLA_V_DIM = 128
MLA_QK_DIM = MLA_NOPE_DIM + MLA_ROPE_DIM
MLA_QK_PAD = 256

DIL_PATTERNS = ((128, 1), (512, 4), (2048, 16))
DIL_GROUPS = 3
DIL_HEADS = 6
DIL_WIDTH = DIL_HEADS * HEAD_DIM
DIL_SPAN = 128

OFF_CQ = 0
OFF_CKV = OFF_CQ + MLA_Q_RANK
OFF_KROPE = OFF_CKV + MLA_KV_RANK
OFF_DQ = OFF_KROPE + MLA_ROPE_DIM
OFF_DK = OFF_DQ + DIL_GROUPS * DIL_WIDTH
OFF_DV = OFF_DK + DIL_WIDTH
W_IN_COLS = OFF_DV + DIL_WIDTH

FOX_HEADS = 16
FOX_WIDTH = FOX_HEADS * HEAD_DIM

D_FF = 5632
N_EXPERTS = 8
DEPTH = 2
ALPHA = (2.0 * DEPTH) ** 0.25

NEG = -1e30
LOG2E = 1.4426950408889634
VMEM_LIMIT = 56 * 1024 * 1024

ROW_TILE = 512
WIDE_ROW_TILE = 1024
FLASH_TQ = 1024
DIL_TILE = 2048
GATE_ROW_TILE = 512
QKV_COL_TILE = 1024
FFN_COL_TILE = 512
FFN_K_TILE = 1408
MOE_TILE = 512
MOE_COL_TILE = 1408
MOE_DOWN_COL_TILE = 1024
DMA_UNROLL = 8


def _params(*sem, vmem=VMEM_LIMIT):
    return pltpu.CompilerParams(dimension_semantics=sem, vmem_limit_bytes=vmem)


def _dot(a, b):
    return jnp.dot(a, b, preferred_element_type=F32)


def _dot_nt(a, b):
    return lax.dot_general(a, b, (((1,), (1,)), ((), ())), preferred_element_type=F32)


def _layer_norm(y, g, b):
    mu = jnp.mean(y, axis=-1, keepdims=True)
    d = y - mu
    var = jnp.mean(d * d, axis=-1, keepdims=True)
    return d * lax.rsqrt(var + LN_EPS) * g + b


def _rope128(x, cos_f, sin_f):
    return x * cos_f + pltpu.roll(x, 64, 1) * sin_f


def _rope64(x, cos_t, sin_a, sin_b):
    return x * cos_t + pltpu.roll(x, 96, 1) * sin_a + pltpu.roll(x, 32, 1) * sin_b


def _mla_prep_kernel(x_ref, wa_ref, kvg_ref, qg_ref, wqt_ref, wk_ref, wvt_ref, cos_ref, sa_ref, sb_ref,
                     ct_ref, st_ref, qt_ref, k_ref, vt_ref, xb_ref):
    xb = x_ref[...].astype(BF16)
    xb_ref[...] = xb
    h = _dot(xb, wa_ref[...])
    ckv = h[:, 0:128]
    kr = h[:, 128:256]
    cq = h[:, 256:768]
    ckv_n = (ckv * lax.rsqrt(jnp.mean(ckv * ckv, axis=-1, keepdims=True) + RMS_EPS) * kvg_ref[...]).astype(BF16)
    cq_ms = jnp.sum(cq * cq, axis=-1, keepdims=True) * (1.0 / MLA_Q_RANK)
    cq_n = (cq * lax.rsqrt(cq_ms + RMS_EPS) * qg_ref[...]).astype(BF16)
    kr_r = _rope64(kr, cos_ref[...], sa_ref[...], sb_ref[...]).astype(BF16)
    scale = MLA_QK_DIM ** -0.5 * LOG2E
    qt = _dot_nt(wqt_ref[...], cq_n)
    c, s = ct_ref[...], st_ref[...]
    k_nope = _dot(ckv_n, wk_ref[...])
    for hd in range(MLA_HEADS):
        o = hd * MLA_QK_PAD
        qt_ref[o:o + 128, :] = (qt[o:o + 128] * scale).astype(BF16)
        x1, x2 = qt[o + 128:o + 160], qt[o + 160:o + 192]
        qt_ref[o + 128:o + 160, :] = ((x1 * c - x2 * s) * scale).astype(BF16)
        qt_ref[o + 160:o + 192, :] = ((x2 * c + x1 * s) * scale).astype(BF16)
        qt_ref[o + 192:o + 256, :] = jnp.zeros((64, qt.shape[1]), BF16)
        k_ref[:, o:o + 128] = k_nope[:, hd * 128:(hd + 1) * 128].astype(BF16)
        k_ref[:, o + 128:o + 256] = kr_r
    vt_ref[...] = _dot_nt(wvt_ref[...], ckv_n).astype(BF16)


def _mla_prep(x3, wa, kvg, qg, wqt, wk, wvt, cos_t, sin_a, sin_b, cos_tt, sin_tt, tm):
    b, s, _ = x3.shape
    full = lambda shape: pl.BlockSpec(shape, lambda bi, i: (0, 0))
    tab = pl.BlockSpec((tm, LANES), lambda bi, i: (i, 0))
    tab_t = pl.BlockSpec((MLA_ROPE_DIM // 2, tm), lambda bi, i: (0, i))
    wide = MLA_HEADS * MLA_QK_PAD
    vw = MLA_HEADS * MLA_V_DIM
    return pl.pallas_call(
        _mla_prep_kernel,
        grid=(b, s // tm),
        in_specs=[pl.BlockSpec((None, tm, D_MODEL), lambda bi, i: (bi, i, 0)), full(wa.shape), full(kvg.shape),
                  full(qg.shape), full(wqt.shape), full(wk.shape), full(wvt.shape), tab, tab, tab, tab_t, tab_t],
        out_specs=[pl.BlockSpec((None, wide, tm), lambda bi, i: (bi, 0, i)),
                   pl.BlockSpec((None, tm, wide), lambda bi, i: (bi, i, 0)),
                   pl.BlockSpec((None, vw, tm), lambda bi, i: (bi, 0, i)),
                   pl.BlockSpec((None, tm, D_MODEL), lambda bi, i: (bi, i, 0))],
        out_shape=[jax.ShapeDtypeStruct((b, wide, s), BF16), jax.ShapeDtypeStruct((b, s, wide), BF16),
                   jax.ShapeDtypeStruct((b, vw, s), BF16), jax.ShapeDtypeStruct((b, s, D_MODEL), BF16)],
        compiler_params=_params("parallel", "parallel"),
        name="mla_prep",
    )(x3, wa, kvg, qg, wqt, wk, wvt, cos_t, sin_a, sin_b, cos_tt, sin_tt)


def _dqkv_kernel(x_ref, w_ref, cos_ref, sin_ref, o_ref):
    j = pl.program_id(0)
    h = _dot(x_ref[...], w_ref[...])

    @pl.when(j < 4)
    def _():
        cos_f, sin_f = cos_ref[...], sin_ref[...]
        sc = jnp.where(j < 3, HEAD_DIM ** -0.5, 1.0).astype(F32)
        for hd in range(DIL_HEADS):
            sl = slice(hd * 128, (hd + 1) * 128)
            o_ref[:, sl] = (_rope128(h[:, sl], cos_f, sin_f) * sc).astype(BF16)

    @pl.when(j == 4)
    def _():
        o_ref[...] = h.astype(BF16)


def _dqkv(xb, wd, cos_f, sin_f, seq, tm):
    m = xb.shape[0]
    nrow = seq // tm
    n_col = wd.shape[1] // DIL_WIDTH
    tab = pl.BlockSpec((tm, LANES), lambda j, i: (i % nrow, 0))
    return pl.pallas_call(
        _dqkv_kernel,
        grid=(n_col, m // tm),
        in_specs=[pl.BlockSpec((tm, D_MODEL), lambda j, i: (i, 0)),
                  pl.BlockSpec((D_MODEL, DIL_WIDTH), lambda j, i: (0, j)), tab, tab],
        out_specs=pl.BlockSpec((tm, DIL_WIDTH), lambda j, i: (i, j)),
        out_shape=jax.ShapeDtypeStruct((m, wd.shape[1]), BF16),
        compiler_params=_params("parallel", "parallel"),
        name="dil_qkv",
    )(xb, wd, cos_f, sin_f)


def _flash_kernel(*refs, tq, use_c):
    if use_c:
        qt_ref, k_ref, vt_ref, c_ref, o_ref, acc_ref, s0_ref, s1_ref, kaug_ref = refs
    else:
        qt_ref, k_ref, vt_ref, o_ref, acc_ref, s0_ref, s1_ref = refs
    seq = k_ref.shape[0]
    nq = seq // tq
    if use_c:
        row = lax.broadcasted_iota(jnp.int32, (LANES, tq), 0)

        def build(j, _):
            start = pl.multiple_of(j * tq, tq)
            neg = -LOG2E * c_ref[:, pl.ds(start, tq)]
            hi = neg.astype(BF16).astype(F32)
            mid = (neg - hi).astype(BF16).astype(F32)
            lo = neg - hi - mid
            blk = jnp.where(row == 0, hi, jnp.where(row == 1, mid, jnp.where(row == 2, lo, 0.0)))
            kaug_ref[pl.ds(start, tq), :] = blk.T.astype(BF16)
            return 0

        lax.fori_loop(0, nq, build, 0)

    def scores(i, j, s_ref):
        q = qt_ref[:, pl.ds(pl.multiple_of(i * tq, tq), tq)]
        if use_c:
            ones = (lax.broadcasted_iota(jnp.int32, (LANES, tq), 0) < 3).astype(BF16)
            q = jnp.concatenate([q, ones], axis=0)
        start = pl.multiple_of(j * tq, tq)
        kt = k_ref[pl.ds(start, tq), :]
        if use_c:
            kt = jnp.concatenate([kt, kaug_ref[pl.ds(start, tq), :]], axis=1)
        s_ref[...] = _dot(kt, q)

    def update(j, s_ref, stats):
        m, l = stats
        s = s_ref[...]
        m_new = jnp.maximum(m, jnp.max(s, axis=0, keepdims=True))
        a = jnp.exp2(m - m_new)
        p = jnp.exp2(s - m_new)
        l = a * l + jnp.sum(p, axis=0, keepdims=True)
        start = pl.multiple_of(j * tq, tq)
        acc_ref[...] = a * acc_ref[...] + _dot(vt_ref[:, pl.ds(start, tq)], p.astype(BF16))
        return m_new, l

    half = tq // 2

    def scores_diag(i, s_ref):
        q = qt_ref[:, pl.ds(pl.multiple_of(i * tq, tq), tq)]
        if use_c:
            ones = (lax.broadcasted_iota(jnp.int32, (LANES, tq), 0) < 3).astype(BF16)
            q = jnp.concatenate([q, ones], axis=0)
        start = pl.multiple_of(i * tq, tq)
        kt = k_ref[pl.ds(start, tq), :]
        if use_c:
            kt = jnp.concatenate([kt, kaug_ref[pl.ds(start, tq), :]], axis=1)
        s_ref[0:half, 0:half] = _dot(kt[0:half], q[:, 0:half])
        s_ref[:, half:tq] = _dot(kt, q[:, half:tq])

    def update_diag(j, s_ref, stats):
        m, l = stats
        start = pl.multiple_of(j * tq, tq)
        parts = []
        for lo, nkeys in ((0, half), (half, tq)):
            cols = slice(lo, lo + half)
            key = lax.broadcasted_iota(jnp.int32, (nkeys, half), 0)
            qry = lax.broadcasted_iota(jnp.int32, (nkeys, half), 1) + lo
            s = jnp.where(key <= qry, s_ref[0:nkeys, cols], NEG)
            m_new = jnp.maximum(m[:, cols], jnp.max(s, axis=0, keepdims=True))
            a = jnp.exp2(m[:, cols] - m_new)
            p = jnp.exp2(s - m_new)
            parts.append((m_new, a * l[:, cols] + jnp.sum(p, axis=0, keepdims=True)))
            acc_ref[:, cols] = a * acc_ref[:, cols] + _dot(vt_ref[:, pl.ds(start, nkeys)], p.astype(BF16))
        return tuple(jnp.concatenate([parts[0][n], parts[1][n]], axis=1) for n in range(2))

    def query_tile(i, first, second):
        acc_ref[...] = jnp.zeros_like(acc_ref)

        def pair(jj, stats):
            scores(i, 2 * jj + 1, second)
            stats = update(2 * jj, first, stats)
            scores(i, 2 * jj + 2, first)
            return update(2 * jj + 1, second, stats)

        init = (jnp.full((1, tq), NEG, F32), jnp.zeros((1, tq), F32))
        stats = lax.fori_loop(0, i // 2, pair, init)
        nxt = jnp.minimum(i + 1, nq - 1)

        def odd_tail(stats):
            scores_diag(i, second)
            stats = update(i - 1, first, stats)
            scores(nxt, 0, first)
            return update_diag(i, second, stats)

        def even_tail(stats):
            scores(nxt, 0, second)
            return update_diag(i, first, stats)

        _, l = lax.cond(i % 2 == 1, odd_tail, even_tail, stats)
        o_ref[pl.ds(pl.multiple_of(i * tq, tq), tq), :] = (acc_ref[...] / l).T.astype(o_ref.dtype)
        return 0

    scores(0, 0, s0_ref)

    def query_loop(i, _):
        return lax.cond(((i + 1) // 2) % 2 == 0, lambda: query_tile(i, s0_ref, s1_ref),
                        lambda: query_tile(i, s1_ref, s0_ref))

    lax.fori_loop(0, nq, query_loop, 0)


def _flash(qt, k, vt, c, *, heads, dq, dk, dv, tq):
    b, s, _ = k.shape
    use_c = c is not None
    in_specs = [pl.BlockSpec((None, dq, s), lambda bi, h: (bi, h, 0)),
                pl.BlockSpec((None, s, dk), lambda bi, h: (bi, 0, h)),
                pl.BlockSpec((None, dv, s), lambda bi, h: (bi, h, 0))]
    args = [qt, k, vt]
    scratch = [pltpu.VMEM((dv, tq), F32), pltpu.VMEM((tq, tq), F32), pltpu.VMEM((tq, tq), F32)]
    if use_c:
        in_specs.append(pl.BlockSpec((None, None, 1, s), lambda bi, h: (bi, h, 0, 0)))
        args.append(c)
        scratch.append(pltpu.VMEM((s, LANES), BF16))
    return pl.pallas_call(
        functools.partial(_flash_kernel, tq=tq, use_c=use_c),
        grid=(b, heads),
        in_specs=in_specs,
        out_specs=pl.BlockSpec((None, s, dv), lambda bi, h: (bi, 0, h)),
        out_shape=jax.ShapeDtypeStruct((b, s, heads * dv), BF16),
        scratch_shapes=scratch,
        compiler_params=_params("parallel", "parallel"),
        name="flash_fox" if use_c else "flash_mla",
    )(*args)


def _dilated_kernel(q_ref, kc_ref, kp_ref, vc_ref, vp_ref, o_ref, lse_ref, qf, kf, vf, of, *, dil, tn):
    i = pl.program_id(1)
    per_class = tn // dil
    qf[...] = q_ref[...].astype(F32)
    kf[0:tn, :] = kp_ref[...].astype(F32)
    kf[tn:2 * tn, :] = kc_ref[...].astype(F32)
    vf[0:tn, :] = vp_ref[...].astype(F32)
    vf[tn:2 * tn, :] = vc_ref[...].astype(F32)
    row = lax.broadcasted_iota(jnp.int32, (DIL_SPAN, 2 * DIL_SPAN), 0)
    col = lax.broadcasted_iota(jnp.int32, (DIL_SPAN, 2 * DIL_SPAN), 1)
    back = row + DIL_SPAN - col
    in_band = jnp.where(back >= 0, jnp.where(back <= DIL_SPAN, 0.0, NEG), NEG)
    first_band = jnp.where(col >= jnp.where(i > 0, 0, DIL_SPAN), in_band, NEG)
    for r in range(dil):
        k_r = jnp.concatenate([kf[pl.ds(tn - DIL_SPAN * dil + r, DIL_SPAN, stride=dil), :],
                               kf[pl.ds(tn + r, per_class, stride=dil), :]], axis=0).astype(BF16)
        v_r = jnp.concatenate([vf[pl.ds(tn - DIL_SPAN * dil + r, DIL_SPAN, stride=dil), :],
                               vf[pl.ds(tn + r, per_class, stride=dil), :]], axis=0).astype(BF16)
        q_r = qf[pl.ds(r, per_class, stride=dil), :].astype(BF16)
        for a in range(per_class // DIL_SPAN):
            lo = a * DIL_SPAN
            s = _dot_nt(q_r[lo:lo + DIL_SPAN], k_r[lo:lo + 2 * DIL_SPAN]) + (first_band if a == 0 else in_band)
            m = jnp.max(s, axis=-1, keepdims=True)
            p = jnp.exp(s - m)
            l = jnp.sum(p, axis=-1, keepdims=True)
            rows = pl.ds(r + lo * dil, DIL_SPAN, stride=dil)
            of[rows, :] = _dot(p.astype(BF16), v_r[lo:lo + 2 * DIL_SPAN]) / l
            lse_ref[rows, :] = jnp.broadcast_to(m + jnp.log(l), (DIL_SPAN, LANES))
    o_ref[...] = of[...].astype(o_ref.dtype)


def _dilated_group(dqkv3, g, dil, tn):
    batch, seq, _ = dqkv3.shape
    assert tn % (DIL_SPAN * dil) == 0
    blk = lambda col, prev: pl.BlockSpec(
        (None, tn, HEAD_DIM), lambda b, i, h: (b, jnp.maximum(i - 1, 0) if prev else i, col * DIL_HEADS + h))
    out = pl.BlockSpec((None, tn, HEAD_DIM), lambda b, i, h: (b, i, h))
    o, lse = pl.pallas_call(
        functools.partial(_dilated_kernel, dil=dil, tn=tn),
        grid=(batch, seq // tn, DIL_HEADS),
        in_specs=[blk(g, False), blk(3, False), blk(3, True), blk(4, False), blk(4, True)],
        out_specs=[out, out],
        out_shape=[jax.ShapeDtypeStruct((batch, seq, DIL_WIDTH), BF16),
                   jax.ShapeDtypeStruct((batch, seq, DIL_WIDTH), F32)],
        scratch_shapes=[pltpu.VMEM((tn, HEAD_DIM), F32), pltpu.VMEM((2 * tn, HEAD_DIM), F32),
                        pltpu.VMEM((2 * tn, HEAD_DIM), F32), pltpu.VMEM((tn, HEAD_DIM), F32)],
        compiler_params=_params("parallel", "parallel", "parallel"),
        name=f"dilated_{dil}",
    )(dqkv3, dqkv3, dqkv3, dqkv3, dqkv3)
    return o.reshape(batch * seq, DIL_WIDTH), lse.reshape(batch * seq, DIL_WIDTH)


def _even_out_kernel(om_ref, o0, o1, o2, l0, l1, l2, wm_ref, wd_ref, x_ref, g_ref, b_ref, o_ref, ob_ref):
    ls = [l0[...], l1[...], l2[...]]
    mx = jnp.maximum(jnp.maximum(ls[0], ls[1]), ls[2])
    es = [jnp.exp(v - mx) for v in ls]
    den = es[0] + es[1] + es[2]
    o_dil = ((es[0] / den) * o0[...].astype(F32) + (es[1] / den) * o1[...].astype(F32)
             + (es[2] / den) * o2[...].astype(F32)).astype(BF16)
    y = _dot(om_ref[...], wm_ref[...]) + _dot(o_dil, wd_ref[...])
    out = _layer_norm(ALPHA * x_ref[...] + y, g_ref[...], b_ref[...])
    o_ref[...] = out
    ob_ref[...] = out.astype(BF16)


def _even_out(o_mla, outs, lses, w_mla, w_dil, x, g, b, tm):
    m = x.shape[0]
    row = lambda width: pl.BlockSpec((tm, width), lambda i: (i, 0))
    full = lambda shape: pl.BlockSpec(shape, lambda i: (0, 0))
    return pl.pallas_call(
        _even_out_kernel,
        grid=(m // tm,),
        in_specs=[row(o_mla.shape[1])] + [row(DIL_WIDTH)] * 6
        + [full(w_mla.shape), full(w_dil.shape), row(D_MODEL), full(g.shape), full(b.shape)],
        out_specs=[row(D_MODEL), row(D_MODEL)],
        out_shape=[jax.ShapeDtypeStruct((m, D_MODEL), F32), jax.ShapeDtypeStruct((m, D_MODEL), BF16)],
        compiler_params=_params("parallel"),
        name="even_out_ln",
    )(o_mla, *outs, *lses, w_mla, w_dil, x, g, b)


def _odd_out_kernel(a_ref, w_ref, x_ref, g_ref, b_ref, o_ref):
    o_ref[...] = _layer_norm(ALPHA * x_ref[...] + _dot(a_ref[...], w_ref[...]), g_ref[...], b_ref[...])


def _odd_out(a, w, x, g, b, tm):
    m = x.shape[0]
    row = lambda width: pl.BlockSpec((tm, width), lambda i: (i, 0))
    full = lambda shape: pl.BlockSpec(shape, lambda i: (0, 0))
    return pl.pallas_call(
        _odd_out_kernel,
        grid=(m // tm,),
        in_specs=[row(a.shape[1]), full(w.shape), row(D_MODEL), full(g.shape), full(b.shape)],
        out_specs=row(D_MODEL),
        out_shape=jax.ShapeDtypeStruct((m, D_MODEL), F32),
        compiler_params=_params("parallel"),
        name="odd_out_ln",
    )(a, w, x, g, b)


def _router_kernel(x_ref, rwh_ref, rwl_ref, rb_ref, route_ref, cnt_ref, carry_ref):
    @pl.when(pl.program_id(0) == 0)
    def _():
        carry_ref[...] = jnp.zeros_like(carry_ref)

    x = x_ref[...]
    xh = x.astype(BF16)
    xl = (x - xh.astype(F32)).astype(BF16)
    logits = _dot(xh, rwh_ref[...]) + (_dot(xh, rwl_ref[...]) + _dot(xl, rwh_ref[...])) + rb_ref[...]
    tm = logits.shape[0]
    lane = lax.broadcasted_iota(jnp.int32, (tm, LANES), 1)
    l1 = jnp.max(logits, axis=-1, keepdims=True)
    i1 = jnp.min(jnp.where(logits == l1, lane, LANES), axis=-1, keepdims=True)
    rest = jnp.where(lane == i1, NEG, logits)
    l2 = jnp.max(rest, axis=-1, keepdims=True)
    i2 = jnp.min(jnp.where(rest == l2, lane, LANES), axis=-1, keepdims=True)
    e = jnp.exp(l2 - l1)
    w1 = 1.0 / (1.0 + e)
    w2 = e / (1.0 + e)
    hot1 = (lane == i1).astype(F32)
    hot2 = (lane == i2).astype(F32)
    cnt = hot1 + hot2
    strict = (lax.broadcasted_iota(jnp.int32, (tm, tm), 1) < lax.broadcasted_iota(jnp.int32, (tm, tm), 0)).astype(BF16)
    before = _dot(strict, cnt.astype(BF16)) + carry_ref[...]
    r1 = jnp.sum(before * hot1, axis=-1, keepdims=True)
    r2 = jnp.sum(before * hot2, axis=-1, keepdims=True)
    vals = (i1.astype(F32), i2.astype(F32), w1, w2, r1, r2)
    route = jnp.zeros((tm, LANES), F32)
    for idx, val in enumerate(vals):
        route = jnp.where(lane == idx, val, route)
    route_ref[...] = route
    total = carry_ref[...] + jnp.sum(cnt, axis=0, keepdims=True)
    carry_ref[...] = total
    cnt_ref[...] = jnp.broadcast_to(total, cnt_ref.shape)


def _router(x, rw, rb, tm):
    m = x.shape[0]
    full = lambda shape: pl.BlockSpec(shape, lambda i: (0, 0))
    rw_hi = rw.astype(BF16)
    rw_lo = (rw - rw_hi.astype(F32)).astype(BF16)
    return pl.pallas_call(
        _router_kernel,
        grid=(m // tm,),
        in_specs=[pl.BlockSpec((tm, D_MODEL), lambda i: (i, 0)), full(rw.shape), full(rw.shape), full(rb.shape)],
        out_specs=[pl.BlockSpec((tm, LANES), lambda i: (i, 0)), pl.BlockSpec((8, LANES), lambda i: (0, 0))],
        out_shape=[jax.ShapeDtypeStruct((m, LANES), F32), jax.ShapeDtypeStruct((8, LANES), F32)],
        scratch_shapes=[pltpu.VMEM((1, LANES), F32)],
        compiler_params=_params("arbitrary"),
        name="moe_router",
    )(x, rw_hi, rw_lo, rb)


def _silu_mul(g, u):
    return g * (1.0 / (1.0 + jnp.exp(-g))) * u


def _ffn_up_kernel(x_ref, wg_ref, wu_ref, o_ref, wgb_ref, wub_ref):
    @pl.when(pl.program_id(1) == 0)
    def _():
        wgb_ref[...] = wg_ref[...].astype(BF16)
        wub_ref[...] = wu_ref[...].astype(BF16)

    x = x_ref[...]
    o_ref[...] = _silu_mul(_dot(x, wgb_ref[...]), _dot(x, wub_ref[...])).astype(BF16)


def _ffn_up(xb, wg, wu, tm, tn):
    m = xb.shape[0]
    return pl.pallas_call(
        _ffn_up_kernel,
        grid=(D_FF // tn, m // tm),
        in_specs=[pl.BlockSpec((tm, D_MODEL), lambda j, i: (i, 0)),
                  pl.BlockSpec((D_MODEL, tn), lambda j, i: (0, j)),
                  pl.BlockSpec((D_MODEL, tn), lambda j, i: (0, j))],
        out_specs=pl.BlockSpec((tm, tn), lambda j, i: (i, j)),
        out_shape=jax.ShapeDtypeStruct((m, D_FF), BF16),
        scratch_shapes=[pltpu.VMEM((D_MODEL, tn), BF16), pltpu.VMEM((D_MODEL, tn), BF16)],
        compiler_params=_params("arbitrary", "arbitrary"),
        name="ffn_up",
    )(xb, wg, wu)


def _ffn_down_ln_kernel(h_ref, w_ref, x_ref, g_ref, b_ref, o_ref, ob_ref, acc_ref):
    k = pl.program_id(1)

    @pl.when(k == 0)
    def _():
        acc_ref[...] = jnp.zeros_like(acc_ref)

    acc_ref[...] += _dot(h_ref[...], w_ref[...])

    @pl.when(k == pl.num_programs(1) - 1)
    def _():
        out = _layer_norm(ALPHA * x_ref[...] + acc_ref[...], g_ref[...], b_ref[...])
        o_ref[...] = out
        ob_ref[...] = out.astype(BF16)


def _ffn_down_ln(h, wd, x, g, b, tm, tk):
    m = x.shape[0]
    row = pl.BlockSpec((tm, D_MODEL), lambda i, k: (i, 0))
    vec = pl.BlockSpec((1, D_MODEL), lambda i, k: (0, 0))
    return pl.pallas_call(
        _ffn_down_ln_kernel,
        grid=(m // tm, D_FF // tk),
        in_specs=[pl.BlockSpec((tm, tk), lambda i, k: (i, k)), pl.BlockSpec((tk, D_MODEL), lambda i, k: (k, 0)),
                  row, vec, vec],
        out_specs=[row, row],
        out_shape=[jax.ShapeDtypeStruct((m, D_MODEL), F32), jax.ShapeDtypeStruct((m, D_MODEL), BF16)],
        scratch_shapes=[pltpu.VMEM((tm, D_MODEL), F32)],
        compiler_params=_params("parallel", "arbitrary"),
        name="ffn_down_ln",
    )(h, wd, x, g, b)


def _proj_kernel(x_ref, w_ref, o_ref):
    o_ref[...] = _dot(x_ref[...], w_ref[...]).astype(BF16)


def _proj(xb, w, tm, tn):
    m = xb.shape[0]
    n = w.shape[1]
    return pl.pallas_call(
        _proj_kernel,
        grid=(n // tn, m // tm),
        in_specs=[pl.BlockSpec((tm, D_MODEL), lambda j, i: (i, 0)), pl.BlockSpec((D_MODEL, tn), lambda j, i: (0, j))],
        out_specs=pl.BlockSpec((tm, tn), lambda j, i: (i, j)),
        out_shape=jax.ShapeDtypeStruct((m, n), BF16),
        compiler_params=_params("parallel", "parallel"),
        name="proj",
    )(xb, w)


def _proj_t_kernel(w_ref, x_ref, o_ref, *, scale):
    o_ref[...] = (_dot_nt(w_ref[...], x_ref[...]) * scale).astype(BF16)


def _proj_t(xb3, wt, scale, tm, tn):
    b, s, _ = xb3.shape
    n = wt.shape[0]
    return pl.pallas_call(
        functools.partial(_proj_t_kernel, scale=scale),
        grid=(n // tn, b, s // tm),
        in_specs=[pl.BlockSpec((tn, D_MODEL), lambda j, bi, i: (j, 0)),
                  pl.BlockSpec((None, tm, D_MODEL), lambda j, bi, i: (bi, i, 0))],
        out_specs=pl.BlockSpec((None, tn, tm), lambda j, bi, i: (bi, j, i)),
        out_shape=jax.ShapeDtypeStruct((b, n, s), BF16),
        compiler_params=_params("parallel", "parallel", "parallel"),
        name="proj_t",
    )(wt, xb3)


def _fgate_kernel(x_ref, w_ref, b_ref, c_ref, carry_ref):
    @pl.when(pl.program_id(1) == 0)
    def _():
        carry_ref[...] = jnp.zeros_like(carry_ref)

    z = _dot(x_ref[...], w_ref[...]) + b_ref[...]
    log_f = jnp.minimum(z, 0.0) - jnp.log(1.0 + jnp.exp(-jnp.abs(z)))
    tm = z.shape[0]
    tri = (lax.broadcasted_iota(jnp.int32, (tm, tm), 1) <= lax.broadcasted_iota(jnp.int32, (tm, tm), 0)).astype(F32)
    c = jnp.dot(tri, log_f, preferred_element_type=F32, precision=lax.Precision.HIGHEST) + carry_ref[...]
    c_ref[...] = c
    carry_ref[...] = c[tm - 1:tm, :]


def _fgate(xb3, wf, bf, tm):
    b, s, _ = xb3.shape
    return pl.pallas_call(
        _fgate_kernel,
        grid=(b, s // tm),
        in_specs=[pl.BlockSpec((None, tm, D_MODEL), lambda bi, i: (bi, i, 0)),
                  pl.BlockSpec(wf.shape, lambda bi, i: (0, 0)), pl.BlockSpec(bf.shape, lambda bi, i: (0, 0))],
        out_specs=pl.BlockSpec((None, tm, LANES), lambda bi, i: (bi, i, 0)),
        out_shape=jax.ShapeDtypeStruct((b, s, LANES), F32),
        scratch_shapes=[pltpu.VMEM((1, LANES), F32)],
        compiler_params=_params("parallel", "arbitrary"),
        name="fox_gate",
    )(xb3, wf, bf)


def _dispatch_kernel(fill_ref, pos_ref, x_ref, xs_ref, zero_ref, sem, zero_sem):
    tm = x_ref.shape[0]

    @pl.when(pl.program_id(0) == 0)
    def _():
        zero_ref[...] = jnp.zeros_like(zero_ref)
        rows = zero_ref.shape[0]

        def fill(f):
            start = pl.multiple_of(jnp.maximum(fill_ref[f], 0), rows)
            return pltpu.make_async_copy(zero_ref, xs_ref.at[pl.ds(start, rows), :], zero_sem)

        for f in range(fill_ref.shape[0]):
            @pl.when(fill_ref[f] >= 0)
            def _():
                fill(f).start()

        for f in range(fill_ref.shape[0]):
            @pl.when(fill_ref[f] >= 0)
            def _():
                fill(f).wait()

    def copy(r, k):
        return pltpu.make_async_copy(x_ref.at[pl.ds(r, 1), :], xs_ref.at[pl.ds(pos_ref[0, 0, 2 * r + k], 1), :], sem)

    def start(r, _):
        copy(r, 0).start(priority=0)
        copy(r, 1).start(priority=1)
        return 0

    def wait(r, _):
        copy(r, 0).wait()
        copy(r, 1).wait()
        return 0

    lax.fori_loop(0, tm, start, 0, unroll=DMA_UNROLL)
    lax.fori_loop(0, tm, wait, 0, unroll=DMA_UNROLL)


def _dispatch(x, pos, fill_rows, n_rows, tile, tm):
    m, width = x.shape
    pos3 = pos.reshape(m // tm, 1, 2 * tm)
    return pl.pallas_call(
        _dispatch_kernel,
        grid_spec=pltpu.PrefetchScalarGridSpec(
            num_scalar_prefetch=1,
            grid=(m // tm,),
            in_specs=[pl.BlockSpec((1, 1, 2 * tm), lambda i, fill: (i, 0, 0), memory_space=pltpu.SMEM),
                      pl.BlockSpec((tm, width), lambda i, fill: (i, 0))],
            out_specs=pl.BlockSpec(memory_space=pl.ANY),
            scratch_shapes=[pltpu.VMEM((tile, width), x.dtype), pltpu.SemaphoreType.DMA(()),
                            pltpu.SemaphoreType.DMA(())],
        ),
        out_shape=jax.ShapeDtypeStruct((n_rows, width), x.dtype),
        compiler_params=_params("arbitrary"),
        name="moe_dispatch",
    )(fill_rows, pos3, x)


def _moe_up_kernel(te_ref, nu_ref, nx_ref, x_ref, wg_hbm, wu_hbm, o_ref, sg_ref, su_ref, wb_ref, sems, *, tn):
    groups = [(lo, min(MXU_COLS, tn - lo)) for lo in range(0, tn, MXU_COLS)]
    j = pl.program_id(0)
    t = pl.program_id(1)
    live = t < nu_ref[0]
    fresh = jnp.logical_or(t == 0, te_ref[t] != te_ref[jnp.maximum(t - 1, 0)])

    def weight_copies(e, jj):
        cols = pl.ds(pl.multiple_of(jj * tn, LANES), tn)
        return (pltpu.make_async_copy(wg_hbm.at[e, :, cols], sg_ref, sems.at[0]),
                pltpu.make_async_copy(wu_hbm.at[e, :, cols], su_ref, sems.at[1]))

    def start(e, jj):
        for c in weight_copies(e, jj):
            c.start()

    @pl.when(jnp.logical_and(j == 0, t == 0))
    def _():
        start(te_ref[0], 0)

    @pl.when(jnp.logical_and(live, fresh))
    def _():
        for c in weight_copies(te_ref[t], j):
            c.wait()
        def cast_rows(c, _):
            rows = pl.ds(pl.multiple_of(c * 256, 256), 256)
            for lo, width in groups:
                wb_ref[rows, 2 * lo:2 * lo + width] = sg_ref[rows, lo:lo + width].astype(BF16)
                wb_ref[rows, 2 * lo + width:2 * lo + 2 * width] = su_ref[rows, lo:lo + width].astype(BF16)
            return 0

        lax.fori_loop(0, D_MODEL // 256, cast_rows, 0)
        nxt = nx_ref[t]

        @pl.when(nxt >= 0)
        def _():
            start(nxt, j)

        @pl.when(jnp.logical_and(nxt < 0, j + 1 < pl.num_programs(0)))
        def _():
            start(te_ref[0], j + 1)

    @pl.when(live)
    def _():
        x = x_ref[...].astype(BF16)
        for lo, width in groups:
            gu = _dot(x, wb_ref[:, 2 * lo:2 * lo + 2 * width])
            o_ref[:, lo:lo + width] = _silu_mul(gu[:, :width], gu[:, width:]).astype(BF16)

    @pl.when(jnp.logical_not(live))
    def _():
        o_ref[...] = jnp.zeros_like(o_ref)


def _moe_up(xs, wg, wu, tile_expert, n_used, next_expert, tm, tn):
    p = xs.shape[0]
    live = lambda t, nu: jnp.minimum(t, nu[0] - 1)
    return pl.pallas_call(
        functools.partial(_moe_up_kernel, tn=tn),
        grid_spec=pltpu.PrefetchScalarGridSpec(
            num_scalar_prefetch=3,
            grid=(D_FF // tn, p // tm),
            in_specs=[pl.BlockSpec((tm, D_MODEL), lambda j, t, te, nu, nx: (live(t, nu), 0)),
                      pl.BlockSpec(memory_space=pl.ANY), pl.BlockSpec(memory_space=pl.ANY)],
            out_specs=pl.BlockSpec((tm, tn), lambda j, t, te, nu, nx: (t, j)),
            scratch_shapes=[pltpu.VMEM((D_MODEL, tn), F32), pltpu.VMEM((D_MODEL, tn), F32),
                            pltpu.VMEM((D_MODEL, 2 * tn), BF16), pltpu.SemaphoreType.DMA((2,))],
        ),
        out_shape=jax.ShapeDtypeStruct((p, D_FF), BF16),
        compiler_params=_params("arbitrary", "arbitrary"),
        name="moe_up",
    )(tile_expert, n_used, next_expert, xs, wg, wu)


def _moe_down_kernel(te_ref, nu_ref, nx_ref, h_ref, w_hbm, o_ref, st_ref, wb_ref, sem, *, tn):
    j = pl.program_id(0)
    t = pl.program_id(1)
    live = t < nu_ref[0]
    fresh = jnp.logical_or(t == 0, te_ref[t] != te_ref[jnp.maximum(t - 1, 0)])

    def weight_copy(e, jj):
        cols = pl.ds(pl.multiple_of(jj * tn, LANES), tn)
        return pltpu.make_async_copy(w_hbm.at[e, :, cols], st_ref, sem)

    @pl.when(jnp.logical_and(j == 0, t == 0))
    def _():
        weight_copy(te_ref[0], 0).start()

    @pl.when(jnp.logical_and(live, fresh))
    def _():
        weight_copy(te_ref[t], j).wait()

        def cast_rows(c, _):
            rows = pl.ds(pl.multiple_of(c * 256, 256), 256)
            wb_ref[rows, :] = st_ref[rows, :].astype(BF16)
            return 0

        lax.fori_loop(0, D_FF // 256, cast_rows, 0)
        nxt = nx_ref[t]

        @pl.when(nxt >= 0)
        def _():
            weight_copy(nxt, j).start()

        @pl.when(jnp.logical_and(nxt < 0, j + 1 < pl.num_programs(0)))
        def _():
            weight_copy(te_ref[0], j + 1).start()

    @pl.when(live)
    def _():
        h = h_ref[...]
        for lo in range(0, tn, 512):
            o_ref[:, lo:lo + 512] = _dot(h, wb_ref[:, lo:lo + 512])

    @pl.when(jnp.logical_not(live))
    def _():
        o_ref[...] = jnp.zeros_like(o_ref)


def _moe_down(hs, wd, tile_expert, n_used, next_expert, tm, tn):
    p = hs.shape[0]
    live = lambda t, nu: jnp.minimum(t, nu[0] - 1)
    return pl.pallas_call(
        functools.partial(_moe_down_kernel, tn=tn),
        grid_spec=pltpu.PrefetchScalarGridSpec(
            num_scalar_prefetch=3,
            grid=(D_MODEL // tn, p // tm),
            in_specs=[pl.BlockSpec((tm, D_FF), lambda j, t, te, nu, nx: (live(t, nu), 0)),
                      pl.BlockSpec(memory_space=pl.ANY)],
            out_specs=pl.BlockSpec((tm, tn), lambda j, t, te, nu, nx: (t, j)),
            scratch_shapes=[pltpu.VMEM((D_FF, tn), F32), pltpu.VMEM((D_FF, tn), BF16),
                            pltpu.SemaphoreType.DMA(())],
        ),
        out_shape=jax.ShapeDtypeStruct((p, D_MODEL), F32),
        compiler_params=_params("arbitrary", "arbitrary"),
        name="moe_down",
    )(tile_expert, n_used, next_expert, hs, wd)


def _combine_ln_kernel(pos_ref, ys_ref, x_ref, route_ref, g_ref, b_ref, o_ref, buf0, buf1, sem):
    tm = x_ref.shape[0]

    def copies(r):
        c0 = pltpu.make_async_copy(ys_ref.at[pl.ds(pos_ref[0, 0, 2 * r], 1), :], buf0.at[pl.ds(r, 1), :], sem)
        c1 = pltpu.make_async_copy(ys_ref.at[pl.ds(pos_ref[0, 0, 2 * r + 1], 1), :], buf1.at[pl.ds(r, 1), :], sem)
        return c0, c1

    def start(r, _):
        c0, c1 = copies(r)
        c0.start(priority=0)
        c1.start(priority=1)
        return 0

    def wait(r, _):
        c0, c1 = copies(r)
        c0.wait()
        c1.wait()
        return 0

    lax.fori_loop(0, tm, start, 0, unroll=DMA_UNROLL)
    lax.fori_loop(0, tm, wait, 0, unroll=DMA_UNROLL)
    route = route_ref[...]
    y = route[:, 2:3] * buf0[...] + route[:, 3:4] * buf1[...]
    o_ref[...] = _layer_norm(ALPHA * x_ref[...] + y, g_ref[...], b_ref[...])


def _combine_ln(ys, pos, x, route, g, b, tm):
    m = x.shape[0]
    pos3 = pos.reshape(m // tm, 1, 2 * tm)
    vec = pl.BlockSpec((1, D_MODEL), lambda i: (0, 0))
    return pl.pallas_call(
        _combine_ln_kernel,
        grid=(m // tm,),
        in_specs=[pl.BlockSpec((1, 1, 2 * tm), lambda i: (i, 0, 0), memory_space=pltpu.SMEM),
                  pl.BlockSpec(memory_space=pl.ANY),
                  pl.BlockSpec((tm, D_MODEL), lambda i: (i, 0)),
                  pl.BlockSpec((tm, LANES), lambda i: (i, 0)), vec, vec],
        out_specs=pl.BlockSpec((tm, D_MODEL), lambda i: (i, 0)),
        out_shape=jax.ShapeDtypeStruct((m, D_MODEL), F32),
        scratch_shapes=[pltpu.VMEM((tm, D_MODEL), F32), pltpu.VMEM((tm, D_MODEL), F32), pltpu.SemaphoreType.DMA(())],
        compiler_params=_params("arbitrary"),
        name="moe_combine_ln",
    )(pos3, ys, x, route, g, b)


def _rope_tables(seq):
    def angles(dim):
        inv_freq = 1.0 / (ROPE_THETA ** (jnp.arange(0, dim, 2, dtype=F32) / dim))
        ang = jnp.arange(seq, dtype=F32)[:, None] * inv_freq[None, :]
        return jnp.cos(ang), jnp.sin(ang)

    c128, s128 = angles(HEAD_DIM)
    cos_f = jnp.concatenate([c128, c128], axis=-1)
    sin_f = jnp.concatenate([-s128, s128], axis=-1)
    c64, s64 = angles(MLA_ROPE_DIM)
    z32 = jnp.zeros_like(c64)
    cos_t = jnp.concatenate([c64, c64, z32, z32], axis=-1)
    sin_a = jnp.concatenate([-s64, z32, z32, z32], axis=-1)
    sin_b = jnp.concatenate([z32, s64, z32, z32], axis=-1)
    return cos_f, sin_f, cos_t, sin_a, sin_b, c64.T, s64.T


def _pad_cols(a, width):
    return jnp.pad(a, ((0, 0), (0, width - a.shape[1])))


def _row(v):
    return v.reshape(1, -1).astype(F32)


def _even_layer(x, batch, seq, tables, w_in, q_norm, w_q_b, kv_norm, w_kv_b, w_out, ln1_g, ln1_b,
                w_gate, w_up, w_down, ln2_g, ln2_b):
    cos_f, sin_f, cos_t, sin_a, sin_b, cos_tt, sin_tt = tables
    tm = min(ROW_TILE, seq)
    tw = min(WIDE_ROW_TILE, seq)
    wa = jnp.concatenate([w_in[:, OFF_CKV:OFF_KROPE], _pad_cols(w_in[:, OFF_KROPE:OFF_DQ], LANES),
                          _pad_cols(w_in[:, OFF_CQ:OFF_CKV], MLA_Q_RANK_PAD)], axis=1).astype(BF16)
    qg = _pad_cols(_row(q_norm), MLA_Q_RANK_PAD)
    wq = jnp.pad(w_q_b.reshape(MLA_Q_RANK, MLA_HEADS, MLA_QK_DIM),
                 ((0, MLA_Q_RANK_PAD - MLA_Q_RANK), (0, 0), (0, MLA_QK_PAD - MLA_QK_DIM)))
    wqt = wq.reshape(MLA_Q_RANK_PAD, MLA_HEADS * MLA_QK_PAD).T.astype(BF16)
    wkv3 = w_kv_b.reshape(MLA_KV_RANK, MLA_HEADS, MLA_NOPE_DIM + MLA_V_DIM)
    wk = wkv3[:, :, :MLA_NOPE_DIM].reshape(MLA_KV_RANK, -1).astype(BF16)
    wvt = wkv3[:, :, MLA_NOPE_DIM:].reshape(MLA_KV_RANK, -1).T.astype(BF16)
    qt_mla, k_mla, vt_mla, xb3 = _mla_prep(x.reshape(batch, seq, D_MODEL), wa, _row(kv_norm), qg, wqt, wk, wvt,
                                           cos_t, sin_a, sin_b, cos_tt, sin_tt, tm)
    xb = xb3.reshape(batch * seq, D_MODEL)
    tq = min(FLASH_TQ, seq)
    o_mla = _flash(qt_mla, k_mla, vt_mla, None, heads=MLA_HEADS, dq=MLA_QK_PAD, dk=MLA_QK_PAD, dv=MLA_V_DIM,
                   tq=tq)
    o_mla = o_mla.reshape(batch * seq, -1)

    dqkv = _dqkv(xb, w_in[:, OFF_DQ:].astype(BF16), cos_f, sin_f, seq, tw)
    dqkv3 = dqkv.reshape(batch, seq, -1)
    outs, lses = [], []
    for g, (window, dil) in enumerate(DIL_PATTERNS):
        assert window == DIL_SPAN * dil
        o_g, lse_g = _dilated_group(dqkv3, g, dil, min(seq, max(DIL_TILE, DIL_SPAN * dil)))
        outs.append(o_g)
        lses.append(lse_g)
    n_mla = MLA_HEADS * MLA_V_DIM
    wo = w_out.astype(BF16)
    x1, x1b = _even_out(o_mla, outs, lses, wo[:n_mla], wo[n_mla:], x, _row(ln1_g), _row(ln1_b), tm)
    hmid = _ffn_up(x1b, w_gate, w_up, tw, FFN_COL_TILE)
    return _ffn_down_ln(hmid, w_down.astype(BF16), x1, _row(ln2_g), _row(ln2_b), tm, FFN_K_TILE)


def _odd_layer(x, xb, batch, seq, w_qkv, w_f, b_f, w_out, ln1_g, ln1_b, router_w, router_b,
               exp_w_gate, exp_w_up, exp_w_down, ln2_g, ln2_b):
    m = batch * seq
    tm = min(ROW_TILE, seq)
    tw = min(WIDE_ROW_TILE, seq)
    xb3 = xb.reshape(batch, seq, D_MODEL)
    wb = w_qkv.astype(BF16)
    qt = _proj_t(xb3, wb[:, :FOX_WIDTH].T, HEAD_DIM ** -0.5 * LOG2E, tw, QKV_COL_TILE)
    k = _proj(xb, wb[:, FOX_WIDTH:2 * FOX_WIDTH], tw, QKV_COL_TILE).reshape(batch, seq, FOX_WIDTH)
    vt = _proj_t(xb3, wb[:, 2 * FOX_WIDTH:].T, 1.0, tw, QKV_COL_TILE)
    c = _fgate(xb3, _pad_cols(w_f, LANES).astype(BF16), _pad_cols(_row(b_f), LANES), min(GATE_ROW_TILE, seq))
    c_t = jnp.transpose(c[:, :, :FOX_HEADS], (0, 2, 1)).reshape(batch, FOX_HEADS, 1, seq)
    tq = min(FLASH_TQ, seq)
    o = _flash(qt, k, vt, c_t, heads=FOX_HEADS, dq=HEAD_DIM, dk=HEAD_DIM, dv=HEAD_DIM, tq=tq)
    rb = jnp.full((1, LANES), NEG, F32).at[0, :N_EXPERTS].set(router_b.astype(F32))
    x1 = _odd_out(o.reshape(m, -1), w_out.astype(BF16), x, _row(ln1_g), _row(ln1_b), tm)
    route, counts = _router(x1, _pad_cols(router_w, LANES), rb, tm)
    tile = MOE_TILE
    n_tiles = (2 * m) // tile + N_EXPERTS
    cnt = counts[0, :N_EXPERTS].astype(jnp.int32)
    tiles_per = (cnt + tile - 1) // tile
    tile_end = jnp.cumsum(tiles_per)
    offset = (tile_end - tiles_per) * tile
    idx = route[:, 0:2].astype(jnp.int32)
    pos = (offset[idx] + route[:, 4:6].astype(jnp.int32)).reshape(-1)
    n_used = tile_end[-1:]
    tile_ids = jnp.arange(n_tiles, dtype=jnp.int32)
    tile_expert = jnp.minimum(jnp.sum((tile_end[None, :] <= tile_ids[:, None]).astype(jnp.int32), axis=1),
                              N_EXPERTS - 1)
    experts = jnp.arange(N_EXPERTS, dtype=jnp.int32)
    last_tile = jnp.where(tiles_per > 0, tile_end - 1, -1)
    unused = n_used[0] + experts
    unused = jnp.where(unused < n_tiles, unused, -1)
    fill_tiles = jnp.concatenate([last_tile, unused])
    fill_rows = jnp.where(fill_tiles >= 0, fill_tiles * tile, -1).astype(jnp.int32)
    xs = _dispatch(x1, pos, fill_rows, n_tiles * tile, tile, tm)
    later = jnp.where((experts[None, :] > experts[:, None]) & (tiles_per[None, :] > 0), experts[None, :], N_EXPERTS)
    next_nonempty = jnp.min(later, axis=1)
    next_expert = jnp.where(next_nonempty < N_EXPERTS, next_nonempty, -1)[tile_expert].astype(jnp.int32)
    hs = _moe_up(xs, exp_w_gate, exp_w_up, tile_expert, n_used, next_expert, tile, MOE_COL_TILE)
    ys = _moe_down(hs, exp_w_down, tile_expert, n_used, next_expert, tile, MOE_DOWN_COL_TILE)
    return _combine_ln(ys, pos, x1, route, _row(ln2_g), _row(ln2_b), tm)


def kernel(x, ev_w_in, ev_q_norm, ev_w_q_b, ev_kv_norm, ev_w_kv_b, ev_w_out, ev_ln1_g, ev_ln1_b, ev_ffn_w_gate, ev_ffn_w_up, ev_ffn_w_down, ev_ln2_g, ev_ln2_b, od_w_qkv, od_w_f, od_b_f, od_w_out, od_ln1_g, od_ln1_b, od_router_w, od_router_b, od_exp_w_gate, od_exp_w_up, od_exp_w_down, od_ln2_g, od_ln2_b):
    batch, seq, _ = x.shape
    tables = _rope_tables(seq)
    h = x.reshape(batch * seq, D_MODEL)
    hb = None
    for layer in range(DEPTH):
        i = layer // 2
        if layer % 2 == 0:
            h, hb = _even_layer(h, batch, seq, tables, ev_w_in[i], ev_q_norm[i], ev_w_q_b[i], ev_kv_norm[i],
                                ev_w_kv_b[i], ev_w_out[i], ev_ln1_g[i], ev_ln1_b[i], ev_ffn_w_gate[i],
                                ev_ffn_w_up[i], ev_ffn_w_down[i], ev_ln2_g[i], ev_ln2_b[i])
        else:
            h = _odd_layer(h, hb, batch, seq, od_w_qkv[i], od_w_f[i], od_b_f[i], od_w_out[i], od_ln1_g[i],
                           od_ln1_b[i], od_router_w[i], od_router_b[i], od_exp_w_gate[i], od_exp_w_up[i],
                           od_exp_w_down[i], od_ln2_g[i], od_ln2_b[i])
    return h.reshape(batch, seq, D_MODEL)
```

```python
import functools

import jax
import jax.numpy as jnp
from jax import lax
from jax.experimental import pallas as pl
from jax.experimental.pallas import tpu as pltpu

F32 = jnp.float32
BF16 = jnp.bfloat16

D_MODEL = 2048
HEAD_DIM = 128
LANES = 128
MXU_COLS = 256
ROPE_THETA = 10000.0
LN_EPS = 1e-5
RMS_EPS = 1e-6

MLA_HEADS = 10
MLA_Q_RANK = 448
MLA_Q_RANK_PAD = 512
MLA_KV_RANK = 128
MLA_NOPE_DIM = 128
MLA_ROPE_DIM = 64
MLA_V_DIM = 128
MLA_QK_DIM = MLA_NOPE_DIM + MLA_ROPE_DIM
MLA_QK_PAD = 256

DIL_PATTERNS = ((128, 1), (512, 4), (2048, 16))
DIL_GROUPS = 3
DIL_HEADS = 6
DIL_WIDTH = DIL_HEADS * HEAD_DIM
DIL_SPAN = 128

OFF_CQ = 0
OFF_CKV = OFF_CQ + MLA_Q_RANK
OFF_KROPE = OFF_CKV + MLA_KV_RANK
OFF_DQ = OFF_KROPE + MLA_ROPE_DIM
OFF_DK = OFF_DQ + DIL_GROUPS * DIL_WIDTH
OFF_DV = OFF_DK + DIL_WIDTH
W_IN_COLS = OFF_DV + DIL_WIDTH

FOX_HEADS = 16
FOX_WIDTH = FOX_HEADS * HEAD_DIM

D_FF = 5632
N_EXPERTS = 8
DEPTH = 2
ALPHA = (2.0 * DEPTH) ** 0.25

NEG = -1e30
LOG2E = 1.4426950408889634
VMEM_LIMIT = 56 * 1024 * 1024

ROW_TILE = 512
WIDE_ROW_TILE = 1024
FLASH_TQ = 1024
DIL_TILE = 2048
GATE_ROW_TILE = 512
QKV_COL_TILE = 1024
FFN_COL_TILE = 512
FFN_K_TILE = 1024
MOE_TILE = 512
MOE_COL_TILE = 1408
MOE_DOWN_COL_TILE = 1024
DMA_UNROLL = 8


def _params(*sem, vmem=VMEM_LIMIT):
    return pltpu.CompilerParams(dimension_semantics=sem, vmem_limit_bytes=vmem)


def _dot(a, b):
    return jnp.dot(a, b, preferred_element_type=F32)


def _dot_nt(a, b):
    return lax.dot_general(a, b, (((1,), (1,)), ((), ())), preferred_element_type=F32)


def _layer_norm(y, g, b):
    mu = jnp.mean(y, axis=-1, keepdims=True)
    d = y - mu
    var = jnp.mean(d * d, axis=-1, keepdims=True)
    return d * lax.rsqrt(var + LN_EPS) * g + b


def _rope128(x, cos_f, sin_f):
    return x * cos_f + pltpu.roll(x, 64, 1) * sin_f


def _rope64(x, cos_t, sin_a, sin_b):
    return x * cos_t + pltpu.roll(x, 96, 1) * sin_a + pltpu.roll(x, 32, 1) * sin_b


def _mla_prep_kernel(x_ref, wa_ref, kvg_ref, qg_ref, wqt_ref, wk_ref, wvt_ref, cos_ref, sa_ref, sb_ref,
                     ct_ref, st_ref, qt_ref, k_ref, vt_ref, xb_ref):
    xb = x_ref[...].astype(BF16)
    xb_ref[...] = xb
    h = _dot(xb, wa_ref[...])
    ckv = h[:, 0:128]
    kr = h[:, 128:256]
    cq = h[:, 256:768]
    ckv_n = (ckv * lax.rsqrt(jnp.mean(ckv * ckv, axis=-1, keepdims=True) + RMS_EPS) * kvg_ref[...]).astype(BF16)
    cq_ms = jnp.sum(cq * cq, axis=-1, keepdims=True) * (1.0 / MLA_Q_RANK)
    cq_n = (cq * lax.rsqrt(cq_ms + RMS_EPS) * qg_ref[...]).astype(BF16)
    kr_r = _rope64(kr, cos_ref[...], sa_ref[...], sb_ref[...]).astype(BF16)
    scale = MLA_QK_DIM ** -0.5 * LOG2E
    qt = _dot_nt(wqt_ref[...], cq_n)
    c, s = ct_ref[...], st_ref[...]
    k_nope = _dot(ckv_n, wk_ref[...])
    for hd in range(MLA_HEADS):
        o = hd * MLA_QK_PAD
        qt_ref[o:o + 128, :] = (qt[o:o + 128] * scale).astype(BF16)
        x1, x2 = qt[o + 128:o + 160], qt[o + 160:o + 192]
        qt_ref[o + 128:o + 160, :] = ((x1 * c - x2 * s) * scale).astype(BF16)
        qt_ref[o + 160:o + 192, :] = ((x2 * c + x1 * s) * scale).astype(BF16)
        qt_ref[o + 192:o + 256, :] = jnp.zeros((64, qt.shape[1]), BF16)
        k_ref[:, o:o + 128] = k_nope[:, hd * 128:(hd + 1) * 128].astype(BF16)
        k_ref[:, o + 128:o + 256] = kr_r
    vt_ref[...] = _dot_nt(wvt_ref[...], ckv_n).astype(BF16)


def _mla_prep(x3, wa, kvg, qg, wqt, wk, wvt, cos_t, sin_a, sin_b, cos_tt, sin_tt, tm):
    b, s, _ = x3.shape
    full = lambda shape: pl.BlockSpec(shape, lambda bi, i: (0, 0))
    tab = pl.BlockSpec((tm, LANES), lambda bi, i: (i, 0))
    tab_t = pl.BlockSpec((MLA_ROPE_DIM // 2, tm), lambda bi, i: (0, i))
    wide = MLA_HEADS * MLA_QK_PAD
    vw = MLA_HEADS * MLA_V_DIM
    return pl.pallas_call(
        _mla_prep_kernel,
        grid=(b, s // tm),
        in_specs=[pl.BlockSpec((None, tm, D_MODEL), lambda bi, i: (bi, i, 0)), full(wa.shape), full(kvg.shape),
                  full(qg.shape), full(wqt.shape), full(wk.shape), full(wvt.shape), tab, tab, tab, tab_t, tab_t],
        out_specs=[pl.BlockSpec((None, wide, tm), lambda bi, i: (bi, 0, i)),
                   pl.BlockSpec((None, tm, wide), lambda bi, i: (bi, i, 0)),
                   pl.BlockSpec((None, vw, tm), lambda bi, i: (bi, 0, i)),
                   pl.BlockSpec((None, tm, D_MODEL), lambda bi, i: (bi, i, 0))],
        out_shape=[jax.ShapeDtypeStruct((b, wide, s), BF16), jax.ShapeDtypeStruct((b, s, wide), BF16),
                   jax.ShapeDtypeStruct((b, vw, s), BF16), jax.ShapeDtypeStruct((b, s, D_MODEL), BF16)],
        compiler_params=_params("parallel", "parallel"),
        name="mla_prep",
    )(x3, wa, kvg, qg, wqt, wk, wvt, cos_t, sin_a, sin_b, cos_tt, sin_tt)


def _dqkv_kernel(x_ref, w_ref, cos_ref, sin_ref, o_ref):
    j = pl.program_id(0)
    h = _dot(x_ref[...], w_ref[...])

    @pl.when(j < 4)
    def _():
        cos_f, sin_f = cos_ref[...], sin_ref[...]
        sc = jnp.where(j < 3, HEAD_DIM ** -0.5, 1.0).astype(F32)
        for hd in range(DIL_HEADS):
            sl = slice(hd * 128, (hd + 1) * 128)
            o_ref[:, sl] = (_rope128(h[:, sl], cos_f, sin_f) * sc).astype(BF16)

    @pl.when(j == 4)
    def _():
        o_ref[...] = h.astype(BF16)


def _dqkv(xb, wd, cos_f, sin_f, seq, tm):
    m = xb.shape[0]
    nrow = seq // tm
    n_col = wd.shape[1] // DIL_WIDTH
    tab = pl.BlockSpec((tm, LANES), lambda j, i: (i % nrow, 0))
    return pl.pallas_call(
        _dqkv_kernel,
        grid=(n_col, m // tm),
        in_specs=[pl.BlockSpec((tm, D_MODEL), lambda j, i: (i, 0)),
                  pl.BlockSpec((D_MODEL, DIL_WIDTH), lambda j, i: (0, j)), tab, tab],
        out_specs=pl.BlockSpec((tm, DIL_WIDTH), lambda j, i: (i, j)),
        out_shape=jax.ShapeDtypeStruct((m, wd.shape[1]), BF16),
        compiler_params=_params("parallel", "parallel"),
        name="dil_qkv",
    )(xb, wd, cos_f, sin_f)


def _flash_kernel(*refs, tq, use_c):
    if use_c:
        qt_ref, k_ref, vt_ref, c_ref, o_ref, acc_ref, s0_ref, s1_ref, kaug_ref = refs
    else:
        qt_ref, k_ref, vt_ref, o_ref, acc_ref, s0_ref, s1_ref = refs
    seq = k_ref.shape[0]
    nq = seq // tq
    if use_c:
        row = lax.broadcasted_iota(jnp.int32, (LANES, tq), 0)

        def build(j, _):
            start = pl.multiple_of(j * tq, tq)
            neg = -LOG2E * c_ref[:, pl.ds(start, tq)]
            hi = neg.astype(BF16).astype(F32)
            mid = (neg - hi).astype(BF16).astype(F32)
            lo = neg - hi - mid
            blk = jnp.where(row == 0, hi, jnp.where(row == 1, mid, jnp.where(row == 2, lo, 0.0)))
            kaug_ref[pl.ds(start, tq), :] = blk.T.astype(BF16)
            return 0

        lax.fori_loop(0, nq, build, 0)

    def scores(i, j, s_ref):
        q = qt_ref[:, pl.ds(pl.multiple_of(i * tq, tq), tq)]
        if use_c:
            ones = (lax.broadcasted_iota(jnp.int32, (LANES, tq), 0) < 3).astype(BF16)
            q = jnp.concatenate([q, ones], axis=0)
        start = pl.multiple_of(j * tq, tq)
        kt = k_ref[pl.ds(start, tq), :]
        if use_c:
            kt = jnp.concatenate([kt, kaug_ref[pl.ds(start, tq), :]], axis=1)
        s_ref[...] = _dot(kt, q)

    def update(j, s_ref, stats):
        m, l = stats
        s = s_ref[...]
        m_new = jnp.maximum(m, jnp.max(s, axis=0, keepdims=True))
        a = jnp.exp2(m - m_new)
        p = jnp.exp2(s - m_new)
        l = a * l + jnp.sum(p, axis=0, keepdims=True)
        start = pl.multiple_of(j * tq, tq)
        acc_ref[...] = a * acc_ref[...] + _dot(vt_ref[:, pl.ds(start, tq)], p.astype(BF16))
        return m_new, l

    half = tq // 2

    def scores_diag(i, s_ref):
        q = qt_ref[:, pl.ds(pl.multiple_of(i * tq, tq), tq)]
        if use_c:
            ones = (lax.broadcasted_iota(jnp.int32, (LANES, tq), 0) < 3).astype(BF16)
            q = jnp.concatenate([q, ones], axis=0)
        start = pl.multiple_of(i * tq, tq)
        kt = k_ref[pl.ds(start, tq), :]
        if use_c:
            kt = jnp.concatenate([kt, kaug_ref[pl.ds(start, tq), :]], axis=1)
        s_ref[0:half, 0:half] = _dot(kt[0:half], q[:, 0:half])
        s_ref[:, half:tq] = _dot(kt, q[:, half:tq])

    def update_diag(j, s_ref, stats):
        m, l = stats
        start = pl.multiple_of(j * tq, tq)
        parts = []
        for lo, nkeys in ((0, half), (half, tq)):
            cols = slice(lo, lo + half)
            key = lax.broadcasted_iota(jnp.int32, (nkeys, half), 0)
            qry = lax.broadcasted_iota(jnp.int32, (nkeys, half), 1) + lo
            s = jnp.where(key <= qry, s_ref[0:nkeys, cols], NEG)
            m_new = jnp.maximum(m[:, cols], jnp.max(s, axis=0, keepdims=True))
            a = jnp.exp2(m[:, cols] - m_new)
            p = jnp.exp2(s - m_new)
            parts.append((m_new, a * l[:, cols] + jnp.sum(p, axis=0, keepdims=True)))
            acc_ref[:, cols] = a * acc_ref[:, cols] + _dot(vt_ref[:, pl.ds(start, nkeys)], p.astype(BF16))
        return tuple(jnp.concatenate([parts[0][n], parts[1][n]], axis=1) for n in range(2))

    def query_tile(i, first, second):
        acc_ref[...] = jnp.zeros_like(acc_ref)

        def pair(jj, stats):
            scores(i, 2 * jj + 1, second)
            stats = update(2 * jj, first, stats)
            scores(i, 2 * jj + 2, first)
            return update(2 * jj + 1, second, stats)

        init = (jnp.full((1, tq), NEG, F32), jnp.zeros((1, tq), F32))
        stats = lax.fori_loop(0, i // 2, pair, init)
        nxt = jnp.minimum(i + 1, nq - 1)

        def odd_tail(stats):
            scores_diag(i, second)
            stats = update(i - 1, first, stats)
            scores(nxt, 0, first)
            return update_diag(i, second, stats)

        def even_tail(stats):
            scores(nxt, 0, second)
            return update_diag(i, first, stats)

        _, l = lax.cond(i % 2 == 1, odd_tail, even_tail, stats)
        o_ref[pl.ds(pl.multiple_of(i * tq, tq), tq), :] = (acc_ref[...] / l).T.astype(o_ref.dtype)
        return 0

    scores(0, 0, s0_ref)

    def query_loop(i, _):
        return lax.cond(((i + 1) // 2) % 2 == 0, lambda: query_tile(i, s0_ref, s1_ref),
                        lambda: query_tile(i, s1_ref, s0_ref))

    lax.fori_loop(0, nq, query_loop, 0)


def _flash(qt, k, vt, c, *, heads, dq, dk, dv, tq):
    b, s, _ = k.shape
    use_c = c is not None
    in_specs = [pl.BlockSpec((None, dq, s), lambda bi, h: (bi, h, 0)),
                pl.BlockSpec((None, s, dk), lambda bi, h: (bi, 0, h)),
                pl.BlockSpec((None, dv, s), lambda bi, h: (bi, h, 0))]
    args = [qt, k, vt]
    scratch = [pltpu.VMEM((dv, tq), F32), pltpu.VMEM((tq, tq), F32), pltpu.VMEM((tq, tq), F32)]
    if use_c:
        in_specs.append(pl.BlockSpec((None, None, 1, s), lambda bi, h: (bi, h, 0, 0)))
        args.append(c)
        scratch.append(pltpu.VMEM((s, LANES), BF16))
    return pl.pallas_call(
        functools.partial(_flash_kernel, tq=tq, use_c=use_c),
        grid=(b, heads),
        in_specs=in_specs,
        out_specs=pl.BlockSpec((None, s, dv), lambda bi, h: (bi, 0, h)),
        out_shape=jax.ShapeDtypeStruct((b, s, heads * dv), BF16),
        scratch_shapes=scratch,
        compiler_params=_params("parallel", "parallel"),
        name="flash_fox" if use_c else "flash_mla",
    )(*args)


def _dilated_kernel(q_ref, kc_ref, kp_ref, vc_ref, vp_ref, o_ref, lse_ref, qf, kf, vf, *, dil, tn):
    i = pl.program_id(1)
    per_class = tn // dil
    qf[...] = q_ref[...].astype(F32)
    kf[0:tn, :] = kp_ref[...].astype(F32)
    kf[tn:2 * tn, :] = kc_ref[...].astype(F32)
    vf[0:tn, :] = vp_ref[...].astype(F32)
    vf[tn:2 * tn, :] = vc_ref[...].astype(F32)
    row = lax.broadcasted_iota(jnp.int32, (DIL_SPAN, 2 * DIL_SPAN), 0)
    col = lax.broadcasted_iota(jnp.int32, (DIL_SPAN, 2 * DIL_SPAN), 1)
    back = row + DIL_SPAN - col
    in_band = jnp.where(back >= 0, jnp.where(back <= DIL_SPAN, 0.0, NEG), NEG)
    first_band = jnp.where(col >= jnp.where(i > 0, 0, DIL_SPAN), in_band, NEG)
    for r in range(dil):
        k_r = jnp.concatenate([kf[pl.ds(tn - DIL_SPAN * dil + r, DIL_SPAN, stride=dil), :],
                               kf[pl.ds(tn + r, per_class, stride=dil), :]], axis=0).astype(BF16)
        v_r = jnp.concatenate([vf[pl.ds(tn - DIL_SPAN * dil + r, DIL_SPAN, stride=dil), :],
                               vf[pl.ds(tn + r, per_class, stride=dil), :]], axis=0).astype(BF16)
        q_r = qf[pl.ds(r, per_class, stride=dil), :].astype(BF16)
        for a in range(per_class // DIL_SPAN):
            lo = a * DIL_SPAN
            s = _dot_nt(q_r[lo:lo + DIL_SPAN], k_r[lo:lo + 2 * DIL_SPAN]) + (first_band if a == 0 else in_band)
            m = jnp.max(s, axis=-1, keepdims=True)
            p = jnp.exp(s - m)
            l = jnp.sum(p, axis=-1, keepdims=True)
            rows = pl.ds(r + lo * dil, DIL_SPAN, stride=dil)
            o_ref[rows, :] = _dot(p.astype(BF16), v_r[lo:lo + 2 * DIL_SPAN]) / l
            lse_ref[rows, :] = jnp.broadcast_to(m + jnp.log(l), (DIL_SPAN, LANES))


def _dilated_group(dqkv3, g, dil, tn):
    batch, seq, _ = dqkv3.shape
    assert tn % (DIL_SPAN * dil) == 0
    blk = lambda col, prev: pl.BlockSpec(
        (None, tn, HEAD_DIM), lambda b, i, h: (b, jnp.maximum(i - 1, 0) if prev else i, col * DIL_HEADS + h))
    out = pl.BlockSpec((None, tn, HEAD_DIM), lambda b, i, h: (b, i, h))
    o, lse = pl.pallas_call(
        functools.partial(_dilated_kernel, dil=dil, tn=tn),
        grid=(batch, seq // tn, DIL_HEADS),
        in_specs=[blk(g, False), blk(3, False), blk(3, True), blk(4, False), blk(4, True)],
        out_specs=[out, out],
        out_shape=[jax.ShapeDtypeStruct((batch, seq, DIL_WIDTH), F32)] * 2,
        scratch_shapes=[pltpu.VMEM((tn, HEAD_DIM), F32), pltpu.VMEM((2 * tn, HEAD_DIM), F32),
                        pltpu.VMEM((2 * tn, HEAD_DIM), F32)],
        compiler_params=_params("parallel", "parallel", "parallel"),
        name=f"dilated_{dil}",
    )(dqkv3, dqkv3, dqkv3, dqkv3, dqkv3)
    return o.reshape(batch * seq, DIL_WIDTH), lse.reshape(batch * seq, DIL_WIDTH)


def _even_out_kernel(om_ref, o0, o1, o2, l0, l1, l2, wm_ref, wd_ref, x_ref, g_ref, b_ref, o_ref, ob_ref):
    ls = [l0[...], l1[...], l2[...]]
    mx = jnp.maximum(jnp.maximum(ls[0], ls[1]), ls[2])
    es = [jnp.exp(v - mx) for v in ls]
    den = es[0] + es[1] + es[2]
    o_dil = ((es[0] / den) * o0[...] + (es[1] / den) * o1[...] + (es[2] / den) * o2[...]).astype(BF16)
    y = _dot(om_ref[...], wm_ref[...]) + _dot(o_dil, wd_ref[...])
    out = _layer_norm(ALPHA * x_ref[...] + y, g_ref[...], b_ref[...])
    o_ref[...] = out
    ob_ref[...] = out.astype(BF16)


def _even_out(o_mla, outs, lses, w_mla, w_dil, x, g, b, tm):
    m = x.shape[0]
    row = lambda width: pl.BlockSpec((tm, width), lambda i: (i, 0))
    full = lambda shape: pl.BlockSpec(shape, lambda i: (0, 0))
    return pl.pallas_call(
        _even_out_kernel,
        grid=(m // tm,),
        in_specs=[row(o_mla.shape[1])] + [row(DIL_WIDTH)] * 6
        + [full(w_mla.shape), full(w_dil.shape), row(D_MODEL), full(g.shape), full(b.shape)],
        out_specs=[row(D_MODEL), row(D_MODEL)],
        out_shape=[jax.ShapeDtypeStruct((m, D_MODEL), F32), jax.ShapeDtypeStruct((m, D_MODEL), BF16)],
        compiler_params=_params("parallel"),
        name="even_out_ln",
    )(o_mla, *outs, *lses, w_mla, w_dil, x, g, b)


def _odd_out_kernel(a_ref, w_ref, x_ref, g_ref, b_ref, o_ref):
    o_ref[...] = _layer_norm(ALPHA * x_ref[...] + _dot(a_ref[...], w_ref[...]), g_ref[...], b_ref[...])


def _odd_out(a, w, x, g, b, tm):
    m = x.shape[0]
    row = lambda width: pl.BlockSpec((tm, width), lambda i: (i, 0))
    full = lambda shape: pl.BlockSpec(shape, lambda i: (0, 0))
    return pl.pallas_call(
        _odd_out_kernel,
        grid=(m // tm,),
        in_specs=[row(a.shape[1]), full(w.shape), row(D_MODEL), full(g.shape), full(b.shape)],
        out_specs=row(D_MODEL),
        out_shape=jax.ShapeDtypeStruct((m, D_MODEL), F32),
        compiler_params=_params("parallel"),
        name="odd_out_ln",
    )(a, w, x, g, b)


def _router_kernel(x_ref, rwh_ref, rwl_ref, rb_ref, route_ref, cnt_ref, carry_ref):
    @pl.when(pl.program_id(0) == 0)
    def _():
        carry_ref[...] = jnp.zeros_like(carry_ref)

    x = x_ref[...]
    xh = x.astype(BF16)
    xl = (x - xh.astype(F32)).astype(BF16)
    logits = _dot(xh, rwh_ref[...]) + (_dot(xh, rwl_ref[...]) + _dot(xl, rwh_ref[...])) + rb_ref[...]
    tm = logits.shape[0]
    lane = lax.broadcasted_iota(jnp.int32, (tm, LANES), 1)
    l1 = jnp.max(logits, axis=-1, keepdims=True)
    i1 = jnp.min(jnp.where(logits == l1, lane, LANES), axis=-1, keepdims=True)
    rest = jnp.where(lane == i1, NEG, logits)
    l2 = jnp.max(rest, axis=-1, keepdims=True)
    i2 = jnp.min(jnp.where(rest == l2, lane, LANES), axis=-1, keepdims=True)
    e = jnp.exp(l2 - l1)
    w1 = 1.0 / (1.0 + e)
    w2 = e / (1.0 + e)
    hot1 = (lane == i1).astype(F32)
    hot2 = (lane == i2).astype(F32)
    cnt = hot1 + hot2
    strict = (lax.broadcasted_iota(jnp.int32, (tm, tm), 1) < lax.broadcasted_iota(jnp.int32, (tm, tm), 0)).astype(BF16)
    before = _dot(strict, cnt.astype(BF16)) + carry_ref[...]
    r1 = jnp.sum(before * hot1, axis=-1, keepdims=True)
    r2 = jnp.sum(before * hot2, axis=-1, keepdims=True)
    vals = (i1.astype(F32), i2.astype(F32), w1, w2, r1, r2)
    route = jnp.zeros((tm, LANES), F32)
    for idx, val in enumerate(vals):
        route = jnp.where(lane == idx, val, route)
    route_ref[...] = route
    total = carry_ref[...] + jnp.sum(cnt, axis=0, keepdims=True)
    carry_ref[...] = total
    cnt_ref[...] = jnp.broadcast_to(total, cnt_ref.shape)


def _router(x, rw, rb, tm):
    m = x.shape[0]
    full = lambda shape: pl.BlockSpec(shape, lambda i: (0, 0))
    rw_hi = rw.astype(BF16)
    rw_lo = (rw - rw_hi.astype(F32)).astype(BF16)
    return pl.pallas_call(
        _router_kernel,
        grid=(m // tm,),
        in_specs=[pl.BlockSpec((tm, D_MODEL), lambda i: (i, 0)), full(rw.shape), full(rw.shape), full(rb.shape)],
        out_specs=[pl.BlockSpec((tm, LANES), lambda i: (i, 0)), pl.BlockSpec((8, LANES), lambda i: (0, 0))],
        out_shape=[jax.ShapeDtypeStruct((m, LANES), F32), jax.ShapeDtypeStruct((8, LANES), F32)],
        scratch_shapes=[pltpu.VMEM((1, LANES), F32)],
        compiler_params=_params("arbitrary"),
        name="moe_router",
    )(x, rw_hi, rw_lo, rb)


def _silu_mul(g, u):
    return g * (1.0 / (1.0 + jnp.exp(-g))) * u


def _ffn_up_kernel(x_ref, wg_ref, wu_ref, o_ref, wgb_ref, wub_ref):
    @pl.when(pl.program_id(1) == 0)
    def _():
        wgb_ref[...] = wg_ref[...].astype(BF16)
        wub_ref[...] = wu_ref[...].astype(BF16)

    x = x_ref[...]
    o_ref[...] = _silu_mul(_dot(x, wgb_ref[...]), _dot(x, wub_ref[...])).astype(BF16)


def _ffn_up(xb, wg, wu, tm, tn):
    m = xb.shape[0]
    return pl.pallas_call(
        _ffn_up_kernel,
        grid=(D_FF // tn, m // tm),
        in_specs=[pl.BlockSpec((tm, D_MODEL), lambda j, i: (i, 0)),
                  pl.BlockSpec((D_MODEL, tn), lambda j, i: (0, j)),
                  pl.BlockSpec((D_MODEL, tn), lambda j, i: (0, j))],
        out_specs=pl.BlockSpec((tm, tn), lambda j, i: (i, j)),
        out_shape=jax.ShapeDtypeStruct((m, D_FF), BF16),
        scratch_shapes=[pltpu.VMEM((D_MODEL, tn), BF16), pltpu.VMEM((D_MODEL, tn), BF16)],
        compiler_params=_params("arbitrary", "arbitrary"),
        name="ffn_up",
    )(xb, wg, wu)


def _ffn_down_ln_kernel(h_ref, w_hbm, x_ref, g_ref, b_ref, o_ref, ob_ref, acc_ref, wbuf, sems, *, tk):
    i = pl.program_id(0)
    chunks = [(lo, min(tk, D_FF - lo)) for lo in range(0, D_FF, tk)]

    def copy(c):
        lo, size = chunks[c]
        return pltpu.make_async_copy(w_hbm.at[pl.ds(lo, size), :], wbuf.at[c % 2, pl.ds(0, size), :], sems.at[c % 2])

    @pl.when(i == 0)
    def _():
        copy(0).start()

    for c, (lo, size) in enumerate(chunks):
        if c + 1 < len(chunks):
            copy(c + 1).start()
        copy(c).wait()
        part = _dot(h_ref[:, lo:lo + size], wbuf[c % 2, 0:size, :])
        if c == 0:
            acc_ref[...] = part
        else:
            acc_ref[...] += part
        if c == len(chunks) - 2 and len(chunks) % 2 == 0:
            @pl.when(i + 1 < pl.num_programs(0))
            def _():
                copy(0).start()

    out = _layer_norm(ALPHA * x_ref[...] + acc_ref[...], g_ref[...], b_ref[...])
    o_ref[...] = out
    ob_ref[...] = out.astype(BF16)


def _ffn_down_ln(h, wd, x, g, b, tm, tk):
    m = x.shape[0]
    assert (-(-D_FF // tk)) % 2 == 0
    row = pl.BlockSpec((tm, D_MODEL), lambda i: (i, 0))
    vec = pl.BlockSpec((1, D_MODEL), lambda i: (0, 0))
    return pl.pallas_call(
        functools.partial(_ffn_down_ln_kernel, tk=tk),
        grid=(m // tm,),
        in_specs=[pl.BlockSpec((tm, D_FF), lambda i: (i, 0)), pl.BlockSpec(memory_space=pl.ANY), row, vec, vec],
        out_specs=[row, row],
        out_shape=[jax.ShapeDtypeStruct((m, D_MODEL), F32), jax.ShapeDtypeStruct((m, D_MODEL), BF16)],
        scratch_shapes=[pltpu.VMEM((tm, D_MODEL), F32), pltpu.VMEM((2, tk, D_MODEL), BF16),
                        pltpu.SemaphoreType.DMA((2,))],
        compiler_params=_params("arbitrary"),
        name="ffn_down_ln",
    )(h, wd, x, g, b)


def _proj_kernel(x_ref, w_ref, o_ref):
    o_ref[...] = _dot(x_ref[...], w_ref[...]).astype(BF16)


def _proj(xb, w, tm, tn):
    m = xb.shape[0]
    n = w.shape[1]
    return pl.pallas_call(
        _proj_kernel,
        grid=(n // tn, m // tm),
        in_specs=[pl.BlockSpec((tm, D_MODEL), lambda j, i: (i, 0)), pl.BlockSpec((D_MODEL, tn), lambda j, i: (0, j))],
        out_specs=pl.BlockSpec((tm, tn), lambda j, i: (i, j)),
        out_shape=jax.ShapeDtypeStruct((m, n), BF16),
        compiler_params=_params("parallel", "parallel"),
        name="proj",
    )(xb, w)


def _proj_t_kernel(w_ref, x_ref, o_ref, *, scale):
    o_ref[...] = (_dot_nt(w_ref[...], x_ref[...]) * scale).astype(BF16)


def _proj_t(xb3, wt, scale, tm, tn):
    b, s, _ = xb3.shape
    n = wt.shape[0]
    return pl.pallas_call(
        functools.partial(_proj_t_kernel, scale=scale),
        grid=(n // tn, b, s // tm),
        in_specs=[pl.BlockSpec((tn, D_MODEL), lambda j, bi, i: (j, 0)),
                  pl.BlockSpec((None, tm, D_MODEL), lambda j, bi, i: (bi, i, 0))],
        out_specs=pl.BlockSpec((None, tn, tm), lambda j, bi, i: (bi, j, i)),
        out_shape=jax.ShapeDtypeStruct((b, n, s), BF16),
        compiler_params=_params("parallel", "parallel", "parallel"),
        name="proj_t",
    )(wt, xb3)


def _fgate_kernel(x_ref, w_ref, b_ref, c_ref, carry_ref):
    @pl.when(pl.program_id(1) == 0)
    def _():
        carry_ref[...] = jnp.zeros_like(carry_ref)

    z = _dot(x_ref[...], w_ref[...]) + b_ref[...]
    log_f = jnp.minimum(z, 0.0) - jnp.log(1.0 + jnp.exp(-jnp.abs(z)))
    tm = z.shape[0]
    tri = (lax.broadcasted_iota(jnp.int32, (tm, tm), 1) <= lax.broadcasted_iota(jnp.int32, (tm, tm), 0)).astype(F32)
    c = jnp.dot(tri, log_f, preferred_element_type=F32, precision=lax.Precision.HIGHEST) + carry_ref[...]
    c_ref[...] = c
    carry_ref[...] = c[tm - 1:tm, :]


def _fgate(xb3, wf, bf, tm):
    b, s, _ = xb3.shape
    return pl.pallas_call(
        _fgate_kernel,
        grid=(b, s // tm),
        in_specs=[pl.BlockSpec((None, tm, D_MODEL), lambda bi, i: (bi, i, 0)),
                  pl.BlockSpec(wf.shape, lambda bi, i: (0, 0)), pl.BlockSpec(bf.shape, lambda bi, i: (0, 0))],
        out_specs=pl.BlockSpec((None, tm, LANES), lambda bi, i: (bi, i, 0)),
        out_shape=jax.ShapeDtypeStruct((b, s, LANES), F32),
        scratch_shapes=[pltpu.VMEM((1, LANES), F32)],
        compiler_params=_params("parallel", "arbitrary"),
        name="fox_gate",
    )(xb3, wf, bf)


def _dispatch_kernel(fill_ref, pos_ref, x_ref, xs_ref, zero_ref, sem, zero_sem):
    tm = x_ref.shape[0]

    @pl.when(pl.program_id(0) == 0)
    def _():
        zero_ref[...] = jnp.zeros_like(zero_ref)
        rows = zero_ref.shape[0]

        def fill(f):
            start = pl.multiple_of(jnp.maximum(fill_ref[f], 0), rows)
            return pltpu.make_async_copy(zero_ref, xs_ref.at[pl.ds(start, rows), :], zero_sem)

        for f in range(fill_ref.shape[0]):
            @pl.when(fill_ref[f] >= 0)
            def _():
                fill(f).start()

        for f in range(fill_ref.shape[0]):
            @pl.when(fill_ref[f] >= 0)
            def _():
                fill(f).wait()

    def copy(r, k):
        return pltpu.make_async_copy(x_ref.at[pl.ds(r, 1), :], xs_ref.at[pl.ds(pos_ref[0, 0, 2 * r + k], 1), :], sem)

    def start(r, _):
        copy(r, 0).start(priority=0)
        copy(r, 1).start(priority=1)
        return 0

    def wait(r, _):
        copy(r, 0).wait()
        copy(r, 1).wait()
        return 0

    lax.fori_loop(0, tm, start, 0, unroll=DMA_UNROLL)
    lax.fori_loop(0, tm, wait, 0, unroll=DMA_UNROLL)


def _dispatch(x, pos, fill_rows, n_rows, tile, tm):
    m, width = x.shape
    pos3 = pos.reshape(m // tm, 1, 2 * tm)
    return pl.pallas_call(
        _dispatch_kernel,
        grid_spec=pltpu.PrefetchScalarGridSpec(
            num_scalar_prefetch=1,
            grid=(m // tm,),
            in_specs=[pl.BlockSpec((1, 1, 2 * tm), lambda i, fill: (i, 0, 0), memory_space=pltpu.SMEM),
                      pl.BlockSpec((tm, width), lambda i, fill: (i, 0))],
            out_specs=pl.BlockSpec(memory_space=pl.ANY),
            scratch_shapes=[pltpu.VMEM((tile, width), x.dtype), pltpu.SemaphoreType.DMA(()),
                            pltpu.SemaphoreType.DMA(())],
        ),
        out_shape=jax.ShapeDtypeStruct((n_rows, width), x.dtype),
        compiler_params=_params("arbitrary"),
        name="moe_dispatch",
    )(fill_rows, pos3, x)


def _moe_up_kernel(te_ref, nu_ref, nx_ref, x_ref, wg_hbm, wu_hbm, o_ref, sg_ref, su_ref, wb_ref, sems, *, tn):
    groups = [(lo, min(MXU_COLS, tn - lo)) for lo in range(0, tn, MXU_COLS)]
    j = pl.program_id(0)
    t = pl.program_id(1)
    live = t < nu_ref[0]
    fresh = jnp.logical_or(t == 0, te_ref[t] != te_ref[jnp.maximum(t - 1, 0)])

    def weight_copies(e, jj):
        cols = pl.ds(pl.multiple_of(jj * tn, LANES), tn)
        return (pltpu.make_async_copy(wg_hbm.at[e, :, cols], sg_ref, sems.at[0]),
                pltpu.make_async_copy(wu_hbm.at[e, :, cols], su_ref, sems.at[1]))

    def start(e, jj):
        for c in weight_copies(e, jj):
            c.start()

    @pl.when(jnp.logical_and(j == 0, t == 0))
    def _():
        start(te_ref[0], 0)

    @pl.when(jnp.logical_and(live, fresh))
    def _():
        for c in weight_copies(te_ref[t], j):
            c.wait()
        def cast_rows(c, _):
            rows = pl.ds(pl.multiple_of(c * 256, 256), 256)
            for lo, width in groups:
                wb_ref[rows, 2 * lo:2 * lo + width] = sg_ref[rows, lo:lo + width].astype(BF16)
                wb_ref[rows, 2 * lo + width:2 * lo + 2 * width] = su_ref[rows, lo:lo + width].astype(BF16)
            return 0

        lax.fori_loop(0, D_MODEL // 256, cast_rows, 0)
        nxt = nx_ref[t]

        @pl.when(nxt >= 0)
        def _():
            start(nxt, j)

        @pl.when(jnp.logical_and(nxt < 0, j + 1 < pl.num_programs(0)))
        def _():
            start(te_ref[0], j + 1)

    @pl.when(live)
    def _():
        x = x_ref[...].astype(BF16)
        for lo, width in groups:
            gu = _dot(x, wb_ref[:, 2 * lo:2 * lo + 2 * width])
            o_ref[:, lo:lo + width] = _silu_mul(gu[:, :width], gu[:, width:]).astype(BF16)

    @pl.when(jnp.logical_not(live))
    def _():
        o_ref[...] = jnp.zeros_like(o_ref)


def _moe_up(xs, wg, wu, tile_expert, n_used, next_expert, tm, tn):
    p = xs.shape[0]
    live = lambda t, nu: jnp.minimum(t, nu[0] - 1)
    return pl.pallas_call(
        functools.partial(_moe_up_kernel, tn=tn),
        grid_spec=pltpu.PrefetchScalarGridSpec(
            num_scalar_prefetch=3,
            grid=(D_FF // tn, p // tm),
            in_specs=[pl.BlockSpec((tm, D_MODEL), lambda j, t, te, nu, nx: (live(t, nu), 0)),
                      pl.BlockSpec(memory_space=pl.ANY), pl.BlockSpec(memory_space=pl.ANY)],
            out_specs=pl.BlockSpec((tm, tn), lambda j, t, te, nu, nx: (t, j)),
            scratch_shapes=[pltpu.VMEM((D_MODEL, tn), F32), pltpu.VMEM((D_MODEL, tn), F32),
                            pltpu.VMEM((D_MODEL, 2 * tn), BF16), pltpu.SemaphoreType.DMA((2,))],
        ),
        out_shape=jax.ShapeDtypeStruct((p, D_FF), BF16),
        compiler_params=_params("arbitrary", "arbitrary"),
        name="moe_up",
    )(tile_expert, n_used, next_expert, xs, wg, wu)


def _moe_down_kernel(te_ref, nu_ref, nx_ref, h_ref, w_hbm, o_ref, st_ref, wb_ref, sem, *, tn):
    j = pl.program_id(0)
    t = pl.program_id(1)
    live = t < nu_ref[0]
    fresh = jnp.logical_or(t == 0, te_ref[t] != te_ref[jnp.maximum(t - 1, 0)])

    def weight_copy(e, jj):
        cols = pl.ds(pl.multiple_of(jj * tn, LANES), tn)
        return pltpu.make_async_copy(w_hbm.at[e, :, cols], st_ref, sem)

    @pl.when(jnp.logical_and(j == 0, t == 0))
    def _():
        weight_copy(te_ref[0], 0).start()

    @pl.when(jnp.logical_and(live, fresh))
    def _():
        weight_copy(te_ref[t], j).wait()

        def cast_rows(c, _):
            rows = pl.ds(pl.multiple_of(c * 256, 256), 256)
            wb_ref[rows, :] = st_ref[rows, :].astype(BF16)
            return 0

        lax.fori_loop(0, D_FF // 256, cast_rows, 0)
        nxt = nx_ref[t]

        @pl.when(nxt >= 0)
        def _():
            weight_copy(nxt, j).start()

        @pl.when(jnp.logical_and(nxt < 0, j + 1 < pl.num_programs(0)))
        def _():
            weight_copy(te_ref[0], j + 1).start()

    @pl.when(live)
    def _():
        h = h_ref[...]
        for lo in range(0, tn, 512):
            o_ref[:, lo:lo + 512] = _dot(h, wb_ref[:, lo:lo + 512])

    @pl.when(jnp.logical_not(live))
    def _():
        o_ref[...] = jnp.zeros_like(o_ref)


def _moe_down(hs, wd, tile_expert, n_used, next_expert, tm, tn):
    p = hs.shape[0]
    live = lambda t, nu: jnp.minimum(t, nu[0] - 1)
    return pl.pallas_call(
        functools.partial(_moe_down_kernel, tn=tn),
        grid_spec=pltpu.PrefetchScalarGridSpec(
            num_scalar_prefetch=3,
            grid=(D_MODEL // tn, p // tm),
            in_specs=[pl.BlockSpec((tm, D_FF), lambda j, t, te, nu, nx: (live(t, nu), 0)),
                      pl.BlockSpec(memory_space=pl.ANY)],
            out_specs=pl.BlockSpec((tm, tn), lambda j, t, te, nu, nx: (t, j)),
            scratch_shapes=[pltpu.VMEM((D_FF, tn), F32), pltpu.VMEM((D_FF, tn), BF16),
                            pltpu.SemaphoreType.DMA(())],
        ),
        out_shape=jax.ShapeDtypeStruct((p, D_MODEL), F32),
        compiler_params=_params("arbitrary", "arbitrary"),
        name="moe_down",
    )(tile_expert, n_used, next_expert, hs, wd)


def _combine_ln_kernel(pos_ref, ys_ref, x_ref, route_ref, g_ref, b_ref, o_ref, buf0, buf1, sem):
    tm = x_ref.shape[0]

    def copies(r):
        c0 = pltpu.make_async_copy(ys_ref.at[pl.ds(pos_ref[0, 0, 2 * r], 1), :], buf0.at[pl.ds(r, 1), :], sem)
        c1 = pltpu.make_async_copy(ys_ref.at[pl.ds(pos_ref[0, 0, 2 * r + 1], 1), :], buf1.at[pl.ds(r, 1), :], sem)
        return c0, c1

    def start(r, _):
        c0, c1 = copies(r)
        c0.start(priority=0)
        c1.start(priority=1)
        return 0

    def wait(r, _):
        c0, c1 = copies(r)
        c0.wait()
        c1.wait()
        return 0

    lax.fori_loop(0, tm, start, 0, unroll=DMA_UNROLL)
    lax.fori_loop(0, tm, wait, 0, unroll=DMA_UNROLL)
    route = route_ref[...]
    y = route[:, 2:3] * buf0[...] + route[:, 3:4] * buf1[...]
    o_ref[...] = _layer_norm(ALPHA * x_ref[...] + y, g_ref[...], b_ref[...])


def _combine_ln(ys, pos, x, route, g, b, tm):
    m = x.shape[0]
    pos3 = pos.reshape(m // tm, 1, 2 * tm)
    vec = pl.BlockSpec((1, D_MODEL), lambda i: (0, 0))
    return pl.pallas_call(
        _combine_ln_kernel,
        grid=(m // tm,),
        in_specs=[pl.BlockSpec((1, 1, 2 * tm), lambda i: (i, 0, 0), memory_space=pltpu.SMEM),
                  pl.BlockSpec(memory_space=pl.ANY),
                  pl.BlockSpec((tm, D_MODEL), lambda i: (i, 0)),
                  pl.BlockSpec((tm, LANES), lambda i: (i, 0)), vec, vec],
        out_specs=pl.BlockSpec((tm, D_MODEL), lambda i: (i, 0)),
        out_shape=jax.ShapeDtypeStruct((m, D_MODEL), F32),
        scratch_shapes=[pltpu.VMEM((tm, D_MODEL), F32), pltpu.VMEM((tm, D_MODEL), F32), pltpu.SemaphoreType.DMA(())],
        compiler_params=_params("arbitrary"),
        name="moe_combine_ln",
    )(pos3, ys, x, route, g, b)


def _rope_tables(seq):
    def angles(dim):
        inv_freq = 1.0 / (ROPE_THETA ** (jnp.arange(0, dim, 2, dtype=F32) / dim))
        ang = jnp.arange(seq, dtype=F32)[:, None] * inv_freq[None, :]
        return jnp.cos(ang), jnp.sin(ang)

    c128, s128 = angles(HEAD_DIM)
    cos_f = jnp.concatenate([c128, c128], axis=-1)
    sin_f = jnp.concatenate([-s128, s128], axis=-1)
    c64, s64 = angles(MLA_ROPE_DIM)
    z32 = jnp.zeros_like(c64)
    cos_t = jnp.concatenate([c64, c64, z32, z32], axis=-1)
    sin_a = jnp.concatenate([-s64, z32, z32, z32], axis=-1)
    sin_b = jnp.concatenate([z32, s64, z32, z32], axis=-1)
    return cos_f, sin_f, cos_t, sin_a, sin_b, c64.T, s64.T


def _pad_cols(a, width):
    return jnp.pad(a, ((0, 0), (0, width - a.shape[1])))


def _row(v):
    return v.reshape(1, -1).astype(F32)


def _even_layer(x, batch, seq, tables, w_in, q_norm, w_q_b, kv_norm, w_kv_b, w_out, ln1_g, ln1_b,
                w_gate, w_up, w_down, ln2_g, ln2_b):
    cos_f, sin_f, cos_t, sin_a, sin_b, cos_tt, sin_tt = tables
    tm = min(ROW_TILE, seq)
    tw = min(WIDE_ROW_TILE, seq)
    wa = jnp.concatenate([w_in[:, OFF_CKV:OFF_KROPE], _pad_cols(w_in[:, OFF_KROPE:OFF_DQ], LANES),
                          _pad_cols(w_in[:, OFF_CQ:OFF_CKV], MLA_Q_RANK_PAD)], axis=1).astype(BF16)
    qg = _pad_cols(_row(q_norm), MLA_Q_RANK_PAD)
    wq = jnp.pad(w_q_b.reshape(MLA_Q_RANK, MLA_HEADS, MLA_QK_DIM),
                 ((0, MLA_Q_RANK_PAD - MLA_Q_RANK), (0, 0), (0, MLA_QK_PAD - MLA_QK_DIM)))
    wqt = wq.reshape(MLA_Q_RANK_PAD, MLA_HEADS * MLA_QK_PAD).T.astype(BF16)
    wkv3 = w_kv_b.reshape(MLA_KV_RANK, MLA_HEADS, MLA_NOPE_DIM + MLA_V_DIM)
    wk = wkv3[:, :, :MLA_NOPE_DIM].reshape(MLA_KV_RANK, -1).astype(BF16)
    wvt = wkv3[:, :, MLA_NOPE_DIM:].reshape(MLA_KV_RANK, -1).T.astype(BF16)
    qt_mla, k_mla, vt_mla, xb3 = _mla_prep(x.reshape(batch, seq, D_MODEL), wa, _row(kv_norm), qg, wqt, wk, wvt,
                                           cos_t, sin_a, sin_b, cos_tt, sin_tt, tm)
    xb = xb3.reshape(batch * seq, D_MODEL)
    tq = min(FLASH_TQ, seq)
    o_mla = _flash(qt_mla, k_mla, vt_mla, None, heads=MLA_HEADS, dq=MLA_QK_PAD, dk=MLA_QK_PAD, dv=MLA_V_DIM,
                   tq=tq)
    o_mla = o_mla.reshape(batch * seq, -1)

    dqkv = _dqkv(xb, w_in[:, OFF_DQ:].astype(BF16), cos_f, sin_f, seq, tw)
    dqkv3 = dqkv.reshape(batch, seq, -1)
    outs, lses = [], []
    for g, (window, dil) in enumerate(DIL_PATTERNS):
        assert window == DIL_SPAN * dil
        o_g, lse_g = _dilated_group(dqkv3, g, dil, min(seq, max(DIL_TILE, DIL_SPAN * dil)))
        outs.append(o_g)
        lses.append(lse_g)
    n_mla = MLA_HEADS * MLA_V_DIM
    wo = w_out.astype(BF16)
    x1, x1b = _even_out(o_mla, outs, lses, wo[:n_mla], wo[n_mla:], x, _row(ln1_g), _row(ln1_b), tm)
    hmid = _ffn_up(x1b, w_gate, w_up, tw, FFN_COL_TILE)
    return _ffn_down_ln(hmid, w_down.astype(BF16), x1, _row(ln2_g), _row(ln2_b), tm, FFN_K_TILE)


def _odd_layer(x, xb, batch, seq, w_qkv, w_f, b_f, w_out, ln1_g, ln1_b, router_w, router_b,
               exp_w_gate, exp_w_up, exp_w_down, ln2_g, ln2_b):
    m = batch * seq
    tm = min(ROW_TILE, seq)
    tw = min(WIDE_ROW_TILE, seq)
    xb3 = xb.reshape(batch, seq, D_MODEL)
    wb = w_qkv.astype(BF16)
    qt = _proj_t(xb3, wb[:, :FOX_WIDTH].T, HEAD_DIM ** -0.5 * LOG2E, tw, QKV_COL_TILE)
    k = _proj(xb, wb[:, FOX_WIDTH:2 * FOX_WIDTH], tw, QKV_COL_TILE).reshape(batch, seq, FOX_WIDTH)
    vt = _proj_t(xb3, wb[:, 2 * FOX_WIDTH:].T, 1.0, tw, QKV_COL_TILE)
    c = _fgate(xb3, _pad_cols(w_f, LANES).astype(BF16), _pad_cols(_row(b_f), LANES), min(GATE_ROW_TILE, seq))
    c_t = jnp.transpose(c[:, :, :FOX_HEADS], (0, 2, 1)).reshape(batch, FOX_HEADS, 1, seq)
    tq = min(FLASH_TQ, seq)
    o = _flash(qt, k, vt, c_t, heads=FOX_HEADS, dq=HEAD_DIM, dk=HEAD_DIM, dv=HEAD_DIM, tq=tq)
    rb = jnp.full((1, LANES), NEG, F32).at[0, :N_EXPERTS].set(router_b.astype(F32))
    x1 = _odd_out(o.reshape(m, -1), w_out.astype(BF16), x, _row(ln1_g), _row(ln1_b), tm)
    route, counts = _router(x1, _pad_cols(router_w, LANES), rb, tm)
    tile = MOE_TILE
    n_tiles = (2 * m) // tile + N_EXPERTS
    cnt = counts[0, :N_EXPERTS].astype(jnp.int32)
    tiles_per = (cnt + tile - 1) // tile
    tile_end = jnp.cumsum(tiles_per)
    offset = (tile_end - tiles_per) * tile
    idx = route[:, 0:2].astype(jnp.int32)
    pos = (offset[idx] + route[:, 4:6].astype(jnp.int32)).reshape(-1)
    n_used = tile_end[-1:]
    tile_ids = jnp.arange(n_tiles, dtype=jnp.int32)
    tile_expert = jnp.minimum(jnp.sum((tile_end[None, :] <= tile_ids[:, None]).astype(jnp.int32), axis=1),
                              N_EXPERTS - 1)
    experts = jnp.arange(N_EXPERTS, dtype=jnp.int32)
    last_tile = jnp.where(tiles_per > 0, tile_end - 1, -1)
    unused = n_used[0] + experts
    unused = jnp.where(unused < n_tiles, unused, -1)
    fill_tiles = jnp.concatenate([last_tile, unused])
    fill_rows = jnp.where(fill_tiles >= 0, fill_tiles * tile, -1).astype(jnp.int32)
    xs = _dispatch(x1, pos, fill_rows, n_tiles * tile, tile, tm)
    later = jnp.where((experts[None, :] > experts[:, None]) & (tiles_per[None, :] > 0), experts[None, :], N_EXPERTS)
    next_nonempty = jnp.min(later, axis=1)
    next_expert = jnp.where(next_nonempty < N_EXPERTS, next_nonempty, -1)[tile_expert].astype(jnp.int32)
    hs = _moe_up(xs, exp_w_gate, exp_w_up, tile_expert, n_used, next_expert, tile, MOE_COL_TILE)
    ys = _moe_down(hs, exp_w_down, tile_expert, n_used, next_expert, tile, MOE_DOWN_COL_TILE)
    return _combine_ln(ys, pos, x1, route, _row(ln2_g), _row(ln2_b), tm)


def kernel(x, ev_w_in, ev_q_norm, ev_w_q_b, ev_kv_norm, ev_w_kv_b, ev_w_out, ev_ln1_g, ev_ln1_b, ev_ffn_w_gate, ev_ffn_w_up, ev_ffn_w_down, ev_ln2_g, ev_ln2_b, od_w_qkv, od_w_f, od_b_f, od_w_out, od_ln1_g, od_ln1_b, od_router_w, od_router_b, od_exp_w_gate, od_exp_w_up, od_exp_w_down, od_ln2_g, od_ln2_b):
    batch, seq, _ = x.shape
    tables = _rope_tables(seq)
    h = x.reshape(batch * seq, D_MODEL)
    hb = None
    for layer in range(DEPTH):
        i = layer // 2
        if layer % 2 == 0:
            h, hb = _even_layer(h, batch, seq, tables, ev_w_in[i], ev_q_norm[i], ev_w_q_b[i], ev_kv_norm[i],
                                ev_w_kv_b[i], ev_w_out[i], ev_ln1_g[i], ev_ln1_b[i], ev_ffn_w_gate[i],
                                ev_ffn_w_up[i], ev_ffn_w_down[i], ev_ln2_g[i], ev_ln2_b[i])
        else:
            h = _odd_layer(h, hb, batch, seq, od_w_qkv[i], od_w_f[i], od_b_f[i], od_w_out[i], od_ln1_g[i],
                           od_ln1_b[i], od_router_w[i], od_router_b[i], od_exp_w_gate[i], od_exp_w_up[i],
                           od_exp_w_down[i], od_ln2_g[i], od_ln2_b[i])
    return h.reshape(batch, seq, D_MODEL)
```

```python
import functools

import jax
import jax.numpy as jnp
from jax import lax
from jax.experimental import pallas as pl
from jax.experimental.pallas import tpu as pltpu

F32 = jnp.float32
BF16 = jnp.bfloat16

D_MODEL = 2048
HEAD_DIM = 128
LANES = 128
MXU_COLS = 256
ROPE_THETA = 10000.0
LN_EPS = 1e-5
RMS_EPS = 1e-6

MLA_HEADS = 10
MLA_Q_RANK = 448
MLA_Q_RANK_PAD = 512
MLA_KV_RANK = 128
MLA_NOPE_DIM = 128
MLA_ROPE_DIM = 64
MLA_V_DIM = 128
MLA_QK_DIM = MLA_NOPE_DIM + MLA_ROPE_DIM
MLA_QK_PAD = 256

DIL_PATTERNS = ((128, 1), (512, 4), (2048, 16))
DIL_GROUPS = 3
DIL_HEADS = 6
DIL_WIDTH = DIL_HEADS * HEAD_DIM
DIL_SPAN = 128

OFF_CQ = 0
OFF_CKV = OFF_CQ + MLA_Q_RANK
OFF_KROPE = OFF_CKV + MLA_KV_RANK
OFF_DQ = OFF_KROPE + MLA_ROPE_DIM
OFF_DK = OFF_DQ + DIL_GROUPS * DIL_WIDTH
OFF_DV = OFF_DK + DIL_WIDTH
W_IN_COLS = OFF_DV + DIL_WIDTH

FOX_HEADS = 16
FOX_WIDTH = FOX_HEADS * HEAD_DIM

D_FF = 5632
N_EXPERTS = 8
DEPTH = 2
ALPHA = (2.0 * DEPTH) ** 0.25

NEG = -1e30
LOG2E = 1.4426950408889634
VMEM_LIMIT = 56 * 1024 * 1024

ROW_TILE = 512
WIDE_ROW_TILE = 1024
FLASH_TQ = 1024
DIL_TILE = 2048
GATE_ROW_TILE = 512
QKV_COL_TILE = 1024
FFN_COL_TILE = 512
FFN_K_TILE = 1408
MOE_TILE = 512
MOE_COL_TILE = 1408
MOE_DOWN_COL_TILE = 1024
DMA_UNROLL = 8
STREAM_BUFFERS = 3


def _params(*sem, vmem=VMEM_LIMIT):
    return pltpu.CompilerParams(dimension_semantics=sem, vmem_limit_bytes=vmem)


def _dot(a, b):
    return jnp.dot(a, b, preferred_element_type=F32)


def _dot_nt(a, b):
    return lax.dot_general(a, b, (((1,), (1,)), ((), ())), preferred_element_type=F32)


def _layer_norm(y, g, b):
    mu = jnp.mean(y, axis=-1, keepdims=True)
    d = y - mu
    var = jnp.mean(d * d, axis=-1, keepdims=True)
    return d * lax.rsqrt(var + LN_EPS) * g + b


def _rope128(x, cos_f, sin_f):
    return x * cos_f + pltpu.roll(x, 64, 1) * sin_f


def _rope64(x, cos_t, sin_a, sin_b):
    return x * cos_t + pltpu.roll(x, 96, 1) * sin_a + pltpu.roll(x, 32, 1) * sin_b


def _mla_prep_kernel(x_ref, wa_ref, kvg_ref, qg_ref, wqt_ref, wk_ref, wvt_ref, cos_ref, sa_ref, sb_ref,
                     ct_ref, st_ref, qt_ref, k_ref, vt_ref, xb_ref):
    xb = x_ref[...].astype(BF16)
    xb_ref[...] = xb
    h = _dot(xb, wa_ref[...])
    ckv = h[:, 0:128]
    kr = h[:, 128:256]
    cq = h[:, 256:768]
    ckv_n = (ckv * lax.rsqrt(jnp.mean(ckv * ckv, axis=-1, keepdims=True) + RMS_EPS) * kvg_ref[...]).astype(BF16)
    cq_ms = jnp.sum(cq * cq, axis=-1, keepdims=True) * (1.0 / MLA_Q_RANK)
    cq_n = (cq * lax.rsqrt(cq_ms + RMS_EPS) * qg_ref[...]).astype(BF16)
    kr_r = _rope64(kr, cos_ref[...], sa_ref[...], sb_ref[...]).astype(BF16)
    scale = MLA_QK_DIM ** -0.5 * LOG2E
    qt = _dot_nt(wqt_ref[...], cq_n)
    c, s = ct_ref[...], st_ref[...]
    k_nope = _dot(ckv_n, wk_ref[...])
    for hd in range(MLA_HEADS):
        o = hd * MLA_QK_PAD
        qt_ref[o:o + 128, :] = (qt[o:o + 128] * scale).astype(BF16)
        x1, x2 = qt[o + 128:o + 160], qt[o + 160:o + 192]
        qt_ref[o + 128:o + 160, :] = ((x1 * c - x2 * s) * scale).astype(BF16)
        qt_ref[o + 160:o + 192, :] = ((x2 * c + x1 * s) * scale).astype(BF16)
        qt_ref[o + 192:o + 256, :] = jnp.zeros((64, qt.shape[1]), BF16)
        k_ref[:, o:o + 128] = k_nope[:, hd * 128:(hd + 1) * 128].astype(BF16)
        k_ref[:, o + 128:o + 256] = kr_r
    vt_ref[...] = _dot_nt(wvt_ref[...], ckv_n).astype(BF16)


def _mla_prep(x3, wa, kvg, qg, wqt, wk, wvt, cos_t, sin_a, sin_b, cos_tt, sin_tt, tm):
    b, s, _ = x3.shape
    full = lambda shape: pl.BlockSpec(shape, lambda bi, i: (0, 0))
    tab = pl.BlockSpec((tm, LANES), lambda bi, i: (i, 0))
    tab_t = pl.BlockSpec((MLA_ROPE_DIM // 2, tm), lambda bi, i: (0, i))
    wide = MLA_HEADS * MLA_QK_PAD
    vw = MLA_HEADS * MLA_V_DIM
    return pl.pallas_call(
        _mla_prep_kernel,
        grid=(b, s // tm),
        in_specs=[pl.BlockSpec((None, tm, D_MODEL), lambda bi, i: (bi, i, 0)), full(wa.shape), full(kvg.shape),
                  full(qg.shape), full(wqt.shape), full(wk.shape), full(wvt.shape), tab, tab, tab, tab_t, tab_t],
        out_specs=[pl.BlockSpec((None, wide, tm), lambda bi, i: (bi, 0, i)),
                   pl.BlockSpec((None, tm, wide), lambda bi, i: (bi, i, 0)),
                   pl.BlockSpec((None, vw, tm), lambda bi, i: (bi, 0, i)),
                   pl.BlockSpec((None, tm, D_MODEL), lambda bi, i: (bi, i, 0))],
        out_shape=[jax.ShapeDtypeStruct((b, wide, s), BF16), jax.ShapeDtypeStruct((b, s, wide), BF16),
                   jax.ShapeDtypeStruct((b, vw, s), BF16), jax.ShapeDtypeStruct((b, s, D_MODEL), BF16)],
        compiler_params=_params("parallel", "parallel"),
        name="mla_prep",
    )(x3, wa, kvg, qg, wqt, wk, wvt, cos_t, sin_a, sin_b, cos_tt, sin_tt)


def _dqkv_kernel(x_ref, w_ref, cos_ref, sin_ref, o_ref):
    j = pl.program_id(0)
    h = _dot(x_ref[...], w_ref[...])

    @pl.when(j < 4)
    def _():
        cos_f, sin_f = cos_ref[...], sin_ref[...]
        sc = jnp.where(j < 3, HEAD_DIM ** -0.5, 1.0).astype(F32)
        for hd in range(DIL_HEADS):
            sl = slice(hd * 128, (hd + 1) * 128)
            o_ref[:, sl] = (_rope128(h[:, sl], cos_f, sin_f) * sc).astype(BF16)

    @pl.when(j == 4)
    def _():
        o_ref[...] = h.astype(BF16)


def _dqkv(xb, wd, cos_f, sin_f, seq, tm):
    m = xb.shape[0]
    nrow = seq // tm
    n_col = wd.shape[1] // DIL_WIDTH
    tab = pl.BlockSpec((tm, LANES), lambda j, i: (i % nrow, 0))
    return pl.pallas_call(
        _dqkv_kernel,
        grid=(n_col, m // tm),
        in_specs=[pl.BlockSpec((tm, D_MODEL), lambda j, i: (i, 0)),
                  pl.BlockSpec((D_MODEL, DIL_WIDTH), lambda j, i: (0, j)), tab, tab],
        out_specs=pl.BlockSpec((tm, DIL_WIDTH), lambda j, i: (i, j)),
        out_shape=jax.ShapeDtypeStruct((m, wd.shape[1]), BF16),
        compiler_params=_params("parallel", "parallel"),
        name="dil_qkv",
    )(xb, wd, cos_f, sin_f)


def _flash_kernel(*refs, tq, use_c):
    if use_c:
        qt_ref, k_ref, vt_ref, c_ref, o_ref, acc_ref, s0_ref, s1_ref, kaug_ref = refs
    else:
        qt_ref, k_ref, vt_ref, o_ref, acc_ref, s0_ref, s1_ref = refs
    seq = k_ref.shape[0]
    nq = seq // tq
    if use_c:
        row = lax.broadcasted_iota(jnp.int32, (LANES, tq), 0)

        def build(j, _):
            start = pl.multiple_of(j * tq, tq)
            neg = -LOG2E * c_ref[:, pl.ds(start, tq)]
            hi = neg.astype(BF16).astype(F32)
            mid = (neg - hi).astype(BF16).astype(F32)
            lo = neg - hi - mid
            blk = jnp.where(row == 0, hi, jnp.where(row == 1, mid, jnp.where(row == 2, lo, 0.0)))
            kaug_ref[pl.ds(start, tq), :] = blk.T.astype(BF16)
            return 0

        lax.fori_loop(0, nq, build, 0)

    def scores(i, j, s_ref):
        q = qt_ref[:, pl.ds(pl.multiple_of(i * tq, tq), tq)]
        if use_c:
            ones = (lax.broadcasted_iota(jnp.int32, (LANES, tq), 0) < 3).astype(BF16)
            q = jnp.concatenate([q, ones], axis=0)
        start = pl.multiple_of(j * tq, tq)
        kt = k_ref[pl.ds(start, tq), :]
        if use_c:
            kt = jnp.concatenate([kt, kaug_ref[pl.ds(start, tq), :]], axis=1)
        s_ref[...] = _dot(kt, q)

    def update(j, s_ref, stats):
        m, l = stats
        s = s_ref[...]
        m_new = jnp.maximum(m, jnp.max(s, axis=0, keepdims=True))
        a = jnp.exp2(m - m_new)
        p = jnp.exp2(s - m_new)
        l = a * l + jnp.sum(p, axis=0, keepdims=True)
        start = pl.multiple_of(j * tq, tq)
        acc_ref[...] = a * acc_ref[...] + _dot(vt_ref[:, pl.ds(start, tq)], p.astype(BF16))
        return m_new, l

    half = tq // 2

    def scores_diag(i, s_ref):
        q = qt_ref[:, pl.ds(pl.multiple_of(i * tq, tq), tq)]
        if use_c:
            ones = (lax.broadcasted_iota(jnp.int32, (LANES, tq), 0) < 3).astype(BF16)
            q = jnp.concatenate([q, ones], axis=0)
        start = pl.multiple_of(i * tq, tq)
        kt = k_ref[pl.ds(start, tq), :]
        if use_c:
            kt = jnp.concatenate([kt, kaug_ref[pl.ds(start, tq), :]], axis=1)
        s_ref[0:half, 0:half] = _dot(kt[0:half], q[:, 0:half])
        s_ref[:, half:tq] = _dot(kt, q[:, half:tq])

    def update_diag(j, s_ref, stats):
        m, l = stats
        start = pl.multiple_of(j * tq, tq)
        parts = []
        for lo, nkeys in ((0, half), (half, tq)):
            cols = slice(lo, lo + half)
            key = lax.broadcasted_iota(jnp.int32, (nkeys, half), 0)
            qry = lax.broadcasted_iota(jnp.int32, (nkeys, half), 1) + lo
            s = jnp.where(key <= qry, s_ref[0:nkeys, cols], NEG)
            m_new = jnp.maximum(m[:, cols], jnp.max(s, axis=0, keepdims=True))
            a = jnp.exp2(m[:, cols] - m_new)
            p = jnp.exp2(s - m_new)
            parts.append((m_new, a * l[:, cols] + jnp.sum(p, axis=0, keepdims=True)))
            acc_ref[:, cols] = a * acc_ref[:, cols] + _dot(vt_ref[:, pl.ds(start, nkeys)], p.astype(BF16))
        return tuple(jnp.concatenate([parts[0][n], parts[1][n]], axis=1) for n in range(2))

    def query_tile(i, first, second):
        acc_ref[...] = jnp.zeros_like(acc_ref)

        def pair(jj, stats):
            scores(i, 2 * jj + 1, second)
            stats = update(2 * jj, first, stats)
            scores(i, 2 * jj + 2, first)
            return update(2 * jj + 1, second, stats)

        init = (jnp.full((1, tq), NEG, F32), jnp.zeros((1, tq), F32))
        stats = lax.fori_loop(0, i // 2, pair, init)
        nxt = jnp.minimum(i + 1, nq - 1)

        def odd_tail(stats):
            scores_diag(i, second)
            stats = update(i - 1, first, stats)
            scores(nxt, 0, first)
            return update_diag(i, second, stats)

        def even_tail(stats):
            scores(nxt, 0, second)
            return update_diag(i, first, stats)

        _, l = lax.cond(i % 2 == 1, odd_tail, even_tail, stats)
        o_ref[pl.ds(pl.multiple_of(i * tq, tq), tq), :] = (acc_ref[...] / l).T.astype(o_ref.dtype)
        return 0

    scores(0, 0, s0_ref)

    def query_loop(i, _):
        return lax.cond(((i + 1) // 2) % 2 == 0, lambda: query_tile(i, s0_ref, s1_ref),
                        lambda: query_tile(i, s1_ref, s0_ref))

    lax.fori_loop(0, nq, query_loop, 0)


def _flash(qt, k, vt, c, *, heads, dq, dk, dv, tq):
    b, s, _ = k.shape
    use_c = c is not None
    in_specs = [pl.BlockSpec((None, dq, s), lambda bi, h: (bi, h, 0)),
                pl.BlockSpec((None, s, dk), lambda bi, h: (bi, 0, h)),
                pl.BlockSpec((None, dv, s), lambda bi, h: (bi, h, 0))]
    args = [qt, k, vt]
    scratch = [pltpu.VMEM((dv, tq), F32), pltpu.VMEM((tq, tq), F32), pltpu.VMEM((tq, tq), F32)]
    if use_c:
        in_specs.append(pl.BlockSpec((None, None, 1, s), lambda bi, h: (bi, h, 0, 0)))
        args.append(c)
        scratch.append(pltpu.VMEM((s, LANES), BF16))
    return pl.pallas_call(
        functools.partial(_flash_kernel, tq=tq, use_c=use_c),
        grid=(b, heads),
        in_specs=in_specs,
        out_specs=pl.BlockSpec((None, s, dv), lambda bi, h: (bi, 0, h)),
        out_shape=jax.ShapeDtypeStruct((b, s, heads * dv), BF16),
        scratch_shapes=scratch,
        compiler_params=_params("parallel", "parallel"),
        name="flash_fox" if use_c else "flash_mla",
    )(*args)


def _dilated_kernel(q_ref, kc_ref, kp_ref, vc_ref, vp_ref, o_ref, lse_ref, qf, kf, vf, *, dil, tn):
    i = pl.program_id(1)
    per_class = tn // dil
    qf[...] = q_ref[...].astype(F32)
    kf[0:tn, :] = kp_ref[...].astype(F32)
    kf[tn:2 * tn, :] = kc_ref[...].astype(F32)
    vf[0:tn, :] = vp_ref[...].astype(F32)
    vf[tn:2 * tn, :] = vc_ref[...].astype(F32)
    row = lax.broadcasted_iota(jnp.int32, (DIL_SPAN, 2 * DIL_SPAN), 0)
    col = lax.broadcasted_iota(jnp.int32, (DIL_SPAN, 2 * DIL_SPAN), 1)
    back = row + DIL_SPAN - col
    in_band = jnp.where(back >= 0, jnp.where(back <= DIL_SPAN, 0.0, NEG), NEG)
    first_band = jnp.where(col >= jnp.where(i > 0, 0, DIL_SPAN), in_band, NEG)
    for r in range(dil):
        k_r = jnp.concatenate([kf[pl.ds(tn - DIL_SPAN * dil + r, DIL_SPAN, stride=dil), :],
                               kf[pl.ds(tn + r, per_class, stride=dil), :]], axis=0).astype(BF16)
        v_r = jnp.concatenate([vf[pl.ds(tn - DIL_SPAN * dil + r, DIL_SPAN, stride=dil), :],
                               vf[pl.ds(tn + r, per_class, stride=dil), :]], axis=0).astype(BF16)
        q_r = qf[pl.ds(r, per_class, stride=dil), :].astype(BF16)
        for a in range(per_class // DIL_SPAN):
            lo = a * DIL_SPAN
            s = _dot_nt(q_r[lo:lo + DIL_SPAN], k_r[lo:lo + 2 * DIL_SPAN]) + (first_band if a == 0 else in_band)
            m = jnp.max(s, axis=-1, keepdims=True)
            p = jnp.exp(s - m)
            l = jnp.sum(p, axis=-1, keepdims=True)
            rows = pl.ds(r + lo * dil, DIL_SPAN, stride=dil)
            o_ref[rows, :] = _dot(p.astype(BF16), v_r[lo:lo + 2 * DIL_SPAN]) / l
            lse_ref[rows, :] = jnp.broadcast_to(m + jnp.log(l), (DIL_SPAN, LANES))


def _dilated_group(dqkv3, g, dil, tn):
    batch, seq, _ = dqkv3.shape
    assert tn % (DIL_SPAN * dil) == 0
    blk = lambda col, prev: pl.BlockSpec(
        (None, tn, HEAD_DIM), lambda b, i, h: (b, jnp.maximum(i - 1, 0) if prev else i, col * DIL_HEADS + h))
    out = pl.BlockSpec((None, tn, HEAD_DIM), lambda b, i, h: (b, i, h))
    o, lse = pl.pallas_call(
        functools.partial(_dilated_kernel, dil=dil, tn=tn),
        grid=(batch, seq // tn, DIL_HEADS),
        in_specs=[blk(g, False), blk(3, False), blk(3, True), blk(4, False), blk(4, True)],
        out_specs=[out, out],
        out_shape=[jax.ShapeDtypeStruct((batch, seq, DIL_WIDTH), F32)] * 2,
        scratch_shapes=[pltpu.VMEM((tn, HEAD_DIM), F32), pltpu.VMEM((2 * tn, HEAD_DIM), F32),
                        pltpu.VMEM((2 * tn, HEAD_DIM), F32)],
        compiler_params=_params("parallel", "parallel", "parallel"),
        name=f"dilated_{dil}",
    )(dqkv3, dqkv3, dqkv3, dqkv3, dqkv3)
    return o.reshape(batch * seq, DIL_WIDTH), lse.reshape(batch * seq, DIL_WIDTH)


def _even_out_kernel(om_ref, o0, o1, o2, l0, l1, l2, wm_ref, wd_ref, x_ref, g_ref, b_ref, o_ref, ob_ref):
    ls = [l0[...], l1[...], l2[...]]
    mx = jnp.maximum(jnp.maximum(ls[0], ls[1]), ls[2])
    es = [jnp.exp(v - mx) for v in ls]
    den = es[0] + es[1] + es[2]
    o_dil = ((es[0] / den) * o0[...] + (es[1] / den) * o1[...] + (es[2] / den) * o2[...]).astype(BF16)
    y = _dot(om_ref[...], wm_ref[...]) + _dot(o_dil, wd_ref[...])
    out = _layer_norm(ALPHA * x_ref[...] + y, g_ref[...], b_ref[...])
    o_ref[...] = out
    ob_ref[...] = out.astype(BF16)


def _even_out(o_mla, outs, lses, w_mla, w_dil, x, g, b, tm):
    m = x.shape[0]
    row = lambda width: pl.BlockSpec((tm, width), lambda i: (i, 0))
    full = lambda shape: pl.BlockSpec(shape, lambda i: (0, 0))
    return pl.pallas_call(
        _even_out_kernel,
        grid=(m // tm,),
        in_specs=[row(o_mla.shape[1])] + [row(DIL_WIDTH)] * 6
        + [full(w_mla.shape), full(w_dil.shape), row(D_MODEL), full(g.shape), full(b.shape)],
        out_specs=[row(D_MODEL), row(D_MODEL)],
        out_shape=[jax.ShapeDtypeStruct((m, D_MODEL), F32), jax.ShapeDtypeStruct((m, D_MODEL), BF16)],
        compiler_params=_params("parallel"),
        name="even_out_ln",
    )(o_mla, *outs, *lses, w_mla, w_dil, x, g, b)


def _odd_out_kernel(a_ref, w_ref, x_ref, g_ref, b_ref, o_ref):
    o_ref[...] = _layer_norm(ALPHA * x_ref[...] + _dot(a_ref[...], w_ref[...]), g_ref[...], b_ref[...])


def _odd_out(a, w, x, g, b, tm):
    m = x.shape[0]
    row = lambda width: pl.BlockSpec((tm, width), lambda i: (i, 0))
    full = lambda shape: pl.BlockSpec(shape, lambda i: (0, 0))
    return pl.pallas_call(
        _odd_out_kernel,
        grid=(m // tm,),
        in_specs=[row(a.shape[1]), full(w.shape), row(D_MODEL), full(g.shape), full(b.shape)],
        out_specs=row(D_MODEL),
        out_shape=jax.ShapeDtypeStruct((m, D_MODEL), F32),
        compiler_params=_params("parallel"),
        name="odd_out_ln",
    )(a, w, x, g, b)


def _router_kernel(x_ref, rwh_ref, rwl_ref, rb_ref, route_ref, cnt_ref, carry_ref):
    @pl.when(pl.program_id(0) == 0)
    def _():
        carry_ref[...] = jnp.zeros_like(carry_ref)

    x = x_ref[...]
    xh = x.astype(BF16)
    xl = (x - xh.astype(F32)).astype(BF16)
    logits = _dot(xh, rwh_ref[...]) + (_dot(xh, rwl_ref[...]) + _dot(xl, rwh_ref[...])) + rb_ref[...]
    tm = logits.shape[0]
    lane = lax.broadcasted_iota(jnp.int32, (tm, LANES), 1)
    l1 = jnp.max(logits, axis=-1, keepdims=True)
    i1 = jnp.min(jnp.where(logits == l1, lane, LANES), axis=-1, keepdims=True)
    rest = jnp.where(lane == i1, NEG, logits)
    l2 = jnp.max(rest, axis=-1, keepdims=True)
    i2 = jnp.min(jnp.where(rest == l2, lane, LANES), axis=-1, keepdims=True)
    e = jnp.exp(l2 - l1)
    w1 = 1.0 / (1.0 + e)
    w2 = e / (1.0 + e)
    hot1 = (lane == i1).astype(F32)
    hot2 = (lane == i2).astype(F32)
    cnt = hot1 + hot2
    strict = (lax.broadcasted_iota(jnp.int32, (tm, tm), 1) < lax.broadcasted_iota(jnp.int32, (tm, tm), 0)).astype(BF16)
    before = _dot(strict, cnt.astype(BF16)) + carry_ref[...]
    r1 = jnp.sum(before * hot1, axis=-1, keepdims=True)
    r2 = jnp.sum(before * hot2, axis=-1, keepdims=True)
    vals = (i1.astype(F32), i2.astype(F32), w1, w2, r1, r2)
    route = jnp.zeros((tm, LANES), F32)
    for idx, val in enumerate(vals):
        route = jnp.where(lane == idx, val, route)
    route_ref[...] = route
    total = carry_ref[...] + jnp.sum(cnt, axis=0, keepdims=True)
    carry_ref[...] = total
    cnt_ref[...] = jnp.broadcast_to(total, cnt_ref.shape)


def _router(x, rw, rb, tm):
    m = x.shape[0]
    full = lambda shape: pl.BlockSpec(shape, lambda i: (0, 0))
    rw_hi = rw.astype(BF16)
    rw_lo = (rw - rw_hi.astype(F32)).astype(BF16)
    return pl.pallas_call(
        _router_kernel,
        grid=(m // tm,),
        in_specs=[pl.BlockSpec((tm, D_MODEL), lambda i: (i, 0)), full(rw.shape), full(rw.shape), full(rb.shape)],
        out_specs=[pl.BlockSpec((tm, LANES), lambda i: (i, 0)), pl.BlockSpec((8, LANES), lambda i: (0, 0))],
        out_shape=[jax.ShapeDtypeStruct((m, LANES), F32), jax.ShapeDtypeStruct((8, LANES), F32)],
        scratch_shapes=[pltpu.VMEM((1, LANES), F32)],
        compiler_params=_params("arbitrary"),
        name="moe_router",
    )(x, rw_hi, rw_lo, rb)


def _silu_mul(g, u):
    return g * (1.0 / (1.0 + jnp.exp(-g))) * u


def _ffn_up_kernel(x_ref, wg_ref, wu_ref, o_ref, wgb_ref, wub_ref):
    @pl.when(pl.program_id(1) == 0)
    def _():
        wgb_ref[...] = wg_ref[...].astype(BF16)
        wub_ref[...] = wu_ref[...].astype(BF16)

    x = x_ref[...]
    o_ref[...] = _silu_mul(_dot(x, wgb_ref[...]), _dot(x, wub_ref[...])).astype(BF16)


def _ffn_up(xb, wg, wu, tm, tn):
    m = xb.shape[0]
    return pl.pallas_call(
        _ffn_up_kernel,
        grid=(D_FF // tn, m // tm),
        in_specs=[pl.BlockSpec((tm, D_MODEL), lambda j, i: (i, 0)),
                  pl.BlockSpec((D_MODEL, tn), lambda j, i: (0, j)),
                  pl.BlockSpec((D_MODEL, tn), lambda j, i: (0, j))],
        out_specs=pl.BlockSpec((tm, tn), lambda j, i: (i, j)),
        out_shape=jax.ShapeDtypeStruct((m, D_FF), BF16),
        scratch_shapes=[pltpu.VMEM((D_MODEL, tn), BF16), pltpu.VMEM((D_MODEL, tn), BF16)],
        compiler_params=_params("arbitrary", "arbitrary"),
        name="ffn_up",
    )(xb, wg, wu)


def _ffn_down_ln_kernel(h_ref, w_ref, x_ref, g_ref, b_ref, o_ref, ob_ref, acc_ref):
    k = pl.program_id(1)

    @pl.when(k == 0)
    def _():
        acc_ref[...] = jnp.zeros_like(acc_ref)

    acc_ref[...] += _dot(h_ref[...], w_ref[...])

    @pl.when(k == pl.num_programs(1) - 1)
    def _():
        out = _layer_norm(ALPHA * x_ref[...] + acc_ref[...], g_ref[...], b_ref[...])
        o_ref[...] = out
        ob_ref[...] = out.astype(BF16)


def _ffn_down_ln(h, wd, x, g, b, tm, tk):
    m = x.shape[0]
    row = pl.BlockSpec((tm, D_MODEL), lambda i, k: (i, 0))
    vec = pl.BlockSpec((1, D_MODEL), lambda i, k: (0, 0))
    deep = pl.Buffered(STREAM_BUFFERS)

    def outer(h_hbm, w_hbm, x_hbm, g_hbm, b_hbm, o_hbm, ob_hbm, acc_ref):
        pltpu.emit_pipeline(
            lambda *refs: _ffn_down_ln_kernel(*refs, acc_ref),
            grid=(m // tm, D_FF // tk),
            in_specs=[pl.BlockSpec((tm, tk), lambda i, k: (i, k), pipeline_mode=deep),
                      pl.BlockSpec((tk, D_MODEL), lambda i, k: (k, 0), pipeline_mode=deep), row, vec, vec],
            out_specs=[row, row],
        )(h_hbm, w_hbm, x_hbm, g_hbm, b_hbm, o_hbm, ob_hbm)

    hbm = pl.BlockSpec(memory_space=pl.ANY)
    return pl.pallas_call(
        outer,
        in_specs=[hbm] * 5,
        out_specs=[hbm, hbm],
        out_shape=[jax.ShapeDtypeStruct((m, D_MODEL), F32), jax.ShapeDtypeStruct((m, D_MODEL), BF16)],
        scratch_shapes=[pltpu.VMEM((tm, D_MODEL), F32)],
        compiler_params=pltpu.CompilerParams(vmem_limit_bytes=VMEM_LIMIT),
        name="ffn_down_ln",
    )(h, wd, x, g, b)


def _proj_kernel(x_ref, w_ref, o_ref):
    o_ref[...] = _dot(x_ref[...], w_ref[...]).astype(BF16)


def _proj(xb, w, tm, tn):
    m = xb.shape[0]
    n = w.shape[1]
    return pl.pallas_call(
        _proj_kernel,
        grid=(n // tn, m // tm),
        in_specs=[pl.BlockSpec((tm, D_MODEL), lambda j, i: (i, 0)), pl.BlockSpec((D_MODEL, tn), lambda j, i: (0, j))],
        out_specs=pl.BlockSpec((tm, tn), lambda j, i: (i, j)),
        out_shape=jax.ShapeDtypeStruct((m, n), BF16),
        compiler_params=_params("parallel", "parallel"),
        name="proj",
    )(xb, w)


def _proj_t_kernel(w_ref, x_ref, o_ref, *, scale):
    o_ref[...] = (_dot_nt(w_ref[...], x_ref[...]) * scale).astype(BF16)


def _proj_t(xb3, wt, scale, tm, tn):
    b, s, _ = xb3.shape
    n = wt.shape[0]
    return pl.pallas_call(
        functools.partial(_proj_t_kernel, scale=scale),
        grid=(n // tn, b, s // tm),
        in_specs=[pl.BlockSpec((tn, D_MODEL), lambda j, bi, i: (j, 0)),
                  pl.BlockSpec((None, tm, D_MODEL), lambda j, bi, i: (bi, i, 0))],
        out_specs=pl.BlockSpec((None, tn, tm), lambda j, bi, i: (bi, j, i)),
        out_shape=jax.ShapeDtypeStruct((b, n, s), BF16),
        compiler_params=_params("parallel", "parallel", "parallel"),
        name="proj_t",
    )(wt, xb3)


def _fgate_kernel(x_ref, w_ref, b_ref, c_ref, carry_ref):
    @pl.when(pl.program_id(1) == 0)
    def _():
        carry_ref[...] = jnp.zeros_like(carry_ref)

    z = _dot(x_ref[...], w_ref[...]) + b_ref[...]
    log_f = jnp.minimum(z, 0.0) - jnp.log(1.0 + jnp.exp(-jnp.abs(z)))
    tm = z.shape[0]
    tri = (lax.broadcasted_iota(jnp.int32, (tm, tm), 1) <= lax.broadcasted_iota(jnp.int32, (tm, tm), 0)).astype(F32)
    c = jnp.dot(tri, log_f, preferred_element_type=F32, precision=lax.Precision.HIGHEST) + carry_ref[...]
    c_ref[...] = c
    carry_ref[...] = c[tm - 1:tm, :]


def _fgate(xb3, wf, bf, tm):
    b, s, _ = xb3.shape
    return pl.pallas_call(
        _fgate_kernel,
        grid=(b, s // tm),
        in_specs=[pl.BlockSpec((None, tm, D_MODEL), lambda bi, i: (bi, i, 0)),
                  pl.BlockSpec(wf.shape, lambda bi, i: (0, 0)), pl.BlockSpec(bf.shape, lambda bi, i: (0, 0))],
        out_specs=pl.BlockSpec((None, tm, LANES), lambda bi, i: (bi, i, 0)),
        out_shape=jax.ShapeDtypeStruct((b, s, LANES), F32),
        scratch_shapes=[pltpu.VMEM((1, LANES), F32)],
        compiler_params=_params("parallel", "arbitrary"),
        name="fox_gate",
    )(xb3, wf, bf)


def _dispatch_kernel(fill_ref, pos_ref, x_ref, xs_ref, zero_ref, sem, zero_sem):
    tm = x_ref.shape[0]

    @pl.when(pl.program_id(0) == 0)
    def _():
        zero_ref[...] = jnp.zeros_like(zero_ref)
        rows = zero_ref.shape[0]

        def fill(f):
            start = pl.multiple_of(jnp.maximum(fill_ref[f], 0), rows)
            return pltpu.make_async_copy(zero_ref, xs_ref.at[pl.ds(start, rows), :], zero_sem)

        for f in range(fill_ref.shape[0]):
            @pl.when(fill_ref[f] >= 0)
            def _():
                fill(f).start()

        for f in range(fill_ref.shape[0]):
            @pl.when(fill_ref[f] >= 0)
            def _():
                fill(f).wait()

    def copy(r, k):
        return pltpu.make_async_copy(x_ref.at[pl.ds(r, 1), :], xs_ref.at[pl.ds(pos_ref[0, 0, 2 * r + k], 1), :], sem)

    def start(r, _):
        copy(r, 0).start(priority=0)
        copy(r, 1).start(priority=1)
        return 0

    def wait(r, _):
        copy(r, 0).wait()
        copy(r, 1).wait()
        return 0

    lax.fori_loop(0, tm, start, 0, unroll=DMA_UNROLL)
    lax.fori_loop(0, tm, wait, 0, unroll=DMA_UNROLL)


def _dispatch(x, pos, fill_rows, n_rows, tile, tm):
    m, width = x.shape
    pos3 = pos.reshape(m // tm, 1, 2 * tm)
    return pl.pallas_call(
        _dispatch_kernel,
        grid_spec=pltpu.PrefetchScalarGridSpec(
            num_scalar_prefetch=1,
            grid=(m // tm,),
            in_specs=[pl.BlockSpec((1, 1, 2 * tm), lambda i, fill: (i, 0, 0), memory_space=pltpu.SMEM),
                      pl.BlockSpec((tm, width), lambda i, fill: (i, 0))],
            out_specs=pl.BlockSpec(memory_space=pl.ANY),
            scratch_shapes=[pltpu.VMEM((tile, width), x.dtype), pltpu.SemaphoreType.DMA(()),
                            pltpu.SemaphoreType.DMA(())],
        ),
        out_shape=jax.ShapeDtypeStruct((n_rows, width), x.dtype),
        compiler_params=_params("arbitrary"),
        name="moe_dispatch",
    )(fill_rows, pos3, x)


def _moe_up_kernel(te_ref, nu_ref, nx_ref, x_ref, wg_hbm, wu_hbm, o_ref, sg_ref, su_ref, wb_ref, sems, *, tn):
    groups = [(lo, min(MXU_COLS, tn - lo)) for lo in range(0, tn, MXU_COLS)]
    j = pl.program_id(0)
    t = pl.program_id(1)
    live = t < nu_ref[0]
    fresh = jnp.logical_or(t == 0, te_ref[t] != te_ref[jnp.maximum(t - 1, 0)])

    def weight_copies(e, jj):
        cols = pl.ds(pl.multiple_of(jj * tn, LANES), tn)
        return (pltpu.make_async_copy(wg_hbm.at[e, :, cols], sg_ref, sems.at[0]),
                pltpu.make_async_copy(wu_hbm.at[e, :, cols], su_ref, sems.at[1]))

    def start(e, jj):
        for c in weight_copies(e, jj):
            c.start()

    @pl.when(jnp.logical_and(j == 0, t == 0))
    def _():
        start(te_ref[0], 0)

    @pl.when(jnp.logical_and(live, fresh))
    def _():
        for c in weight_copies(te_ref[t], j):
            c.wait()
        def cast_rows(c, _):
            rows = pl.ds(pl.multiple_of(c * 256, 256), 256)
            for lo, width in groups:
                wb_ref[rows, 2 * lo:2 * lo + width] = sg_ref[rows, lo:lo + width].astype(BF16)
                wb_ref[rows, 2 * lo + width:2 * lo + 2 * width] = su_ref[rows, lo:lo + width].astype(BF16)
            return 0

        lax.fori_loop(0, D_MODEL // 256, cast_rows, 0)
        nxt = nx_ref[t]

        @pl.when(nxt >= 0)
        def _():
            start(nxt, j)

        @pl.when(jnp.logical_and(nxt < 0, j + 1 < pl.num_programs(0)))
        def _():
            start(te_ref[0], j + 1)

    @pl.when(live)
    def _():
        x = x_ref[...].astype(BF16)
        for lo, width in groups:
            gu = _dot(x, wb_ref[:, 2 * lo:2 * lo + 2 * width])
            o_ref[:, lo:lo + width] = _silu_mul(gu[:, :width], gu[:, width:]).astype(BF16)

    @pl.when(jnp.logical_not(live))
    def _():
        o_ref[...] = jnp.zeros_like(o_ref)


def _moe_up(xs, wg, wu, tile_expert, n_used, next_expert, tm, tn):
    p = xs.shape[0]
    live = lambda t, nu: jnp.minimum(t, nu[0] - 1)
    return pl.pallas_call(
        functools.partial(_moe_up_kernel, tn=tn),
        grid_spec=pltpu.PrefetchScalarGridSpec(
            num_scalar_prefetch=3,
            grid=(D_FF // tn, p // tm),
            in_specs=[pl.BlockSpec((tm, D_MODEL), lambda j, t, te, nu, nx: (live(t, nu), 0)),
                      pl.BlockSpec(memory_space=pl.ANY), pl.BlockSpec(memory_space=pl.ANY)],
            out_specs=pl.BlockSpec((tm, tn), lambda j, t, te, nu, nx: (t, j)),
            scratch_shapes=[pltpu.VMEM((D_MODEL, tn), F32), pltpu.VMEM((D_MODEL, tn), F32),
                            pltpu.VMEM((D_MODEL, 2 * tn), BF16), pltpu.SemaphoreType.DMA((2,))],
        ),
        out_shape=jax.ShapeDtypeStruct((p, D_FF), BF16),
        compiler_params=_params("arbitrary", "arbitrary"),
        name="moe_up",
    )(tile_expert, n_used, next_expert, xs, wg, wu)


def _moe_down_kernel(te_ref, nu_ref, nx_ref, h_ref, w_hbm, o_ref, st_ref, wb_ref, sem, *, tn):
    j = pl.program_id(0)
    t = pl.program_id(1)
    live = t < nu_ref[0]
    fresh = jnp.logical_or(t == 0, te_ref[t] != te_ref[jnp.maximum(t - 1, 0)])

    def weight_copy(e, jj):
        cols = pl.ds(pl.multiple_of(jj * tn, LANES), tn)
        return pltpu.make_async_copy(w_hbm.at[e, :, cols], st_ref, sem)

    @pl.when(jnp.logical_and(j == 0, t == 0))
    def _():
        weight_copy(te_ref[0], 0).start()

    @pl.when(jnp.logical_and(live, fresh))
    def _():
        weight_copy(te_ref[t], j).wait()

        def cast_rows(c, _):
            rows = pl.ds(pl.multiple_of(c * 256, 256), 256)
            wb_ref[rows, :] = st_ref[rows, :].astype(BF16)
            return 0

        lax.fori_loop(0, D_FF // 256, cast_rows, 0)
        nxt = nx_ref[t]

        @pl.when(nxt >= 0)
        def _():
            weight_copy(nxt, j).start()

        @pl.when(jnp.logical_and(nxt < 0, j + 1 < pl.num_programs(0)))
        def _():
            weight_copy(te_ref[0], j + 1).start()

    @pl.when(live)
    def _():
        h = h_ref[...]
        for lo in range(0, tn, 512):
            o_ref[:, lo:lo + 512] = _dot(h, wb_ref[:, lo:lo + 512])

    @pl.when(jnp.logical_not(live))
    def _():
        o_ref[...] = jnp.zeros_like(o_ref)


def _moe_down(hs, wd, tile_expert, n_used, next_expert, tm, tn):
    p = hs.shape[0]
    live = lambda t, nu: jnp.minimum(t, nu[0] - 1)
    return pl.pallas_call(
        functools.partial(_moe_down_kernel, tn=tn),
        grid_spec=pltpu.PrefetchScalarGridSpec(
            num_scalar_prefetch=3,
            grid=(D_MODEL // tn, p // tm),
            in_specs=[pl.BlockSpec((tm, D_FF), lambda j, t, te, nu, nx: (live(t, nu), 0)),
                      pl.BlockSpec(memory_space=pl.ANY)],
            out_specs=pl.BlockSpec((tm, tn), lambda j, t, te, nu, nx: (t, j)),
            scratch_shapes=[pltpu.VMEM((D_FF, tn), F32), pltpu.VMEM((D_FF, tn), BF16),
                            pltpu.SemaphoreType.DMA(())],
        ),
        out_shape=jax.ShapeDtypeStruct((p, D_MODEL), F32),
        compiler_params=_params("arbitrary", "arbitrary"),
        name="moe_down",
    )(tile_expert, n_used, next_expert, hs, wd)


def _combine_ln_kernel(pos_ref, ys_ref, x_ref, route_ref, g_ref, b_ref, o_ref, buf0, buf1, sem):
    tm = x_ref.shape[0]

    def copies(r):
        c0 = pltpu.make_async_copy(ys_ref.at[pl.ds(pos_ref[0, 0, 2 * r], 1), :], buf0.at[pl.ds(r, 1), :], sem)
        c1 = pltpu.make_async_copy(ys_ref.at[pl.ds(pos_ref[0, 0, 2 * r + 1], 1), :], buf1.at[pl.ds(r, 1), :], sem)
        return c0, c1

    def start(r, _):
        c0, c1 = copies(r)
        c0.start(priority=0)
        c1.start(priority=1)
        return 0

    def wait(r, _):
        c0, c1 = copies(r)
        c0.wait()
        c1.wait()
        return 0

    lax.fori_loop(0, tm, start, 0, unroll=DMA_UNROLL)
    lax.fori_loop(0, tm, wait, 0, unroll=DMA_UNROLL)
    route = route_ref[...]
    y = route[:, 2:3] * buf0[...] + route[:, 3:4] * buf1[...]
    o_ref[...] = _layer_norm(ALPHA * x_ref[...] + y, g_ref[...], b_ref[...])


def _combine_ln(ys, pos, x, route, g, b, tm):
    m = x.shape[0]
    pos3 = pos.reshape(m // tm, 1, 2 * tm)
    vec = pl.BlockSpec((1, D_MODEL), lambda i: (0, 0))
    return pl.pallas_call(
        _combine_ln_kernel,
        grid=(m // tm,),
        in_specs=[pl.BlockSpec((1, 1, 2 * tm), lambda i: (i, 0, 0), memory_space=pltpu.SMEM),
                  pl.BlockSpec(memory_space=pl.ANY),
                  pl.BlockSpec((tm, D_MODEL), lambda i: (i, 0)),
                  pl.BlockSpec((tm, LANES), lambda i: (i, 0)), vec, vec],
        out_specs=pl.BlockSpec((tm, D_MODEL), lambda i: (i, 0)),
        out_shape=jax.ShapeDtypeStruct((m, D_MODEL), F32),
        scratch_shapes=[pltpu.VMEM((tm, D_MODEL), F32), pltpu.VMEM((tm, D_MODEL), F32), pltpu.SemaphoreType.DMA(())],
        compiler_params=_params("arbitrary"),
        name="moe_combine_ln",
    )(pos3, ys, x, route, g, b)


def _rope_tables(seq):
    def angles(dim):
        inv_freq = 1.0 / (ROPE_THETA ** (jnp.arange(0, dim, 2, dtype=F32) / dim))
        ang = jnp.arange(seq, dtype=F32)[:, None] * inv_freq[None, :]
        return jnp.cos(ang), jnp.sin(ang)

    c128, s128 = angles(HEAD_DIM)
    cos_f = jnp.concatenate([c128, c128], axis=-1)
    sin_f = jnp.concatenate([-s128, s128], axis=-1)
    c64, s64 = angles(MLA_ROPE_DIM)
    z32 = jnp.zeros_like(c64)
    cos_t = jnp.concatenate([c64, c64, z32, z32], axis=-1)
    sin_a = jnp.concatenate([-s64, z32, z32, z32], axis=-1)
    sin_b = jnp.concatenate([z32, s64, z32, z32], axis=-1)
    return cos_f, sin_f, cos_t, sin_a, sin_b, c64.T, s64.T


def _pad_cols(a, width):
    return jnp.pad(a, ((0, 0), (0, width - a.shape[1])))


def _row(v):
    return v.reshape(1, -1).astype(F32)


def _even_layer(x, batch, seq, tables, w_in, q_norm, w_q_b, kv_norm, w_kv_b, w_out, ln1_g, ln1_b,
                w_gate, w_up, w_down, ln2_g, ln2_b):
    cos_f, sin_f, cos_t, sin_a, sin_b, cos_tt, sin_tt = tables
    tm = min(ROW_TILE, seq)
    tw = min(WIDE_ROW_TILE, seq)
    wa = jnp.concatenate([w_in[:, OFF_CKV:OFF_KROPE], _pad_cols(w_in[:, OFF_KROPE:OFF_DQ], LANES),
                          _pad_cols(w_in[:, OFF_CQ:OFF_CKV], MLA_Q_RANK_PAD)], axis=1).astype(BF16)
    qg = _pad_cols(_row(q_norm), MLA_Q_RANK_PAD)
    wq = jnp.pad(w_q_b.reshape(MLA_Q_RANK, MLA_HEADS, MLA_QK_DIM),
                 ((0, MLA_Q_RANK_PAD - MLA_Q_RANK), (0, 0), (0, MLA_QK_PAD - MLA_QK_DIM)))
    wqt = wq.reshape(MLA_Q_RANK_PAD, MLA_HEADS * MLA_QK_PAD).T.astype(BF16)
    wkv3 = w_kv_b.reshape(MLA_KV_RANK, MLA_HEADS, MLA_NOPE_DIM + MLA_V_DIM)
    wk = wkv3[:, :, :MLA_NOPE_DIM].reshape(MLA_KV_RANK, -1).astype(BF16)
    wvt = wkv3[:, :, MLA_NOPE_DIM:].reshape(MLA_KV_RANK, -1).T.astype(BF16)
    qt_mla, k_mla, vt_mla, xb3 = _mla_prep(x.reshape(batch, seq, D_MODEL), wa, _row(kv_norm), qg, wqt, wk, wvt,
                                           cos_t, sin_a, sin_b, cos_tt, sin_tt, tm)
    xb = xb3.reshape(batch * seq, D_MODEL)
    tq = min(FLASH_TQ, seq)
    o_mla = _flash(qt_mla, k_mla, vt_mla, None, heads=MLA_HEADS, dq=MLA_QK_PAD, dk=MLA_QK_PAD, dv=MLA_V_DIM,
                   tq=tq)
    o_mla = o_mla.reshape(batch * seq, -1)

    dqkv = _dqkv(xb, w_in[:, OFF_DQ:].astype(BF16), cos_f, sin_f, seq, tw)
    dqkv3 = dqkv.reshape(batch, seq, -1)
    outs, lses = [], []
    for g, (window, dil) in enumerate(DIL_PATTERNS):
        assert window == DIL_SPAN * dil
        o_g, lse_g = _dilated_group(dqkv3, g, dil, min(seq, max(DIL_TILE, DIL_SPAN * dil)))
        outs.append(o_g)
        lses.append(lse_g)
    n_mla = MLA_HEADS * MLA_V_DIM
    wo = w_out.astype(BF16)
    x1, x1b = _even_out(o_mla, outs, lses, wo[:n_mla], wo[n_mla:], x, _row(ln1_g), _row(ln1_b), tm)
    hmid = _ffn_up(x1b, w_gate, w_up, tw, FFN_COL_TILE)
    return _ffn_down_ln(hmid, w_down.astype(BF16), x1, _row(ln2_g), _row(ln2_b), tm, FFN_K_TILE)


def _odd_layer(x, xb, batch, seq, w_qkv, w_f, b_f, w_out, ln1_g, ln1_b, router_w, router_b,
               exp_w_gate, exp_w_up, exp_w_down, ln2_g, ln2_b):
    m = batch * seq
    tm = min(ROW_TILE, seq)
    tw = min(WIDE_ROW_TILE, seq)
    xb3 = xb.reshape(batch, seq, D_MODEL)
    wb = w_qkv.astype(BF16)
    qt = _proj_t(xb3, wb[:, :FOX_WIDTH].T, HEAD_DIM ** -0.5 * LOG2E, tw, QKV_COL_TILE)
    k = _proj(xb, wb[:, FOX_WIDTH:2 * FOX_WIDTH], tw, QKV_COL_TILE).reshape(batch, seq, FOX_WIDTH)
    vt = _proj_t(xb3, wb[:, 2 * FOX_WIDTH:].T, 1.0, tw, QKV_COL_TILE)
    c = _fgate(xb3, _pad_cols(w_f, LANES).astype(BF16), _pad_cols(_row(b_f), LANES), min(GATE_ROW_TILE, seq))
    c_t = jnp.transpose(c[:, :, :FOX_HEADS], (0, 2, 1)).reshape(batch, FOX_HEADS, 1, seq)
    tq = min(FLASH_TQ, seq)
    o = _flash(qt, k, vt, c_t, heads=FOX_HEADS, dq=HEAD_DIM, dk=HEAD_DIM, dv=HEAD_DIM, tq=tq)
    rb = jnp.full((1, LANES), NEG, F32).at[0, :N_EXPERTS].set(router_b.astype(F32))
    x1 = _odd_out(o.reshape(m, -1), w_out.astype(BF16), x, _row(ln1_g), _row(ln1_b), tm)
    route, counts = _router(x1, _pad_cols(router_w, LANES), rb, tm)
    tile = MOE_TILE
    n_tiles = (2 * m) // tile + N_EXPERTS
    cnt = counts[0, :N_EXPERTS].astype(jnp.int32)
    tiles_per = (cnt + tile - 1) // tile
    tile_end = jnp.cumsum(tiles_per)
    offset = (tile_end - tiles_per) * tile
    idx = route[:, 0:2].astype(jnp.int32)
    pos = (offset[idx] + route[:, 4:6].astype(jnp.int32)).reshape(-1)
    n_used = tile_end[-1:]
    tile_ids = jnp.arange(n_tiles, dtype=jnp.int32)
    tile_expert = jnp.minimum(jnp.sum((tile_end[None, :] <= tile_ids[:, None]).astype(jnp.int32), axis=1),
                              N_EXPERTS - 1)
    experts = jnp.arange(N_EXPERTS, dtype=jnp.int32)
    last_tile = jnp.where(tiles_per > 0, tile_end - 1, -1)
    unused = n_used[0] + experts
    unused = jnp.where(unused < n_tiles, unused, -1)
    fill_tiles = jnp.concatenate([last_tile, unused])
    fill_rows = jnp.where(fill_tiles >= 0, fill_tiles * tile, -1).astype(jnp.int32)
    xs = _dispatch(x1, pos, fill_rows, n_tiles * tile, tile, tm)
    later = jnp.where((experts[None, :] > experts[:, None]) & (tiles_per[None, :] > 0), experts[None, :], N_EXPERTS)
    next_nonempty = jnp.min(later, axis=1)
    next_expert = jnp.where(next_nonempty < N_EXPERTS, next_nonempty, -1)[tile_expert].astype(jnp.int32)
    hs = _moe_up(xs, exp_w_gate, exp_w_up, tile_expert, n_used, next_expert, tile, MOE_COL_TILE)
    ys = _moe_down(hs, exp_w_down, tile_expert, n_used, next_expert, tile, MOE_DOWN_COL_TILE)
    return _combine_ln(ys, pos, x1, route, _row(ln2_g), _row(ln2_b), tm)


def kernel(x, ev_w_in, ev_q_norm, ev_w_q_b, ev_kv_norm, ev_w_kv_b, ev_w_out, ev_ln1_g, ev_ln1_b, ev_ffn_w_gate, ev_ffn_w_up, ev_ffn_w_down, ev_ln2_g, ev_ln2_b, od_w_qkv, od_w_f, od_b_f, od_w_out, od_ln1_g, od_ln1_b, od_router_w, od_router_b, od_exp_w_gate, od_exp_w_up, od_exp_w_down, od_ln2_g, od_ln2_b):
    batch, seq, _ = x.shape
    tables = _rope_tables(seq)
    h = x.reshape(batch * seq, D_MODEL)
    hb = None
    for layer in range(DEPTH):
        i = layer // 2
        if layer % 2 == 0:
            h, hb = _even_layer(h, batch, seq, tables, ev_w_in[i], ev_q_norm[i], ev_w_q_b[i], ev_kv_norm[i],
                                ev_w_kv_b[i], ev_w_out[i], ev_ln1_g[i], ev_ln1_b[i], ev_ffn_w_gate[i],
                                ev_ffn_w_up[i], ev_ffn_w_down[i], ev_ln2_g[i], ev_ln2_b[i])
        else:
            h = _odd_layer(h, hb, batch, seq, od_w_qkv[i], od_w_f[i], od_b_f[i], od_w_out[i], od_ln1_g[i],
                           od_ln1_b[i], od_router_w[i], od_router_b[i], od_exp_w_gate[i], od_exp_w_up[i],
                           od_exp_w_down[i], od_ln2_g[i], od_ln2_b[i])
    return h.reshape(batch, seq, D_MODEL)
```

```python
import functools

import jax
import jax.numpy as jnp
from jax import lax
from jax.experimental import pallas as pl
from jax.experimental.pallas import tpu as pltpu

F32 = jnp.float32
BF16 = jnp.bfloat16

D_MODEL = 2048
HEAD_DIM = 128
LANES = 128
MXU_COLS = 256
ROPE_THETA = 10000.0
LN_EPS = 1e-5
RMS_EPS = 1e-6

MLA_HEADS = 10
MLA_Q_RANK = 448
MLA_Q_RANK_PAD = 512
MLA_KV_RANK = 128
MLA_NOPE_DIM = 128
MLA_ROPE_DIM = 64
MLA_V_DIM = 128
MLA_QK_DIM = MLA_NOPE_DIM + MLA_ROPE_DIM
MLA_QK_PAD = 256

DIL_PATTERNS = ((128, 1), (512, 4), (2048, 16))
DIL_GROUPS = 3
DIL_HEADS = 6
DIL_WIDTH = DIL_HEADS * HEAD_DIM
DIL_SPAN = 128

OFF_CQ = 0
OFF_CKV = OFF_CQ + MLA_Q_RANK
OFF_KROPE = OFF_CKV + MLA_KV_RANK
OFF_DQ = OFF_KROPE + MLA_ROPE_DIM
OFF_DK = OFF_DQ + DIL_GROUPS * DIL_WIDTH
OFF_DV = OFF_DK + DIL_WIDTH
W_IN_COLS = OFF_DV + DIL_WIDTH

FOX_HEADS = 16
FOX_WIDTH = FOX_HEADS * HEAD_DIM

D_FF = 5632
N_EXPERTS = 8
DEPTH = 2
ALPHA = (2.0 * DEPTH) ** 0.25

NEG = -1e30
LOG2E = 1.4426950408889634
VMEM_LIMIT = 56 * 1024 * 1024

ROW_TILE = 512
WIDE_ROW_TILE = 1024
FLASH_TQ = 1024
DIL_TILE = 2048
GATE_ROW_TILE = 512
QKV_COL_TILE = 1024
FFN_COL_TILE = 512
FFN_K_TILE = 1408
MOE_TILE = 512
MOE_COL_TILE = 1408
MOE_DOWN_COL_TILE = 1024
DMA_UNROLL = 8
STREAM_BUFFERS = 3


def _params(*sem, vmem=VMEM_LIMIT):
    return pltpu.CompilerParams(dimension_semantics=sem, vmem_limit_bytes=vmem)


def _pipelined_call(body, grid, in_specs, out_specs, out_shape, name):
    n_in = len(in_specs)

    def outer(*refs):
        pltpu.emit_pipeline(body, grid=grid, in_specs=in_specs, out_specs=out_specs)(*refs)

    hbm = pl.BlockSpec(memory_space=pl.ANY)
    single = not isinstance(out_shape, (list, tuple))
    return pl.pallas_call(
        outer,
        in_specs=[hbm] * n_in,
        out_specs=hbm if single else [hbm] * len(out_shape),
        out_shape=out_shape,
        compiler_params=pltpu.CompilerParams(vmem_limit_bytes=VMEM_LIMIT),
        name=name,
    )


def _dot(a, b):
    return jnp.dot(a, b, preferred_element_type=F32)


def _dot_nt(a, b):
    return lax.dot_general(a, b, (((1,), (1,)), ((), ())), preferred_element_type=F32)


def _layer_norm(y, g, b):
    mu = jnp.mean(y, axis=-1, keepdims=True)
    d = y - mu
    var = jnp.mean(d * d, axis=-1, keepdims=True)
    return d * lax.rsqrt(var + LN_EPS) * g + b


def _rope128(x, cos_f, sin_f):
    return x * cos_f + pltpu.roll(x, 64, 1) * sin_f


def _rope64(x, cos_t, sin_a, sin_b):
    return x * cos_t + pltpu.roll(x, 96, 1) * sin_a + pltpu.roll(x, 32, 1) * sin_b


def _mla_prep_kernel(x_ref, wa_ref, kvg_ref, qg_ref, wqt_ref, wk_ref, wvt_ref, cos_ref, sa_ref, sb_ref,
                     ct_ref, st_ref, qt_ref, k_ref, vt_ref, xb_ref):
    xb = x_ref[...].astype(BF16)
    xb_ref[...] = xb
    h = _dot(xb, wa_ref[...])
    ckv = h[:, 0:128]
    kr = h[:, 128:256]
    cq = h[:, 256:768]
    ckv_n = (ckv * lax.rsqrt(jnp.mean(ckv * ckv, axis=-1, keepdims=True) + RMS_EPS) * kvg_ref[...]).astype(BF16)
    cq_ms = jnp.sum(cq * cq, axis=-1, keepdims=True) * (1.0 / MLA_Q_RANK)
    cq_n = (cq * lax.rsqrt(cq_ms + RMS_EPS) * qg_ref[...]).astype(BF16)
    kr_r = _rope64(kr, cos_ref[...], sa_ref[...], sb_ref[...]).astype(BF16)
    scale = MLA_QK_DIM ** -0.5 * LOG2E
    qt = _dot_nt(wqt_ref[...], cq_n)
    c, s = ct_ref[...], st_ref[...]
    k_nope = _dot(ckv_n, wk_ref[...])
    for hd in range(MLA_HEADS):
        o = hd * MLA_QK_PAD
        qt_ref[o:o + 128, :] = (qt[o:o + 128] * scale).astype(BF16)
        x1, x2 = qt[o + 128:o + 160], qt[o + 160:o + 192]
        qt_ref[o + 128:o + 160, :] = ((x1 * c - x2 * s) * scale).astype(BF16)
        qt_ref[o + 160:o + 192, :] = ((x2 * c + x1 * s) * scale).astype(BF16)
        qt_ref[o + 192:o + 256, :] = jnp.zeros((64, qt.shape[1]), BF16)
        k_ref[:, o:o + 128] = k_nope[:, hd * 128:(hd + 1) * 128].astype(BF16)
        k_ref[:, o + 128:o + 256] = kr_r
    vt_ref[...] = _dot_nt(wvt_ref[...], ckv_n).astype(BF16)


def _mla_prep(x3, wa, kvg, qg, wqt, wk, wvt, cos_t, sin_a, sin_b, cos_tt, sin_tt, tm):
    b, s, _ = x3.shape
    full = lambda shape: pl.BlockSpec(shape, lambda bi, i: (0, 0))
    tab = pl.BlockSpec((tm, LANES), lambda bi, i: (i, 0))
    tab_t = pl.BlockSpec((MLA_ROPE_DIM // 2, tm), lambda bi, i: (0, i))
    wide = MLA_HEADS * MLA_QK_PAD
    vw = MLA_HEADS * MLA_V_DIM
    return pl.pallas_call(
        _mla_prep_kernel,
        grid=(b, s // tm),
        in_specs=[pl.BlockSpec((None, tm, D_MODEL), lambda bi, i: (bi, i, 0)), full(wa.shape), full(kvg.shape),
                  full(qg.shape), full(wqt.shape), full(wk.shape), full(wvt.shape), tab, tab, tab, tab_t, tab_t],
        out_specs=[pl.BlockSpec((None, wide, tm), lambda bi, i: (bi, 0, i)),
                   pl.BlockSpec((None, tm, wide), lambda bi, i: (bi, i, 0)),
                   pl.BlockSpec((None, vw, tm), lambda bi, i: (bi, 0, i)),
                   pl.BlockSpec((None, tm, D_MODEL), lambda bi, i: (bi, i, 0))],
        out_shape=[jax.ShapeDtypeStruct((b, wide, s), BF16), jax.ShapeDtypeStruct((b, s, wide), BF16),
                   jax.ShapeDtypeStruct((b, vw, s), BF16), jax.ShapeDtypeStruct((b, s, D_MODEL), BF16)],
        compiler_params=_params("parallel", "parallel"),
        name="mla_prep",
    )(x3, wa, kvg, qg, wqt, wk, wvt, cos_t, sin_a, sin_b, cos_tt, sin_tt)


def _dqkv_kernel(x_ref, w_ref, cos_ref, sin_ref, o_ref):
    j = pl.program_id(0)
    h = _dot(x_ref[...], w_ref[...])

    @pl.when(j < 4)
    def _():
        cos_f, sin_f = cos_ref[...], sin_ref[...]
        sc = jnp.where(j < 3, HEAD_DIM ** -0.5, 1.0).astype(F32)
        for hd in range(DIL_HEADS):
            sl = slice(hd * 128, (hd + 1) * 128)
            o_ref[:, sl] = (_rope128(h[:, sl], cos_f, sin_f) * sc).astype(BF16)

    @pl.when(j == 4)
    def _():
        o_ref[...] = h.astype(BF16)


def _dqkv(xb, wd, cos_f, sin_f, seq, tm):
    m = xb.shape[0]
    nrow = seq // tm
    n_col = wd.shape[1] // DIL_WIDTH
    tab = pl.BlockSpec((tm, LANES), lambda j, i: (i % nrow, 0))
    return _pipelined_call(
        _dqkv_kernel,
        grid=(n_col, m // tm),
        in_specs=[pl.BlockSpec((tm, D_MODEL), lambda j, i: (i, 0), pipeline_mode=pl.Buffered(STREAM_BUFFERS)),
                  pl.BlockSpec((D_MODEL, DIL_WIDTH), lambda j, i: (0, j)), tab, tab],
        out_specs=[pl.BlockSpec((tm, DIL_WIDTH), lambda j, i: (i, j))],
        out_shape=[jax.ShapeDtypeStruct((m, wd.shape[1]), BF16)],
        name="dil_qkv",
    )(xb, wd, cos_f, sin_f)[0]


def _flash_kernel(*refs, tq, use_c):
    if use_c:
        qt_ref, k_ref, vt_ref, c_ref, o_ref, acc_ref, s0_ref, s1_ref, kaug_ref = refs
    else:
        qt_ref, k_ref, vt_ref, o_ref, acc_ref, s0_ref, s1_ref = refs
    seq = k_ref.shape[0]
    nq = seq // tq
    if use_c:
        row = lax.broadcasted_iota(jnp.int32, (LANES, tq), 0)

        def build(j, _):
            start = pl.multiple_of(j * tq, tq)
            neg = -LOG2E * c_ref[:, pl.ds(start, tq)]
            hi = neg.astype(BF16).astype(F32)
            mid = (neg - hi).astype(BF16).astype(F32)
            lo = neg - hi - mid
            blk = jnp.where(row == 0, hi, jnp.where(row == 1, mid, jnp.where(row == 2, lo, 0.0)))
            kaug_ref[pl.ds(start, tq), :] = blk.T.astype(BF16)
            return 0

        lax.fori_loop(0, nq, build, 0)

    def scores(i, j, s_ref):
        q = qt_ref[:, pl.ds(pl.multiple_of(i * tq, tq), tq)]
        if use_c:
            ones = (lax.broadcasted_iota(jnp.int32, (LANES, tq), 0) < 3).astype(BF16)
            q = jnp.concatenate([q, ones], axis=0)
        start = pl.multiple_of(j * tq, tq)
        kt = k_ref[pl.ds(start, tq), :]
        if use_c:
            kt = jnp.concatenate([kt, kaug_ref[pl.ds(start, tq), :]], axis=1)
        s_ref[...] = _dot(kt, q)

    def update(j, s_ref, stats):
        m, l = stats
        s = s_ref[...]
        m_new = jnp.maximum(m, jnp.max(s, axis=0, keepdims=True))
        a = jnp.exp2(m - m_new)
        p = jnp.exp2(s - m_new)
        l = a * l + jnp.sum(p, axis=0, keepdims=True)
        start = pl.multiple_of(j * tq, tq)
        acc_ref[...] = a * acc_ref[...] + _dot(vt_ref[:, pl.ds(start, tq)], p.astype(BF16))
        return m_new, l

    half = tq // 2

    def scores_diag(i, s_ref):
        q = qt_ref[:, pl.ds(pl.multiple_of(i * tq, tq), tq)]
        if use_c:
            ones = (lax.broadcasted_iota(jnp.int32, (LANES, tq), 0) < 3).astype(BF16)
            q = jnp.concatenate([q, ones], axis=0)
        start = pl.multiple_of(i * tq, tq)
        kt = k_ref[pl.ds(start, tq), :]
        if use_c:
            kt = jnp.concatenate([kt, kaug_ref[pl.ds(start, tq), :]], axis=1)
        s_ref[0:half, 0:half] = _dot(kt[0:half], q[:, 0:half])
        s_ref[:, half:tq] = _dot(kt, q[:, half:tq])

    def update_diag(j, s_ref, stats):
        m, l = stats
        start = pl.multiple_of(j * tq, tq)
        parts = []
        for lo, nkeys in ((0, half), (half, tq)):
            cols = slice(lo, lo + half)
            key = lax.broadcasted_iota(jnp.int32, (nkeys, half), 0)
            qry = lax.broadcasted_iota(jnp.int32, (nkeys, half), 1) + lo
            s = jnp.where(key <= qry, s_ref[0:nkeys, cols], NEG)
            m_new = jnp.maximum(m[:, cols], jnp.max(s, axis=0, keepdims=True))
            a = jnp.exp2(m[:, cols] - m_new)
            p = jnp.exp2(s - m_new)
            parts.append((m_new, a * l[:, cols] + jnp.sum(p, axis=0, keepdims=True)))
            acc_ref[:, cols] = a * acc_ref[:, cols] + _dot(vt_ref[:, pl.ds(start, nkeys)], p.astype(BF16))
        return tuple(jnp.concatenate([parts[0][n], parts[1][n]], axis=1) for n in range(2))

    def query_tile(i, first, second):
        acc_ref[...] = jnp.zeros_like(acc_ref)

        def pair(jj, stats):
            scores(i, 2 * jj + 1, second)
            stats = update(2 * jj, first, stats)
            scores(i, 2 * jj + 2, first)
            return update(2 * jj + 1, second, stats)

        init = (jnp.full((1, tq), NEG, F32), jnp.zeros((1, tq), F32))
        stats = lax.fori_loop(0, i // 2, pair, init)
        nxt = jnp.minimum(i + 1, nq - 1)

        def odd_tail(stats):
            scores_diag(i, second)
            stats = update(i - 1, first, stats)
            scores(nxt, 0, first)
            return update_diag(i, second, stats)

        def even_tail(stats):
            scores(nxt, 0, second)
            return update_diag(i, first, stats)

        _, l = lax.cond(i % 2 == 1, odd_tail, even_tail, stats)
        o_ref[pl.ds(pl.multiple_of(i * tq, tq), tq), :] = (acc_ref[...] / l).T.astype(o_ref.dtype)
        return 0

    scores(0, 0, s0_ref)

    def query_loop(i, _):
        return lax.cond(((i + 1) // 2) % 2 == 0, lambda: query_tile(i, s0_ref, s1_ref),
                        lambda: query_tile(i, s1_ref, s0_ref))

    lax.fori_loop(0, nq, query_loop, 0)


def _flash(qt, k, vt, c, *, heads, dq, dk, dv, tq):
    b, s, _ = k.shape
    use_c = c is not None
    in_specs = [pl.BlockSpec((None, dq, s), lambda bi, h: (bi, h, 0)),
                pl.BlockSpec((None, s, dk), lambda bi, h: (bi, 0, h)),
                pl.BlockSpec((None, dv, s), lambda bi, h: (bi, h, 0))]
    args = [qt, k, vt]
    scratch = [pltpu.VMEM((dv, tq), F32), pltpu.VMEM((tq, tq), F32), pltpu.VMEM((tq, tq), F32)]
    if use_c:
        in_specs.append(pl.BlockSpec((None, None, 1, s), lambda bi, h: (bi, h, 0, 0)))
        args.append(c)
        scratch.append(pltpu.VMEM((s, LANES), BF16))
    return pl.pallas_call(
        functools.partial(_flash_kernel, tq=tq, use_c=use_c),
        grid=(b, heads),
        in_specs=in_specs,
        out_specs=pl.BlockSpec((None, s, dv), lambda bi, h: (bi, 0, h)),
        out_shape=jax.ShapeDtypeStruct((b, s, heads * dv), BF16),
        scratch_shapes=scratch,
        compiler_params=_params("parallel", "parallel"),
        name="flash_fox" if use_c else "flash_mla",
    )(*args)


def _dilated_kernel(q_ref, kc_ref, kp_ref, vc_ref, vp_ref, o_ref, lse_ref, qf, kf, vf, *, dil, tn):
    i = pl.program_id(1)
    per_class = tn // dil
    qf[...] = q_ref[...].astype(F32)
    kf[0:tn, :] = kp_ref[...].astype(F32)
    kf[tn:2 * tn, :] = kc_ref[...].astype(F32)
    vf[0:tn, :] = vp_ref[...].astype(F32)
    vf[tn:2 * tn, :] = vc_ref[...].astype(F32)
    row = lax.broadcasted_iota(jnp.int32, (DIL_SPAN, 2 * DIL_SPAN), 0)
    col = lax.broadcasted_iota(jnp.int32, (DIL_SPAN, 2 * DIL_SPAN), 1)
    back = row + DIL_SPAN - col
    in_band = jnp.where(back >= 0, jnp.where(back <= DIL_SPAN, 0.0, NEG), NEG)
    first_band = jnp.where(col >= jnp.where(i > 0, 0, DIL_SPAN), in_band, NEG)
    for r in range(dil):
        k_r = jnp.concatenate([kf[pl.ds(tn - DIL_SPAN * dil + r, DIL_SPAN, stride=dil), :],
                               kf[pl.ds(tn + r, per_class, stride=dil), :]], axis=0).astype(BF16)
        v_r = jnp.concatenate([vf[pl.ds(tn - DIL_SPAN * dil + r, DIL_SPAN, stride=dil), :],
                               vf[pl.ds(tn + r, per_class, stride=dil), :]], axis=0).astype(BF16)
        q_r = qf[pl.ds(r, per_class, stride=dil), :].astype(BF16)
        for a in range(per_class // DIL_SPAN):
            lo = a * DIL_SPAN
            s = _dot_nt(q_r[lo:lo + DIL_SPAN], k_r[lo:lo + 2 * DIL_SPAN]) + (first_band if a == 0 else in_band)
            m = jnp.max(s, axis=-1, keepdims=True)
            p = jnp.exp(s - m)
            l = jnp.sum(p, axis=-1, keepdims=True)
            rows = pl.ds(r + lo * dil, DIL_SPAN, stride=dil)
            o_ref[rows, :] = _dot(p.astype(BF16), v_r[lo:lo + 2 * DIL_SPAN]) / l
            lse_ref[rows, :] = jnp.broadcast_to(m + jnp.log(l), (DIL_SPAN, LANES))


def _dilated_group(dqkv3, g, dil, tn):
    batch, seq, _ = dqkv3.shape
    assert tn % (DIL_SPAN * dil) == 0
    blk = lambda col, prev: pl.BlockSpec(
        (None, tn, HEAD_DIM), lambda b, i, h: (b, jnp.maximum(i - 1, 0) if prev else i, col * DIL_HEADS + h))
    out = pl.BlockSpec((None, tn, HEAD_DIM), lambda b, i, h: (b, i, h))
    o, lse = pl.pallas_call(
        functools.partial(_dilated_kernel, dil=dil, tn=tn),
        grid=(batch, seq // tn, DIL_HEADS),
        in_specs=[blk(g, False), blk(3, False), blk(3, True), blk(4, False), blk(4, True)],
        out_specs=[out, out],
        out_shape=[jax.ShapeDtypeStruct((batch, seq, DIL_WIDTH), F32)] * 2,
        scratch_shapes=[pltpu.VMEM((tn, HEAD_DIM), F32), pltpu.VMEM((2 * tn, HEAD_DIM), F32),
                        pltpu.VMEM((2 * tn, HEAD_DIM), F32)],
        compiler_params=_params("parallel", "parallel", "parallel"),
        name=f"dilated_{dil}",
    )(dqkv3, dqkv3, dqkv3, dqkv3, dqkv3)
    return o.reshape(batch * seq, DIL_WIDTH), lse.reshape(batch * seq, DIL_WIDTH)


def _even_out_kernel(om_ref, o0, o1, o2, l0, l1, l2, wm_ref, wd_ref, x_ref, g_ref, b_ref, o_ref, ob_ref):
    ls = [l0[...], l1[...], l2[...]]
    mx = jnp.maximum(jnp.maximum(ls[0], ls[1]), ls[2])
    es = [jnp.exp(v - mx) for v in ls]
    den = es[0] + es[1] + es[2]
    o_dil = ((es[0] / den) * o0[...] + (es[1] / den) * o1[...] + (es[2] / den) * o2[...]).astype(BF16)
    y = _dot(om_ref[...], wm_ref[...]) + _dot(o_dil, wd_ref[...])
    out = _layer_norm(ALPHA * x_ref[...] + y, g_ref[...], b_ref[...])
    o_ref[...] = out
    ob_ref[...] = out.astype(BF16)


def _even_out(o_mla, outs, lses, w_mla, w_dil, x, g, b, tm):
    m = x.shape[0]
    row = lambda width: pl.BlockSpec((tm, width), lambda i: (i, 0))
    full = lambda shape: pl.BlockSpec(shape, lambda i: (0, 0))
    return pl.pallas_call(
        _even_out_kernel,
        grid=(m // tm,),
        in_specs=[row(o_mla.shape[1])] + [row(DIL_WIDTH)] * 6
        + [full(w_mla.shape), full(w_dil.shape), row(D_MODEL), full(g.shape), full(b.shape)],
        out_specs=[row(D_MODEL), row(D_MODEL)],
        out_shape=[jax.ShapeDtypeStruct((m, D_MODEL), F32), jax.ShapeDtypeStruct((m, D_MODEL), BF16)],
        compiler_params=_params("parallel"),
        name="even_out_ln",
    )(o_mla, *outs, *lses, w_mla, w_dil, x, g, b)


def _odd_out_kernel(a_ref, w_ref, x_ref, g_ref, b_ref, o_ref):
    o_ref[...] = _layer_norm(ALPHA * x_ref[...] + _dot(a_ref[...], w_ref[...]), g_ref[...], b_ref[...])


def _odd_out(a, w, x, g, b, tm):
    m = x.shape[0]
    deep = pl.Buffered(STREAM_BUFFERS)
    row = lambda width, **kw: pl.BlockSpec((tm, width), lambda i: (i, 0), **kw)
    full = lambda shape: pl.BlockSpec(shape, lambda i: (0, 0))
    return _pipelined_call(
        _odd_out_kernel,
        grid=(m // tm,),
        in_specs=[row(a.shape[1], pipeline_mode=deep), full(w.shape), row(D_MODEL, pipeline_mode=deep),
                  full(g.shape), full(b.shape)],
        out_specs=[row(D_MODEL)],
        out_shape=[jax.ShapeDtypeStruct((m, D_MODEL), F32)],
        name="odd_out_ln",
    )(a, w, x, g, b)[0]


def _router_kernel(x_ref, rwh_ref, rwl_ref, rb_ref, route_ref, cnt_ref, carry_ref):
    @pl.when(pl.program_id(0) == 0)
    def _():
        carry_ref[...] = jnp.zeros_like(carry_ref)

    x = x_ref[...]
    xh = x.astype(BF16)
    xl = (x - xh.astype(F32)).astype(BF16)
    logits = _dot(xh, rwh_ref[...]) + (_dot(xh, rwl_ref[...]) + _dot(xl, rwh_ref[...])) + rb_ref[...]
    tm = logits.shape[0]
    lane = lax.broadcasted_iota(jnp.int32, (tm, LANES), 1)
    l1 = jnp.max(logits, axis=-1, keepdims=True)
    i1 = jnp.min(jnp.where(logits == l1, lane, LANES), axis=-1, keepdims=True)
    rest = jnp.where(lane == i1, NEG, logits)
    l2 = jnp.max(rest, axis=-1, keepdims=True)
    i2 = jnp.min(jnp.where(rest == l2, lane, LANES), axis=-1, keepdims=True)
    e = jnp.exp(l2 - l1)
    w1 = 1.0 / (1.0 + e)
    w2 = e / (1.0 + e)
    hot1 = (lane == i1).astype(F32)
    hot2 = (lane == i2).astype(F32)
    cnt = hot1 + hot2
    strict = (lax.broadcasted_iota(jnp.int32, (tm, tm), 1) < lax.broadcasted_iota(jnp.int32, (tm, tm), 0)).astype(BF16)
    before = _dot(strict, cnt.astype(BF16)) + carry_ref[...]
    r1 = jnp.sum(before * hot1, axis=-1, keepdims=True)
    r2 = jnp.sum(before * hot2, axis=-1, keepdims=True)
    vals = (i1.astype(F32), i2.astype(F32), w1, w2, r1, r2)
    route = jnp.zeros((tm, LANES), F32)
    for idx, val in enumerate(vals):
        route = jnp.where(lane == idx, val, route)
    route_ref[...] = route
    total = carry_ref[...] + jnp.sum(cnt, axis=0, keepdims=True)
    carry_ref[...] = total
    cnt_ref[...] = jnp.broadcast_to(total, cnt_ref.shape)


def _router(x, rw, rb, tm):
    m = x.shape[0]
    full = lambda shape: pl.BlockSpec(shape, lambda i: (0, 0))
    rw_hi = rw.astype(BF16)
    rw_lo = (rw - rw_hi.astype(F32)).astype(BF16)
    return pl.pallas_call(
        _router_kernel,
        grid=(m // tm,),
        in_specs=[pl.BlockSpec((tm, D_MODEL), lambda i: (i, 0)), full(rw.shape), full(rw.shape), full(rb.shape)],
        out_specs=[pl.BlockSpec((tm, LANES), lambda i: (i, 0)), pl.BlockSpec((8, LANES), lambda i: (0, 0))],
        out_shape=[jax.ShapeDtypeStruct((m, LANES), F32), jax.ShapeDtypeStruct((8, LANES), F32)],
        scratch_shapes=[pltpu.VMEM((1, LANES), F32)],
        compiler_params=_params("arbitrary"),
        name="moe_router",
    )(x, rw_hi, rw_lo, rb)


def _silu_mul(g, u):
    return g * (1.0 / (1.0 + jnp.exp(-g))) * u


def _ffn_up_kernel(x_ref, wg_ref, wu_ref, o_ref, wgb_ref, wub_ref):
    @pl.when(pl.program_id(1) == 0)
    def _():
        wgb_ref[...] = wg_ref[...].astype(BF16)
        wub_ref[...] = wu_ref[...].astype(BF16)

    x = x_ref[...]
    o_ref[...] = _silu_mul(_dot(x, wgb_ref[...]), _dot(x, wub_ref[...])).astype(BF16)


def _ffn_up(xb, wg, wu, tm, tn):
    m = xb.shape[0]
    return pl.pallas_call(
        _ffn_up_kernel,
        grid=(D_FF // tn, m // tm),
        in_specs=[pl.BlockSpec((tm, D_MODEL), lambda j, i: (i, 0)),
                  pl.BlockSpec((D_MODEL, tn), lambda j, i: (0, j)),
                  pl.BlockSpec((D_MODEL, tn), lambda j, i: (0, j))],
        out_specs=pl.BlockSpec((tm, tn), lambda j, i: (i, j)),
        out_shape=jax.ShapeDtypeStruct((m, D_FF), BF16),
        scratch_shapes=[pltpu.VMEM((D_MODEL, tn), BF16), pltpu.VMEM((D_MODEL, tn), BF16)],
        compiler_params=_params("arbitrary", "arbitrary"),
        name="ffn_up",
    )(xb, wg, wu)


def _ffn_down_ln_kernel(h_ref, w_ref, x_ref, g_ref, b_ref, o_ref, ob_ref, acc_ref):
    k = pl.program_id(1)

    @pl.when(k == 0)
    def _():
        acc_ref[...] = jnp.zeros_like(acc_ref)

    acc_ref[...] += _dot(h_ref[...], w_ref[...])

    @pl.when(k == pl.num_programs(1) - 1)
    def _():
        out = _layer_norm(ALPHA * x_ref[...] + acc_ref[...], g_ref[...], b_ref[...])
        o_ref[...] = out
        ob_ref[...] = out.astype(BF16)


def _ffn_down_ln(h, wd, x, g, b, tm, tk):
    m = x.shape[0]
    row = pl.BlockSpec((tm, D_MODEL), lambda i, k: (i, 0))
    vec = pl.BlockSpec((1, D_MODEL), lambda i, k: (0, 0))
    deep = pl.Buffered(STREAM_BUFFERS)

    def outer(h_hbm, w_hbm, x_hbm, g_hbm, b_hbm, o_hbm, ob_hbm, acc_ref):
        pltpu.emit_pipeline(
            lambda *refs: _ffn_down_ln_kernel(*refs, acc_ref),
            grid=(m // tm, D_FF // tk),
            in_specs=[pl.BlockSpec((tm, tk), lambda i, k: (i, k), pipeline_mode=deep),
                      pl.BlockSpec((tk, D_MODEL), lambda i, k: (k, 0), pipeline_mode=deep), row, vec, vec],
            out_specs=[row, row],
        )(h_hbm, w_hbm, x_hbm, g_hbm, b_hbm, o_hbm, ob_hbm)

    hbm = pl.BlockSpec(memory_space=pl.ANY)
    return pl.pallas_call(
        outer,
        in_specs=[hbm] * 5,
        out_specs=[hbm, hbm],
        out_shape=[jax.ShapeDtypeStruct((m, D_MODEL), F32), jax.ShapeDtypeStruct((m, D_MODEL), BF16)],
        scratch_shapes=[pltpu.VMEM((tm, D_MODEL), F32)],
        compiler_params=pltpu.CompilerParams(vmem_limit_bytes=VMEM_LIMIT),
        name="ffn_down_ln",
    )(h, wd, x, g, b)


def _proj_kernel(x_ref, w_ref, o_ref):
    o_ref[...] = _dot(x_ref[...], w_ref[...]).astype(BF16)


def _proj(xb, w, tm, tn):
    m = xb.shape[0]
    n = w.shape[1]
    return pl.pallas_call(
        _proj_kernel,
        grid=(n // tn, m // tm),
        in_specs=[pl.BlockSpec((tm, D_MODEL), lambda j, i: (i, 0)), pl.BlockSpec((D_MODEL, tn), lambda j, i: (0, j))],
        out_specs=pl.BlockSpec((tm, tn), lambda j, i: (i, j)),
        out_shape=jax.ShapeDtypeStruct((m, n), BF16),
        compiler_params=_params("parallel", "parallel"),
        name="proj",
    )(xb, w)


def _proj_t_kernel(w_ref, x_ref, o_ref, *, scale):
    o_ref[...] = (_dot_nt(w_ref[...], x_ref[...]) * scale).astype(BF16)


def _proj_t(xb3, wt, scale, tm, tn):
    b, s, _ = xb3.shape
    n = wt.shape[0]
    return pl.pallas_call(
        functools.partial(_proj_t_kernel, scale=scale),
        grid=(n // tn, b, s // tm),
        in_specs=[pl.BlockSpec((tn, D_MODEL), lambda j, bi, i: (j, 0)),
                  pl.BlockSpec((None, tm, D_MODEL), lambda j, bi, i: (bi, i, 0))],
        out_specs=pl.BlockSpec((None, tn, tm), lambda j, bi, i: (bi, j, i)),
        out_shape=jax.ShapeDtypeStruct((b, n, s), BF16),
        compiler_params=_params("parallel", "parallel", "parallel"),
        name="proj_t",
    )(wt, xb3)


def _fgate_kernel(x_ref, w_ref, b_ref, c_ref, carry_ref):
    @pl.when(pl.program_id(1) == 0)
    def _():
        carry_ref[...] = jnp.zeros_like(carry_ref)

    z = _dot(x_ref[...], w_ref[...]) + b_ref[...]
    log_f = jnp.minimum(z, 0.0) - jnp.log(1.0 + jnp.exp(-jnp.abs(z)))
    tm = z.shape[0]
    tri = (lax.broadcasted_iota(jnp.int32, (tm, tm), 1) <= lax.broadcasted_iota(jnp.int32, (tm, tm), 0)).astype(F32)
    c = jnp.dot(tri, log_f, preferred_element_type=F32, precision=lax.Precision.HIGHEST) + carry_ref[...]
    c_ref[...] = c
    carry_ref[...] = c[tm - 1:tm, :]


def _fgate(xb3, wf, bf, tm):
    b, s, _ = xb3.shape
    return pl.pallas_call(
        _fgate_kernel,
        grid=(b, s // tm),
        in_specs=[pl.BlockSpec((None, tm, D_MODEL), lambda bi, i: (bi, i, 0)),
                  pl.BlockSpec(wf.shape, lambda bi, i: (0, 0)), pl.BlockSpec(bf.shape, lambda bi, i: (0, 0))],
        out_specs=pl.BlockSpec((None, tm, LANES), lambda bi, i: (bi, i, 0)),
        out_shape=jax.ShapeDtypeStruct((b, s, LANES), F32),
        scratch_shapes=[pltpu.VMEM((1, LANES), F32)],
        compiler_params=_params("parallel", "arbitrary"),
        name="fox_gate",
    )(xb3, wf, bf)


def _dispatch_kernel(fill_ref, pos_ref, x_ref, xs_ref, zero_ref, sem, zero_sem):
    tm = x_ref.shape[0]

    @pl.when(pl.program_id(0) == 0)
    def _():
        zero_ref[...] = jnp.zeros_like(zero_ref)
        rows = zero_ref.shape[0]

        def fill(f):
            start = pl.multiple_of(jnp.maximum(fill_ref[f], 0), rows)
            return pltpu.make_async_copy(zero_ref, xs_ref.at[pl.ds(start, rows), :], zero_sem)

        for f in range(fill_ref.shape[0]):
            @pl.when(fill_ref[f] >= 0)
            def _():
                fill(f).start()

        for f in range(fill_ref.shape[0]):
            @pl.when(fill_ref[f] >= 0)
            def _():
                fill(f).wait()

    def copy(r, k):
        return pltpu.make_async_copy(x_ref.at[pl.ds(r, 1), :], xs_ref.at[pl.ds(pos_ref[0, 0, 2 * r + k], 1), :], sem)

    def start(r, _):
        copy(r, 0).start(priority=0)
        copy(r, 1).start(priority=1)
        return 0

    def wait(r, _):
        copy(r, 0).wait()
        copy(r, 1).wait()
        return 0

    lax.fori_loop(0, tm, start, 0, unroll=DMA_UNROLL)
    lax.fori_loop(0, tm, wait, 0, unroll=DMA_UNROLL)


def _dispatch(x, pos, fill_rows, n_rows, tile, tm):
    m, width = x.shape
    pos3 = pos.reshape(m // tm, 1, 2 * tm)
    return pl.pallas_call(
        _dispatch_kernel,
        grid_spec=pltpu.PrefetchScalarGridSpec(
            num_scalar_prefetch=1,
            grid=(m // tm,),
            in_specs=[pl.BlockSpec((1, 1, 2 * tm), lambda i, fill: (i, 0, 0), memory_space=pltpu.SMEM),
                      pl.BlockSpec((tm, width), lambda i, fill: (i, 0))],
            out_specs=pl.BlockSpec(memory_space=pl.ANY),
            scratch_shapes=[pltpu.VMEM((tile, width), x.dtype), pltpu.SemaphoreType.DMA(()),
                            pltpu.SemaphoreType.DMA(())],
        ),
        out_shape=jax.ShapeDtypeStruct((n_rows, width), x.dtype),
        compiler_params=_params("arbitrary"),
        name="moe_dispatch",
    )(fill_rows, pos3, x)


def _moe_up_kernel(te_ref, nu_ref, nx_ref, x_ref, wg_hbm, wu_hbm, o_ref, sg_ref, su_ref, wb_ref, sems, *, tn):
    groups = [(lo, min(MXU_COLS, tn - lo)) for lo in range(0, tn, MXU_COLS)]
    j = pl.program_id(0)
    t = pl.program_id(1)
    live = t < nu_ref[0]
    fresh = jnp.logical_or(t == 0, te_ref[t] != te_ref[jnp.maximum(t - 1, 0)])

    def weight_copies(e, jj):
        cols = pl.ds(pl.multiple_of(jj * tn, LANES), tn)
        return (pltpu.make_async_copy(wg_hbm.at[e, :, cols], sg_ref, sems.at[0]),
                pltpu.make_async_copy(wu_hbm.at[e, :, cols], su_ref, sems.at[1]))

    def start(e, jj):
        for c in weight_copies(e, jj):
            c.start()

    @pl.when(jnp.logical_and(j == 0, t == 0))
    def _():
        start(te_ref[0], 0)

    @pl.when(jnp.logical_and(live, fresh))
    def _():
        for c in weight_copies(te_ref[t], j):
            c.wait()
        def cast_rows(c, _):
            rows = pl.ds(pl.multiple_of(c * 256, 256), 256)
            for lo, width in groups:
                wb_ref[rows, 2 * lo:2 * lo + width] = sg_ref[rows, lo:lo + width].astype(BF16)
                wb_ref[rows, 2 * lo + width:2 * lo + 2 * width] = su_ref[rows, lo:lo + width].astype(BF16)
            return 0

        lax.fori_loop(0, D_MODEL // 256, cast_rows, 0)
        nxt = nx_ref[t]

        @pl.when(nxt >= 0)
        def _():
            start(nxt, j)

        @pl.when(jnp.logical_and(nxt < 0, j + 1 < pl.num_programs(0)))
        def _():
            start(te_ref[0], j + 1)

    @pl.when(live)
    def _():
        x = x_ref[...].astype(BF16)
        for lo, width in groups:
            gu = _dot(x, wb_ref[:, 2 * lo:2 * lo + 2 * width])
            o_ref[:, lo:lo + width] = _silu_mul(gu[:, :width], gu[:, width:]).astype(BF16)

    @pl.when(jnp.logical_not(live))
    def _():
        o_ref[...] = jnp.zeros_like(o_ref)


def _moe_up(xs, wg, wu, tile_expert, n_used, next_expert, tm, tn):
    p = xs.shape[0]
    live = lambda t, nu: jnp.minimum(t, nu[0] - 1)
    return pl.pallas_call(
        functools.partial(_moe_up_kernel, tn=tn),
        grid_spec=pltpu.PrefetchScalarGridSpec(
            num_scalar_prefetch=3,
            grid=(D_FF // tn, p // tm),
            in_specs=[pl.BlockSpec((tm, D_MODEL), lambda j, t, te, nu, nx: (live(t, nu), 0)),
                      pl.BlockSpec(memory_space=pl.ANY), pl.BlockSpec(memory_space=pl.ANY)],
            out_specs=pl.BlockSpec((tm, tn), lambda j, t, te, nu, nx: (t, j)),
            scratch_shapes=[pltpu.VMEM((D_MODEL, tn), F32), pltpu.VMEM((D_MODEL, tn), F32),
                            pltpu.VMEM((D_MODEL, 2 * tn), BF16), pltpu.SemaphoreType.DMA((2,))],
        ),
        out_shape=jax.ShapeDtypeStruct((p, D_FF), BF16),
        compiler_params=_params("arbitrary", "arbitrary"),
        name="moe_up",
    )(tile_expert, n_used, next_expert, xs, wg, wu)


def _moe_down_kernel(te_ref, nu_ref, nx_ref, h_ref, w_hbm, o_ref, st_ref, wb_ref, sem, *, tn):
    j = pl.program_id(0)
    t = pl.program_id(1)
    live = t < nu_ref[0]
    fresh = jnp.logical_or(t == 0, te_ref[t] != te_ref[jnp.maximum(t - 1, 0)])

    def weight_copy(e, jj):
        cols = pl.ds(pl.multiple_of(jj * tn, LANES), tn)
        return pltpu.make_async_copy(w_hbm.at[e, :, cols], st_ref, sem)

    @pl.when(jnp.logical_and(j == 0, t == 0))
    def _():
        weight_copy(te_ref[0], 0).start()

    @pl.when(jnp.logical_and(live, fresh))
    def _():
        weight_copy(te_ref[t], j).wait()

        def cast_rows(c, _):
            rows = pl.ds(pl.multiple_of(c * 256, 256), 256)
            wb_ref[rows, :] = st_ref[rows, :].astype(BF16)
            return 0

        lax.fori_loop(0, D_FF // 256, cast_rows, 0)
        nxt = nx_ref[t]

        @pl.when(nxt >= 0)
        def _():
            weight_copy(nxt, j).start()

        @pl.when(jnp.logical_and(nxt < 0, j + 1 < pl.num_programs(0)))
        def _():
            weight_copy(te_ref[0], j + 1).start()

    @pl.when(live)
    def _():
        h = h_ref[...]
        for lo in range(0, tn, 512):
            o_ref[:, lo:lo + 512] = _dot(h, wb_ref[:, lo:lo + 512])

    @pl.when(jnp.logical_not(live))
    def _():
        o_ref[...] = jnp.zeros_like(o_ref)


def _moe_down(hs, wd, tile_expert, n_used, next_expert, tm, tn):
    p = hs.shape[0]
    live = lambda t, nu: jnp.minimum(t, nu[0] - 1)
    return pl.pallas_call(
        functools.partial(_moe_down_kernel, tn=tn),
        grid_spec=pltpu.PrefetchScalarGridSpec(
            num_scalar_prefetch=3,
            grid=(D_MODEL // tn, p // tm),
            in_specs=[pl.BlockSpec((tm, D_FF), lambda j, t, te, nu, nx: (live(t, nu), 0)),
                      pl.BlockSpec(memory_space=pl.ANY)],
            out_specs=pl.BlockSpec((tm, tn), lambda j, t, te, nu, nx: (t, j)),
            scratch_shapes=[pltpu.VMEM((D_FF, tn), F32), pltpu.VMEM((D_FF, tn), BF16),
                            pltpu.SemaphoreType.DMA(())],
        ),
        out_shape=jax.ShapeDtypeStruct((p, D_MODEL), F32),
        compiler_params=_params("arbitrary", "arbitrary"),
        name="moe_down",
    )(tile_expert, n_used, next_expert, hs, wd)


def _combine_ln_kernel(pos_ref, ys_ref, x_ref, route_ref, g_ref, b_ref, o_ref, buf0, buf1, sem):
    tm = x_ref.shape[0]

    def copies(r):
        c0 = pltpu.make_async_copy(ys_ref.at[pl.ds(pos_ref[0, 0, 2 * r], 1), :], buf0.at[pl.ds(r, 1), :], sem)
        c1 = pltpu.make_async_copy(ys_ref.at[pl.ds(pos_ref[0, 0, 2 * r + 1], 1), :], buf1.at[pl.ds(r, 1), :], sem)
        return c0, c1

    def start(r, _):
        c0, c1 = copies(r)
        c0.start(priority=0)
        c1.start(priority=1)
        return 0

    def wait(r, _):
        c0, c1 = copies(r)
        c0.wait()
        c1.wait()
        return 0

    lax.fori_loop(0, tm, start, 0, unroll=DMA_UNROLL)
    lax.fori_loop(0, tm, wait, 0, unroll=DMA_UNROLL)
    route = route_ref[...]
    y = route[:, 2:3] * buf0[...] + route[:, 3:4] * buf1[...]
    o_ref[...] = _layer_norm(ALPHA * x_ref[...] + y, g_ref[...], b_ref[...])


def _combine_ln(ys, pos, x, route, g, b, tm):
    m = x.shape[0]
    pos3 = pos.reshape(m // tm, 1, 2 * tm)
    vec = pl.BlockSpec((1, D_MODEL), lambda i: (0, 0))
    return pl.pallas_call(
        _combine_ln_kernel,
        grid=(m // tm,),
        in_specs=[pl.BlockSpec((1, 1, 2 * tm), lambda i: (i, 0, 0), memory_space=pltpu.SMEM),
                  pl.BlockSpec(memory_space=pl.ANY),
                  pl.BlockSpec((tm, D_MODEL), lambda i: (i, 0)),
                  pl.BlockSpec((tm, LANES), lambda i: (i, 0)), vec, vec],
        out_specs=pl.BlockSpec((tm, D_MODEL), lambda i: (i, 0)),
        out_shape=jax.ShapeDtypeStruct((m, D_MODEL), F32),
        scratch_shapes=[pltpu.VMEM((tm, D_MODEL), F32), pltpu.VMEM((tm, D_MODEL), F32), pltpu.SemaphoreType.DMA(())],
        compiler_params=_params("arbitrary"),
        name="moe_combine_ln",
    )(pos3, ys, x, route, g, b)


def _rope_tables(seq):
    def angles(dim):
        inv_freq = 1.0 / (ROPE_THETA ** (jnp.arange(0, dim, 2, dtype=F32) / dim))
        ang = jnp.arange(seq, dtype=F32)[:, None] * inv_freq[None, :]
        return jnp.cos(ang), jnp.sin(ang)

    c128, s128 = angles(HEAD_DIM)
    cos_f = jnp.concatenate([c128, c128], axis=-1)
    sin_f = jnp.concatenate([-s128, s128], axis=-1)
    c64, s64 = angles(MLA_ROPE_DIM)
    z32 = jnp.zeros_like(c64)
    cos_t = jnp.concatenate([c64, c64, z32, z32], axis=-1)
    sin_a = jnp.concatenate([-s64, z32, z32, z32], axis=-1)
    sin_b = jnp.concatenate([z32, s64, z32, z32], axis=-1)
    return cos_f, sin_f, cos_t, sin_a, sin_b, c64.T, s64.T


def _pad_cols(a, width):
    return jnp.pad(a, ((0, 0), (0, width - a.shape[1])))


def _row(v):
    return v.reshape(1, -1).astype(F32)


def _even_layer(x, batch, seq, tables, w_in, q_norm, w_q_b, kv_norm, w_kv_b, w_out, ln1_g, ln1_b,
                w_gate, w_up, w_down, ln2_g, ln2_b):
    cos_f, sin_f, cos_t, sin_a, sin_b, cos_tt, sin_tt = tables
    tm = min(ROW_TILE, seq)
    tw = min(WIDE_ROW_TILE, seq)
    wa = jnp.concatenate([w_in[:, OFF_CKV:OFF_KROPE], _pad_cols(w_in[:, OFF_KROPE:OFF_DQ], LANES),
                          _pad_cols(w_in[:, OFF_CQ:OFF_CKV], MLA_Q_RANK_PAD)], axis=1).astype(BF16)
    qg = _pad_cols(_row(q_norm), MLA_Q_RANK_PAD)
    wq = jnp.pad(w_q_b.reshape(MLA_Q_RANK, MLA_HEADS, MLA_QK_DIM),
                 ((0, MLA_Q_RANK_PAD - MLA_Q_RANK), (0, 0), (0, MLA_QK_PAD - MLA_QK_DIM)))
    wqt = wq.reshape(MLA_Q_RANK_PAD, MLA_HEADS * MLA_QK_PAD).T.astype(BF16)
    wkv3 = w_kv_b.reshape(MLA_KV_RANK, MLA_HEADS, MLA_NOPE_DIM + MLA_V_DIM)
    wk = wkv3[:, :, :MLA_NOPE_DIM].reshape(MLA_KV_RANK, -1).astype(BF16)
    wvt = wkv3[:, :, MLA_NOPE_DIM:].reshape(MLA_KV_RANK, -1).T.astype(BF16)
    qt_mla, k_mla, vt_mla, xb3 = _mla_prep(x.reshape(batch, seq, D_MODEL), wa, _row(kv_norm), qg, wqt, wk, wvt,
                                           cos_t, sin_a, sin_b, cos_tt, sin_tt, tm)
    xb = xb3.reshape(batch * seq, D_MODEL)
    tq = min(FLASH_TQ, seq)
    o_mla = _flash(qt_mla, k_mla, vt_mla, None, heads=MLA_HEADS, dq=MLA_QK_PAD, dk=MLA_QK_PAD, dv=MLA_V_DIM,
                   tq=tq)
    o_mla = o_mla.reshape(batch * seq, -1)

    dqkv = _dqkv(xb, w_in[:, OFF_DQ:].astype(BF16), cos_f, sin_f, seq, tw)
    dqkv3 = dqkv.reshape(batch, seq, -1)
    outs, lses = [], []
    for g, (window, dil) in enumerate(DIL_PATTERNS):
        assert window == DIL_SPAN * dil
        o_g, lse_g = _dilated_group(dqkv3, g, dil, min(seq, max(DIL_TILE, DIL_SPAN * dil)))
        outs.append(o_g)
        lses.append(lse_g)
    n_mla = MLA_HEADS * MLA_V_DIM
    wo = w_out.astype(BF16)
    x1, x1b = _even_out(o_mla, outs, lses, wo[:n_mla], wo[n_mla:], x, _row(ln1_g), _row(ln1_b), tm)
    hmid = _ffn_up(x1b, w_gate, w_up, tw, FFN_COL_TILE)
    return _ffn_down_ln(hmid, w_down.astype(BF16), x1, _row(ln2_g), _row(ln2_b), tm, FFN_K_TILE)


def _odd_layer(x, xb, batch, seq, w_qkv, w_f, b_f, w_out, ln1_g, ln1_b, router_w, router_b,
               exp_w_gate, exp_w_up, exp_w_down, ln2_g, ln2_b):
    m = batch * seq
    tm = min(ROW_TILE, seq)
    tw = min(WIDE_ROW_TILE, seq)
    xb3 = xb.reshape(batch, seq, D_MODEL)
    wb = w_qkv.astype(BF16)
    qt = _proj_t(xb3, wb[:, :FOX_WIDTH].T, HEAD_DIM ** -0.5 * LOG2E, tw, QKV_COL_TILE)
    k = _proj(xb, wb[:, FOX_WIDTH:2 * FOX_WIDTH], tw, QKV_COL_TILE).reshape(batch, seq, FOX_WIDTH)
    vt = _proj_t(xb3, wb[:, 2 * FOX_WIDTH:].T, 1.0, tw, QKV_COL_TILE)
    c = _fgate(xb3, _pad_cols(w_f, LANES).astype(BF16), _pad_cols(_row(b_f), LANES), min(GATE_ROW_TILE, seq))
    c_t = jnp.transpose(c[:, :, :FOX_HEADS], (0, 2, 1)).reshape(batch, FOX_HEADS, 1, seq)
    tq = min(FLASH_TQ, seq)
    o = _flash(qt, k, vt, c_t, heads=FOX_HEADS, dq=HEAD_DIM, dk=HEAD_DIM, dv=HEAD_DIM, tq=tq)
    rb = jnp.full((1, LANES), NEG, F32).at[0, :N_EXPERTS].set(router_b.astype(F32))
    x1 = _odd_out(o.reshape(m, -1), w_out.astype(BF16), x, _row(ln1_g), _row(ln1_b), tm)
    route, counts = _router(x1, _pad_cols(router_w, LANES), rb, tm)
    tile = MOE_TILE
    n_tiles = (2 * m) // tile + N_EXPERTS
    cnt = counts[0, :N_EXPERTS].astype(jnp.int32)
    tiles_per = (cnt + tile - 1) // tile
    tile_end = jnp.cumsum(tiles_per)
    offset = (tile_end - tiles_per) * tile
    idx = route[:, 0:2].astype(jnp.int32)
    pos = (offset[idx] + route[:, 4:6].astype(jnp.int32)).reshape(-1)
    n_used = tile_end[-1:]
    tile_ids = jnp.arange(n_tiles, dtype=jnp.int32)
    tile_expert = jnp.minimum(jnp.sum((tile_end[None, :] <= tile_ids[:, None]).astype(jnp.int32), axis=1),
                              N_EXPERTS - 1)
    experts = jnp.arange(N_EXPERTS, dtype=jnp.int32)
    last_tile = jnp.where(tiles_per > 0, tile_end - 1, -1)
    unused = n_used[0] + experts
    unused = jnp.where(unused < n_tiles, unused, -1)
    fill_tiles = jnp.concatenate([last_tile, unused])
    fill_rows = jnp.where(fill_tiles >= 0, fill_tiles * tile, -1).astype(jnp.int32)
    xs = _dispatch(x1, pos, fill_rows, n_tiles * tile, tile, tm)
    later = jnp.where((experts[None, :] > experts[:, None]) & (tiles_per[None, :] > 0), experts[None, :], N_EXPERTS)
    next_nonempty = jnp.min(later, axis=1)
    next_expert = jnp.where(next_nonempty < N_EXPERTS, next_nonempty, -1)[tile_expert].astype(jnp.int32)
    hs = _moe_up(xs, exp_w_gate, exp_w_up, tile_expert, n_used, next_expert, tile, MOE_COL_TILE)
    ys = _moe_down(hs, exp_w_down, tile_expert, n_used, next_expert, tile, MOE_DOWN_COL_TILE)
    return _combine_ln(ys, pos, x1, route, _row(ln2_g), _row(ln2_b), tm)


def kernel(x, ev_w_in, ev_q_norm, ev_w_q_b, ev_kv_norm, ev_w_kv_b, ev_w_out, ev_ln1_g, ev_ln1_b, ev_ffn_w_gate, ev_ffn_w_up, ev_ffn_w_down, ev_ln2_g, ev_ln2_b, od_w_qkv, od_w_f, od_b_f, od_w_out, od_ln1_g, od_ln1_b, od_router_w, od_router_b, od_exp_w_gate, od_exp_w_up, od_exp_w_down, od_ln2_g, od_ln2_b):
    batch, seq, _ = x.shape
    tables = _rope_tables(seq)
    h = x.reshape(batch * seq, D_MODEL)
    hb = None
    for layer in range(DEPTH):
        i = layer // 2
        if layer % 2 == 0:
            h, hb = _even_layer(h, batch, seq, tables, ev_w_in[i], ev_q_norm[i], ev_w_q_b[i], ev_kv_norm[i],
                                ev_w_kv_b[i], ev_w_out[i], ev_ln1_g[i], ev_ln1_b[i], ev_ffn_w_gate[i],
                                ev_ffn_w_up[i], ev_ffn_w_down[i], ev_ln2_g[i], ev_ln2_b[i])
        else:
            h = _odd_layer(h, hb, batch, seq, od_w_qkv[i], od_w_f[i], od_b_f[i], od_w_out[i], od_ln1_g[i],
                           od_ln1_b[i], od_router_w[i], od_router_b[i], od_exp_w_gate[i], od_exp_w_up[i],
                           od_exp_w_down[i], od_ln2_g[i], od_ln2_b[i])
    return h.reshape(batch, seq, D_MODEL)
```
